```python
import math
import jax
import jax.numpy as jnp
from jax import lax
import numpy as np

D_MODEL = 1024
BATCH = 2
SEQ = 8192
DEPTH = 2

D_FF = 4 * D_MODEL
NORM_EPS = 1e-6
N_EVEN = (DEPTH + 1) // 2
N_ODD = DEPTH // 2

GDN_HEADS = 8
GDN_HEAD_DIM = 64
GDN_WIDTH = GDN_HEADS * GDN_HEAD_DIM
GDN_CONV = 4
GDN_CHUNK = 64

SC_WIDTH = D_MODEL - GDN_WIDTH
SC_CONV = 3

EVEN_IN = 4 * GDN_WIDTH + 2 * GDN_HEADS + 3 * SC_WIDTH

NSA_HEADS = 16
NSA_HEAD_DIM = D_MODEL // NSA_HEADS
NSA_KV_GROUPS = 4
NSA_HPG = NSA_HEADS // NSA_KV_GROUPS
NSA_KV_WIDTH = NSA_KV_GROUPS * NSA_HEAD_DIM
CMP_BLOCK = 32
CMP_STRIDE = 16
CMP_HIDDEN = 256
SEL_BLOCK = 64
N_SELECT = 16
WINDOW = 512
Q_BLOCK = 128
ODD_IN = NSA_HEADS * NSA_HEAD_DIM + 6 * NSA_KV_WIDTH + 3 * NSA_HEADS

ROPE_THETA = 500000.0
ROPE_DIM = NSA_HEAD_DIM // 4

kernel_name = 'hybrid_gdn_shortconv_nsa_trunk'


def _split_points(sizes):
    return [int(v) for v in np.cumsum(sizes)[:-1]]


def rms_norm(x, w):
    xf = x.astype(jnp.float32)
    y = xf * lax.rsqrt(jnp.mean(xf * xf, axis=-1, keepdims=True) + NORM_EPS)
    return (y * w.astype(jnp.float32)).astype(x.dtype)


def l2_normalize(x):
    xf = x.astype(jnp.float32)
    return xf * lax.rsqrt(jnp.sum(xf * xf, axis=-1, keepdims=True) + NORM_EPS)


def causal_dwconv(x, w):
    k_width = w.shape[0]
    t_len = x.shape[1]
    xp = jnp.pad(x, ((0, 0), (k_width - 1, 0), (0, 0)))
    return sum(xp[:, j:j + t_len] * w[j] for j in range(k_width))


def masked_softmax(s, mask):
    s = jnp.where(mask, s, -jnp.inf)
    m = jnp.max(s, axis=-1, keepdims=True)
    m = jnp.where(jnp.isfinite(m), m, 0.0)
    e = jnp.exp(s - m)
    den = jnp.sum(e, axis=-1, keepdims=True)
    return e / jnp.where(den > 0.0, den, 1.0)


def rope_tables(t_len):
    inv_freq = ROPE_THETA ** (-jnp.arange(0, ROPE_DIM, 2, dtype=jnp.float32) / ROPE_DIM)
    ang = jnp.arange(t_len, dtype=jnp.float32)[:, None] * inv_freq[None, :]
    return jnp.cos(ang), jnp.sin(ang)


def apply_partial_rope(x, cos, sin):
    half = ROPE_DIM // 2
    xf = x.astype(jnp.float32)
    x1, x2, rest = xf[..., :half], xf[..., half:ROPE_DIM], xf[..., ROPE_DIM:]
    c = cos[None, :, None, :]
    s = sin[None, :, None, :]
    return jnp.concatenate([x1 * c - x2 * s, x2 * c + x1 * s, rest], axis=-1).astype(x.dtype)


def squared_relu_mlp(h, w_up, w_down):
    return jnp.square(jax.nn.relu(h @ w_up)) @ w_down


def gated_delta_rule_chunked(q, k, v, g, beta):
    f32 = jnp.float32
    b_sz, t_len, n_h, dk = q.shape
    dv = v.shape[-1]
    c = GDN_CHUNK
    n_chunks = t_len // c

    def chunks(a):
        a = a.astype(f32).reshape((b_sz, n_chunks, c, n_h) + a.shape[3:])
        return jnp.moveaxis(a, (1, 3), (0, 2))

    qc = chunks(q) * (dk ** -0.5)
    kc = chunks(k)
    vc = chunks(v)
    bc = chunks(beta)
    gc = jnp.cumsum(chunks(g), axis=-1)
    incl = jnp.tril(jnp.ones((c, c), bool))
    strict = jnp.tril(jnp.ones((c, c), bool), -1)
    decay = jnp.exp(jnp.where(incl, gc[..., :, None] - gc[..., None, :], -jnp.inf))
    kb = kc * bc[..., None]
    m_low = jnp.where(strict, jnp.einsum('nbhik,nbhjk->nbhij', kb, kc) * decay, 0.0)
    a_mat = m_low + jnp.eye(c, dtype=f32)
    u = lax.linalg.triangular_solve(a_mat, vc * bc[..., None], left_side=True, lower=True, unit_diagonal=True)
    w = lax.linalg.triangular_solve(a_mat, kb * jnp.exp(gc)[..., None], left_side=True, lower=True, unit_diagonal=True)
    attn = jnp.einsum('nbhik,nbhjk->nbhij', qc, kc) * decay

    def step(state, inp):
        q_c, k_c, u_c, w_c, attn_c, g_c = inp
        v_new = u_c - jnp.einsum('bhck,bhkv->bhcv', w_c, state)
        o_c = (jnp.einsum('bhck,bhkv->bhcv', q_c * jnp.exp(g_c)[..., None], state)
               + jnp.einsum('bhij,bhjv->bhiv', attn_c, v_new))
        g_last = g_c[..., -1:]
        state = (state * jnp.exp(g_last)[..., None]
                 + jnp.einsum('bhck,bhcv->bhkv', k_c * jnp.exp(g_last - g_c)[..., None], v_new))
        return state, o_c

    s0 = jnp.zeros((b_sz, n_h, dk, dv), f32)
    _, o = lax.scan(step, s0, (qc, kc, u, w, attn, gc))
    return jnp.moveaxis(o, (0, 2), (1, 3)).reshape(b_sz, t_len, n_h, dv)


def gdn_shortconv_mixer(h, w_in, qkv_conv, a_log, dt_bias, o_norm, sc_conv, w_out):
    f32 = jnp.float32
    b_sz, t_len, _ = h.shape
    sizes = (3 * GDN_WIDTH, GDN_WIDTH, GDN_HEADS, GDN_HEADS, SC_WIDTH, SC_WIDTH, SC_WIDTH)
    qkv, z, a, b, b_gate, c_gate, hs = jnp.split(h @ w_in, _split_points(sizes), axis=-1)
    qkv = jax.nn.silu(causal_dwconv(qkv, qkv_conv))
    q, k, v = [t.reshape(b_sz, t_len, GDN_HEADS, GDN_HEAD_DIM) for t in jnp.split(qkv, 3, axis=-1)]
    g = -jnp.exp(a_log.astype(f32)) * jax.nn.softplus(a.astype(f32) + dt_bias.astype(f32))
    beta = jax.nn.sigmoid(b.astype(f32))
    o = gated_delta_rule_chunked(l2_normalize(q), l2_normalize(k), v, g, beta)
    o = rms_norm(o, o_norm) * jax.nn.silu(z.reshape(b_sz, t_len, GDN_HEADS, GDN_HEAD_DIM).astype(f32))
    y_a = o.reshape(b_sz, t_len, GDN_WIDTH).astype(h.dtype)
    y_b = b_gate * causal_dwconv(c_gate * hs, sc_conv)
    return jnp.concatenate([y_a, y_b], axis=-1) @ w_out


def compress_blocks(x, pos, w1, w2):
    b_sz, t_len, g_sz, d = x.shape
    r = CMP_BLOCK // CMP_STRIDE
    n_cmp = t_len // CMP_STRIDE - r + 1
    ch = x.reshape(b_sz, t_len // CMP_STRIDE, CMP_STRIDE, g_sz, d)
    blk = jnp.concatenate([ch[:, i:i + n_cmp] for i in range(r)], axis=2)
    blk = blk + pos[None, None, :, None, :].astype(x.dtype)
    blk = jnp.moveaxis(blk, 3, 2).reshape(b_sz, n_cmp, g_sz, CMP_BLOCK * d)
    return jax.nn.silu(blk @ w1) @ w2


def gather_blocks(blocks, idx):
    return jax.vmap(jax.vmap(lambda bl, ix: bl[ix]))(blocks, idx)


def nsa_mixer(h, w_in, cmp_pos, cmp_k_w1, cmp_k_w2, cmp_v_w1, cmp_v_w2, w_out):
    f32 = jnp.float32
    b_sz, t_len, _ = h.shape
    n_g, hpg, dh = NSA_KV_GROUPS, NSA_HPG, NSA_HEAD_DIM
    sizes = (NSA_HEADS * dh,) + (NSA_KV_WIDTH,) * 6 + (3 * NSA_HEADS,)
    q, k_c, v_c, k_s, v_s, k_w, v_w, gl = jnp.split(h @ w_in, _split_points(sizes), axis=-1)
    q = q.reshape(b_sz, t_len, NSA_HEADS, dh)
    k_c, v_c, k_s, v_s, k_w, v_w = [a.reshape(b_sz, t_len, n_g, dh) for a in (k_c, v_c, k_s, v_s, k_w, v_w)]
    gates = jax.nn.sigmoid(gl.astype(f32)).reshape(b_sz, t_len, n_g, hpg, 3)
    cos, sin = rope_tables(t_len)
    q_r = apply_partial_rope(q, cos, sin)
    k_s = apply_partial_rope(k_s, cos, sin)
    k_w = apply_partial_rope(k_w, cos, sin)
    kc_blk = compress_blocks(k_c, cmp_pos, cmp_k_w1, cmp_k_w2).astype(f32)
    vc_blk = compress_blocks(v_c, cmp_pos, cmp_v_w1, cmp_v_w2).astype(f32)
    n_cmp = kc_blk.shape[1]
    cmp_end = jnp.arange(n_cmp) * CMP_STRIDE + CMP_BLOCK - 1
    n_sel_blocks = t_len // SEL_BLOCK
    n_sel = min(N_SELECT, n_sel_blocks)
    c_start = jnp.arange(n_cmp)[:, None] * CMP_STRIDE
    s_start = jnp.arange(n_sel_blocks)[None, :] * SEL_BLOCK
    overlap = ((c_start < s_start + SEL_BLOCK) & (c_start + CMP_BLOCK > s_start)).astype(f32)
    ks_blk = jnp.moveaxis(k_s.reshape(b_sz, n_sel_blocks, SEL_BLOCK, n_g, dh), 3, 1)
    vs_blk = jnp.moveaxis(v_s.reshape(b_sz, n_sel_blocks, SEL_BLOCK, n_g, dh), 3, 1)
    pad = ((0, 0), (WINDOW, 0), (0, 0), (0, 0))
    kw_pad = jnp.pad(k_w, pad)
    vw_pad = jnp.pad(v_w, pad)
    scale = dh ** -0.5
    blk_ids = jnp.arange(n_sel_blocks)
    in_blk = jnp.arange(SEL_BLOCK)
    win_off = jnp.arange(Q_BLOCK + WINDOW)

    def query_block(qb):
        q0 = qb * Q_BLOCK
        t = q0 + jnp.arange(Q_BLOCK)
        take = lambda a: lax.dynamic_slice_in_dim(a, q0, Q_BLOCK, axis=1)
        qn = take(q).astype(f32).reshape(b_sz, Q_BLOCK, n_g, hpg, dh) * scale
        qr = take(q_r).astype(f32).reshape(b_sz, Q_BLOCK, n_g, hpg, dh) * scale
        p_c = masked_softmax(jnp.einsum('bqghd,bcgd->bghqc', qn, kc_blk), cmp_end[None, :] <= t[:, None])
        o_c = jnp.einsum('bghqc,bcgd->bqghd', p_c, vc_blk)
        imp = jnp.einsum('bghqc,cs->bgqs', p_c, overlap)
        cur = (t // SEL_BLOCK)[:, None]
        js = blk_ids[None, :]
        forced = (js == 0) | (js == cur) | (js == cur - 1)
        imp = jnp.where(js > cur, -jnp.inf, jnp.where(forced, jnp.inf, imp))
        _, idx = lax.top_k(imp, n_sel)
        k_sel = gather_blocks(ks_blk, idx).astype(f32).reshape(b_sz, n_g, Q_BLOCK, n_sel * SEL_BLOCK, dh)
        v_sel = gather_blocks(vs_blk, idx).astype(f32).reshape(b_sz, n_g, Q_BLOCK, n_sel * SEL_BLOCK, dh)
        kpos = (idx[..., None] * SEL_BLOCK + in_blk).reshape(b_sz, n_g, Q_BLOCK, n_sel * SEL_BLOCK)
        p_s = masked_softmax(jnp.einsum('bqghd,bgqkd->bghqk', qr, k_sel), (kpos <= t[:, None])[:, :, None])
        o_s = jnp.einsum('bghqk,bgqkd->bqghd', p_s, v_sel)
        spos = q0 - WINDOW + win_off
        dist = t[:, None] - spos[None, :]
        m_w = (dist >= 0) & (dist < WINDOW) & (spos[None, :] >= 0)
        kwin = lax.dynamic_slice_in_dim(kw_pad, q0, Q_BLOCK + WINDOW, axis=1).astype(f32)
        vwin = lax.dynamic_slice_in_dim(vw_pad, q0, Q_BLOCK + WINDOW, axis=1).astype(f32)
        p_w = masked_softmax(jnp.einsum('bqghd,bkgd->bghqk', qr, kwin), m_w)
        o_w = jnp.einsum('bghqk,bkgd->bqghd', p_w, vwin)
        g = take(gates)
        o = g[..., 0:1] * o_c + g[..., 1:2] * o_s + g[..., 2:3] * o_w
        return o.reshape(b_sz, Q_BLOCK, NSA_HEADS * dh).astype(h.dtype)

    out = lax.map(query_block, jnp.arange(t_len // Q_BLOCK))
    out = jnp.moveaxis(out, 0, 1).reshape(b_sz, t_len, NSA_HEADS * dh)
    return out @ w_out


def setup_inputs(seed: int = 0) -> dict:
    key = jax.random.key(seed)
    ks = jax.random.split(key, 24)
    f32 = jnp.float32

    def nrm(k, shape, scale):
        return jax.random.normal(k, shape, f32) * scale

    def gain(k, shape):
        return 1.0 + 0.01 * jax.random.normal(k, shape, f32)

    dt = jnp.exp(jax.random.uniform(ks[9], (N_EVEN, GDN_HEADS), f32, math.log(1e-3), math.log(0.1)))
    return {
        'x': nrm(ks[0], (BATCH, SEQ, D_MODEL), 1.0),
        'mix_norm': gain(ks[1], (DEPTH, D_MODEL)),
        'mlp_norm': gain(ks[2], (DEPTH, D_MODEL)),
        'w_up': nrm(ks[3], (DEPTH, D_MODEL, D_FF), D_MODEL ** -0.5),
        'w_down': nrm(ks[4], (DEPTH, D_FF, D_MODEL), D_FF ** -0.5),
        'final_norm': gain(ks[5], (D_MODEL,)),
        'ev_w_in': nrm(ks[6], (N_EVEN, D_MODEL, EVEN_IN), D_MODEL ** -0.5),
        'ev_qkv_conv': nrm(ks[7], (N_EVEN, GDN_CONV, 3 * GDN_WIDTH), GDN_CONV ** -0.5),
        'ev_a_log': jnp.log(jax.random.uniform(ks[8], (N_EVEN, GDN_HEADS), f32, 1.0, 16.0)),
        'ev_dt_bias': dt + jnp.log(-jnp.expm1(-dt)),
        'ev_o_norm': gain(ks[10], (N_EVEN, GDN_HEAD_DIM)),
        'ev_sc_conv': nrm(ks[11], (N_EVEN, SC_CONV, SC_WIDTH), SC_CONV ** -0.5),
        'ev_w_out': nrm(ks[12], (N_EVEN, GDN_WIDTH + SC_WIDTH, D_MODEL), D_MODEL ** -0.5),
        'od_w_in': nrm(ks[13], (N_ODD, D_MODEL, ODD_IN), D_MODEL ** -0.5),
        'od_cmp_pos': nrm(ks[14], (N_ODD, CMP_BLOCK, NSA_HEAD_DIM), 0.1),
        'od_cmp_k_w1': nrm(ks[15], (N_ODD, CMP_BLOCK * NSA_HEAD_DIM, CMP_HIDDEN), (CMP_BLOCK * NSA_HEAD_DIM) ** -0.5),
        'od_cmp_k_w2': nrm(ks[16], (N_ODD, CMP_HIDDEN, NSA_HEAD_DIM), CMP_HIDDEN ** -0.5),
        'od_cmp_v_w1': nrm(ks[17], (N_ODD, CMP_BLOCK * NSA_HEAD_DIM, CMP_HIDDEN), (CMP_BLOCK * NSA_HEAD_DIM) ** -0.5),
        'od_cmp_v_w2': nrm(ks[18], (N_ODD, CMP_HIDDEN, NSA_HEAD_DIM), CMP_HIDDEN ** -0.5),
        'od_w_out': nrm(ks[19], (N_ODD, NSA_HEADS * NSA_HEAD_DIM, D_MODEL), D_MODEL ** -0.5),
    }


def reference(x, mix_norm, mlp_norm, w_up, w_down, final_norm,
              ev_w_in, ev_qkv_conv, ev_a_log, ev_dt_bias, ev_o_norm, ev_sc_conv, ev_w_out,
              od_w_in, od_cmp_pos, od_cmp_k_w1, od_cmp_k_w2, od_cmp_v_w1, od_cmp_v_w2, od_w_out):
    for layer in range(DEPTH):
        h = rms_norm(x, mix_norm[layer])
        i = layer // 2
        if layer % 2 == 0:
            x = x + gdn_shortconv_mixer(h, ev_w_in[i], ev_qkv_conv[i], ev_a_log[i], ev_dt_bias[i],
                                        ev_o_norm[i], ev_sc_conv[i], ev_w_out[i])
        else:
            x = x + nsa_mixer(h, od_w_in[i], od_cmp_pos[i], od_cmp_k_w1[i], od_cmp_k_w2[i],
                              od_cmp_v_w1[i], od_cmp_v_w2[i], od_w_out[i])
        x = x + squared_relu_mlp(rms_norm(x, mlp_norm[layer]), w_up[layer], w_down[layer])
    return rms_norm(x, final_norm)
```

```python
import functools

import jax
import jax.numpy as jnp
import numpy as np
from jax import lax
from jax.experimental import pallas as pl
from jax.experimental.pallas import tpu as pltpu

F32 = jnp.float32
BF16 = jnp.bfloat16
HIGHEST = lax.Precision.HIGHEST

NORM_EPS = 1e-6
GDN_HEADS = 8
GDN_HEAD_DIM = 64
GDN_WIDTH = GDN_HEADS * GDN_HEAD_DIM
GDN_CONV = 4
GDN_CHUNK = 64
SOLVE_BLOCK = 16
SC_CONV = 3
NSA_HEADS = 16
NSA_HEAD_DIM = 64
NSA_KV_GROUPS = 4
NSA_HPG = NSA_HEADS // NSA_KV_GROUPS
NSA_GROUP_WIDTH = NSA_HPG * NSA_HEAD_DIM
NSA_KV_WIDTH = NSA_KV_GROUPS * NSA_HEAD_DIM
CMP_BLOCK = 32
CMP_STRIDE = 16
SEL_BLOCK = 64
N_SELECT = 16
WINDOW = 512
Q_BLOCK = 128
SEL_KEY_TILE = 512
ROPE_THETA = 500000.0
ROPE_DIM = NSA_HEAD_DIM // 4
LANES = 128
SUBLANES = 8
VMEM_LIMIT = 56 * 1024 * 1024


def _round_up(n, m):
    return (n + m - 1) // m * m


def _params(*semantics):
    return pltpu.CompilerParams(dimension_semantics=semantics, vmem_limit_bytes=VMEM_LIMIT)


def _rms(x, gain):
    return x * lax.rsqrt(jnp.mean(x * x, axis=-1, keepdims=True) + NORM_EPS) * gain


def _sigmoid(x):
    return 1.0 / (1.0 + jnp.exp(-x))


def _silu(x):
    return x * _sigmoid(x)


def _softplus(x):
    return jnp.maximum(x, 0.0) + jnp.log(1.0 + jnp.exp(-jnp.abs(x)))


def _dot(a, b):
    return jnp.dot(a.astype(BF16), b.astype(BF16), preferred_element_type=F32)


def _dot_nt(a, b):
    return lax.dot_general(a.astype(BF16), b.astype(BF16), (((1,), (1,)), ((), ())),
                           preferred_element_type=F32)


def _dot_f32(a, b):
    return jnp.dot(a, b, precision=HIGHEST, preferred_element_type=F32)


def _norm_matmul_kernel(x_ref, g_ref, w_ref, o_ref):
    h = _rms(x_ref[...], g_ref[...])
    o_ref[...] = jnp.dot(h.astype(BF16), w_ref[...], preferred_element_type=F32).astype(o_ref.dtype)


def _norm_matmul(x, gain, w, tm=512):
    m, d = x.shape
    n = w.shape[1]
    return pl.pallas_call(
        _norm_matmul_kernel,
        grid=(m // tm,),
        in_specs=[pl.BlockSpec((tm, d), lambda i: (i, 0)),
                  pl.BlockSpec((1, d), lambda i: (0, 0)),
                  pl.BlockSpec((d, n), lambda i: (0, 0))],
        out_specs=pl.BlockSpec((tm, n), lambda i: (i, 0)),
        out_shape=jax.ShapeDtypeStruct((m, n), F32),
        compiler_params=_params("parallel"),
        name="norm_proj",
    )(x, gain.reshape(1, d), w)


def _gdn_kernel(qkv_ref, z_ref, ab_ref, cw_ref, gp_ref, on_ref, o_ref, xbuf, act, state):
    c = GDN_CHUNK
    dh = GDN_HEAD_DIM
    hist = SUBLANES

    @pl.when(pl.program_id(1) == 0)
    def _():
        xbuf[0:hist, :] = jnp.zeros((hist, 3 * GDN_WIDTH), F32)
        state[...] = jnp.zeros(state.shape, F32)

    x = qkv_ref[...]
    xbuf[hist:hist + c, :] = x
    y = x * cw_ref[GDN_CONV - 1:GDN_CONV, :]
    for j in range(GDN_CONV - 1):
        shift = GDN_CONV - 1 - j
        y = y + xbuf[hist - shift:hist - shift + c, :] * cw_ref[j:j + 1, :]
    xbuf[0:hist, :] = x[c - hist:c, :]
    act[...] = _silu(y)

    ab = ab_ref[...]
    g_all = -jnp.exp(gp_ref[0:1, :]) * _softplus(ab + gp_ref[1:2, :])
    beta_all = _sigmoid(ab)
    row = lax.broadcasted_iota(jnp.int32, (c, c), 0)
    col = lax.broadcasted_iota(jnp.int32, (c, c), 1)
    incl = row >= col
    strict = row > col
    eye = (row == col).astype(F32)
    diag_blk = strict & ((row // SOLVE_BLOCK) == (col // SOLVE_BLOCK))
    gc_all = _dot_f32(incl.astype(F32), g_all)

    scale = dh ** -0.5
    for h in range(GDN_HEADS):
        q = act[:, h * dh:(h + 1) * dh]
        k = act[:, GDN_WIDTH + h * dh:GDN_WIDTH + (h + 1) * dh]
        v = act[:, 2 * GDN_WIDTH + h * dh:2 * GDN_WIDTH + (h + 1) * dh]
        q = q * lax.rsqrt(jnp.sum(q * q, axis=-1, keepdims=True) + NORM_EPS) * scale
        k = k * lax.rsqrt(jnp.sum(k * k, axis=-1, keepdims=True) + NORM_EPS)
        gcol = gc_all[:, h:h + 1]
        grow = jnp.sum(gcol * eye, axis=0, keepdims=True)
        bcol = beta_all[:, GDN_HEADS + h:GDN_HEADS + h + 1]
        g_last = gcol[c - 1:c, :]
        decay = jnp.exp(jnp.where(incl, gcol - grow, -jnp.inf))
        kb = k * bcol
        m_low = jnp.where(strict, _dot_nt(kb, k) * decay, 0.0)
        attn = _dot_nt(q, k) * decay
        d_mat = jnp.where(diag_blk, m_low, 0.0)
        l_mat = m_low - d_mat
        p_inv = eye - d_mat
        d_pow = d_mat
        for _ in range(int(np.log2(SOLVE_BLOCK)) - 1):
            d_pow = _dot_f32(d_pow, d_pow)
            p_inv = p_inv + _dot_f32(p_inv, d_pow)
        n_mat = _dot_f32(p_inv, l_mat)
        q_inv = eye - n_mat
        n_pow = n_mat
        for _ in range(int(np.log2(c // SOLVE_BLOCK)) - 1):
            n_pow = _dot_f32(n_pow, n_pow)
            q_inv = q_inv + _dot_f32(q_inv, n_pow)
        a_inv = _dot_f32(q_inv, p_inv)
        rhs = jnp.concatenate([v * bcol, kb * jnp.exp(gcol)], axis=-1)
        uw = _dot_f32(a_inv, rhs)
        u = uw[:, :dh]
        w = uw[:, dh:]
        s = state[h]
        v_new = u - _dot(w, s)
        o = _dot(q * jnp.exp(gcol), s) + _dot(attn, v_new)
        k_dec_t = _dot_nt(eye, k * jnp.exp(g_last - gcol))
        state[h] = s * jnp.exp(g_last) + _dot(k_dec_t, v_new)
        zh = z_ref[:, h * dh:(h + 1) * dh]
        o_ref[:, h * dh:(h + 1) * dh] = (_rms(o, on_ref[...]) * _silu(zh)).astype(o_ref.dtype)


def _gdn(proj, conv_w, gate_params, o_norm, batch, t_len):
    m = proj.shape[0]
    c = GDN_CHUNK
    n_chunks = t_len // c
    w3 = 3 * GDN_WIDTH
    ab_col = (4 * GDN_WIDTH + 3 * GDN_WIDTH) // LANES
    row = lambda b, i: b * n_chunks + i
    return pl.pallas_call(
        _gdn_kernel,
        grid=(batch, n_chunks),
        in_specs=[pl.BlockSpec((c, w3), lambda b, i: (row(b, i), 0)),
                  pl.BlockSpec((c, GDN_WIDTH), lambda b, i: (row(b, i), 3)),
                  pl.BlockSpec((c, LANES), lambda b, i: (row(b, i), ab_col)),
                  pl.BlockSpec((GDN_CONV, w3), lambda b, i: (0, 0)),
                  pl.BlockSpec((SUBLANES, LANES), lambda b, i: (0, 0)),
                  pl.BlockSpec((1, GDN_HEAD_DIM), lambda b, i: (0, 0))],
        out_specs=pl.BlockSpec((c, GDN_WIDTH), lambda b, i: (row(b, i), 0)),
        out_shape=jax.ShapeDtypeStruct((m, GDN_WIDTH), BF16),
        scratch_shapes=[pltpu.VMEM((c + SUBLANES, w3), F32),
                        pltpu.VMEM((c, w3), F32),
                        pltpu.VMEM((GDN_HEADS, GDN_HEAD_DIM, GDN_HEAD_DIM), F32)],
        compiler_params=_params("parallel", "arbitrary"),
        name="gdn",
    )(proj, proj, proj, conv_w, gate_params, o_norm.reshape(1, GDN_HEAD_DIM))


def _even_out_kernel(tiles_per_batch, x_ref, ya_ref, bg_ref, cg_ref, hs_ref, cgp_ref, hsp_ref,
                     cw_ref, wa_ref, wb_ref, o_ref, ubuf):
    tm = x_ref.shape[0]
    hist = SUBLANES
    first = (pl.program_id(0) % tiles_per_batch) == 0
    prev = cgp_ref[...] * hsp_ref[...]
    ubuf[0:hist, :] = jnp.where(first, 0.0, prev)
    u = cg_ref[...] * hs_ref[...]
    ubuf[hist:hist + tm, :] = u
    conv = u * cw_ref[SC_CONV - 1:SC_CONV, :]
    for j in range(SC_CONV - 1):
        shift = SC_CONV - 1 - j
        conv = conv + ubuf[hist - shift:hist - shift + tm, :] * cw_ref[j:j + 1, :]
    yb = bg_ref[...] * conv
    acc = jnp.dot(ya_ref[...], wa_ref[...], preferred_element_type=F32)
    acc = acc + jnp.dot(yb.astype(BF16), wb_ref[...], preferred_element_type=F32)
    o_ref[...] = x_ref[...] + acc


def _even_out(x, ya, proj, sc_conv, w_a, w_b, t_len, tm=512):
    m, d = x.shape
    wd = GDN_WIDTH
    tiles_per_batch = t_len // tm
    hb = tm // SUBLANES
    prev = lambda col: pl.BlockSpec((SUBLANES, wd), lambda i: (jnp.maximum(i * hb - 1, 0), col))
    cur = lambda col: pl.BlockSpec((tm, wd), lambda i: (i, col))
    return pl.pallas_call(
        functools.partial(_even_out_kernel, tiles_per_batch),
        grid=(m // tm,),
        in_specs=[pl.BlockSpec((tm, d), lambda i: (i, 0)),
                  pl.BlockSpec((tm, wd), lambda i: (i, 0)),
                  cur(4), cur(5), cur(6), prev(5), prev(6),
                  pl.BlockSpec((SC_CONV, wd), lambda i: (0, 0)),
                  pl.BlockSpec((wd, d), lambda i: (0, 0)),
                  pl.BlockSpec((wd, d), lambda i: (0, 0))],
        out_specs=pl.BlockSpec((tm, d), lambda i: (i, 0)),
        out_shape=jax.ShapeDtypeStruct((m, d), F32),
        scratch_shapes=[pltpu.VMEM((tm + SUBLANES, wd), F32)],
        compiler_params=_params("parallel"),
        name="even_out",
    )(x, ya, proj, proj, proj, proj, proj, sc_conv, w_a, w_b)


def _mlp_kernel(final, x_ref, g_ref, wu_ref, wd_ref, fg_ref, o_ref, hn, acc):
    j = pl.program_id(1)

    @pl.when(j == 0)
    def _():
        hn[...] = _rms(x_ref[...], g_ref[...]).astype(BF16)
        acc[...] = jnp.zeros(acc.shape, F32)

    u = jnp.dot(hn[...], wu_ref[...], preferred_element_type=F32)
    u = jnp.square(jnp.maximum(u, 0.0))
    acc[...] += jnp.dot(u.astype(BF16), wd_ref[...], preferred_element_type=F32)

    @pl.when(j == pl.num_programs(1) - 1)
    def _():
        y = x_ref[...] + acc[...]
        if final:
            y = _rms(y, fg_ref[...])
        o_ref[...] = y


def _mlp(x, gain, w_up, w_down, final_gain, final, tm=512, tf=1024):
    m, d = x.shape
    ff = w_up.shape[1]
    return pl.pallas_call(
        functools.partial(_mlp_kernel, final),
        grid=(m // tm, ff // tf),
        in_specs=[pl.BlockSpec((tm, d), lambda i, j: (i, 0)),
                  pl.BlockSpec((1, d), lambda i, j: (0, 0)),
                  pl.BlockSpec((d, tf), lambda i, j: (0, j)),
                  pl.BlockSpec((tf, d), lambda i, j: (j, 0)),
                  pl.BlockSpec((1, d), lambda i, j: (0, 0))],
        out_specs=pl.BlockSpec((tm, d), lambda i, j: (i, 0)),
        out_shape=jax.ShapeDtypeStruct((m, d), F32),
        scratch_shapes=[pltpu.VMEM((tm, d), BF16), pltpu.VMEM((tm, d), F32)],
        compiler_params=_params("parallel", "arbitrary"),
        name="mlp",
    )(x, gain.reshape(1, d), w_up, w_down, final_gain.reshape(1, d))


def _rope(x, cos_t, sin_lo, sin_hi):
    half = ROPE_DIM // 2
    outs = []
    for j in range(x.shape[1] // LANES):
        xs = x[:, j * LANES:(j + 1) * LANES]
        up = pltpu.roll(xs, LANES - half, axis=1)
        down = pltpu.roll(xs, half, axis=1)
        outs.append(xs * cos_t + up * sin_lo + down * sin_hi)
    return jnp.concatenate(outs, axis=-1)


def _odd_proj_kernel(x_ref, g_ref, w_ref, cos_ref, slo_ref, shi_ref,
                     qn_ref, qr_ref, kvc_ref, ks_ref, vs_ref, kw_ref, vw_ref, gl_ref):
    h = _rms(x_ref[...], g_ref[...])
    y = jnp.dot(h.astype(BF16), w_ref[...], preferred_element_type=F32)
    cos_t, sin_lo, sin_hi = cos_ref[...], slo_ref[...], shi_ref[...]
    scale = NSA_HEAD_DIM ** -0.5
    dq = NSA_HEADS * NSA_HEAD_DIM
    kv = NSA_KV_WIDTH
    q = y[:, :dq] * scale
    qn_ref[...] = q.astype(BF16)
    qr_ref[...] = _rope(q, cos_t, sin_lo, sin_hi).astype(BF16)
    kvc_ref[...] = y[:, dq:dq + 2 * kv].astype(BF16)
    ks_ref[...] = _rope(y[:, dq + 2 * kv:dq + 3 * kv], cos_t, sin_lo, sin_hi).astype(BF16)
    vs_ref[...] = y[:, dq + 3 * kv:dq + 4 * kv].astype(BF16)
    kw_ref[...] = _rope(y[:, dq + 4 * kv:dq + 5 * kv], cos_t, sin_lo, sin_hi).astype(BF16)
    vw_ref[...] = y[:, dq + 5 * kv:dq + 6 * kv].astype(BF16)
    gl_ref[...] = y[:, dq + 6 * kv:]


def _odd_proj(x, gain, w, tables, t_len, tm=512):
    m, d = x.shape
    n = w.shape[1]
    dq = NSA_HEADS * NSA_HEAD_DIM
    kv = NSA_KV_WIDTH
    ng = n - dq - 6 * kv
    tpb = t_len // tm
    row = lambda width: pl.BlockSpec((tm, width), lambda i: (i, 0))
    tab = pl.BlockSpec((tm, LANES), lambda i: (i % tpb, 0))
    shapes = [(dq, BF16), (dq, BF16), (2 * kv, BF16), (kv, BF16), (kv, BF16), (kv, BF16), (kv, BF16),
              (ng, F32)]
    return pl.pallas_call(
        _odd_proj_kernel,
        grid=(m // tm,),
        in_specs=[row(d), pl.BlockSpec((1, d), lambda i: (0, 0)),
                  pl.BlockSpec((d, n), lambda i: (0, 0)), tab, tab, tab],
        out_specs=[row(width) for width, _ in shapes],
        out_shape=[jax.ShapeDtypeStruct((m, width), dt) for width, dt in shapes],
        compiler_params=_params("parallel"),
        name="odd_proj",
    )(x, gain.reshape(1, d), w, *tables)


def _compress_kernel(xk_ref, xv_ref, pos_ref, k1_ref, k2t_ref, v1_ref, v2_ref, kct_ref, vc_ref):
    half = xk_ref.shape[2]
    n = xk_ref.shape[1]

    def hidden(x_ref, w1_ref):
        x = x_ref[0]
        top = jnp.dot(x, w1_ref[0:half, :], preferred_element_type=F32)
        bot = jnp.dot(x, w1_ref[half:2 * half, :], preferred_element_type=F32)
        bias = (jnp.dot(pos_ref[:, 0:half], w1_ref[0:half, :], preferred_element_type=F32)
                + jnp.dot(pos_ref[:, half:2 * half], w1_ref[half:2 * half, :], preferred_element_type=F32))
        nxt = pltpu.roll(bot, n - 1, axis=0)
        return _silu(top + nxt + bias[0:1, :]).astype(BF16)

    kct_ref[0] = lax.dot_general(k2t_ref[...], hidden(xk_ref, k1_ref), (((1,), (1,)), ((), ())),
                                 preferred_element_type=F32).astype(BF16)
    vc_ref[0] = jnp.dot(hidden(xv_ref, v1_ref), v2_ref[...], preferred_element_type=F32).astype(BF16)


def _compress(xk, xv, pos, k1, k2t, v1, v2):
    bg, n, half = xk.shape
    hid = k1.shape[1]
    dh = NSA_HEAD_DIM
    full = lambda a: pl.BlockSpec(a.shape, lambda i: (0,) * a.ndim)
    return pl.pallas_call(
        _compress_kernel,
        grid=(bg,),
        in_specs=[pl.BlockSpec((1, n, half), lambda i: (i, 0, 0)),
                  pl.BlockSpec((1, n, half), lambda i: (i, 0, 0)),
                  full(pos), full(k1), full(k2t), full(v1), full(v2)],
        out_specs=[pl.BlockSpec((1, dh, n), lambda i: (i, 0, 0)),
                   pl.BlockSpec((1, n, dh), lambda i: (i, 0, 0))],
        out_shape=[jax.ShapeDtypeStruct((bg, dh, n), BF16), jax.ShapeDtypeStruct((bg, n, dh), BF16)],
        compiler_params=_params("parallel"),
        name="compress",
    )(xk, xv, pos, k1, k2t, v1, v2)


def _softmax_rows(s):
    m = jnp.max(s, axis=-1, keepdims=True)
    m = jnp.where(m == -jnp.inf, 0.0, m)
    e = jnp.exp(s - m)
    den = jnp.sum(e, axis=-1, keepdims=True)
    return e / jnp.where(den > 0.0, den, 1.0)


def _nsa_kernel(qn_ref, qr_ref, gl_ref, kct_ref, vc_ref, kst_ref, vs_ref, kwt_ref, vw_ref,
                ovl_ref, exp_ref, o_ref):
    qb = Q_BLOCK
    dh = NSA_HEAD_DIM
    hpg = NSA_HPG
    n_cmp = kct_ref.shape[2]
    n_blk = ovl_ref.shape[1]
    q0 = pl.program_id(2) * qb
    t_col = q0 + lax.broadcasted_iota(jnp.int32, (qb, 1), 0)

    qn = jnp.concatenate([qn_ref[:, h * dh:(h + 1) * dh] for h in range(hpg)], axis=0)
    qr = jnp.concatenate([qr_ref[:, h * dh:(h + 1) * dh] for h in range(hpg)], axis=0)

    def masked(s, mask):
        return jnp.concatenate([jnp.where(mask, s[h * qb:(h + 1) * qb], -jnp.inf) for h in range(hpg)],
                               axis=0)

    cmp_end = lax.broadcasted_iota(jnp.int32, (1, n_cmp), 1) * CMP_STRIDE + (CMP_BLOCK - 1)
    s_c = jnp.dot(qn, kct_ref[0], preferred_element_type=F32)
    p_c = _softmax_rows(masked(s_c, cmp_end <= t_col))
    o_c = jnp.dot(p_c.astype(BF16), vc_ref[0], preferred_element_type=F32)
    p_sum = p_c[0:qb]
    for h in range(1, hpg):
        p_sum = p_sum + p_c[h * qb:(h + 1) * qb]
    p_hi = p_sum.astype(BF16)
    p_lo = (p_sum - p_hi.astype(F32)).astype(BF16)
    imp = (jnp.dot(p_hi, ovl_ref[...], preferred_element_type=F32)
           + jnp.dot(p_lo, ovl_ref[...], preferred_element_type=F32))

    imp_t = imp.T
    js = lax.broadcasted_iota(jnp.int32, (n_blk, qb), 0).astype(F32)
    cur = ((q0 + lax.broadcasted_iota(jnp.int32, (1, qb), 1)) // SEL_BLOCK).astype(F32)
    forced = (js == 0.0) | (js == cur) | (js == cur - 1.0)
    val = jnp.where(js > cur, -jnp.inf, jnp.where(forced, jnp.inf, imp_t))
    sel_t = jnp.zeros((n_blk, qb), F32)
    for _ in range(N_SELECT):
        best = jnp.max(val, axis=0, keepdims=True)
        first = jnp.min(jnp.where(val == best, js, float(n_blk)), axis=0, keepdims=True)
        pick = js == first
        sel_t = jnp.where(pick, 1.0, sel_t)
        val = jnp.where(pick, -jnp.inf, val)
    sel_t = jnp.where(js <= cur, sel_t, 0.0)
    sel = sel_t.T.astype(BF16)

    kt = SEL_KEY_TILE

    def sel_step(i, carry):
        m_run, l_run, acc = carry
        start = pl.multiple_of(i * kt, kt)
        member = jnp.dot(sel, exp_ref[:, pl.ds(start, kt)], preferred_element_type=F32)
        kpos = start + lax.broadcasted_iota(jnp.int32, (1, kt), 1)
        valid = (member > 0.5) & (kpos <= t_col)
        s = masked(jnp.dot(qr, kst_ref[0, :, pl.ds(start, kt)], preferred_element_type=F32), valid)
        m_new = jnp.maximum(m_run, jnp.max(s, axis=-1, keepdims=True))
        m_safe = jnp.where(m_new == -jnp.inf, 0.0, m_new)
        alpha = jnp.exp(m_run - m_safe)
        p = jnp.exp(s - m_safe)
        l_new = alpha * l_run + jnp.sum(p, axis=-1, keepdims=True)
        acc = alpha * acc + jnp.dot(p.astype(BF16), vs_ref[0, pl.ds(start, kt), :],
                                    preferred_element_type=F32)
        return m_new, l_new, acc

    rows = hpg * qb
    init = (jnp.full((rows, 1), -jnp.inf, F32), jnp.zeros((rows, 1), F32), jnp.zeros((rows, dh), F32))
    _, l_s, acc_s = lax.fori_loop(0, q0 // kt + 1, sel_step, init)
    o_s = acc_s / jnp.where(l_s > 0.0, l_s, 1.0)

    span = WINDOW + qb
    w_start = pl.multiple_of(jnp.maximum(q0 - WINDOW, 0), qb)
    dist = t_col - (w_start + lax.broadcasted_iota(jnp.int32, (1, span), 1))
    s_w = jnp.dot(qr, kwt_ref[0, :, pl.ds(w_start, span)], preferred_element_type=F32)
    p_w = _softmax_rows(masked(s_w, (dist >= 0) & (dist < WINDOW)))
    o_w = jnp.dot(p_w.astype(BF16), vw_ref[0, pl.ds(w_start, span), :], preferred_element_type=F32)

    gates = _sigmoid(gl_ref[...])
    for h in range(hpg):
        r = slice(h * qb, (h + 1) * qb)
        o = (gates[:, 3 * h:3 * h + 1] * o_c[r] + gates[:, 3 * h + 1:3 * h + 2] * o_s[r]
             + gates[:, 3 * h + 2:3 * h + 3] * o_w[r])
        o_ref[:, h * dh:(h + 1) * dh] = o.astype(o_ref.dtype)


def _nsa_attention(qn, qr, gl, kct, vc, kst, vs, kwt, vw, overlap, expand, batch, t_len):
    m = qn.shape[0]
    n_g = NSA_KV_GROUPS
    gw = NSA_GROUP_WIDTH
    dh = NSA_HEAD_DIM
    nq = t_len // Q_BLOCK
    n_cmp = kct.shape[2]
    qspec = pl.BlockSpec((Q_BLOCK, gw), lambda b, g, i: (b * nq + i, g))
    kv_t = lambda n: pl.BlockSpec((1, dh, n), lambda b, g, i: (b * n_g + g, 0, 0))
    kv_n = lambda n: pl.BlockSpec((1, n, dh), lambda b, g, i: (b * n_g + g, 0, 0))
    const = lambda a: pl.BlockSpec(a.shape, lambda b, g, i: (0, 0))
    return pl.pallas_call(
        _nsa_kernel,
        grid=(batch, n_g, nq),
        in_specs=[qspec, qspec,
                  pl.BlockSpec((Q_BLOCK, LANES), lambda b, g, i: (b * nq + i, g)),
                  kv_t(n_cmp), kv_n(n_cmp), kv_t(t_len), kv_n(t_len), kv_t(t_len), kv_n(t_len),
                  const(overlap), const(expand)],
        out_specs=qspec,
        out_shape=jax.ShapeDtypeStruct((m, n_g * gw), BF16),
        compiler_params=_params("parallel", "parallel", "arbitrary"),
        name="nsa_attention",
    )(qn, qr, gl, kct, vc, kst, vs, kwt, vw, overlap, expand)


def _proj_residual_kernel(x_ref, a_ref, w_ref, o_ref):
    o_ref[...] = x_ref[...] + jnp.dot(a_ref[...], w_ref[...], preferred_element_type=F32)


def _proj_residual(x, a, w, tm=512):
    m, d = x.shape
    k = a.shape[1]
    return pl.pallas_call(
        _proj_residual_kernel,
        grid=(m // tm,),
        in_specs=[pl.BlockSpec((tm, d), lambda i: (i, 0)),
                  pl.BlockSpec((tm, k), lambda i: (i, 0)),
                  pl.BlockSpec((k, d), lambda i: (0, 0))],
        out_specs=pl.BlockSpec((tm, d), lambda i: (i, 0)),
        out_shape=jax.ShapeDtypeStruct((m, d), F32),
        compiler_params=_params("parallel"),
        name="proj_residual",
    )(x, a, w)


def _pad_cols(w, n):
    return jnp.pad(w, ((0, 0), (0, n - w.shape[1])))


def _even_mixer(x, gain, w_in, qkv_conv, a_log, dt_bias, o_norm, sc_conv, w_out, batch, t_len):
    gw = GDN_WIDTH
    main = jnp.concatenate([w_in[:, :4 * gw], w_in[:, 4 * gw + 2 * GDN_HEADS:]], axis=1)
    ab = _pad_cols(w_in[:, 4 * gw:4 * gw + 2 * GDN_HEADS], LANES)
    w_all = jnp.concatenate([main, ab], axis=1).astype(BF16)
    proj = _norm_matmul(x, gain, w_all)
    gate_params = jnp.zeros((SUBLANES, LANES), F32)
    gate_params = gate_params.at[0, :GDN_HEADS].set(a_log).at[1, :GDN_HEADS].set(dt_bias)
    ya = _gdn(proj, qkv_conv, gate_params, o_norm, batch, t_len)
    return _even_out(x, ya, proj, sc_conv, w_out[:gw].astype(BF16), w_out[gw:].astype(BF16), t_len)


def _rope_tables(t_len):
    half = ROPE_DIM // 2
    inv_freq = ROPE_THETA ** (-jnp.arange(0, ROPE_DIM, 2, dtype=F32) / ROPE_DIM)
    ang = jnp.arange(t_len, dtype=F32)[:, None] * inv_freq[None, :]
    cos, sin = jnp.cos(ang), jnp.sin(ang)
    pad = NSA_HEAD_DIM - ROPE_DIM
    head = lambda a, b, fill: jnp.concatenate([a, b, jnp.full((t_len, pad), fill, F32)], axis=1)
    zeros = jnp.zeros((t_len, half), F32)
    reps = LANES // NSA_HEAD_DIM
    return (jnp.tile(head(cos, cos, 1.0), (1, reps)),
            jnp.tile(head(-sin, zeros, 0.0), (1, reps)),
            jnp.tile(head(zeros, sin, 0.0), (1, reps)))


def _odd_mixer(x, gain, w_in, cmp_pos, k_w1, k_w2, v_w1, v_w2, w_out, batch, t_len):
    n_g, dh, hpg = NSA_KV_GROUPS, NSA_HEAD_DIM, NSA_HPG
    dq = NSA_HEADS * dh
    kv = NSA_KV_WIDTH
    gate_w = w_in[:, dq + 6 * kv:].reshape(-1, n_g, 3 * hpg)
    gate_w = jnp.pad(gate_w, ((0, 0), (0, 0), (0, LANES - 3 * hpg))).reshape(-1, n_g * LANES)
    w_all = jnp.concatenate([w_in[:, :dq + 6 * kv], gate_w], axis=1).astype(BF16)
    qn, qr, kvc, ks, vs, kw, vw, gl = _odd_proj(x, gain, w_all, _rope_tables(t_len), t_len)

    n_chunk = t_len // CMP_STRIDE

    def chunked(a):
        a = a.reshape(batch, n_chunk, CMP_STRIDE, n_g, dh)
        return jnp.moveaxis(a, 3, 1).reshape(batch * n_g, n_chunk, CMP_STRIDE * dh)

    pos = jnp.zeros((SUBLANES, CMP_BLOCK * dh), F32).at[0].set(cmp_pos.reshape(-1)).astype(BF16)
    kct, vc = _compress(chunked(kvc[:, :kv]), chunked(kvc[:, kv:]), pos, k_w1.astype(BF16),
                        k_w2.T.astype(BF16), v_w1.astype(BF16), v_w2.astype(BF16))

    def keys_t(a):
        return jnp.transpose(a.reshape(batch, t_len, n_g, dh), (0, 2, 3, 1)).reshape(batch * n_g, dh, t_len)

    def values(a):
        return jnp.transpose(a.reshape(batch, t_len, n_g, dh), (0, 2, 1, 3)).reshape(batch * n_g, t_len, dh)

    n_blk = _round_up(t_len // SEL_BLOCK, LANES)
    c_start = np.arange(n_chunk)[:, None] * CMP_STRIDE
    s_start = np.arange(n_blk)[None, :] * SEL_BLOCK
    overlap = jnp.asarray((c_start < s_start + SEL_BLOCK) & (c_start + CMP_BLOCK > s_start), BF16)
    expand = jnp.asarray(np.arange(n_blk)[:, None] == (np.arange(t_len)[None, :] // SEL_BLOCK), BF16)
    o = _nsa_attention(qn, qr, gl, kct, vc, keys_t(ks), values(vs), keys_t(kw), values(vw),
                       overlap, expand, batch, t_len)
    return _proj_residual(x, o, w_out.astype(BF16))


def kernel(x, mix_norm, mlp_norm, w_up, w_down, final_norm, ev_w_in, ev_qkv_conv, ev_a_log, ev_dt_bias,
           ev_o_norm, ev_sc_conv, ev_w_out, od_w_in, od_cmp_pos, od_cmp_k_w1, od_cmp_k_w2, od_cmp_v_w1,
           od_cmp_v_w2, od_w_out):
    batch, t_len, d = x.shape
    depth = mix_norm.shape[0]
    xs = x.reshape(batch * t_len, d)
    for layer in range(depth):
        i = layer // 2
        if layer % 2 == 0:
            xs = _even_mixer(xs, mix_norm[layer], ev_w_in[i], ev_qkv_conv[i], ev_a_log[i], ev_dt_bias[i],
                             ev_o_norm[i], ev_sc_conv[i], ev_w_out[i], batch, t_len)
        else:
            xs = _odd_mixer(xs, mix_norm[layer], od_w_in[i], od_cmp_pos[i], od_cmp_k_w1[i], od_cmp_k_w2[i],
                            od_cmp_v_w1[i], od_cmp_v_w2[i], od_w_out[i], batch, t_len)
        xs = _mlp(xs, mlp_norm[layer], w_up[layer].astype(BF16), w_down[layer].astype(BF16), final_norm,
                  final=(layer == depth - 1))
    return xs.reshape(batch, t_len, d)
```

```python
import functools

import jax
import jax.numpy as jnp
import numpy as np
from jax import lax
from jax.experimental import pallas as pl
from jax.experimental.pallas import tpu as pltpu

F32 = jnp.float32
BF16 = jnp.bfloat16
HIGHEST = lax.Precision.HIGHEST

NORM_EPS = 1e-6
GDN_HEADS = 8
GDN_HEAD_DIM = 64
GDN_WIDTH = GDN_HEADS * GDN_HEAD_DIM
GDN_CONV = 4
GDN_CHUNK = 64
SOLVE_BLOCK = 16
SC_CONV = 3
NSA_HEADS = 16
NSA_HEAD_DIM = 64
NSA_KV_GROUPS = 4
NSA_HPG = NSA_HEADS // NSA_KV_GROUPS
NSA_GROUP_WIDTH = NSA_HPG * NSA_HEAD_DIM
NSA_KV_WIDTH = NSA_KV_GROUPS * NSA_HEAD_DIM
CMP_BLOCK = 32
CMP_STRIDE = 16
SEL_BLOCK = 64
N_SELECT = 16
WINDOW = 512
Q_BLOCK = 128
SEL_KEY_TILE = 512
ROPE_THETA = 500000.0
ROPE_DIM = NSA_HEAD_DIM // 4
LOG2_E = 1.4426950408889634
LANES = 128
SUBLANES = 8
VMEM_LIMIT = 56 * 1024 * 1024


def _round_up(n, m):
    return (n + m - 1) // m * m


def _params(*semantics):
    return pltpu.CompilerParams(dimension_semantics=semantics, vmem_limit_bytes=VMEM_LIMIT)


def _rms(x, gain):
    return x * lax.rsqrt(jnp.mean(x * x, axis=-1, keepdims=True) + NORM_EPS) * gain


def _sigmoid(x):
    return 1.0 / (1.0 + jnp.exp(-x))


def _silu(x):
    return x * _sigmoid(x)


def _softplus(x):
    return jnp.maximum(x, 0.0) + jnp.log(1.0 + jnp.exp(-jnp.abs(x)))


def _dot(a, b):
    return jnp.dot(a.astype(BF16), b.astype(BF16), preferred_element_type=F32)


def _dot_nt(a, b):
    return lax.dot_general(a.astype(BF16), b.astype(BF16), (((1,), (1,)), ((), ())),
                           preferred_element_type=F32)


def _dot_f32(a, b):
    return jnp.dot(a, b, precision=HIGHEST, preferred_element_type=F32)


def _norm_matmul_kernel(x_ref, g_ref, w_ref, o_ref):
    h = _rms(x_ref[...], g_ref[...])
    o_ref[...] = jnp.dot(h.astype(BF16), w_ref[...], preferred_element_type=F32).astype(o_ref.dtype)


def _norm_matmul(x, gain, w, tm=512):
    m, d = x.shape
    n = w.shape[1]
    return pl.pallas_call(
        _norm_matmul_kernel,
        grid=(m // tm,),
        in_specs=[pl.BlockSpec((tm, d), lambda i: (i, 0)),
                  pl.BlockSpec((1, d), lambda i: (0, 0)),
                  pl.BlockSpec((d, n), lambda i: (0, 0))],
        out_specs=pl.BlockSpec((tm, n), lambda i: (i, 0)),
        out_shape=jax.ShapeDtypeStruct((m, n), F32),
        compiler_params=_params("parallel"),
        name="norm_proj",
    )(x, gain.reshape(1, d), w)


def _split(a):
    hi = a.astype(BF16)
    return hi, (a - hi.astype(F32)).astype(BF16)


def _dot3(a, b):
    (ah, al), (bh, bl) = a, b
    return (jnp.dot(ah, bh, preferred_element_type=F32) + jnp.dot(ah, bl, preferred_element_type=F32)
            + jnp.dot(al, bh, preferred_element_type=F32))


def _gdn_kernel(qkv_ref, z_ref, ab_ref, cw_ref, gp_ref, on_ref, o_ref, xbuf, act, state):
    c = GDN_CHUNK
    dh = GDN_HEAD_DIM
    pw = 2 * dh
    n_pairs = GDN_HEADS // 2
    n_batch = qkv_ref.shape[0]
    hist = SUBLANES
    chains = [(b, p) for b in range(n_batch) for p in range(n_pairs)]

    @pl.when(pl.program_id(0) == 0)
    def _():
        xbuf[:, 0:hist, :] = jnp.zeros((n_batch, hist, 3 * GDN_WIDTH), F32)
        state[...] = jnp.zeros(state.shape, F32)

    row = lax.broadcasted_iota(jnp.int32, (c, pw), 0)
    lane = lax.broadcasted_iota(jnp.int32, (c, pw), 1)
    first = lane < dh
    col = jnp.where(first, lane, lane - dh)
    incl = row >= col
    strict = row > col
    eye = (row == col).astype(F32)
    diag_blk = strict & ((row // SOLVE_BLOCK) == (col // SOLVE_BLOCK))
    r2 = lax.broadcasted_iota(jnp.int32, (pw, pw), 0)
    c2 = lax.broadcasted_iota(jnp.int32, (pw, pw), 1)
    same_head = (r2 < dh) == (c2 < dh)
    ones_bd = same_head.astype(BF16)
    eye_pw = (r2 == c2).astype(BF16)
    tri = (lax.broadcasted_iota(jnp.int32, (c, c), 0) >= lax.broadcasted_iota(jnp.int32, (c, c), 1)).astype(F32)

    def blockdiag(x):
        return jnp.concatenate([jnp.where(first, x, 0.0), jnp.where(first, 0.0, x)], axis=0)

    def head_sums(x):
        hi, lo = _split(x)
        return jnp.dot(hi, ones_bd, preferred_element_type=F32) + jnp.dot(lo, ones_bd, preferred_element_type=F32)

    def pdot(xs, y):
        return _dot3(xs, _split(blockdiag(y)))

    gcs, betas = [], []
    for b in range(n_batch):
        x = qkv_ref[b]
        xbuf[b, hist:hist + c, :] = x
        y = x * cw_ref[GDN_CONV - 1:GDN_CONV, :]
        for j in range(GDN_CONV - 1):
            shift = GDN_CONV - 1 - j
            y = y + xbuf[b, hist - shift:hist - shift + c, :] * cw_ref[j:j + 1, :]
        xbuf[b, 0:hist, :] = x[c - hist:c, :]
        act[b] = _silu(y)
        ab = ab_ref[b]
        g_all = -jnp.exp(gp_ref[0:1, :]) * _softplus(ab + gp_ref[1:2, :])
        gcs.append(_dot_f32(tri, g_all))
        betas.append(_sigmoid(ab))

    def pair_cols(a, p, offset):
        return jnp.where(first, a[:, offset + 2 * p:offset + 2 * p + 1], a[:, offset + 2 * p + 1:offset + 2 * p + 2])

    scale = dh ** -0.5
    q = [act[b, :, p * pw:(p + 1) * pw] for b, p in chains]
    k = [act[b, :, GDN_WIDTH + p * pw:GDN_WIDTH + (p + 1) * pw] for b, p in chains]
    v = [act[b, :, 2 * GDN_WIDTH + p * pw:2 * GDN_WIDTH + (p + 1) * pw] for b, p in chains]
    q = [x * lax.rsqrt(head_sums(x * x) + NORM_EPS) * scale for x in q]
    k = [x * lax.rsqrt(head_sums(x * x) + NORM_EPS) for x in k]
    gcol = [pair_cols(gcs[b], p, 0) for b, p in chains]
    bcol = [pair_cols(betas[b], p, GDN_HEADS) for b, p in chains]
    grow = [jnp.sum(g * eye, axis=0, keepdims=True) for g in gcol]
    g_last = [g[c - 1:c, :] for g in gcol]
    decay = [jnp.exp(jnp.where(incl, gc_ - gr_, -jnp.inf)) for gc_, gr_ in zip(gcol, grow)]
    kb = [k_ * b_ for k_, b_ in zip(k, bcol)]
    k_bd = [blockdiag(k_).astype(BF16) for k_ in k]
    m_low = [jnp.where(strict, _dot_nt(kb_, kd_) * d_, 0.0) for kb_, kd_, d_ in zip(kb, k_bd, decay)]
    attn = [_dot_nt(q_, kd_) * d_ for q_, kd_, d_ in zip(q, k_bd, decay)]

    d_pow = [jnp.where(diag_blk, m_, 0.0) for m_ in m_low]
    l_mat = [m_ - d_ for m_, d_ in zip(m_low, d_pow)]
    p_inv = [eye - d_ for d_ in d_pow]
    for _ in range(int(np.log2(SOLVE_BLOCK)) - 1):
        d_pow = [pdot(_split(d_), d_) for d_ in d_pow]
        p_inv = [p_ + pdot(_split(p_), d_) for p_, d_ in zip(p_inv, d_pow)]
    n_pow = [pdot(_split(p_), l_) for p_, l_ in zip(p_inv, l_mat)]
    q_inv = [eye - n_ for n_ in n_pow]
    for _ in range(int(np.log2(c // SOLVE_BLOCK)) - 1):
        n_pow = [pdot(_split(n_), n_) for n_ in n_pow]
        q_inv = [q_ + pdot(_split(q_), n_) for q_, n_ in zip(q_inv, n_pow)]
    a_inv = [_split(pdot(_split(q_), p_)) for q_, p_ in zip(q_inv, p_inv)]
    u = [pdot(a_, v_ * b_) for a_, v_, b_ in zip(a_inv, v, bcol)]
    w = [pdot(a_, kb_ * jnp.exp(g_)) for a_, kb_, g_ in zip(a_inv, kb, gcol)]

    s_old = [state[b * n_pairs + p] for b, p in chains]
    v_new = [u_ - _dot(w_, s_) for u_, w_, s_ in zip(u, w, s_old)]
    o = [_dot(q_ * jnp.exp(g_), s_) + _dot(a_, blockdiag(vn_))
         for q_, g_, s_, a_, vn_ in zip(q, gcol, s_old, attn, v_new)]
    k_dec_t = [_dot_nt(eye_pw, k_ * jnp.exp(gl_ - g_)) for k_, gl_, g_ in zip(k, g_last, gcol)]
    for (b, p), s_, gl_, kt_, vn_ in zip(chains, s_old, g_last, k_dec_t, v_new):
        state[b * n_pairs + p] = s_ * jnp.exp(gl_) + jnp.where(same_head, _dot(kt_, vn_), 0.0)

    inv_dh = 1.0 / dh
    for (b, p), o_ in zip(chains, o):
        zp = z_ref[b, :, p * pw:(p + 1) * pw]
        y = o_ * lax.rsqrt(head_sums(o_ * o_) * inv_dh + NORM_EPS) * on_ref[...]
        o_ref[b, :, p * pw:(p + 1) * pw] = (y * _silu(zp)).astype(o_ref.dtype)


def _gdn(proj, conv_w, gate_params, o_norm, batch, t_len):
    c = GDN_CHUNK
    n_chunks = t_len // c
    w3 = 3 * GDN_WIDTH
    ab_col = (4 * GDN_WIDTH + 3 * GDN_WIDTH) // LANES
    proj = proj.reshape(batch, t_len, proj.shape[-1])
    on_pair = jnp.tile(o_norm.reshape(1, GDN_HEAD_DIM), (1, 2))
    out = pl.pallas_call(
        _gdn_kernel,
        grid=(n_chunks,),
        in_specs=[pl.BlockSpec((batch, c, w3), lambda i: (0, i, 0)),
                  pl.BlockSpec((batch, c, GDN_WIDTH), lambda i: (0, i, 3)),
                  pl.BlockSpec((batch, c, LANES), lambda i: (0, i, ab_col)),
                  pl.BlockSpec((GDN_CONV, w3), lambda i: (0, 0)),
                  pl.BlockSpec((SUBLANES, LANES), lambda i: (0, 0)),
                  pl.BlockSpec((1, 2 * GDN_HEAD_DIM), lambda i: (0, 0))],
        out_specs=pl.BlockSpec((batch, c, GDN_WIDTH), lambda i: (0, i, 0)),
        out_shape=jax.ShapeDtypeStruct((batch, t_len, GDN_WIDTH), BF16),
        scratch_shapes=[pltpu.VMEM((batch, c + SUBLANES, w3), F32),
                        pltpu.VMEM((batch, c, w3), F32),
                        pltpu.VMEM((batch * GDN_HEADS // 2, 2 * GDN_HEAD_DIM, 2 * GDN_HEAD_DIM), F32)],
        compiler_params=_params("arbitrary"),
        name="gdn",
    )(proj, proj, proj, conv_w, gate_params, on_pair)
    return out.reshape(batch * t_len, GDN_WIDTH)


def _even_out_kernel(tiles_per_batch, x_ref, ya_ref, bg_ref, cg_ref, hs_ref, cgp_ref, hsp_ref,
                     cw_ref, wa_ref, wb_ref, o_ref, ubuf):
    tm = x_ref.shape[0]
    hist = SUBLANES
    first = (pl.program_id(0) % tiles_per_batch) == 0
    prev = cgp_ref[...] * hsp_ref[...]
    ubuf[0:hist, :] = jnp.where(first, 0.0, prev)
    u = cg_ref[...] * hs_ref[...]
    ubuf[hist:hist + tm, :] = u
    conv = u * cw_ref[SC_CONV - 1:SC_CONV, :]
    for j in range(SC_CONV - 1):
        shift = SC_CONV - 1 - j
        conv = conv + ubuf[hist - shift:hist - shift + tm, :] * cw_ref[j:j + 1, :]
    yb = bg_ref[...] * conv
    acc = jnp.dot(ya_ref[...], wa_ref[...], preferred_element_type=F32)
    acc = acc + jnp.dot(yb.astype(BF16), wb_ref[...], preferred_element_type=F32)
    o_ref[...] = x_ref[...] + acc


def _even_out(x, ya, proj, sc_conv, w_a, w_b, t_len, tm=512):
    m, d = x.shape
    wd = GDN_WIDTH
    tiles_per_batch = t_len // tm
    hb = tm // SUBLANES
    prev = lambda col: pl.BlockSpec((SUBLANES, wd), lambda i: (jnp.maximum(i * hb - 1, 0), col))
    cur = lambda col: pl.BlockSpec((tm, wd), lambda i: (i, col))
    return pl.pallas_call(
        functools.partial(_even_out_kernel, tiles_per_batch),
        grid=(m // tm,),
        in_specs=[pl.BlockSpec((tm, d), lambda i: (i, 0)),
                  pl.BlockSpec((tm, wd), lambda i: (i, 0)),
                  cur(4), cur(5), cur(6), prev(5), prev(6),
                  pl.BlockSpec((SC_CONV, wd), lambda i: (0, 0)),
                  pl.BlockSpec((wd, d), lambda i: (0, 0)),
                  pl.BlockSpec((wd, d), lambda i: (0, 0))],
        out_specs=pl.BlockSpec((tm, d), lambda i: (i, 0)),
        out_shape=jax.ShapeDtypeStruct((m, d), F32),
        scratch_shapes=[pltpu.VMEM((tm + SUBLANES, wd), F32)],
        compiler_params=_params("parallel"),
        name="even_out",
    )(x, ya, proj, proj, proj, proj, proj, sc_conv, w_a, w_b)


def _mlp_kernel(final, x_ref, g_ref, wu_ref, wd_ref, fg_ref, o_ref, hn, acc):
    j = pl.program_id(1)

    @pl.when(j == 0)
    def _():
        hn[...] = _rms(x_ref[...], g_ref[...]).astype(BF16)
        acc[...] = jnp.zeros(acc.shape, F32)

    u = jnp.dot(hn[...], wu_ref[...], preferred_element_type=F32)
    u = jnp.square(jnp.maximum(u, 0.0))
    acc[...] += jnp.dot(u.astype(BF16), wd_ref[...], preferred_element_type=F32)

    @pl.when(j == pl.num_programs(1) - 1)
    def _():
        y = x_ref[...] + acc[...]
        if final:
            y = _rms(y, fg_ref[...])
        o_ref[...] = y


def _mlp(x, gain, w_up, w_down, final_gain, final, tm=512, tf=1024):
    m, d = x.shape
    ff = w_up.shape[1]
    return pl.pallas_call(
        functools.partial(_mlp_kernel, final),
        grid=(m // tm, ff // tf),
        in_specs=[pl.BlockSpec((tm, d), lambda i, j: (i, 0)),
                  pl.BlockSpec((1, d), lambda i, j: (0, 0)),
                  pl.BlockSpec((d, tf), lambda i, j: (0, j)),
                  pl.BlockSpec((tf, d), lambda i, j: (j, 0)),
                  pl.BlockSpec((1, d), lambda i, j: (0, 0))],
        out_specs=pl.BlockSpec((tm, d), lambda i, j: (i, 0)),
        out_shape=jax.ShapeDtypeStruct((m, d), F32),
        scratch_shapes=[pltpu.VMEM((tm, d), BF16), pltpu.VMEM((tm, d), F32)],
        compiler_params=_params("parallel", "arbitrary"),
        name="mlp",
    )(x, gain.reshape(1, d), w_up, w_down, final_gain.reshape(1, d))


def _rope(x, cos_t, sin_lo, sin_hi):
    half = ROPE_DIM // 2
    outs = []
    for j in range(x.shape[1] // LANES):
        xs = x[:, j * LANES:(j + 1) * LANES]
        up = pltpu.roll(xs, LANES - half, axis=1)
        down = pltpu.roll(xs, half, axis=1)
        outs.append(xs * cos_t + up * sin_lo + down * sin_hi)
    return jnp.concatenate(outs, axis=-1)


def _odd_proj_kernel(x_ref, g_ref, w_ref, cos_ref, slo_ref, shi_ref,
                     qn_ref, qr_ref, kvc_ref, ks_ref, vs_ref, kw_ref, vw_ref, gl_ref):
    h = _rms(x_ref[...], g_ref[...])
    y = jnp.dot(h.astype(BF16), w_ref[...], preferred_element_type=F32)
    cos_t, sin_lo, sin_hi = cos_ref[...], slo_ref[...], shi_ref[...]
    dq = NSA_HEADS * NSA_HEAD_DIM
    kv = NSA_KV_WIDTH
    q = y[:, :dq] * (LOG2_E * NSA_HEAD_DIM ** -0.5)
    qn_ref[...] = q.astype(BF16)
    qr_ref[...] = _rope(q, cos_t, sin_lo, sin_hi).astype(BF16)
    kvc_ref[...] = y[:, dq:dq + 2 * kv].astype(BF16)
    ks_ref[...] = _rope(y[:, dq + 2 * kv:dq + 3 * kv], cos_t, sin_lo, sin_hi).astype(BF16)
    vs_ref[...] = y[:, dq + 3 * kv:dq + 4 * kv].astype(BF16)
    kw_ref[...] = _rope(y[:, dq + 4 * kv:dq + 5 * kv], cos_t, sin_lo, sin_hi).astype(BF16)
    vw_ref[...] = y[:, dq + 5 * kv:dq + 6 * kv].astype(BF16)
    gl_ref[...] = y[:, dq + 6 * kv:]


def _odd_proj(x, gain, w, tables, t_len, tm=512):
    m, d = x.shape
    n = w.shape[1]
    dq = NSA_HEADS * NSA_HEAD_DIM
    kv = NSA_KV_WIDTH
    ng = n - dq - 6 * kv
    tpb = t_len // tm
    row = lambda width: pl.BlockSpec((tm, width), lambda i: (i, 0))
    tab = pl.BlockSpec((tm, LANES), lambda i: (i % tpb, 0))
    shapes = [(dq, BF16), (dq, BF16), (2 * kv, BF16), (kv, BF16), (kv, BF16), (kv, BF16), (kv, BF16),
              (ng, F32)]
    return pl.pallas_call(
        _odd_proj_kernel,
        grid=(m // tm,),
        in_specs=[row(d), pl.BlockSpec((1, d), lambda i: (0, 0)),
                  pl.BlockSpec((d, n), lambda i: (0, 0)), tab, tab, tab],
        out_specs=[row(width) for width, _ in shapes],
        out_shape=[jax.ShapeDtypeStruct((m, width), dt) for width, dt in shapes],
        compiler_params=_params("parallel"),
        name="odd_proj",
    )(x, gain.reshape(1, d), w, *tables)


def _compress_kernel(xk_ref, xv_ref, pos_ref, k1_ref, k2t_ref, v1_ref, v2_ref, kct_ref, vc_ref):
    half = xk_ref.shape[2]
    n = xk_ref.shape[1]

    def hidden(x_ref, w1_ref):
        x = x_ref[0]
        top = jnp.dot(x, w1_ref[0:half, :], preferred_element_type=F32)
        bot = jnp.dot(x, w1_ref[half:2 * half, :], preferred_element_type=F32)
        bias = (jnp.dot(pos_ref[:, 0:half], w1_ref[0:half, :], preferred_element_type=F32)
                + jnp.dot(pos_ref[:, half:2 * half], w1_ref[half:2 * half, :], preferred_element_type=F32))
        nxt = pltpu.roll(bot, n - 1, axis=0)
        return _silu(top + nxt + bias[0:1, :]).astype(BF16)

    kct_ref[0] = lax.dot_general(k2t_ref[...], hidden(xk_ref, k1_ref), (((1,), (1,)), ((), ())),
                                 preferred_element_type=F32).astype(BF16)
    vc_ref[0] = jnp.dot(hidden(xv_ref, v1_ref), v2_ref[...], preferred_element_type=F32).astype(BF16)


def _compress(xk, xv, pos, k1, k2t, v1, v2):
    bg, n, half = xk.shape
    dh = NSA_HEAD_DIM
    full = lambda a: pl.BlockSpec(a.shape, lambda i: (0,) * a.ndim)
    return pl.pallas_call(
        _compress_kernel,
        grid=(bg,),
        in_specs=[pl.BlockSpec((1, n, half), lambda i: (i, 0, 0)),
                  pl.BlockSpec((1, n, half), lambda i: (i, 0, 0)),
                  full(pos), full(k1), full(k2t), full(v1), full(v2)],
        out_specs=[pl.BlockSpec((1, dh, n), lambda i: (i, 0, 0)),
                   pl.BlockSpec((1, n, dh), lambda i: (i, 0, 0))],
        out_shape=[jax.ShapeDtypeStruct((bg, dh, n), BF16), jax.ShapeDtypeStruct((bg, n, dh), BF16)],
        compiler_params=_params("parallel"),
        name="compress",
    )(xk, xv, pos, k1, k2t, v1, v2)


MASKED = -1e30


def _nsa_kernel(qn_ref, qr_ref, gl_ref, kct_ref, vc_ref, kst_ref, vs_ref, kwt_ref, vw_ref,
                ovl_ref, exp_ref, o_ref):
    qb = Q_BLOCK
    dh = NSA_HEAD_DIM
    hpg = NSA_HPG
    n_cmp = kct_ref.shape[2]
    n_blk = exp_ref.shape[0]
    q0 = pl.program_id(2) * qb
    t_col = q0 + lax.broadcasted_iota(jnp.int32, (qb, 1), 0)

    qn = jnp.concatenate([qn_ref[:, h * dh:(h + 1) * dh] for h in range(hpg)], axis=0)
    qr_heads = [qr_ref[:, h * dh:(h + 1) * dh] for h in range(hpg)]
    qr = jnp.concatenate(qr_heads, axis=0)

    def masked(s, mask, fill):
        return jnp.concatenate([jnp.where(mask, s[h * qb:(h + 1) * qb], fill) for h in range(hpg)], axis=0)

    def exp2_rows(s):
        m = jnp.max(s, axis=-1, keepdims=True)
        return jnp.exp2(s - jnp.where(m == -jnp.inf, 0.0, m))

    def normalized(acc):
        den = acc[:, dh:dh + 1]
        return acc[:, :dh] / jnp.where(den > 0.0, den, 1.0)

    cmp_end = lax.broadcasted_iota(jnp.int32, (1, n_cmp), 1) * CMP_STRIDE + (CMP_BLOCK - 1)
    s_c = jnp.dot(qn, kct_ref[0], preferred_element_type=F32)
    e_c = exp2_rows(masked(s_c, cmp_end <= t_col, -jnp.inf))
    e_hi = e_c.astype(BF16)
    e_lo = (e_c - e_hi.astype(F32)).astype(BF16)
    o_c = normalized(jnp.dot(e_hi, vc_ref[0], preferred_element_type=F32))
    imp_all = (jnp.dot(e_hi, ovl_ref[...], preferred_element_type=F32)
               + jnp.dot(e_lo, ovl_ref[...], preferred_element_type=F32))
    imp = jnp.zeros((qb, n_blk), F32)
    for h in range(hpg):
        part = imp_all[h * qb:(h + 1) * qb]
        den = part[:, n_blk:n_blk + 1]
        imp = imp + part[:, :n_blk] / jnp.where(den > 0.0, den, 1.0)

    span = WINDOW + qb
    w_start = pl.multiple_of(jnp.maximum(q0 - WINDOW, 0), qb)
    dist = t_col - (w_start + lax.broadcasted_iota(jnp.int32, (1, span), 1))
    s_w = jnp.dot(qr, kwt_ref[0, :, pl.ds(w_start, span)], preferred_element_type=F32)
    e_w = exp2_rows(masked(s_w, (dist >= 0) & (dist < WINDOW), -jnp.inf))
    o_w = normalized(jnp.dot(e_w.astype(BF16), vw_ref[0, pl.ds(w_start, span), :],
                             preferred_element_type=F32))

    imp_t = imp.T
    js = lax.broadcasted_iota(jnp.int32, (n_blk, qb), 0).astype(F32)
    cur = ((q0 + lax.broadcasted_iota(jnp.int32, (1, qb), 1)) // SEL_BLOCK).astype(F32)
    forced = (js == 0.0) | (js == cur) | (js == cur - 1.0)
    val = jnp.where(js > cur, -jnp.inf, jnp.where(forced, jnp.inf, imp_t))
    sel_t = jnp.zeros((n_blk, qb), F32)
    for _ in range(N_SELECT):
        best = jnp.max(val, axis=0, keepdims=True)
        first = jnp.min(jnp.where(val == best, js, float(n_blk)), axis=0, keepdims=True)
        pick = js == first
        sel_t = jnp.where(pick, 1.0, sel_t)
        val = jnp.where(pick, -jnp.inf, val)
    bias = jnp.where((sel_t > 0.5) & (js <= cur), 0.0, MASKED).T.astype(BF16)

    kt = SEL_KEY_TILE
    q_ext = jnp.concatenate([jnp.concatenate([bias, q_h], axis=1) for q_h in qr_heads], axis=0)

    def sel_step(i, carry, causal):
        m_run, acc = carry
        start = pl.multiple_of(i * kt, kt)
        k_ext = jnp.concatenate([exp_ref[:, pl.ds(start, kt)], kst_ref[0, :, pl.ds(start, kt)]], axis=0)
        s = jnp.dot(q_ext, k_ext, preferred_element_type=F32)
        if causal:
            kpos = start + lax.broadcasted_iota(jnp.int32, (1, kt), 1)
            s = masked(s, kpos <= t_col, MASKED)
        m_new = jnp.maximum(m_run, jnp.max(s, axis=-1, keepdims=True))
        p = jnp.exp2(s - m_new)
        acc = jnp.exp2(m_run - m_new) * acc + jnp.dot(p.astype(BF16), vs_ref[0, pl.ds(start, kt), :],
                                                      preferred_element_type=F32)
        return m_new, acc

    rows = hpg * qb
    init = (jnp.full((rows, 1), MASKED, F32), jnp.zeros((rows, vs_ref.shape[2]), F32))
    n_full = q0 // kt
    carry = lax.fori_loop(0, n_full, functools.partial(sel_step, causal=False), init)
    _, acc_s = sel_step(n_full, carry, causal=True)
    o_s = normalized(acc_s)

    gates = _sigmoid(gl_ref[...])
    for h in range(hpg):
        r = slice(h * qb, (h + 1) * qb)
        o = (gates[:, 3 * h:3 * h + 1] * o_c[r] + gates[:, 3 * h + 1:3 * h + 2] * o_s[r]
             + gates[:, 3 * h + 2:3 * h + 3] * o_w[r])
        o_ref[:, h * dh:(h + 1) * dh] = o.astype(o_ref.dtype)


def _nsa_attention(qn, qr, gl, kct, vc, kst, vs, kwt, vw, overlap, expand, batch, t_len):
    m = qn.shape[0]
    n_g = NSA_KV_GROUPS
    gw = NSA_GROUP_WIDTH
    dh = NSA_HEAD_DIM
    nq = t_len // Q_BLOCK
    n_cmp = kct.shape[2]
    qspec = pl.BlockSpec((Q_BLOCK, gw), lambda b, g, i: (b * nq + i, g))
    kv_t = lambda n: pl.BlockSpec((1, dh, n), lambda b, g, i: (b * n_g + g, 0, 0))
    kv_n = lambda n: pl.BlockSpec((1, n, LANES), lambda b, g, i: (b * n_g + g, 0, 0))
    const = lambda a: pl.BlockSpec(a.shape, lambda b, g, i: (0, 0))
    return pl.pallas_call(
        _nsa_kernel,
        grid=(batch, n_g, nq),
        in_specs=[qspec, qspec,
                  pl.BlockSpec((Q_BLOCK, LANES), lambda b, g, i: (b * nq + i, g)),
                  kv_t(n_cmp), kv_n(n_cmp), kv_t(t_len), kv_n(t_len), kv_t(t_len), kv_n(t_len),
                  const(overlap), const(expand)],
        out_specs=qspec,
        out_shape=jax.ShapeDtypeStruct((m, n_g * gw), BF16),
        compiler_params=_params("parallel", "parallel", "arbitrary"),
        name="nsa_attention",
    )(qn, qr, gl, kct, vc, kst, vs, kwt, vw, overlap, expand)


def _proj_residual_kernel(x_ref, a_ref, w_ref, o_ref):
    o_ref[...] = x_ref[...] + jnp.dot(a_ref[...], w_ref[...], preferred_element_type=F32)


def _proj_residual(x, a, w, tm=512):
    m, d = x.shape
    k = a.shape[1]
    return pl.pallas_call(
        _proj_residual_kernel,
        grid=(m // tm,),
        in_specs=[pl.BlockSpec((tm, d), lambda i: (i, 0)),
                  pl.BlockSpec((tm, k), lambda i: (i, 0)),
                  pl.BlockSpec((k, d), lambda i: (0, 0))],
        out_specs=pl.BlockSpec((tm, d), lambda i: (i, 0)),
        out_shape=jax.ShapeDtypeStruct((m, d), F32),
        compiler_params=_params("parallel"),
        name="proj_residual",
    )(x, a, w)


def _pad_cols(w, n):
    return jnp.pad(w, ((0, 0), (0, n - w.shape[1])))


def _even_mixer(x, gain, w_in, qkv_conv, a_log, dt_bias, o_norm, sc_conv, w_out, batch, t_len):
    gw = GDN_WIDTH
    main = jnp.concatenate([w_in[:, :4 * gw], w_in[:, 4 * gw + 2 * GDN_HEADS:]], axis=1)
    ab = _pad_cols(w_in[:, 4 * gw:4 * gw + 2 * GDN_HEADS], LANES)
    w_all = jnp.concatenate([main, ab], axis=1).astype(BF16)
    proj = _norm_matmul(x, gain, w_all)
    gate_params = jnp.zeros((SUBLANES, LANES), F32)
    gate_params = gate_params.at[0, :GDN_HEADS].set(a_log).at[1, :GDN_HEADS].set(dt_bias)
    ya = _gdn(proj, qkv_conv, gate_params, o_norm, batch, t_len)
    return _even_out(x, ya, proj, sc_conv, w_out[:gw].astype(BF16), w_out[gw:].astype(BF16), t_len)


def _rope_tables(t_len):
    half = ROPE_DIM // 2
    inv_freq = ROPE_THETA ** (-jnp.arange(0, ROPE_DIM, 2, dtype=F32) / ROPE_DIM)
    ang = jnp.arange(t_len, dtype=F32)[:, None] * inv_freq[None, :]
    cos, sin = jnp.cos(ang), jnp.sin(ang)
    pad = NSA_HEAD_DIM - ROPE_DIM
    head = lambda a, b, fill: jnp.concatenate([a, b, jnp.full((t_len, pad), fill, F32)], axis=1)
    zeros = jnp.zeros((t_len, half), F32)
    reps = LANES // NSA_HEAD_DIM
    return (jnp.tile(head(cos, cos, 1.0), (1, reps)),
            jnp.tile(head(-sin, zeros, 0.0), (1, reps)),
            jnp.tile(head(zeros, sin, 0.0), (1, reps)))


def _odd_mixer(x, gain, w_in, cmp_pos, k_w1, k_w2, v_w1, v_w2, w_out, batch, t_len):
    n_g, dh, hpg = NSA_KV_GROUPS, NSA_HEAD_DIM, NSA_HPG
    dq = NSA_HEADS * dh
    kv = NSA_KV_WIDTH
    gate_w = w_in[:, dq + 6 * kv:].reshape(-1, n_g, 3 * hpg)
    gate_w = jnp.pad(gate_w, ((0, 0), (0, 0), (0, LANES - 3 * hpg))).reshape(-1, n_g * LANES)
    w_all = jnp.concatenate([w_in[:, :dq + 6 * kv], gate_w], axis=1).astype(BF16)
    qn, qr, kvc, ks, vs, kw, vw, gl = _odd_proj(x, gain, w_all, _rope_tables(t_len), t_len)

    n_chunk = t_len // CMP_STRIDE

    def chunked(a):
        a = a.reshape(batch, n_chunk, CMP_STRIDE, n_g, dh)
        return jnp.moveaxis(a, 3, 1).reshape(batch * n_g, n_chunk, CMP_STRIDE * dh)

    pos = jnp.zeros((SUBLANES, CMP_BLOCK * dh), F32).at[0].set(cmp_pos.reshape(-1)).astype(BF16)
    kct, vc = _compress(chunked(kvc[:, :kv]), chunked(kvc[:, kv:]), pos, k_w1.astype(BF16),
                        k_w2.T.astype(BF16), v_w1.astype(BF16), v_w2.astype(BF16))

    def keys_t(a):
        return jnp.transpose(a.reshape(batch, t_len, n_g, dh), (0, 2, 3, 1)).reshape(batch * n_g, dh, t_len)

    def with_ones(a):
        pad = jnp.zeros(a.shape[:-1] + (LANES - dh - 1,), a.dtype)
        return jnp.concatenate([a, jnp.ones(a.shape[:-1] + (1,), a.dtype), pad], axis=-1)

    def values(a):
        a = jnp.transpose(a.reshape(batch, t_len, n_g, dh), (0, 2, 1, 3)).reshape(batch * n_g, t_len, dh)
        return with_ones(a)

    n_blk = _round_up(t_len // SEL_BLOCK, LANES)
    c_start = np.arange(n_chunk)[:, None] * CMP_STRIDE
    s_start = np.arange(n_blk)[None, :] * SEL_BLOCK
    overlap = (c_start < s_start + SEL_BLOCK) & (c_start + CMP_BLOCK > s_start)
    ones_col = np.zeros((n_chunk, LANES), bool)
    ones_col[:, 0] = True
    overlap = jnp.asarray(np.concatenate([overlap, ones_col], axis=1), BF16)
    expand = jnp.asarray(np.arange(n_blk)[:, None] == (np.arange(t_len)[None, :] // SEL_BLOCK), BF16)
    o = _nsa_attention(qn, qr, gl, kct, with_ones(vc), keys_t(ks), values(vs), keys_t(kw), values(vw),
                       overlap, expand, batch, t_len)
    return _proj_residual(x, o, w_out.astype(BF16))


def kernel(x, mix_norm, mlp_norm, w_up, w_down, final_norm, ev_w_in, ev_qkv_conv, ev_a_log, ev_dt_bias,
           ev_o_norm, ev_sc_conv, ev_w_out, od_w_in, od_cmp_pos, od_cmp_k_w1, od_cmp_k_w2, od_cmp_v_w1,
           od_cmp_v_w2, od_w_out):
    batch, t_len, d = x.shape
    depth = mix_norm.shape[0]
    xs = x.reshape(batch * t_len, d)
    for layer in range(depth):
        i = layer // 2
        if layer % 2 == 0:
            xs = _even_mixer(xs, mix_norm[layer], ev_w_in[i], ev_qkv_conv[i], ev_a_log[i], ev_dt_bias[i],
                             ev_o_norm[i], ev_sc_conv[i], ev_w_out[i], batch, t_len)
        else:
            xs = _odd_mixer(xs, mix_norm[layer], od_w_in[i], od_cmp_pos[i], od_cmp_k_w1[i], od_cmp_k_w2[i],
                            od_cmp_v_w1[i], od_cmp_v_w2[i], od_w_out[i], batch, t_len)
        xs = _mlp(xs, mlp_norm[layer], w_up[layer].astype(BF16), w_down[layer].astype(BF16), final_norm,
                  final=(layer == depth - 1))
    return xs.reshape(batch, t_len, d)
```

```python
import functools

import jax
import jax.numpy as jnp
import numpy as np
from jax import lax
from jax.experimental import pallas as pl
from jax.experimental.pallas import tpu as pltpu

F32 = jnp.float32
BF16 = jnp.bfloat16
HIGHEST = lax.Precision.HIGHEST

NORM_EPS = 1e-6
GDN_HEADS = 8
GDN_HEAD_DIM = 64
GDN_WIDTH = GDN_HEADS * GDN_HEAD_DIM
GDN_CONV = 4
GDN_CHUNK = 64
SOLVE_BLOCK = 16
SC_CONV = 3
NSA_HEADS = 16
NSA_HEAD_DIM = 64
NSA_KV_GROUPS = 4
NSA_HPG = NSA_HEADS // NSA_KV_GROUPS
NSA_GROUP_WIDTH = NSA_HPG * NSA_HEAD_DIM
NSA_KV_WIDTH = NSA_KV_GROUPS * NSA_HEAD_DIM
CMP_BLOCK = 32
CMP_STRIDE = 16
SEL_BLOCK = 64
N_SELECT = 16
N_FORCED = 3
WINDOW = 512
Q_BLOCK = 128
SEL_KEY_TILE = 512
ROPE_THETA = 500000.0
ROPE_DIM = NSA_HEAD_DIM // 4
LOG2_E = 1.4426950408889634
LANES = 128
SUBLANES = 8
VMEM_LIMIT = 56 * 1024 * 1024


def _round_up(n, m):
    return (n + m - 1) // m * m


def _params(*semantics):
    return pltpu.CompilerParams(dimension_semantics=semantics, vmem_limit_bytes=VMEM_LIMIT)


def _rms(x, gain):
    return x * lax.rsqrt(jnp.mean(x * x, axis=-1, keepdims=True) + NORM_EPS) * gain


def _sigmoid(x):
    return 1.0 / (1.0 + jnp.exp(-x))


def _silu(x):
    return x * _sigmoid(x)


def _softplus(x):
    return jnp.maximum(x, 0.0) + jnp.log(1.0 + jnp.exp(-jnp.abs(x)))


def _dot(a, b):
    return jnp.dot(a.astype(BF16), b.astype(BF16), preferred_element_type=F32)


def _dot_nt(a, b):
    return lax.dot_general(a.astype(BF16), b.astype(BF16), (((1,), (1,)), ((), ())),
                           preferred_element_type=F32)


def _dot_f32(a, b):
    return jnp.dot(a, b, precision=HIGHEST, preferred_element_type=F32)


def _norm_matmul_kernel(x_ref, g_ref, w_ref, o_ref):
    h = _rms(x_ref[...], g_ref[...])
    o_ref[...] = jnp.dot(h.astype(BF16), w_ref[...], preferred_element_type=F32).astype(o_ref.dtype)


def _norm_matmul(x, gain, w, tm=512):
    m, d = x.shape
    n = w.shape[1]
    return pl.pallas_call(
        _norm_matmul_kernel,
        grid=(m // tm,),
        in_specs=[pl.BlockSpec((tm, d), lambda i: (i, 0)),
                  pl.BlockSpec((1, d), lambda i: (0, 0)),
                  pl.BlockSpec((d, n), lambda i: (0, 0))],
        out_specs=pl.BlockSpec((tm, n), lambda i: (i, 0)),
        out_shape=jax.ShapeDtypeStruct((m, n), F32),
        compiler_params=_params("parallel"),
        name="norm_proj",
    )(x, gain.reshape(1, d), w)


def _split(a):
    hi = a.astype(BF16)
    return hi, (a - hi.astype(F32)).astype(BF16)


def _dot3(a, b):
    (ah, al), (bh, bl) = a, b
    return (jnp.dot(ah, bh, preferred_element_type=F32) + jnp.dot(ah, bl, preferred_element_type=F32)
            + jnp.dot(al, bh, preferred_element_type=F32))


def _gdn_kernel(qkv_ref, z_ref, ab_ref, cw_ref, gp_ref, on_ref, o_ref, xbuf, act, state):
    c = GDN_CHUNK
    dh = GDN_HEAD_DIM
    pw = 2 * dh
    n_pairs = GDN_HEADS // 2
    n_batch = qkv_ref.shape[0]
    hist = SUBLANES
    chains = [(b, p) for b in range(n_batch) for p in range(n_pairs)]

    @pl.when(pl.program_id(0) == 0)
    def _():
        xbuf[:, 0:hist, :] = jnp.zeros((n_batch, hist, 3 * GDN_WIDTH), F32)
        state[...] = jnp.zeros(state.shape, F32)

    row = lax.broadcasted_iota(jnp.int32, (c, pw), 0)
    lane = lax.broadcasted_iota(jnp.int32, (c, pw), 1)
    first = lane < dh
    col = jnp.where(first, lane, lane - dh)
    incl = row >= col
    strict = row > col
    eye = (row == col).astype(F32)
    diag_blk = strict & ((row // SOLVE_BLOCK) == (col // SOLVE_BLOCK))
    r2 = lax.broadcasted_iota(jnp.int32, (pw, pw), 0)
    c2 = lax.broadcasted_iota(jnp.int32, (pw, pw), 1)
    same_head = (r2 < dh) == (c2 < dh)
    ones_bd = same_head.astype(BF16)
    eye_pw = (r2 == c2).astype(BF16)
    tri = (lax.broadcasted_iota(jnp.int32, (c, c), 0) >= lax.broadcasted_iota(jnp.int32, (c, c), 1)).astype(F32)

    def blockdiag(x):
        return jnp.concatenate([jnp.where(first, x, 0.0), jnp.where(first, 0.0, x)], axis=0)

    def head_sums(x):
        hi, lo = _split(x)
        return jnp.dot(hi, ones_bd, preferred_element_type=F32) + jnp.dot(lo, ones_bd, preferred_element_type=F32)

    def pdot(xs, y):
        return _dot3(xs, _split(blockdiag(y)))

    gcs, betas = [], []
    for b in range(n_batch):
        x = qkv_ref[b]
        xbuf[b, hist:hist + c, :] = x
        y = x * cw_ref[GDN_CONV - 1:GDN_CONV, :]
        for j in range(GDN_CONV - 1):
            shift = GDN_CONV - 1 - j
            y = y + xbuf[b, hist - shift:hist - shift + c, :] * cw_ref[j:j + 1, :]
        xbuf[b, 0:hist, :] = x[c - hist:c, :]
        act[b] = _silu(y)
        ab = ab_ref[b]
        g_all = -jnp.exp(gp_ref[0:1, :]) * _softplus(ab + gp_ref[1:2, :])
        gcs.append(_dot_f32(tri, g_all))
        betas.append(_sigmoid(ab))

    def pair_cols(a, p, offset):
        return jnp.where(first, a[:, offset + 2 * p:offset + 2 * p + 1], a[:, offset + 2 * p + 1:offset + 2 * p + 2])

    scale = dh ** -0.5
    q = [act[b, :, p * pw:(p + 1) * pw] for b, p in chains]
    k = [act[b, :, GDN_WIDTH + p * pw:GDN_WIDTH + (p + 1) * pw] for b, p in chains]
    v = [act[b, :, 2 * GDN_WIDTH + p * pw:2 * GDN_WIDTH + (p + 1) * pw] for b, p in chains]
    q = [x * lax.rsqrt(head_sums(x * x) + NORM_EPS) * scale for x in q]
    k = [x * lax.rsqrt(head_sums(x * x) + NORM_EPS) for x in k]
    gcol = [pair_cols(gcs[b], p, 0) for b, p in chains]
    bcol = [pair_cols(betas[b], p, GDN_HEADS) for b, p in chains]
    grow = [jnp.sum(g * eye, axis=0, keepdims=True) for g in gcol]
    g_last = [g[c - 1:c, :] for g in gcol]
    decay = [jnp.exp(jnp.where(incl, gc_ - gr_, -jnp.inf)) for gc_, gr_ in zip(gcol, grow)]
    kb = [k_ * b_ for k_, b_ in zip(k, bcol)]
    k_bd = [blockdiag(k_).astype(BF16) for k_ in k]
    m_low = [jnp.where(strict, _dot_nt(kb_, kd_) * d_, 0.0) for kb_, kd_, d_ in zip(kb, k_bd, decay)]
    attn = [_dot_nt(q_, kd_) * d_ for q_, kd_, d_ in zip(q, k_bd, decay)]

    d_pow = [jnp.where(diag_blk, m_, 0.0) for m_ in m_low]
    l_mat = [m_ - d_ for m_, d_ in zip(m_low, d_pow)]
    p_inv = [eye - d_ for d_ in d_pow]
    for _ in range(int(np.log2(SOLVE_BLOCK)) - 1):
        d_pow = [pdot(_split(d_), d_) for d_ in d_pow]
        p_inv = [p_ + pdot(_split(p_), d_) for p_, d_ in zip(p_inv, d_pow)]
    n_pow = [pdot(_split(p_), l_) for p_, l_ in zip(p_inv, l_mat)]
    q_inv = [eye - n_ for n_ in n_pow]
    for _ in range(int(np.log2(c // SOLVE_BLOCK)) - 1):
        n_pow = [pdot(_split(n_), n_) for n_ in n_pow]
        q_inv = [q_ + pdot(_split(q_), n_) for q_, n_ in zip(q_inv, n_pow)]
    a_inv = [_split(pdot(_split(q_), p_)) for q_, p_ in zip(q_inv, p_inv)]
    u = [pdot(a_, v_ * b_) for a_, v_, b_ in zip(a_inv, v, bcol)]
    w = [pdot(a_, kb_ * jnp.exp(g_)) for a_, kb_, g_ in zip(a_inv, kb, gcol)]

    s_old = [state[b * n_pairs + p] for b, p in chains]
    v_new = [u_ - _dot(w_, s_) for u_, w_, s_ in zip(u, w, s_old)]
    o = [_dot(q_ * jnp.exp(g_), s_) + _dot(a_, blockdiag(vn_))
         for q_, g_, s_, a_, vn_ in zip(q, gcol, s_old, attn, v_new)]
    k_dec_t = [_dot_nt(eye_pw, k_ * jnp.exp(gl_ - g_)) for k_, gl_, g_ in zip(k, g_last, gcol)]
    for (b, p), s_, gl_, kt_, vn_ in zip(chains, s_old, g_last, k_dec_t, v_new):
        state[b * n_pairs + p] = s_ * jnp.exp(gl_) + jnp.where(same_head, _dot(kt_, vn_), 0.0)

    inv_dh = 1.0 / dh
    for (b, p), o_ in zip(chains, o):
        zp = z_ref[b, :, p * pw:(p + 1) * pw]
        y = o_ * lax.rsqrt(head_sums(o_ * o_) * inv_dh + NORM_EPS) * on_ref[...]
        o_ref[b, :, p * pw:(p + 1) * pw] = (y * _silu(zp)).astype(o_ref.dtype)


def _gdn(proj, conv_w, gate_params, o_norm, batch, t_len):
    c = GDN_CHUNK
    n_chunks = t_len // c
    w3 = 3 * GDN_WIDTH
    ab_col = (4 * GDN_WIDTH + 3 * GDN_WIDTH) // LANES
    proj = proj.reshape(batch, t_len, proj.shape[-1])
    on_pair = jnp.tile(o_norm.reshape(1, GDN_HEAD_DIM), (1, 2))
    out = pl.pallas_call(
        _gdn_kernel,
        grid=(n_chunks,),
        in_specs=[pl.BlockSpec((batch, c, w3), lambda i: (0, i, 0)),
                  pl.BlockSpec((batch, c, GDN_WIDTH), lambda i: (0, i, 3)),
                  pl.BlockSpec((batch, c, LANES), lambda i: (0, i, ab_col)),
                  pl.BlockSpec((GDN_CONV, w3), lambda i: (0, 0)),
                  pl.BlockSpec((SUBLANES, LANES), lambda i: (0, 0)),
                  pl.BlockSpec((1, 2 * GDN_HEAD_DIM), lambda i: (0, 0))],
        out_specs=pl.BlockSpec((batch, c, GDN_WIDTH), lambda i: (0, i, 0)),
        out_shape=jax.ShapeDtypeStruct((batch, t_len, GDN_WIDTH), BF16),
        scratch_shapes=[pltpu.VMEM((batch, c + SUBLANES, w3), F32),
                        pltpu.VMEM((batch, c, w3), F32),
                        pltpu.VMEM((batch * GDN_HEADS // 2, 2 * GDN_HEAD_DIM, 2 * GDN_HEAD_DIM), F32)],
        compiler_params=_params("arbitrary"),
        name="gdn",
    )(proj, proj, proj, conv_w, gate_params, on_pair)
    return out.reshape(batch * t_len, GDN_WIDTH)


def _even_out_kernel(tiles_per_batch, x_ref, ya_ref, bg_ref, cg_ref, hs_ref, cgp_ref, hsp_ref,
                     cw_ref, wa_ref, wb_ref, o_ref, ubuf):
    tm = x_ref.shape[0]
    hist = SUBLANES
    first = (pl.program_id(0) % tiles_per_batch) == 0
    prev = cgp_ref[...] * hsp_ref[...]
    ubuf[0:hist, :] = jnp.where(first, 0.0, prev)
    u = cg_ref[...] * hs_ref[...]
    ubuf[hist:hist + tm, :] = u
    conv = u * cw_ref[SC_CONV - 1:SC_CONV, :]
    for j in range(SC_CONV - 1):
        shift = SC_CONV - 1 - j
        conv = conv + ubuf[hist - shift:hist - shift + tm, :] * cw_ref[j:j + 1, :]
    yb = bg_ref[...] * conv
    acc = jnp.dot(ya_ref[...], wa_ref[...], preferred_element_type=F32)
    acc = acc + jnp.dot(yb.astype(BF16), wb_ref[...], preferred_element_type=F32)
    o_ref[...] = x_ref[...] + acc


def _even_out(x, ya, proj, sc_conv, w_a, w_b, t_len, tm=512):
    m, d = x.shape
    wd = GDN_WIDTH
    tiles_per_batch = t_len // tm
    hb = tm // SUBLANES
    prev = lambda col: pl.BlockSpec((SUBLANES, wd), lambda i: (jnp.maximum(i * hb - 1, 0), col))
    cur = lambda col: pl.BlockSpec((tm, wd), lambda i: (i, col))
    return pl.pallas_call(
        functools.partial(_even_out_kernel, tiles_per_batch),
        grid=(m // tm,),
        in_specs=[pl.BlockSpec((tm, d), lambda i: (i, 0)),
                  pl.BlockSpec((tm, wd), lambda i: (i, 0)),
                  cur(4), cur(5), cur(6), prev(5), prev(6),
                  pl.BlockSpec((SC_CONV, wd), lambda i: (0, 0)),
                  pl.BlockSpec((wd, d), lambda i: (0, 0)),
                  pl.BlockSpec((wd, d), lambda i: (0, 0))],
        out_specs=pl.BlockSpec((tm, d), lambda i: (i, 0)),
        out_shape=jax.ShapeDtypeStruct((m, d), F32),
        scratch_shapes=[pltpu.VMEM((tm + SUBLANES, wd), F32)],
        compiler_params=_params("parallel"),
        name="even_out",
    )(x, ya, proj, proj, proj, proj, proj, sc_conv, w_a, w_b)


def _mlp_kernel(final, x_ref, g_ref, wu_ref, wd_ref, fg_ref, o_ref, hn, acc):
    j = pl.program_id(1)

    @pl.when(j == 0)
    def _():
        hn[...] = _rms(x_ref[...], g_ref[...]).astype(BF16)
        acc[...] = jnp.zeros(acc.shape, F32)

    u = jnp.dot(hn[...], wu_ref[...], preferred_element_type=F32)
    u = jnp.square(jnp.maximum(u, 0.0))
    acc[...] += jnp.dot(u.astype(BF16), wd_ref[...], preferred_element_type=F32)

    @pl.when(j == pl.num_programs(1) - 1)
    def _():
        y = x_ref[...] + acc[...]
        if final:
            y = _rms(y, fg_ref[...])
        o_ref[...] = y


def _mlp(x, gain, w_up, w_down, final_gain, final, tm=512, tf=1024):
    m, d = x.shape
    ff = w_up.shape[1]
    return pl.pallas_call(
        functools.partial(_mlp_kernel, final),
        grid=(m // tm, ff // tf),
        in_specs=[pl.BlockSpec((tm, d), lambda i, j: (i, 0)),
                  pl.BlockSpec((1, d), lambda i, j: (0, 0)),
                  pl.BlockSpec((d, tf), lambda i, j: (0, j)),
                  pl.BlockSpec((tf, d), lambda i, j: (j, 0)),
                  pl.BlockSpec((1, d), lambda i, j: (0, 0))],
        out_specs=pl.BlockSpec((tm, d), lambda i, j: (i, 0)),
        out_shape=jax.ShapeDtypeStruct((m, d), F32),
        scratch_shapes=[pltpu.VMEM((tm, d), BF16), pltpu.VMEM((tm, d), F32)],
        compiler_params=_params("parallel", "arbitrary"),
        name="mlp",
    )(x, gain.reshape(1, d), w_up, w_down, final_gain.reshape(1, d))


def _rope(x, cos_t, sin_lo, sin_hi):
    half = ROPE_DIM // 2
    outs = []
    for j in range(x.shape[1] // LANES):
        xs = x[:, j * LANES:(j + 1) * LANES]
        up = pltpu.roll(xs, LANES - half, axis=1)
        down = pltpu.roll(xs, half, axis=1)
        outs.append(xs * cos_t + up * sin_lo + down * sin_hi)
    return jnp.concatenate(outs, axis=-1)


def _odd_proj_kernel(x_ref, g_ref, w_ref, cos_ref, slo_ref, shi_ref,
                     qn_ref, qr_ref, kvc_ref, ks_ref, vs_ref, kw_ref, vw_ref, gl_ref):
    h = _rms(x_ref[...], g_ref[...])
    y = jnp.dot(h.astype(BF16), w_ref[...], preferred_element_type=F32)
    cos_t, sin_lo, sin_hi = cos_ref[...], slo_ref[...], shi_ref[...]
    dq = NSA_HEADS * NSA_HEAD_DIM
    kv = NSA_KV_WIDTH
    q = y[:, :dq] * (LOG2_E * NSA_HEAD_DIM ** -0.5)
    qn_ref[...] = q.astype(BF16)
    qr_ref[...] = _rope(q, cos_t, sin_lo, sin_hi).astype(BF16)
    kvc_ref[...] = y[:, dq:dq + 2 * kv].astype(BF16)
    ks_ref[...] = _rope(y[:, dq + 2 * kv:dq + 3 * kv], cos_t, sin_lo, sin_hi).astype(BF16)
    vs_ref[...] = y[:, dq + 3 * kv:dq + 4 * kv].astype(BF16)
    kw_ref[...] = _rope(y[:, dq + 4 * kv:dq + 5 * kv], cos_t, sin_lo, sin_hi).astype(BF16)
    vw_ref[...] = y[:, dq + 5 * kv:dq + 6 * kv].astype(BF16)
    gl_ref[...] = y[:, dq + 6 * kv:]


def _odd_proj(x, gain, w, tables, t_len, tm=512):
    m, d = x.shape
    n = w.shape[1]
    dq = NSA_HEADS * NSA_HEAD_DIM
    kv = NSA_KV_WIDTH
    ng = n - dq - 6 * kv
    tpb = t_len // tm
    row = lambda width: pl.BlockSpec((tm, width), lambda i: (i, 0))
    tab = pl.BlockSpec((tm, LANES), lambda i: (i % tpb, 0))
    shapes = [(dq, BF16), (dq, BF16), (2 * kv, BF16), (kv, BF16), (kv, BF16), (kv, BF16), (kv, BF16),
              (ng, F32)]
    return pl.pallas_call(
        _odd_proj_kernel,
        grid=(m // tm,),
        in_specs=[row(d), pl.BlockSpec((1, d), lambda i: (0, 0)),
                  pl.BlockSpec((d, n), lambda i: (0, 0)), tab, tab, tab],
        out_specs=[row(width) for width, _ in shapes],
        out_shape=[jax.ShapeDtypeStruct((m, width), dt) for width, dt in shapes],
        compiler_params=_params("parallel"),
        name="odd_proj",
    )(x, gain.reshape(1, d), w, *tables)


def _compress_kernel(xa_ref, xb_ref, pos_ref, a1_ref, a2t_ref, b1_ref, b2_ref, at_ref, b_ref):
    half = xa_ref.shape[2]
    n = xa_ref.shape[1]

    def hidden(x_ref, w1_ref):
        x = x_ref[0]
        top = jnp.dot(x, w1_ref[0:half, :], preferred_element_type=F32)
        bot = jnp.dot(x, w1_ref[half:2 * half, :], preferred_element_type=F32)
        bias = (jnp.dot(pos_ref[:, 0:half], w1_ref[0:half, :], preferred_element_type=F32)
                + jnp.dot(pos_ref[:, half:2 * half], w1_ref[half:2 * half, :], preferred_element_type=F32))
        nxt = pltpu.roll(bot, n - 1, axis=0)
        return _silu(top + nxt + bias[0:1, :]).astype(BF16)

    at_ref[0] = lax.dot_general(a2t_ref[...], hidden(xa_ref, a1_ref), (((1,), (1,)), ((), ())),
                                preferred_element_type=F32).astype(BF16)
    b_ref[0] = jnp.dot(hidden(xb_ref, b1_ref), b2_ref[...], preferred_element_type=F32).astype(BF16)


def _compress(xa, xb, pos, a1, a2t, b1, b2):
    bg, n, half = xa.shape
    dh = NSA_HEAD_DIM
    full = lambda a: pl.BlockSpec(a.shape, lambda i: (0,) * a.ndim)
    return pl.pallas_call(
        _compress_kernel,
        grid=(bg,),
        in_specs=[pl.BlockSpec((1, n, half), lambda i: (i, 0, 0)),
                  pl.BlockSpec((1, n, half), lambda i: (i, 0, 0)),
                  full(pos), full(a1), full(a2t), full(b1), full(b2)],
        out_specs=[pl.BlockSpec((1, dh, n), lambda i: (i, 0, 0)),
                   pl.BlockSpec((1, n, dh), lambda i: (i, 0, 0))],
        out_shape=[jax.ShapeDtypeStruct((bg, dh, n), BF16), jax.ShapeDtypeStruct((bg, n, dh), BF16)],
        compiler_params=_params("parallel"),
        name="compress",
    )(xa, xb, pos, a1, a2t, b1, b2)


MASKED = -1e30


def _nsa_kernel(qn_ref, qr_ref, gl_ref, kc_ref, vo_ref, ks_ref, vst_ref, kw_ref, vwt_ref, expt_ref,
                o_ref, s_even, s_odd):
    s_buf = (s_even, s_odd)
    qb = Q_BLOCK
    dh = NSA_HEAD_DIM
    hpg = NSA_HPG
    n_cmp = kc_ref.shape[1]
    n_blk = expt_ref.shape[1]
    q0 = pl.program_id(2) * qb
    t_row = q0 + lax.broadcasted_iota(jnp.int32, (1, qb), 1)

    def heads_t(ref):
        xt = ref[...].astype(F32).T
        return jnp.concatenate([xt[h * dh:(h + 1) * dh] for h in range(hpg)], axis=1).astype(BF16)

    qn_t = heads_t(qn_ref)
    qr_t = heads_t(qr_ref)

    def masked(s, mask, fill):
        return jnp.concatenate([jnp.where(mask, s[:, h * qb:(h + 1) * qb], fill) for h in range(hpg)], axis=1)

    def exp2_cols(s):
        m = jnp.max(s, axis=0, keepdims=True)
        return jnp.exp2(s - jnp.where(m == -jnp.inf, 0.0, m))

    cmp_end = lax.broadcasted_iota(jnp.int32, (n_cmp, 1), 0) * CMP_STRIDE + (CMP_BLOCK - 1)
    s_c = jnp.dot(kc_ref[0], qn_t, preferred_element_type=F32)
    e_c = exp2_cols(masked(s_c, cmp_end <= t_row, -jnp.inf)).astype(BF16)
    both = jnp.dot(vo_ref[0], e_c, preferred_element_type=F32)
    acc_c = both[0:LANES]
    den_c = acc_c[dh:dh + 1, :]
    imp_all = both[LANES:LANES + n_blk] * (1.0 / jnp.where(den_c > 0.0, den_c, 1.0))
    imp_t = imp_all[:, 0:qb]
    for h in range(1, hpg):
        imp_t = imp_t + imp_all[:, h * qb:(h + 1) * qb]

    span = WINDOW + qb
    w_start = pl.multiple_of(jnp.maximum(q0 - WINDOW, 0), qb)
    dist = t_row - (w_start + lax.broadcasted_iota(jnp.int32, (span, 1), 0))
    s_w = jnp.dot(kw_ref[0, pl.ds(w_start, span), :], qr_t, preferred_element_type=F32)
    e_w = exp2_cols(masked(s_w, (dist >= 0) & (dist < WINDOW), -jnp.inf))
    acc_w = jnp.dot(vwt_ref[0, :, pl.ds(w_start, span)], e_w.astype(BF16), preferred_element_type=F32)

    js = lax.broadcasted_iota(jnp.int32, (n_blk, qb), 0).astype(F32)
    cur = (t_row // SEL_BLOCK).astype(F32)
    forced = (js == 0.0) | (js == cur) | (js == cur - 1.0)
    val = jnp.where((js > cur) | forced, -jnp.inf, imp_t)
    sel_t = forced.astype(F32)
    for _ in range(N_SELECT - N_FORCED):
        best = jnp.max(val, axis=0, keepdims=True)
        first = jnp.min(jnp.where(val == best, js, float(n_blk)), axis=0, keepdims=True)
        pick = js == first
        sel_t = jnp.where(pick, 1.0, sel_t)
        val = jnp.where(pick, -jnp.inf, val)
    bias_t = jnp.where((sel_t > 0.5) & (js <= cur), 0.0, MASKED).astype(BF16)

    kt = SEL_KEY_TILE
    n_tiles = ks_ref.shape[1] // kt
    q_ext_t = jnp.concatenate([jnp.concatenate([bias_t] * hpg, axis=1), qr_t], axis=0)

    def scores(i):
        start = pl.multiple_of(jnp.minimum(i, n_tiles - 1) * kt, kt)
        k_ext = jnp.concatenate([expt_ref[pl.ds(start, kt), :], ks_ref[0, pl.ds(start, kt), :]], axis=1)
        return jnp.dot(k_ext, q_ext_t, preferred_element_type=F32)

    def consume(i, slot, carry):
        m_run, acc = carry
        start = pl.multiple_of(i * kt, kt)
        off = pl.multiple_of(jnp.clip(q0 - start, 0, kt - qb), qb)
        kpos = start + off + lax.broadcasted_iota(jnp.int32, (qb, 1), 0)
        s_ref = s_buf[slot]
        s_ref[pl.ds(off, qb), :] = masked(s_ref[pl.ds(off, qb), :], kpos <= t_row, MASKED)
        s = s_ref[...]
        m_new = jnp.maximum(m_run, jnp.max(s, axis=0, keepdims=True))
        p = jnp.exp2(s - m_new)
        acc = jnp.exp2(m_run - m_new) * acc + jnp.dot(vst_ref[0, :, pl.ds(start, kt)], p.astype(BF16),
                                                      preferred_element_type=F32)
        return m_new, acc

    def pair_step(j, carry):
        s_odd[...] = scores(2 * j + 1)
        carry = consume(2 * j, 0, carry)
        s_even[...] = scores(2 * j + 2)
        return consume(2 * j + 1, 1, carry)

    cols = hpg * qb
    init = (jnp.full((1, cols), MASKED, F32), jnp.zeros((vst_ref.shape[1], cols), F32))
    s_even[...] = scores(0)
    _, acc_s = lax.fori_loop(0, (q0 // kt) // 2 + 1, pair_step, init)

    gates_t = _sigmoid(gl_ref[...].T)
    outs = []
    for h in range(hpg):
        c = slice(h * qb, (h + 1) * qb)
        o_t = jnp.zeros((dh, qb), F32)
        for j, acc in enumerate((acc_c, acc_s, acc_w)):
            den = acc[dh:dh + 1, c]
            o_t = o_t + (gates_t[3 * h + j:3 * h + j + 1, :] / jnp.where(den > 0.0, den, 1.0)) * acc[0:dh, c]
        outs.append(o_t)
    o_ref[...] = jnp.concatenate(outs, axis=0).T.astype(o_ref.dtype)


def _nsa_attention(qn, qr, gl, kc, vo, ks, vst, kw, vwt, expand_t, batch, t_len):
    assert t_len % (2 * SEL_KEY_TILE) == 0, "key tiles of the selected branch are processed in pairs"
    m = qn.shape[0]
    n_g = NSA_KV_GROUPS
    gw = NSA_GROUP_WIDTH
    dh = NSA_HEAD_DIM
    nq = t_len // Q_BLOCK
    n_cmp = kc.shape[1]
    qspec = pl.BlockSpec((Q_BLOCK, gw), lambda b, g, i: (b * nq + i, g))
    rows = lambda n: pl.BlockSpec((1, n, dh), lambda b, g, i: (b * n_g + g, 0, 0))
    cols = lambda n: pl.BlockSpec((1, LANES, n), lambda b, g, i: (b * n_g + g, 0, 0))
    const = lambda a: pl.BlockSpec(a.shape, lambda b, g, i: (0, 0))
    return pl.pallas_call(
        _nsa_kernel,
        grid=(batch, n_g, nq),
        in_specs=[qspec, qspec,
                  pl.BlockSpec((Q_BLOCK, LANES), lambda b, g, i: (b * nq + i, g)),
                  rows(n_cmp), pl.BlockSpec((1,) + vo.shape[1:], lambda b, g, i: (b * n_g + g, 0, 0)),
                  rows(t_len), cols(t_len), rows(t_len), cols(t_len), const(expand_t)],
        out_specs=qspec,
        out_shape=jax.ShapeDtypeStruct((m, n_g * gw), BF16),
        scratch_shapes=[pltpu.VMEM((SEL_KEY_TILE, NSA_HPG * Q_BLOCK), F32)] * 2,
        compiler_params=_params("parallel", "parallel", "arbitrary"),
        name="nsa_attention",
    )(qn, qr, gl, kc, vo, ks, vst, kw, vwt, expand_t)


def _proj_residual_kernel(x_ref, a_ref, w_ref, o_ref):
    o_ref[...] = x_ref[...] + jnp.dot(a_ref[...], w_ref[...], preferred_element_type=F32)


def _proj_residual(x, a, w, tm=512):
    m, d = x.shape
    k = a.shape[1]
    return pl.pallas_call(
        _proj_residual_kernel,
        grid=(m // tm,),
        in_specs=[pl.BlockSpec((tm, d), lambda i: (i, 0)),
                  pl.BlockSpec((tm, k), lambda i: (i, 0)),
                  pl.BlockSpec((k, d), lambda i: (0, 0))],
        out_specs=pl.BlockSpec((tm, d), lambda i: (i, 0)),
        out_shape=jax.ShapeDtypeStruct((m, d), F32),
        compiler_params=_params("parallel"),
        name="proj_residual",
    )(x, a, w)


def _pad_cols(w, n):
    return jnp.pad(w, ((0, 0), (0, n - w.shape[1])))


def _even_mixer(x, gain, w_in, qkv_conv, a_log, dt_bias, o_norm, sc_conv, w_out, batch, t_len):
    gw = GDN_WIDTH
    main = jnp.concatenate([w_in[:, :4 * gw], w_in[:, 4 * gw + 2 * GDN_HEADS:]], axis=1)
    ab = _pad_cols(w_in[:, 4 * gw:4 * gw + 2 * GDN_HEADS], LANES)
    w_all = jnp.concatenate([main, ab], axis=1).astype(BF16)
    proj = _norm_matmul(x, gain, w_all)
    gate_params = jnp.zeros((SUBLANES, LANES), F32)
    gate_params = gate_params.at[0, :GDN_HEADS].set(a_log).at[1, :GDN_HEADS].set(dt_bias)
    ya = _gdn(proj, qkv_conv, gate_params, o_norm, batch, t_len)
    return _even_out(x, ya, proj, sc_conv, w_out[:gw].astype(BF16), w_out[gw:].astype(BF16), t_len)


def _rope_tables(t_len):
    half = ROPE_DIM // 2
    inv_freq = ROPE_THETA ** (-jnp.arange(0, ROPE_DIM, 2, dtype=F32) / ROPE_DIM)
    ang = jnp.arange(t_len, dtype=F32)[:, None] * inv_freq[None, :]
    cos, sin = jnp.cos(ang), jnp.sin(ang)
    pad = NSA_HEAD_DIM - ROPE_DIM
    head = lambda a, b, fill: jnp.concatenate([a, b, jnp.full((t_len, pad), fill, F32)], axis=1)
    zeros = jnp.zeros((t_len, half), F32)
    reps = LANES // NSA_HEAD_DIM
    return (jnp.tile(head(cos, cos, 1.0), (1, reps)),
            jnp.tile(head(-sin, zeros, 0.0), (1, reps)),
            jnp.tile(head(zeros, sin, 0.0), (1, reps)))


def _odd_mixer(x, gain, w_in, cmp_pos, k_w1, k_w2, v_w1, v_w2, w_out, batch, t_len):
    n_g, dh, hpg = NSA_KV_GROUPS, NSA_HEAD_DIM, NSA_HPG
    dq = NSA_HEADS * dh
    kv = NSA_KV_WIDTH
    gate_w = w_in[:, dq + 6 * kv:].reshape(-1, n_g, 3 * hpg)
    gate_w = jnp.pad(gate_w, ((0, 0), (0, 0), (0, LANES - 3 * hpg))).reshape(-1, n_g * LANES)
    w_all = jnp.concatenate([w_in[:, :dq + 6 * kv], gate_w], axis=1).astype(BF16)
    qn, qr, kvc, ks, vs, kw, vw, gl = _odd_proj(x, gain, w_all, _rope_tables(t_len), t_len)

    n_chunk = t_len // CMP_STRIDE

    def chunked(a):
        a = a.reshape(batch, n_chunk, CMP_STRIDE, n_g, dh)
        return jnp.moveaxis(a, 3, 1).reshape(batch * n_g, n_chunk, CMP_STRIDE * dh)

    pos = jnp.zeros((SUBLANES, CMP_BLOCK * dh), F32).at[0].set(cmp_pos.reshape(-1)).astype(BF16)
    vc_t, kc = _compress(chunked(kvc[:, kv:]), chunked(kvc[:, :kv]), pos, v_w1.astype(BF16),
                         v_w2.T.astype(BF16), k_w1.astype(BF16), k_w2.astype(BF16))

    def keys(a):
        return jnp.transpose(a.reshape(batch, t_len, n_g, dh), (0, 2, 1, 3)).reshape(batch * n_g, t_len, dh)

    def with_ones(a):
        n = a.shape[-1]
        return jnp.concatenate([a, jnp.ones((a.shape[0], 1, n), a.dtype),
                                jnp.zeros((a.shape[0], LANES - dh - 1, n), a.dtype)], axis=1)

    def values_t(a):
        a = jnp.transpose(a.reshape(batch, t_len, n_g, dh), (0, 2, 3, 1)).reshape(batch * n_g, dh, t_len)
        return with_ones(a)

    n_blk = _round_up(t_len // SEL_BLOCK, LANES)
    c_start = np.arange(n_chunk)[None, :] * CMP_STRIDE
    s_start = np.arange(n_blk)[:, None] * SEL_BLOCK
    overlap_t = jnp.asarray((c_start < s_start + SEL_BLOCK) & (c_start + CMP_BLOCK > s_start), BF16)
    expand_t = jnp.asarray((np.arange(t_len)[:, None] // SEL_BLOCK) == np.arange(n_blk)[None, :], BF16)
    vo = jnp.concatenate([with_ones(vc_t), jnp.broadcast_to(overlap_t, (batch * n_g,) + overlap_t.shape)], axis=1)
    o = _nsa_attention(qn, qr, gl, kc, vo, keys(ks), values_t(vs), keys(kw), values_t(vw), expand_t,
                       batch, t_len)
    return _proj_residual(x, o, w_out.astype(BF16))


def kernel(x, mix_norm, mlp_norm, w_up, w_down, final_norm, ev_w_in, ev_qkv_conv, ev_a_log, ev_dt_bias,
           ev_o_norm, ev_sc_conv, ev_w_out, od_w_in, od_cmp_pos, od_cmp_k_w1, od_cmp_k_w2, od_cmp_v_w1,
           od_cmp_v_w2, od_w_out):
    batch, t_len, d = x.shape
    depth = mix_norm.shape[0]
    xs = x.reshape(batch * t_len, d)
    for layer in range(depth):
        i = layer // 2
        if layer % 2 == 0:
            xs = _even_mixer(xs, mix_norm[layer], ev_w_in[i], ev_qkv_conv[i], ev_a_log[i], ev_dt_bias[i],
                             ev_o_norm[i], ev_sc_conv[i], ev_w_out[i], batch, t_len)
        else:
            xs = _odd_mixer(xs, mix_norm[layer], od_w_in[i], od_cmp_pos[i], od_cmp_k_w1[i], od_cmp_k_w2[i],
                            od_cmp_v_w1[i], od_cmp_v_w2[i], od_w_out[i], batch, t_len)
        xs = _mlp(xs, mlp_norm[layer], w_up[layer].astype(BF16), w_down[layer].astype(BF16), final_norm,
                  final=(layer == depth - 1))
    return xs.reshape(batch, t_len, d)
```

```python
import functools

import jax
import jax.numpy as jnp
import numpy as np
from jax import lax
from jax.experimental import pallas as pl
from jax.experimental.pallas import tpu as pltpu

F32 = jnp.float32
BF16 = jnp.bfloat16
HIGHEST = lax.Precision.HIGHEST

NORM_EPS = 1e-6
GDN_HEADS = 8
GDN_HEAD_DIM = 64
GDN_WIDTH = GDN_HEADS * GDN_HEAD_DIM
GDN_CONV = 4
GDN_CHUNK = 64
SOLVE_BLOCK = 16
SC_CONV = 3
NSA_HEADS = 16
NSA_HEAD_DIM = 64
NSA_KV_GROUPS = 4
NSA_HPG = NSA_HEADS // NSA_KV_GROUPS
NSA_GROUP_WIDTH = NSA_HPG * NSA_HEAD_DIM
NSA_KV_WIDTH = NSA_KV_GROUPS * NSA_HEAD_DIM
CMP_BLOCK = 32
CMP_STRIDE = 16
SEL_BLOCK = 64
N_SELECT = 16
N_FORCED = 3
WINDOW = 512
Q_BLOCK = 256
SEL_KEY_TILE = 512
VALUE_ROWS = NSA_HEAD_DIM + 16
ROPE_THETA = 500000.0
ROPE_DIM = NSA_HEAD_DIM // 4
LOG2_E = 1.4426950408889634
LANES = 128
SUBLANES = 8
VMEM_LIMIT = 56 * 1024 * 1024


def _round_up(n, m):
    return (n + m - 1) // m * m


def _params(*semantics):
    return pltpu.CompilerParams(dimension_semantics=semantics, vmem_limit_bytes=VMEM_LIMIT)


def _rms(x, gain):
    return x * lax.rsqrt(jnp.mean(x * x, axis=-1, keepdims=True) + NORM_EPS) * gain


def _sigmoid(x):
    return 1.0 / (1.0 + jnp.exp(-x))


def _silu(x):
    return x * _sigmoid(x)


def _softplus(x):
    return jnp.maximum(x, 0.0) + jnp.log(1.0 + jnp.exp(-jnp.abs(x)))


def _dot(a, b):
    return jnp.dot(a.astype(BF16), b.astype(BF16), preferred_element_type=F32)


def _dot_nt(a, b):
    return lax.dot_general(a.astype(BF16), b.astype(BF16), (((1,), (1,)), ((), ())),
                           preferred_element_type=F32)


def _dot_f32(a, b):
    return jnp.dot(a, b, precision=HIGHEST, preferred_element_type=F32)


def _norm_matmul_kernel(x_ref, g_ref, w_ref, o_ref):
    h = _rms(x_ref[...], g_ref[...])
    o_ref[...] = jnp.dot(h.astype(BF16), w_ref[...], preferred_element_type=F32).astype(o_ref.dtype)


def _norm_matmul(x, gain, w, tm=512):
    m, d = x.shape
    n = w.shape[1]
    return pl.pallas_call(
        _norm_matmul_kernel,
        grid=(m // tm,),
        in_specs=[pl.BlockSpec((tm, d), lambda i: (i, 0)),
                  pl.BlockSpec((1, d), lambda i: (0, 0)),
                  pl.BlockSpec((d, n), lambda i: (0, 0))],
        out_specs=pl.BlockSpec((tm, n), lambda i: (i, 0)),
        out_shape=jax.ShapeDtypeStruct((m, n), F32),
        compiler_params=_params("parallel"),
        name="norm_proj",
    )(x, gain.reshape(1, d), w)


def _split(a):
    hi = a.astype(BF16)
    return hi, (a - hi.astype(F32)).astype(BF16)


def _dot3(a, b):
    (ah, al), (bh, bl) = a, b
    return (jnp.dot(ah, bh, preferred_element_type=F32) + jnp.dot(ah, bl, preferred_element_type=F32)
            + jnp.dot(al, bh, preferred_element_type=F32))


def _gdn_kernel(qkv_ref, z_ref, ab_ref, cw_ref, gp_ref, on_ref, o_ref, xbuf, act, state):
    c = GDN_CHUNK
    dh = GDN_HEAD_DIM
    pw = 2 * dh
    n_pairs = GDN_HEADS // 2
    n_batch = qkv_ref.shape[0]
    hist = SUBLANES
    chains = [(b, p) for b in range(n_batch) for p in range(n_pairs)]

    @pl.when(pl.program_id(0) == 0)
    def _():
        xbuf[:, 0:hist, :] = jnp.zeros((n_batch, hist, 3 * GDN_WIDTH), F32)
        state[...] = jnp.zeros(state.shape, F32)

    row = lax.broadcasted_iota(jnp.int32, (c, pw), 0)
    lane = lax.broadcasted_iota(jnp.int32, (c, pw), 1)
    first = lane < dh
    col = jnp.where(first, lane, lane - dh)
    incl = row >= col
    strict = row > col
    eye = (row == col).astype(F32)
    diag_blk = strict & ((row // SOLVE_BLOCK) == (col // SOLVE_BLOCK))
    r2 = lax.broadcasted_iota(jnp.int32, (pw, pw), 0)
    c2 = lax.broadcasted_iota(jnp.int32, (pw, pw), 1)
    same_head = (r2 < dh) == (c2 < dh)
    ones_bd = same_head.astype(BF16)
    eye_pw = (r2 == c2).astype(BF16)
    tri = (lax.broadcasted_iota(jnp.int32, (c, c), 0) >= lax.broadcasted_iota(jnp.int32, (c, c), 1)).astype(F32)

    def blockdiag(x):
        return jnp.concatenate([jnp.where(first, x, 0.0), jnp.where(first, 0.0, x)], axis=0)

    def head_sums(x):
        hi, lo = _split(x)
        return jnp.dot(hi, ones_bd, preferred_element_type=F32) + jnp.dot(lo, ones_bd, preferred_element_type=F32)

    def pdot(xs, y):
        return _dot3(xs, _split(blockdiag(y)))

    gcs, betas = [], []
    for b in range(n_batch):
        x = qkv_ref[b]
        xbuf[b, hist:hist + c, :] = x
        y = x * cw_ref[GDN_CONV - 1:GDN_CONV, :]
        for j in range(GDN_CONV - 1):
            shift = GDN_CONV - 1 - j
            y = y + xbuf[b, hist - shift:hist - shift + c, :] * cw_ref[j:j + 1, :]
        xbuf[b, 0:hist, :] = x[c - hist:c, :]
        act[b] = _silu(y)
        ab = ab_ref[b]
        g_all = -jnp.exp(gp_ref[0:1, :]) * _softplus(ab + gp_ref[1:2, :])
        gcs.append(_dot_f32(tri, g_all))
        betas.append(_sigmoid(ab))

    def pair_cols(a, p, offset):
        return jnp.where(first, a[:, offset + 2 * p:offset + 2 * p + 1], a[:, offset + 2 * p + 1:offset + 2 * p + 2])

    scale = dh ** -0.5
    q = [act[b, :, p * pw:(p + 1) * pw] for b, p in chains]
    k = [act[b, :, GDN_WIDTH + p * pw:GDN_WIDTH + (p + 1) * pw] for b, p in chains]
    v = [act[b, :, 2 * GDN_WIDTH + p * pw:2 * GDN_WIDTH + (p + 1) * pw] for b, p in chains]
    q = [x * lax.rsqrt(head_sums(x * x) + NORM_EPS) * scale for x in q]
    k = [x * lax.rsqrt(head_sums(x * x) + NORM_EPS) for x in k]
    gcol = [pair_cols(gcs[b], p, 0) for b, p in chains]
    bcol = [pair_cols(betas[b], p, GDN_HEADS) for b, p in chains]
    grow = [jnp.sum(g * eye, axis=0, keepdims=True) for g in gcol]
    g_last = [g[c - 1:c, :] for g in gcol]
    decay = [jnp.exp(jnp.where(incl, gc_ - gr_, -jnp.inf)) for gc_, gr_ in zip(gcol, grow)]
    kb = [k_ * b_ for k_, b_ in zip(k, bcol)]
    k_bd = [blockdiag(k_).astype(BF16) for k_ in k]
    m_low = [jnp.where(strict, _dot_nt(kb_, kd_) * d_, 0.0) for kb_, kd_, d_ in zip(kb, k_bd, decay)]
    attn = [_dot_nt(q_, kd_) * d_ for q_, kd_, d_ in zip(q, k_bd, decay)]

    d_pow = [jnp.where(diag_blk, m_, 0.0) for m_ in m_low]
    l_mat = [m_ - d_ for m_, d_ in zip(m_low, d_pow)]
    p_inv = [eye - d_ for d_ in d_pow]
    for _ in range(int(np.log2(SOLVE_BLOCK)) - 1):
        d_pow = [pdot(_split(d_), d_) for d_ in d_pow]
        p_inv = [p_ + pdot(_split(p_), d_) for p_, d_ in zip(p_inv, d_pow)]
    n_pow = [pdot(_split(p_), l_) for p_, l_ in zip(p_inv, l_mat)]
    q_inv = [eye - n_ for n_ in n_pow]
    for _ in range(int(np.log2(c // SOLVE_BLOCK)) - 1):
        n_pow = [pdot(_split(n_), n_) for n_ in n_pow]
        q_inv = [q_ + pdot(_split(q_), n_) for q_, n_ in zip(q_inv, n_pow)]
    a_inv = [pdot(_split(q_), p_) for q_, p_ in zip(q_inv, p_inv)]
    u = [_dot(a_, blockdiag(v_ * b_)) for a_, v_, b_ in zip(a_inv, v, bcol)]
    w = [_dot(a_, blockdiag(kb_ * jnp.exp(g_))) for a_, kb_, g_ in zip(a_inv, kb, gcol)]

    s_old = [state[b * n_pairs + p] for b, p in chains]
    v_new = [u_ - _dot(w_, s_) for u_, w_, s_ in zip(u, w, s_old)]
    o = [_dot(q_ * jnp.exp(g_), s_) + _dot(a_, blockdiag(vn_))
         for q_, g_, s_, a_, vn_ in zip(q, gcol, s_old, attn, v_new)]
    k_dec_t = [_dot_nt(eye_pw, k_ * jnp.exp(gl_ - g_)) for k_, gl_, g_ in zip(k, g_last, gcol)]
    for (b, p), s_, gl_, kt_, vn_ in zip(chains, s_old, g_last, k_dec_t, v_new):
        state[b * n_pairs + p] = s_ * jnp.exp(gl_) + jnp.where(same_head, _dot(kt_, vn_), 0.0)

    inv_dh = 1.0 / dh
    for (b, p), o_ in zip(chains, o):
        zp = z_ref[b, :, p * pw:(p + 1) * pw]
        y = o_ * lax.rsqrt(head_sums(o_ * o_) * inv_dh + NORM_EPS) * on_ref[...]
        o_ref[b, :, p * pw:(p + 1) * pw] = (y * _silu(zp)).astype(o_ref.dtype)


def _gdn(proj, conv_w, gate_params, o_norm, batch, t_len):
    c = GDN_CHUNK
    n_chunks = t_len // c
    w3 = 3 * GDN_WIDTH
    ab_col = (4 * GDN_WIDTH + 3 * GDN_WIDTH) // LANES
    proj = proj.reshape(batch, t_len, proj.shape[-1])
    on_pair = jnp.tile(o_norm.reshape(1, GDN_HEAD_DIM), (1, 2))
    out = pl.pallas_call(
        _gdn_kernel,
        grid=(n_chunks,),
        in_specs=[pl.BlockSpec((batch, c, w3), lambda i: (0, i, 0)),
                  pl.BlockSpec((batch, c, GDN_WIDTH), lambda i: (0, i, 3)),
                  pl.BlockSpec((batch, c, LANES), lambda i: (0, i, ab_col)),
                  pl.BlockSpec((GDN_CONV, w3), lambda i: (0, 0)),
                  pl.BlockSpec((SUBLANES, LANES), lambda i: (0, 0)),
                  pl.BlockSpec((1, 2 * GDN_HEAD_DIM), lambda i: (0, 0))],
        out_specs=pl.BlockSpec((batch, c, GDN_WIDTH), lambda i: (0, i, 0)),
        out_shape=jax.ShapeDtypeStruct((batch, t_len, GDN_WIDTH), BF16),
        scratch_shapes=[pltpu.VMEM((batch, c + SUBLANES, w3), F32),
                        pltpu.VMEM((batch, c, w3), F32),
                        pltpu.VMEM((batch * GDN_HEADS // 2, 2 * GDN_HEAD_DIM, 2 * GDN_HEAD_DIM), F32)],
        compiler_params=_params("arbitrary"),
        name="gdn",
    )(proj, proj, proj, conv_w, gate_params, on_pair)
    return out.reshape(batch * t_len, GDN_WIDTH)


def _even_out_kernel(tiles_per_batch, x_ref, ya_ref, bg_ref, cg_ref, hs_ref, cgp_ref, hsp_ref,
                     cw_ref, wa_ref, wb_ref, o_ref, ubuf):
    tm = x_ref.shape[0]
    hist = SUBLANES
    first = (pl.program_id(0) % tiles_per_batch) == 0
    prev = cgp_ref[...] * hsp_ref[...]
    ubuf[0:hist, :] = jnp.where(first, 0.0, prev)
    u = cg_ref[...] * hs_ref[...]
    ubuf[hist:hist + tm, :] = u
    conv = u * cw_ref[SC_CONV - 1:SC_CONV, :]
    for j in range(SC_CONV - 1):
        shift = SC_CONV - 1 - j
        conv = conv + ubuf[hist - shift:hist - shift + tm, :] * cw_ref[j:j + 1, :]
    yb = bg_ref[...] * conv
    acc = jnp.dot(ya_ref[...], wa_ref[...], preferred_element_type=F32)
    acc = acc + jnp.dot(yb.astype(BF16), wb_ref[...], preferred_element_type=F32)
    o_ref[...] = x_ref[...] + acc


def _even_out(x, ya, proj, sc_conv, w_a, w_b, t_len, tm=512):
    m, d = x.shape
    wd = GDN_WIDTH
    tiles_per_batch = t_len // tm
    hb = tm // SUBLANES
    prev = lambda col: pl.BlockSpec((SUBLANES, wd), lambda i: (jnp.maximum(i * hb - 1, 0), col))
    cur = lambda col: pl.BlockSpec((tm, wd), lambda i: (i, col))
    return pl.pallas_call(
        functools.partial(_even_out_kernel, tiles_per_batch),
        grid=(m // tm,),
        in_specs=[pl.BlockSpec((tm, d), lambda i: (i, 0)),
                  pl.BlockSpec((tm, wd), lambda i: (i, 0)),
                  cur(4), cur(5), cur(6), prev(5), prev(6),
                  pl.BlockSpec((SC_CONV, wd), lambda i: (0, 0)),
                  pl.BlockSpec((wd, d), lambda i: (0, 0)),
                  pl.BlockSpec((wd, d), lambda i: (0, 0))],
        out_specs=pl.BlockSpec((tm, d), lambda i: (i, 0)),
        out_shape=jax.ShapeDtypeStruct((m, d), F32),
        scratch_shapes=[pltpu.VMEM((tm + SUBLANES, wd), F32)],
        compiler_params=_params("parallel"),
        name="even_out",
    )(x, ya, proj, proj, proj, proj, proj, sc_conv, w_a, w_b)


def _mlp_kernel(final, x_ref, g_ref, wu_ref, wd_ref, fg_ref, o_ref, hn, acc):
    j = pl.program_id(1)

    @pl.when(j == 0)
    def _():
        hn[...] = _rms(x_ref[...], g_ref[...]).astype(BF16)
        acc[...] = jnp.zeros(acc.shape, F32)

    u = jnp.dot(hn[...], wu_ref[...], preferred_element_type=F32)
    u = jnp.square(jnp.maximum(u, 0.0))
    acc[...] += jnp.dot(u.astype(BF16), wd_ref[...], preferred_element_type=F32)

    @pl.when(j == pl.num_programs(1) - 1)
    def _():
        y = x_ref[...] + acc[...]
        if final:
            y = _rms(y, fg_ref[...])
        o_ref[...] = y


def _mlp(x, gain, w_up, w_down, final_gain, final, tm=1024, tf=1024):
    m, d = x.shape
    ff = w_up.shape[1]
    return pl.pallas_call(
        functools.partial(_mlp_kernel, final),
        grid=(m // tm, ff // tf),
        in_specs=[pl.BlockSpec((tm, d), lambda i, j: (i, 0)),
                  pl.BlockSpec((1, d), lambda i, j: (0, 0)),
                  pl.BlockSpec((d, tf), lambda i, j: (0, j)),
                  pl.BlockSpec((tf, d), lambda i, j: (j, 0)),
                  pl.BlockSpec((1, d), lambda i, j: (0, 0))],
        out_specs=pl.BlockSpec((tm, d), lambda i, j: (i, 0)),
        out_shape=jax.ShapeDtypeStruct((m, d), F32),
        scratch_shapes=[pltpu.VMEM((tm, d), BF16), pltpu.VMEM((tm, d), F32)],
        compiler_params=_params("parallel", "arbitrary"),
        name="mlp",
    )(x, gain.reshape(1, d), w_up, w_down, final_gain.reshape(1, d))


def _rope(x, cos_t, sin_lo, sin_hi):
    half = ROPE_DIM // 2
    outs = []
    for j in range(x.shape[1] // LANES):
        xs = x[:, j * LANES:(j + 1) * LANES]
        up = pltpu.roll(xs, LANES - half, axis=1)
        down = pltpu.roll(xs, half, axis=1)
        outs.append(xs * cos_t + up * sin_lo + down * sin_hi)
    return jnp.concatenate(outs, axis=-1)


def _odd_proj_kernel(x_ref, g_ref, w_ref, cos_ref, slo_ref, shi_ref,
                     qn_ref, qr_ref, kvc_ref, ks_ref, vs_ref, kw_ref, vw_ref, gl_ref):
    h = _rms(x_ref[...], g_ref[...])
    y = jnp.dot(h.astype(BF16), w_ref[...], preferred_element_type=F32)
    cos_t, sin_lo, sin_hi = cos_ref[...], slo_ref[...], shi_ref[...]
    dq = NSA_HEADS * NSA_HEAD_DIM
    kv = NSA_KV_WIDTH
    q = y[:, :dq] * (LOG2_E * NSA_HEAD_DIM ** -0.5)
    qn_ref[...] = q.astype(BF16)
    qr_ref[...] = _rope(q, cos_t, sin_lo, sin_hi).astype(BF16)
    kvc_ref[...] = y[:, dq:dq + 2 * kv].astype(BF16)
    ks_ref[...] = _rope(y[:, dq + 2 * kv:dq + 3 * kv], cos_t, sin_lo, sin_hi).astype(BF16)
    vs_ref[...] = y[:, dq + 3 * kv:dq + 4 * kv].astype(BF16)
    kw_ref[...] = _rope(y[:, dq + 4 * kv:dq + 5 * kv], cos_t, sin_lo, sin_hi).astype(BF16)
    vw_ref[...] = y[:, dq + 5 * kv:dq + 6 * kv].astype(BF16)
    gl_ref[...] = y[:, dq + 6 * kv:]


def _odd_proj(x, gain, w, tables, t_len, tm=512):
    m, d = x.shape
    n = w.shape[1]
    dq = NSA_HEADS * NSA_HEAD_DIM
    kv = NSA_KV_WIDTH
    ng = n - dq - 6 * kv
    tpb = t_len // tm
    row = lambda width: pl.BlockSpec((tm, width), lambda i: (i, 0))
    tab = pl.BlockSpec((tm, LANES), lambda i: (i % tpb, 0))
    shapes = [(dq, BF16), (dq, BF16), (2 * kv, BF16), (kv, BF16), (kv, BF16), (kv, BF16), (kv, BF16),
              (ng, F32)]
    return pl.pallas_call(
        _odd_proj_kernel,
        grid=(m // tm,),
        in_specs=[row(d), pl.BlockSpec((1, d), lambda i: (0, 0)),
                  pl.BlockSpec((d, n), lambda i: (0, 0)), tab, tab, tab],
        out_specs=[row(width) for width, _ in shapes],
        out_shape=[jax.ShapeDtypeStruct((m, width), dt) for width, dt in shapes],
        compiler_params=_params("parallel"),
        name="odd_proj",
    )(x, gain.reshape(1, d), w, *tables)


def _compress_kernel(xa_ref, xb_ref, pos_ref, a1_ref, a2t_ref, b1_ref, b2_ref, at_ref, b_ref):
    half = xa_ref.shape[2]
    n = xa_ref.shape[1]

    def hidden(x_ref, w1_ref):
        x = x_ref[0]
        top = jnp.dot(x, w1_ref[0:half, :], preferred_element_type=F32)
        bot = jnp.dot(x, w1_ref[half:2 * half, :], preferred_element_type=F32)
        bias = (jnp.dot(pos_ref[:, 0:half], w1_ref[0:half, :], preferred_element_type=F32)
                + jnp.dot(pos_ref[:, half:2 * half], w1_ref[half:2 * half, :], preferred_element_type=F32))
        nxt = pltpu.roll(bot, n - 1, axis=0)
        return _silu(top + nxt + bias[0:1, :]).astype(BF16)

    at_ref[0] = lax.dot_general(a2t_ref[...], hidden(xa_ref, a1_ref), (((1,), (1,)), ((), ())),
                                preferred_element_type=F32).astype(BF16)
    b_ref[0] = jnp.dot(hidden(xb_ref, b1_ref), b2_ref[...], preferred_element_type=F32).astype(BF16)


def _compress(xa, xb, pos, a1, a2t, b1, b2):
    bg, n, half = xa.shape
    dh = NSA_HEAD_DIM
    full = lambda a: pl.BlockSpec(a.shape, lambda i: (0,) * a.ndim)
    return pl.pallas_call(
        _compress_kernel,
        grid=(bg,),
        in_specs=[pl.BlockSpec((1, n, half), lambda i: (i, 0, 0)),
                  pl.BlockSpec((1, n, half), lambda i: (i, 0, 0)),
                  full(pos), full(a1), full(a2t), full(b1), full(b2)],
        out_specs=[pl.BlockSpec((1, dh, n), lambda i: (i, 0, 0)),
                   pl.BlockSpec((1, n, dh), lambda i: (i, 0, 0))],
        out_shape=[jax.ShapeDtypeStruct((bg, dh, n), BF16), jax.ShapeDtypeStruct((bg, n, dh), BF16)],
        compiler_params=_params("parallel"),
        name="compress",
    )(xa, xb, pos, a1, a2t, b1, b2)


MASKED = -1e30


def _nsa_kernel(qn_ref, qr_ref, gl_ref, kc_ref, vo_ref, ks_ref, vst_ref, kw_ref, vwt_ref, expt_ref,
                o_ref, s_even, s_odd):
    qb = Q_BLOCK
    dh = NSA_HEAD_DIM
    hpg = NSA_HPG
    n_cmp = kc_ref.shape[1]
    n_blk = expt_ref.shape[1]
    q0 = pl.program_id(2) * qb
    t_row = q0 + lax.broadcasted_iota(jnp.int32, (1, qb), 1)

    def heads_t(ref):
        xt = ref[...].astype(F32).T
        return jnp.concatenate([xt[h * dh:(h + 1) * dh] for h in range(hpg)], axis=1).astype(BF16)

    qn_t = heads_t(qn_ref)
    qr_t = heads_t(qr_ref)

    def masked(s, mask, fill):
        return jnp.concatenate([jnp.where(mask, s[:, h * qb:(h + 1) * qb], fill) for h in range(hpg)], axis=1)

    def exp2_cols(s):
        m = jnp.max(s, axis=0, keepdims=True)
        return jnp.exp2(s - jnp.where(m == -jnp.inf, 0.0, m))

    cmp_end = lax.broadcasted_iota(jnp.int32, (n_cmp, 1), 0) * CMP_STRIDE + (CMP_BLOCK - 1)
    s_c = jnp.dot(kc_ref[0], qn_t, preferred_element_type=F32)
    e_c = exp2_cols(masked(s_c, cmp_end <= t_row, -jnp.inf)).astype(BF16)
    both = jnp.dot(vo_ref[0], e_c, preferred_element_type=F32)
    acc_c = both[0:VALUE_ROWS]
    den_c = acc_c[dh:dh + 1, :]
    imp_all = both[VALUE_ROWS:VALUE_ROWS + n_blk] * (1.0 / jnp.where(den_c > 0.0, den_c, 1.0))
    imp_t = imp_all[:, 0:qb]
    for h in range(1, hpg):
        imp_t = imp_t + imp_all[:, h * qb:(h + 1) * qb]

    span = WINDOW + qb
    w_start = pl.multiple_of(jnp.maximum(q0 - WINDOW, 0), qb)
    dist = t_row - (w_start + lax.broadcasted_iota(jnp.int32, (span, 1), 0))
    s_w = jnp.dot(kw_ref[0, pl.ds(w_start, span), :], qr_t, preferred_element_type=F32)
    e_w = exp2_cols(masked(s_w, (dist >= 0) & (dist < WINDOW), -jnp.inf))
    acc_w = jnp.dot(vwt_ref[0, :, pl.ds(w_start, span)], e_w.astype(BF16), preferred_element_type=F32)

    js = lax.broadcasted_iota(jnp.int32, (n_blk, qb), 0).astype(F32)
    cur = (t_row // SEL_BLOCK).astype(F32)
    forced = (js == 0.0) | (js == cur) | (js == cur - 1.0)
    val = jnp.where((js > cur) | forced, -jnp.inf, imp_t)
    sel_t = forced.astype(F32)
    for _ in range(N_SELECT - N_FORCED):
        best = jnp.max(val, axis=0, keepdims=True)
        first = jnp.min(jnp.where(val == best, js, float(n_blk)), axis=0, keepdims=True)
        pick = js == first
        sel_t = jnp.where(pick, 1.0, sel_t)
        val = jnp.where(pick, -jnp.inf, val)
    kt = SEL_KEY_TILE
    n_tiles = ks_ref.shape[1] // kt
    first_own = (q0 // SEL_BLOCK).astype(F32)
    chosen = sel_t > 0.5

    def extended(bias):
        return jnp.concatenate([jnp.concatenate([bias.astype(BF16)] * hpg, axis=1), qr_t], axis=0)

    q_ext_t = extended(jnp.where(chosen & (js < first_own), 0.0, MASKED))
    q_own_t = extended(jnp.where(chosen & (js >= first_own) & (js <= cur), 0.0, MASKED))

    def scores(i, s_ref):
        start = pl.multiple_of(jnp.minimum(i, n_tiles - 1) * kt, kt)
        k_ext = jnp.concatenate([expt_ref[pl.ds(start, kt), :], ks_ref[0, pl.ds(start, kt), :]], axis=1)
        s = jnp.dot(k_ext, q_ext_t, preferred_element_type=F32)
        s_ref[...] = s
        return jnp.max(s, axis=0, keepdims=True)

    def accumulate(s, m_tile, v_t, carry):
        m_run, acc = carry
        m_new = jnp.maximum(m_run, m_tile)
        p = jnp.exp2(s - m_new)
        acc = jnp.exp2(m_run - m_new) * acc + jnp.dot(v_t, p.astype(BF16), preferred_element_type=F32)
        return m_new, acc

    def consume(i, s_ref, m_tile, carry):
        start = pl.multiple_of(i * kt, kt)
        return accumulate(s_ref[...], m_tile, vst_ref[0, :, pl.ds(start, kt)], carry)

    def pair_step(j, carry):
        m_run, acc, m_even = carry
        m_odd = scores(2 * j + 1, s_odd)
        m_run, acc = consume(2 * j, s_even, m_even, (m_run, acc))
        m_even = scores(2 * j + 2, s_even)
        m_run, acc = consume(2 * j + 1, s_odd, m_odd, (m_run, acc))
        return m_run, acc, m_even

    cols = hpg * qb
    init = (jnp.full((1, cols), MASKED, F32), jnp.zeros((vst_ref.shape[1], cols), F32), scores(0, s_even))
    n_past = (q0 + kt - 1) // kt
    m_run, acc_s, _ = lax.fori_loop(0, (n_past + 1) // 2, pair_step, init)
    kpos = q0 + lax.broadcasted_iota(jnp.int32, (qb, 1), 0)
    k_own = jnp.concatenate([expt_ref[pl.ds(q0, qb), :], ks_ref[0, pl.ds(q0, qb), :]], axis=1)
    s_own = masked(jnp.dot(k_own, q_own_t, preferred_element_type=F32), kpos <= t_row, MASKED)
    _, acc_s = accumulate(s_own, jnp.max(s_own, axis=0, keepdims=True), vst_ref[0, :, pl.ds(q0, qb)],
                          (m_run, acc_s))

    gates_t = _sigmoid(gl_ref[...].T)
    outs = []
    for h in range(hpg):
        c = slice(h * qb, (h + 1) * qb)
        o_t = jnp.zeros((dh, qb), F32)
        for j, acc in enumerate((acc_c, acc_s, acc_w)):
            den = acc[dh:dh + 1, c]
            o_t = o_t + (gates_t[3 * h + j:3 * h + j + 1, :] / jnp.where(den > 0.0, den, 1.0)) * acc[0:dh, c]
        outs.append(o_t)
    o_ref[...] = jnp.concatenate(outs, axis=0).T.astype(o_ref.dtype)


def _nsa_attention(qn, qr, gl, kc, vo, ks, vst, kw, vwt, expand_t, batch, t_len):
    assert t_len % (2 * SEL_KEY_TILE) == 0, "key tiles of the selected branch are processed in pairs"
    m = qn.shape[0]
    n_g = NSA_KV_GROUPS
    gw = NSA_GROUP_WIDTH
    dh = NSA_HEAD_DIM
    nq = t_len // Q_BLOCK
    n_cmp = kc.shape[1]
    qspec = pl.BlockSpec((Q_BLOCK, gw), lambda b, g, i: (b * nq + i, g))
    rows = lambda n: pl.BlockSpec((1, n, dh), lambda b, g, i: (b * n_g + g, 0, 0))
    cols = lambda n: pl.BlockSpec((1, VALUE_ROWS, n), lambda b, g, i: (b * n_g + g, 0, 0))
    const = lambda a: pl.BlockSpec(a.shape, lambda b, g, i: (0, 0))
    return pl.pallas_call(
        _nsa_kernel,
        grid=(batch, n_g, nq),
        in_specs=[qspec, qspec,
                  pl.BlockSpec((Q_BLOCK, LANES), lambda b, g, i: (b * nq + i, g)),
                  rows(n_cmp), pl.BlockSpec((1,) + vo.shape[1:], lambda b, g, i: (b * n_g + g, 0, 0)),
                  rows(t_len), cols(t_len), rows(t_len), cols(t_len), const(expand_t)],
        out_specs=qspec,
        out_shape=jax.ShapeDtypeStruct((m, n_g * gw), BF16),
        scratch_shapes=[pltpu.VMEM((SEL_KEY_TILE, NSA_HPG * Q_BLOCK), F32)] * 2,
        compiler_params=_params("parallel", "parallel", "arbitrary"),
        name="nsa_attention",
    )(qn, qr, gl, kc, vo, ks, vst, kw, vwt, expand_t)


def _proj_residual_kernel(x_ref, a_ref, w_ref, o_ref):
    o_ref[...] = x_ref[...] + jnp.dot(a_ref[...], w_ref[...], preferred_element_type=F32)


def _proj_residual(x, a, w, tm=512):
    m, d = x.shape
    k = a.shape[1]
    return pl.pallas_call(
        _proj_residual_kernel,
        grid=(m // tm,),
        in_specs=[pl.BlockSpec((tm, d), lambda i: (i, 0)),
                  pl.BlockSpec((tm, k), lambda i: (i, 0)),
                  pl.BlockSpec((k, d), lambda i: (0, 0))],
        out_specs=pl.BlockSpec((tm, d), lambda i: (i, 0)),
        out_shape=jax.ShapeDtypeStruct((m, d), F32),
        compiler_params=_params("parallel"),
        name="proj_residual",
    )(x, a, w)


def _pad_cols(w, n):
    return jnp.pad(w, ((0, 0), (0, n - w.shape[1])))


def _even_mixer(x, gain, w_in, qkv_conv, a_log, dt_bias, o_norm, sc_conv, w_out, batch, t_len):
    gw = GDN_WIDTH
    main = jnp.concatenate([w_in[:, :4 * gw], w_in[:, 4 * gw + 2 * GDN_HEADS:]], axis=1)
    ab = _pad_cols(w_in[:, 4 * gw:4 * gw + 2 * GDN_HEADS], LANES)
    w_all = jnp.concatenate([main, ab], axis=1).astype(BF16)
    proj = _norm_matmul(x, gain, w_all)
    gate_params = jnp.zeros((SUBLANES, LANES), F32)
    gate_params = gate_params.at[0, :GDN_HEADS].set(a_log).at[1, :GDN_HEADS].set(dt_bias)
    ya = _gdn(proj, qkv_conv, gate_params, o_norm, batch, t_len)
    return _even_out(x, ya, proj, sc_conv, w_out[:gw].astype(BF16), w_out[gw:].astype(BF16), t_len)


def _rope_tables(t_len):
    half = ROPE_DIM // 2
    inv_freq = ROPE_THETA ** (-jnp.arange(0, ROPE_DIM, 2, dtype=F32) / ROPE_DIM)
    ang = jnp.arange(t_len, dtype=F32)[:, None] * inv_freq[None, :]
    cos, sin = jnp.cos(ang), jnp.sin(ang)
    pad = NSA_HEAD_DIM - ROPE_DIM
    head = lambda a, b, fill: jnp.concatenate([a, b, jnp.full((t_len, pad), fill, F32)], axis=1)
    zeros = jnp.zeros((t_len, half), F32)
    reps = LANES // NSA_HEAD_DIM
    return (jnp.tile(head(cos, cos, 1.0), (1, reps)),
            jnp.tile(head(-sin, zeros, 0.0), (1, reps)),
            jnp.tile(head(zeros, sin, 0.0), (1, reps)))


def _odd_mixer(x, gain, w_in, cmp_pos, k_w1, k_w2, v_w1, v_w2, w_out, batch, t_len):
    n_g, dh, hpg = NSA_KV_GROUPS, NSA_HEAD_DIM, NSA_HPG
    dq = NSA_HEADS * dh
    kv = NSA_KV_WIDTH
    gate_w = w_in[:, dq + 6 * kv:].reshape(-1, n_g, 3 * hpg)
    gate_w = jnp.pad(gate_w, ((0, 0), (0, 0), (0, LANES - 3 * hpg))).reshape(-1, n_g * LANES)
    w_all = jnp.concatenate([w_in[:, :dq + 6 * kv], gate_w], axis=1).astype(BF16)
    qn, qr, kvc, ks, vs, kw, vw, gl = _odd_proj(x, gain, w_all, _rope_tables(t_len), t_len)

    n_chunk = t_len // CMP_STRIDE

    def chunked(a):
        a = a.reshape(batch, n_chunk, CMP_STRIDE, n_g, dh)
        return jnp.moveaxis(a, 3, 1).reshape(batch * n_g, n_chunk, CMP_STRIDE * dh)

    pos = jnp.zeros((SUBLANES, CMP_BLOCK * dh), F32).at[0].set(cmp_pos.reshape(-1)).astype(BF16)
    vc_t, kc = _compress(chunked(kvc[:, kv:]), chunked(kvc[:, :kv]), pos, v_w1.astype(BF16),
                         v_w2.T.astype(BF16), k_w1.astype(BF16), k_w2.astype(BF16))

    def keys(a):
        return jnp.transpose(a.reshape(batch, t_len, n_g, dh), (0, 2, 1, 3)).reshape(batch * n_g, t_len, dh)

    def with_ones(a):
        n = a.shape[-1]
        return jnp.concatenate([a, jnp.ones((a.shape[0], 1, n), a.dtype),
                                jnp.zeros((a.shape[0], VALUE_ROWS - dh - 1, n), a.dtype)], axis=1)

    def values_t(a):
        a = jnp.transpose(a.reshape(batch, t_len, n_g, dh), (0, 2, 3, 1)).reshape(batch * n_g, dh, t_len)
        return with_ones(a)

    n_blk = _round_up(t_len // SEL_BLOCK, LANES)
    c_start = np.arange(n_chunk)[None, :] * CMP_STRIDE
    s_start = np.arange(n_blk)[:, None] * SEL_BLOCK
    overlap_t = jnp.asarray((c_start < s_start + SEL_BLOCK) & (c_start + CMP_BLOCK > s_start), BF16)
    expand_t = jnp.asarray((np.arange(t_len)[:, None] // SEL_BLOCK) == np.arange(n_blk)[None, :], BF16)
    vo = jnp.concatenate([with_ones(vc_t), jnp.broadcast_to(overlap_t, (batch * n_g,) + overlap_t.shape)], axis=1)
    o = _nsa_attention(qn, qr, gl, kc, vo, keys(ks), values_t(vs), keys(kw), values_t(vw), expand_t,
                       batch, t_len)
    return _proj_residual(x, o, w_out.astype(BF16))


def kernel(x, mix_norm, mlp_norm, w_up, w_down, final_norm, ev_w_in, ev_qkv_conv, ev_a_log, ev_dt_bias,
           ev_o_norm, ev_sc_conv, ev_w_out, od_w_in, od_cmp_pos, od_cmp_k_w1, od_cmp_k_w2, od_cmp_v_w1,
           od_cmp_v_w2, od_w_out):
    batch, t_len, d = x.shape
    depth = mix_norm.shape[0]
    xs = x.reshape(batch * t_len, d)
    for layer in range(depth):
        i = layer // 2
        if layer % 2 == 0:
            xs = _even_mixer(xs, mix_norm[layer], ev_w_in[i], ev_qkv_conv[i], ev_a_log[i], ev_dt_bias[i],
                             ev_o_norm[i], ev_sc_conv[i], ev_w_out[i], batch, t_len)
        else:
            xs = _odd_mixer(xs, mix_norm[layer], od_w_in[i], od_cmp_pos[i], od_cmp_k_w1[i], od_cmp_k_w2[i],
                            od_cmp_v_w1[i], od_cmp_v_w2[i], od_w_out[i], batch, t_len)
        xs = _mlp(xs, mlp_norm[layer], w_up[layer].astype(BF16), w_down[layer].astype(BF16), final_norm,
                  final=(layer == depth - 1))
    return xs.reshape(batch, t_len, d)
```

```python
import functools

import jax
import jax.numpy as jnp
import numpy as np
from jax import lax
from jax.experimental import pallas as pl
from jax.experimental.pallas import tpu as pltpu

F32 = jnp.float32
BF16 = jnp.bfloat16
HIGHEST = lax.Precision.HIGHEST

NORM_EPS = 1e-6
GDN_HEADS = 8
GDN_HEAD_DIM = 64
GDN_WIDTH = GDN_HEADS * GDN_HEAD_DIM
GDN_CONV = 4
GDN_CHUNK = 64
SOLVE_BLOCK = 16
SC_CONV = 3
NSA_HEADS = 16
NSA_HEAD_DIM = 64
NSA_KV_GROUPS = 4
NSA_HPG = NSA_HEADS // NSA_KV_GROUPS
NSA_GROUP_WIDTH = NSA_HPG * NSA_HEAD_DIM
NSA_KV_WIDTH = NSA_KV_GROUPS * NSA_HEAD_DIM
CMP_BLOCK = 32
CMP_STRIDE = 16
SEL_BLOCK = 64
N_SELECT = 16
N_FORCED = 3
WINDOW = 512
Q_BLOCK = 256
SEL_KEY_TILE = 512
VALUE_ROWS = NSA_HEAD_DIM + 16
ROPE_THETA = 500000.0
ROPE_DIM = NSA_HEAD_DIM // 4
LOG2_E = 1.4426950408889634
LANES = 128
SUBLANES = 8
BF16_ROWS = 16
VMEM_LIMIT = 56 * 1024 * 1024


def _round_up(n, m):
    return (n + m - 1) // m * m


def _params(*semantics):
    return pltpu.CompilerParams(dimension_semantics=semantics, vmem_limit_bytes=VMEM_LIMIT)


def _rms(x, gain):
    return x * lax.rsqrt(jnp.mean(x * x, axis=-1, keepdims=True) + NORM_EPS) * gain


def _sigmoid(x):
    return 1.0 / (1.0 + jnp.exp(-x))


def _silu(x):
    return x * _sigmoid(x)


def _softplus(x):
    return jnp.maximum(x, 0.0) + jnp.log(1.0 + jnp.exp(-jnp.abs(x)))


def _dot(a, b):
    return jnp.dot(a.astype(BF16), b.astype(BF16), preferred_element_type=F32)


def _dot_nt(a, b):
    return lax.dot_general(a.astype(BF16), b.astype(BF16), (((1,), (1,)), ((), ())),
                           preferred_element_type=F32)


def _dot_f32(a, b):
    return jnp.dot(a, b, precision=HIGHEST, preferred_element_type=F32)


def _norm_matmul_kernel(x_ref, g_ref, w_ref, o_ref, ob_ref):
    h = _rms(x_ref[...], g_ref[...])
    y = jnp.dot(h.astype(BF16), w_ref[...], preferred_element_type=F32)
    n = o_ref.shape[1]
    o_ref[...] = y[:, :n]
    ob_ref[...] = y[:, n:].astype(BF16)


def _norm_matmul(x, gain, w, n_f32, tm=512):
    m, d = x.shape
    n = w.shape[1]
    return pl.pallas_call(
        _norm_matmul_kernel,
        grid=(m // tm,),
        in_specs=[pl.BlockSpec((tm, d), lambda i: (i, 0)),
                  pl.BlockSpec((1, d), lambda i: (0, 0)),
                  pl.BlockSpec((d, n), lambda i: (0, 0))],
        out_specs=[pl.BlockSpec((tm, n_f32), lambda i: (i, 0)), pl.BlockSpec((tm, n - n_f32), lambda i: (i, 0))],
        out_shape=[jax.ShapeDtypeStruct((m, n_f32), F32), jax.ShapeDtypeStruct((m, n - n_f32), BF16)],
        compiler_params=_params("parallel"),
        name="norm_proj",
    )(x, gain.reshape(1, d), w)


def _split(a):
    hi = a.astype(BF16)
    return hi, (a - hi.astype(F32)).astype(BF16)


def _dot3(a, b):
    (ah, al), (bh, bl) = a, b
    return (jnp.dot(ah, bh, preferred_element_type=F32) + jnp.dot(ah, bl, preferred_element_type=F32)
            + jnp.dot(al, bh, preferred_element_type=F32))


def _gdn_kernel(qkv_ref, z_ref, ab_ref, cw_ref, gp_ref, on_ref, o_ref, xbuf, act, state):
    c = GDN_CHUNK
    dh = GDN_HEAD_DIM
    pw = 2 * dh
    n_pairs = GDN_HEADS // 2
    n_batch = qkv_ref.shape[0]
    hist = SUBLANES
    chains = [(b, p) for b in range(n_batch) for p in range(n_pairs)]

    @pl.when(pl.program_id(0) == 0)
    def _():
        xbuf[:, 0:hist, :] = jnp.zeros((n_batch, hist, 3 * GDN_WIDTH), F32)
        state[...] = jnp.zeros(state.shape, F32)

    row = lax.broadcasted_iota(jnp.int32, (c, pw), 0)
    lane = lax.broadcasted_iota(jnp.int32, (c, pw), 1)
    first = lane < dh
    col = jnp.where(first, lane, lane - dh)
    incl = row >= col
    strict = row > col
    eye = (row == col).astype(F32)
    diag_blk = strict & ((row // SOLVE_BLOCK) == (col // SOLVE_BLOCK))
    r2 = lax.broadcasted_iota(jnp.int32, (pw, pw), 0)
    c2 = lax.broadcasted_iota(jnp.int32, (pw, pw), 1)
    same_head = (r2 < dh) == (c2 < dh)
    ones_bd = same_head.astype(BF16)
    eye_pw = (r2 == c2).astype(BF16)
    tri = (lax.broadcasted_iota(jnp.int32, (c, c), 0) >= lax.broadcasted_iota(jnp.int32, (c, c), 1)).astype(F32)

    def blockdiag(x):
        return jnp.concatenate([jnp.where(first, x, 0.0), jnp.where(first, 0.0, x)], axis=0)

    def head_sums(x):
        hi, lo = _split(x)
        return jnp.dot(hi, ones_bd, preferred_element_type=F32) + jnp.dot(lo, ones_bd, preferred_element_type=F32)

    def pdot(xs, y):
        return _dot3(xs, _split(blockdiag(y)))

    gcs, betas = [], []
    for b in range(n_batch):
        x = qkv_ref[b]
        xbuf[b, hist:hist + c, :] = x
        y = x * cw_ref[GDN_CONV - 1:GDN_CONV, :]
        for j in range(GDN_CONV - 1):
            shift = GDN_CONV - 1 - j
            y = y + xbuf[b, hist - shift:hist - shift + c, :] * cw_ref[j:j + 1, :]
        xbuf[b, 0:hist, :] = x[c - hist:c, :]
        act[b] = _silu(y)
        ab = ab_ref[b]
        g_all = -jnp.exp(gp_ref[0:1, :]) * _softplus(ab + gp_ref[1:2, :])
        gcs.append(_dot_f32(tri, g_all))
        betas.append(_sigmoid(ab))

    def pair_cols(a, p, offset):
        return jnp.where(first, a[:, offset + 2 * p:offset + 2 * p + 1], a[:, offset + 2 * p + 1:offset + 2 * p + 2])

    scale = dh ** -0.5
    q = [act[b, :, p * pw:(p + 1) * pw] for b, p in chains]
    k = [act[b, :, GDN_WIDTH + p * pw:GDN_WIDTH + (p + 1) * pw] for b, p in chains]
    v = [act[b, :, 2 * GDN_WIDTH + p * pw:2 * GDN_WIDTH + (p + 1) * pw] for b, p in chains]
    q = [x * lax.rsqrt(head_sums(x * x) + NORM_EPS) * scale for x in q]
    k = [x * lax.rsqrt(head_sums(x * x) + NORM_EPS) for x in k]
    gcol = [pair_cols(gcs[b], p, 0) for b, p in chains]
    bcol = [pair_cols(betas[b], p, GDN_HEADS) for b, p in chains]
    grow = [jnp.sum(g * eye, axis=0, keepdims=True) for g in gcol]
    g_last = [g[c - 1:c, :] for g in gcol]
    decay = [jnp.exp(jnp.where(incl, gc_ - gr_, -jnp.inf)) for gc_, gr_ in zip(gcol, grow)]
    kb = [k_ * b_ for k_, b_ in zip(k, bcol)]
    k_bd = [blockdiag(k_).astype(BF16) for k_ in k]
    m_low = [jnp.where(strict, _dot_nt(kb_, kd_) * d_, 0.0) for kb_, kd_, d_ in zip(kb, k_bd, decay)]
    attn = [_dot_nt(q_, kd_) * d_ for q_, kd_, d_ in zip(q, k_bd, decay)]

    d_pow = [jnp.where(diag_blk, m_, 0.0) for m_ in m_low]
    l_mat = [m_ - d_ for m_, d_ in zip(m_low, d_pow)]
    p_inv = [eye - d_ for d_ in d_pow]
    for _ in range(int(np.log2(SOLVE_BLOCK)) - 1):
        d_pow = [pdot(_split(d_), d_) for d_ in d_pow]
        p_inv = [p_ + pdot(_split(p_), d_) for p_, d_ in zip(p_inv, d_pow)]
    n_pow = [pdot(_split(p_), l_) for p_, l_ in zip(p_inv, l_mat)]
    q_inv = [eye - n_ for n_ in n_pow]
    for _ in range(int(np.log2(c // SOLVE_BLOCK)) - 1):
        n_pow = [pdot(_split(n_), n_) for n_ in n_pow]
        q_inv = [q_ + pdot(_split(q_), n_) for q_, n_ in zip(q_inv, n_pow)]
    a_inv = [pdot(_split(q_), p_) for q_, p_ in zip(q_inv, p_inv)]
    u = [_dot(a_, blockdiag(v_ * b_)) for a_, v_, b_ in zip(a_inv, v, bcol)]
    w = [_dot(a_, blockdiag(kb_ * jnp.exp(g_))) for a_, kb_, g_ in zip(a_inv, kb, gcol)]

    s_old = [state[b * n_pairs + p] for b, p in chains]
    v_new = [u_ - _dot(w_, s_) for u_, w_, s_ in zip(u, w, s_old)]
    o = [_dot(q_ * jnp.exp(g_), s_) + _dot(a_, blockdiag(vn_))
         for q_, g_, s_, a_, vn_ in zip(q, gcol, s_old, attn, v_new)]
    k_dec_t = [_dot_nt(eye_pw, k_ * jnp.exp(gl_ - g_)) for k_, gl_, g_ in zip(k, g_last, gcol)]
    for (b, p), s_, gl_, kt_, vn_ in zip(chains, s_old, g_last, k_dec_t, v_new):
        state[b * n_pairs + p] = s_ * jnp.exp(gl_) + jnp.where(same_head, _dot(kt_, vn_), 0.0)

    inv_dh = 1.0 / dh
    for (b, p), o_ in zip(chains, o):
        zp = z_ref[b, :, p * pw:(p + 1) * pw].astype(F32)
        y = o_ * lax.rsqrt(head_sums(o_ * o_) * inv_dh + NORM_EPS) * on_ref[...]
        o_ref[b, :, p * pw:(p + 1) * pw] = (y * _silu(zp)).astype(o_ref.dtype)


def _gdn(proj, gates, conv_w, gate_params, o_norm, batch, t_len):
    c = GDN_CHUNK
    n_chunks = t_len // c
    w3 = 3 * GDN_WIDTH
    ab_col = w3 // LANES
    proj = proj.reshape(batch, t_len, proj.shape[-1])
    gates = gates.reshape(batch, t_len, gates.shape[-1])
    on_pair = jnp.tile(o_norm.reshape(1, GDN_HEAD_DIM), (1, 2))
    out = pl.pallas_call(
        _gdn_kernel,
        grid=(n_chunks,),
        in_specs=[pl.BlockSpec((batch, c, w3), lambda i: (0, i, 0)),
                  pl.BlockSpec((batch, c, GDN_WIDTH), lambda i: (0, i, 0)),
                  pl.BlockSpec((batch, c, LANES), lambda i: (0, i, ab_col)),
                  pl.BlockSpec((GDN_CONV, w3), lambda i: (0, 0)),
                  pl.BlockSpec((SUBLANES, LANES), lambda i: (0, 0)),
                  pl.BlockSpec((1, 2 * GDN_HEAD_DIM), lambda i: (0, 0))],
        out_specs=pl.BlockSpec((batch, c, GDN_WIDTH), lambda i: (0, i, 0)),
        out_shape=jax.ShapeDtypeStruct((batch, t_len, GDN_WIDTH), BF16),
        scratch_shapes=[pltpu.VMEM((batch, c + SUBLANES, w3), F32),
                        pltpu.VMEM((batch, c, w3), F32),
                        pltpu.VMEM((batch * GDN_HEADS // 2, 2 * GDN_HEAD_DIM, 2 * GDN_HEAD_DIM), F32)],
        compiler_params=_params("arbitrary"),
        name="gdn",
    )(proj, gates, proj, conv_w, gate_params, on_pair)
    return out.reshape(batch * t_len, GDN_WIDTH)


def _mlp_tail(final, x, g_ref, wu_ref, wd_ref, fg_ref, o_ref):
    u = jnp.dot(_rms(x, g_ref[...]).astype(BF16), wu_ref[...], preferred_element_type=F32)
    u = jnp.square(jnp.maximum(u, 0.0))
    y = x + jnp.dot(u.astype(BF16), wd_ref[...], preferred_element_type=F32)
    if final:
        y = _rms(y, fg_ref[...])
    o_ref[...] = y


def _even_tail_kernel(tiles_per_batch, final, x_ref, ya_ref, bg_ref, cg_ref, hs_ref, cgp_ref, hsp_ref,
                      cw_ref, wa_ref, wb_ref, g_ref, wu_ref, wd_ref, fg_ref, o_ref, ubuf):
    tm = x_ref.shape[0]
    hist = cgp_ref.shape[0]
    first = (pl.program_id(0) % tiles_per_batch) == 0
    prev = cgp_ref[...].astype(F32) * hsp_ref[...].astype(F32)
    ubuf[0:hist, :] = jnp.where(first, 0.0, prev)
    u = cg_ref[...].astype(F32) * hs_ref[...].astype(F32)
    ubuf[hist:hist + tm, :] = u
    conv = u * cw_ref[SC_CONV - 1:SC_CONV, :]
    for j in range(SC_CONV - 1):
        shift = SC_CONV - 1 - j
        conv = conv + ubuf[hist - shift:hist - shift + tm, :] * cw_ref[j:j + 1, :]
    yb = bg_ref[...].astype(F32) * conv
    mix = jnp.dot(ya_ref[...], wa_ref[...], preferred_element_type=F32)
    mix = mix + jnp.dot(yb.astype(BF16), wb_ref[...], preferred_element_type=F32)
    _mlp_tail(final, x_ref[...] + mix, g_ref, wu_ref, wd_ref, fg_ref, o_ref)


def _odd_tail_kernel(final, x_ref, a_ref, w_ref, g_ref, wu_ref, wd_ref, fg_ref, o_ref):
    mix = jnp.dot(a_ref[...], w_ref[...], preferred_element_type=F32)
    _mlp_tail(final, x_ref[...] + mix, g_ref, wu_ref, wd_ref, fg_ref, o_ref)


def _resident(a):
    return pl.BlockSpec(a.shape, lambda i: (0, 0), pipeline_mode=pl.Buffered(1))


def _layer_tail(body, x, mixer_args, mixer_specs, mlp, final, scratch=(), tm=512):
    gain, w_up, w_down, final_gain = mlp
    m, d = x.shape
    row = pl.BlockSpec((tm, d), lambda i: (i, 0))
    vec = pl.BlockSpec((1, d), lambda i: (0, 0))
    return pl.pallas_call(
        functools.partial(body, final),
        grid=(m // tm,),
        in_specs=[row, *mixer_specs, vec, _resident(w_up), _resident(w_down), vec],
        out_specs=row,
        out_shape=jax.ShapeDtypeStruct((m, d), F32),
        scratch_shapes=list(scratch),
        compiler_params=_params("parallel"),
        name="layer_tail",
    )(x, *mixer_args, gain.reshape(1, d), w_up, w_down, final_gain.reshape(1, d))


def _even_tail(x, ya, gates, sc_conv, w_a, w_b, t_len, mlp, final, tm=512):
    wd = GDN_WIDTH
    hist = BF16_ROWS
    hb = tm // hist
    prev = lambda col: pl.BlockSpec((hist, wd), lambda i: (jnp.maximum(i * hb - 1, 0), col))
    cur = lambda col: pl.BlockSpec((tm, wd), lambda i: (i, col))
    specs = [cur(0), cur(1), cur(2), cur(3), prev(2), prev(3), _resident(sc_conv), _resident(w_a), _resident(w_b)]
    return _layer_tail(functools.partial(_even_tail_kernel, t_len // tm), x,
                       (ya, gates, gates, gates, gates, gates, sc_conv, w_a, w_b), specs, mlp, final,
                       scratch=[pltpu.VMEM((tm + hist, wd), F32)], tm=tm)


def _odd_tail(x, a, w, mlp, final, tm=512):
    specs = [pl.BlockSpec((tm, a.shape[1]), lambda i: (i, 0)), _resident(w)]
    return _layer_tail(_odd_tail_kernel, x, (a, w), specs, mlp, final, tm=tm)


def _rope(x, cos_t, sin_lo, sin_hi):
    half = ROPE_DIM // 2
    outs = []
    for j in range(x.shape[1] // LANES):
        xs = x[:, j * LANES:(j + 1) * LANES]
        up = pltpu.roll(xs, LANES - half, axis=1)
        down = pltpu.roll(xs, half, axis=1)
        outs.append(xs * cos_t + up * sin_lo + down * sin_hi)
    return jnp.concatenate(outs, axis=-1)


def _odd_proj_kernel(x_ref, g_ref, w_ref, cos_ref, slo_ref, shi_ref,
                     qn_ref, qr_ref, kvc_ref, ks_ref, vs_ref, kw_ref, vw_ref, gl_ref):
    h = _rms(x_ref[...], g_ref[...])
    y = jnp.dot(h.astype(BF16), w_ref[...], preferred_element_type=F32)
    cos_t, sin_lo, sin_hi = cos_ref[...], slo_ref[...], shi_ref[...]
    dq = NSA_HEADS * NSA_HEAD_DIM
    kv = NSA_KV_WIDTH
    q = y[:, :dq] * (LOG2_E * NSA_HEAD_DIM ** -0.5)
    qn_ref[...] = q.astype(BF16)
    qr_ref[...] = _rope(q, cos_t, sin_lo, sin_hi).astype(BF16)
    kvc_ref[...] = y[:, dq:dq + 2 * kv].astype(BF16)
    dh = NSA_HEAD_DIM
    tm = y.shape[0]
    ones_rows = (lax.broadcasted_iota(jnp.int32, (VALUE_ROWS - dh, tm), 0) == 0).astype(BF16)
    for k_ref, v_ref, col in ((ks_ref, vs_ref, dq + 2 * kv), (kw_ref, vw_ref, dq + 4 * kv)):
        k = _rope(y[:, col:col + kv], cos_t, sin_lo, sin_hi).astype(BF16)
        v_t = y[:, col + kv:col + 2 * kv].T
        for g in range(NSA_KV_GROUPS):
            k_ref[0, g] = k[:, g * dh:(g + 1) * dh]
            v_ref[0, g, 0:dh, :] = v_t[g * dh:(g + 1) * dh].astype(BF16)
            v_ref[0, g, dh:VALUE_ROWS, :] = ones_rows
    gl_ref[...] = y[:, dq + 6 * kv:]


def _odd_proj(x, gain, w, tables, batch, t_len, tm=512):
    m, d = x.shape
    n = w.shape[1]
    n_g, dh = NSA_KV_GROUPS, NSA_HEAD_DIM
    dq = NSA_HEADS * dh
    kv = NSA_KV_WIDTH
    ng = n - dq - 6 * kv
    tpb = t_len // tm
    row = lambda width: pl.BlockSpec((tm, width), lambda i: (i, 0))
    tab = pl.BlockSpec((tm, LANES), lambda i: (i % tpb, 0))
    keys = (pl.BlockSpec((1, n_g, tm, dh), lambda i: (i // tpb, 0, i % tpb, 0)),
            jax.ShapeDtypeStruct((batch, n_g, t_len, dh), BF16))
    values_t = (pl.BlockSpec((1, n_g, VALUE_ROWS, tm), lambda i: (i // tpb, 0, 0, i % tpb)),
                jax.ShapeDtypeStruct((batch, n_g, VALUE_ROWS, t_len), BF16))
    flat = lambda width, dt: (row(width), jax.ShapeDtypeStruct((m, width), dt))
    outs = [flat(dq, BF16), flat(dq, BF16), flat(2 * kv, BF16), keys, values_t, keys, values_t, flat(ng, F32)]
    return pl.pallas_call(
        _odd_proj_kernel,
        grid=(m // tm,),
        in_specs=[row(d), pl.BlockSpec((1, d), lambda i: (0, 0)),
                  pl.BlockSpec((d, n), lambda i: (0, 0)), tab, tab, tab],
        out_specs=[spec for spec, _ in outs],
        out_shape=[shape for _, shape in outs],
        compiler_params=_params("parallel"),
        name="odd_proj",
    )(x, gain.reshape(1, d), w, *tables)


def _compress_kernel(xa_ref, xb_ref, pos_ref, a1_ref, a2t_ref, b1_ref, b2_ref, at_ref, b_ref):
    half = xa_ref.shape[2]
    n = xa_ref.shape[1]

    def hidden(x_ref, w1_ref):
        x = x_ref[0]
        top = jnp.dot(x, w1_ref[0:half, :], preferred_element_type=F32)
        bot = jnp.dot(x, w1_ref[half:2 * half, :], preferred_element_type=F32)
        bias = (jnp.dot(pos_ref[:, 0:half], w1_ref[0:half, :], preferred_element_type=F32)
                + jnp.dot(pos_ref[:, half:2 * half], w1_ref[half:2 * half, :], preferred_element_type=F32))
        nxt = pltpu.roll(bot, n - 1, axis=0)
        return _silu(top + nxt + bias[0:1, :]).astype(BF16)

    at_ref[0] = lax.dot_general(a2t_ref[...], hidden(xa_ref, a1_ref), (((1,), (1,)), ((), ())),
                                preferred_element_type=F32).astype(BF16)
    b_ref[0] = jnp.dot(hidden(xb_ref, b1_ref), b2_ref[...], preferred_element_type=F32).astype(BF16)


def _compress(xa, xb, pos, a1, a2t, b1, b2):
    bg, n, half = xa.shape
    dh = NSA_HEAD_DIM
    full = lambda a: pl.BlockSpec(a.shape, lambda i: (0,) * a.ndim)
    return pl.pallas_call(
        _compress_kernel,
        grid=(bg,),
        in_specs=[pl.BlockSpec((1, n, half), lambda i: (i, 0, 0)),
                  pl.BlockSpec((1, n, half), lambda i: (i, 0, 0)),
                  full(pos), full(a1), full(a2t), full(b1), full(b2)],
        out_specs=[pl.BlockSpec((1, dh, n), lambda i: (i, 0, 0)),
                   pl.BlockSpec((1, n, dh), lambda i: (i, 0, 0))],
        out_shape=[jax.ShapeDtypeStruct((bg, dh, n), BF16), jax.ShapeDtypeStruct((bg, n, dh), BF16)],
        compiler_params=_params("parallel"),
        name="compress",
    )(xa, xb, pos, a1, a2t, b1, b2)


MASKED = -1e30


def _nsa_kernel(qn_ref, qr_ref, gl_ref, kc_ref, vo_ref, ks_ref, vst_ref, kw_ref, vwt_ref, expt_ref,
                o_ref, s_even, s_odd):
    qb = Q_BLOCK
    dh = NSA_HEAD_DIM
    hpg = NSA_HPG
    n_cmp = kc_ref.shape[1]
    n_blk = expt_ref.shape[1]
    q0 = pl.program_id(2) * qb
    t_row = q0 + lax.broadcasted_iota(jnp.int32, (1, qb), 1)

    def heads_t(ref):
        xt = ref[...].astype(F32).T
        return jnp.concatenate([xt[h * dh:(h + 1) * dh] for h in range(hpg)], axis=1).astype(BF16)

    qn_t = heads_t(qn_ref)
    qr_t = heads_t(qr_ref)

    def masked(s, mask, fill):
        return jnp.concatenate([jnp.where(mask, s[:, h * qb:(h + 1) * qb], fill) for h in range(hpg)], axis=1)

    def exp2_cols(s):
        m = jnp.max(s, axis=0, keepdims=True)
        return jnp.exp2(s - jnp.where(m == -jnp.inf, 0.0, m))

    cmp_end = lax.broadcasted_iota(jnp.int32, (n_cmp, 1), 0) * CMP_STRIDE + (CMP_BLOCK - 1)
    s_c = jnp.dot(kc_ref[0], qn_t, preferred_element_type=F32)
    e_c = exp2_cols(masked(s_c, cmp_end <= t_row, -jnp.inf)).astype(BF16)
    both = jnp.dot(vo_ref[0], e_c, preferred_element_type=F32)
    acc_c = both[0:VALUE_ROWS]
    den_c = acc_c[dh:dh + 1, :]
    imp_all = both[VALUE_ROWS:VALUE_ROWS + n_blk] * (1.0 / jnp.where(den_c > 0.0, den_c, 1.0))
    imp_t = imp_all[:, 0:qb]
    for h in range(1, hpg):
        imp_t = imp_t + imp_all[:, h * qb:(h + 1) * qb]

    span = WINDOW + qb
    w_start = pl.multiple_of(jnp.maximum(q0 - WINDOW, 0), qb)
    dist = t_row - (w_start + lax.broadcasted_iota(jnp.int32, (span, 1), 0))
    s_w = jnp.dot(kw_ref[0, 0, pl.ds(w_start, span), :], qr_t, preferred_element_type=F32)
    e_w = exp2_cols(masked(s_w, (dist >= 0) & (dist < WINDOW), -jnp.inf))
    acc_w = jnp.dot(vwt_ref[0, 0, :, pl.ds(w_start, span)], e_w.astype(BF16), preferred_element_type=F32)

    js = lax.broadcasted_iota(jnp.int32, (n_blk, qb), 0).astype(F32)
    cur = (t_row // SEL_BLOCK).astype(F32)
    forced = (js == 0.0) | (js == cur) | (js == cur - 1.0)
    val = jnp.where((js > cur) | forced, -jnp.inf, imp_t)
    sel_t = forced.astype(F32)
    for _ in range(N_SELECT - N_FORCED):
        best = jnp.max(val, axis=0, keepdims=True)
        first = jnp.min(jnp.where(val == best, js, float(n_blk)), axis=0, keepdims=True)
        pick = js == first
        sel_t = jnp.where(pick, 1.0, sel_t)
        val = jnp.where(pick, -jnp.inf, val)
    kt = SEL_KEY_TILE
    n_tiles = ks_ref.shape[2] // kt
    first_own = (q0 // SEL_BLOCK).astype(F32)
    chosen = sel_t > 0.5

    def extended(bias):
        return jnp.concatenate([jnp.concatenate([bias.astype(BF16)] * hpg, axis=1), qr_t], axis=0)

    q_ext_t = extended(jnp.where(chosen & (js < first_own), 0.0, MASKED))
    q_own_t = extended(jnp.where(chosen & (js >= first_own) & (js <= cur), 0.0, MASKED))

    def scores(i, s_ref):
        start = pl.multiple_of(jnp.minimum(i, n_tiles - 1) * kt, kt)
        k_ext = jnp.concatenate([expt_ref[pl.ds(start, kt), :], ks_ref[0, 0, pl.ds(start, kt), :]], axis=1)
        s = jnp.dot(k_ext, q_ext_t, preferred_element_type=F32)
        s_ref[...] = s
        return jnp.max(s, axis=0, keepdims=True)

    def accumulate(s, m_tile, v_t, carry):
        m_run, acc = carry
        m_new = jnp.maximum(m_run, m_tile)
        p = jnp.exp2(s - m_new)
        acc = jnp.exp2(m_run - m_new) * acc + jnp.dot(v_t, p.astype(BF16), preferred_element_type=F32)
        return m_new, acc

    def consume(i, s_ref, m_tile, carry):
        start = pl.multiple_of(i * kt, kt)
        return accumulate(s_ref[...], m_tile, vst_ref[0, 0, :, pl.ds(start, kt)], carry)

    def pair_step(j, carry):
        m_run, acc, m_even = carry
        m_odd = scores(2 * j + 1, s_odd)
        m_run, acc = consume(2 * j, s_even, m_even, (m_run, acc))
        m_even = scores(2 * j + 2, s_even)
        m_run, acc = consume(2 * j + 1, s_odd, m_odd, (m_run, acc))
        return m_run, acc, m_even

    cols = hpg * qb
    init = (jnp.full((1, cols), MASKED, F32), jnp.zeros((VALUE_ROWS, cols), F32), scores(0, s_even))
    n_past = (q0 + kt - 1) // kt
    m_run, acc_s, _ = lax.fori_loop(0, (n_past + 1) // 2, pair_step, init)
    kpos = q0 + lax.broadcasted_iota(jnp.int32, (qb, 1), 0)
    k_own = jnp.concatenate([expt_ref[pl.ds(q0, qb), :], ks_ref[0, 0, pl.ds(q0, qb), :]], axis=1)
    s_own = masked(jnp.dot(k_own, q_own_t, preferred_element_type=F32), kpos <= t_row, MASKED)
    _, acc_s = accumulate(s_own, jnp.max(s_own, axis=0, keepdims=True), vst_ref[0, 0, :, pl.ds(q0, qb)],
                          (m_run, acc_s))

    gates_t = _sigmoid(gl_ref[...].T)
    outs = []
    for h in range(hpg):
        c = slice(h * qb, (h + 1) * qb)
        o_t = jnp.zeros((dh, qb), F32)
        for j, acc in enumerate((acc_c, acc_s, acc_w)):
            den = acc[dh:dh + 1, c]
            o_t = o_t + (gates_t[3 * h + j:3 * h + j + 1, :] / jnp.where(den > 0.0, den, 1.0)) * acc[0:dh, c]
        outs.append(o_t)
    o_ref[...] = jnp.concatenate(outs, axis=0).T.astype(o_ref.dtype)


def _nsa_attention(qn, qr, gl, kc, vo, ks, vst, kw, vwt, expand_t, batch, t_len):
    assert t_len % (2 * SEL_KEY_TILE) == 0, "key tiles of the selected branch are processed in pairs"
    m = qn.shape[0]
    n_g = NSA_KV_GROUPS
    gw = NSA_GROUP_WIDTH
    dh = NSA_HEAD_DIM
    nq = t_len // Q_BLOCK
    n_cmp = kc.shape[1]
    qspec = pl.BlockSpec((Q_BLOCK, gw), lambda b, g, i: (b * nq + i, g))
    keys = pl.BlockSpec((1, 1, t_len, dh), lambda b, g, i: (b, g, 0, 0))
    values_t = pl.BlockSpec((1, 1, VALUE_ROWS, t_len), lambda b, g, i: (b, g, 0, 0))
    const = lambda a: pl.BlockSpec(a.shape, lambda b, g, i: (0, 0))
    return pl.pallas_call(
        _nsa_kernel,
        grid=(batch, n_g, nq),
        in_specs=[qspec, qspec,
                  pl.BlockSpec((Q_BLOCK, LANES), lambda b, g, i: (b * nq + i, g)),
                  pl.BlockSpec((1, n_cmp, dh), lambda b, g, i: (b * n_g + g, 0, 0)),
                  pl.BlockSpec((1,) + vo.shape[1:], lambda b, g, i: (b * n_g + g, 0, 0)),
                  keys, values_t, keys, values_t, const(expand_t)],
        out_specs=qspec,
        out_shape=jax.ShapeDtypeStruct((m, n_g * gw), BF16),
        scratch_shapes=[pltpu.VMEM((SEL_KEY_TILE, NSA_HPG * Q_BLOCK), F32)] * 2,
        compiler_params=_params("parallel", "parallel", "arbitrary"),
        name="nsa_attention",
    )(qn, qr, gl, kc, vo, ks, vst, kw, vwt, expand_t)


def _pad_cols(w, n):
    return jnp.pad(w, ((0, 0), (0, n - w.shape[1])))


def _even_layer(x, gain, w_in, qkv_conv, a_log, dt_bias, o_norm, sc_conv, w_out, batch, t_len, mlp, final):
    gw = GDN_WIDTH
    ab = _pad_cols(w_in[:, 4 * gw:4 * gw + 2 * GDN_HEADS], LANES)
    w_all = jnp.concatenate([w_in[:, :3 * gw], ab, w_in[:, 3 * gw:4 * gw], w_in[:, 4 * gw + 2 * GDN_HEADS:]],
                            axis=1).astype(BF16)
    proj, gates = _norm_matmul(x, gain, w_all, 3 * gw + LANES)
    gate_params = jnp.zeros((SUBLANES, LANES), F32)
    gate_params = gate_params.at[0, :GDN_HEADS].set(a_log).at[1, :GDN_HEADS].set(dt_bias)
    ya = _gdn(proj, gates, qkv_conv, gate_params, o_norm, batch, t_len)
    return _even_tail(x, ya, gates, sc_conv, w_out[:gw].astype(BF16), w_out[gw:].astype(BF16), t_len, mlp, final)


def _rope_tables(t_len):
    half = ROPE_DIM // 2
    inv_freq = ROPE_THETA ** (-jnp.arange(0, ROPE_DIM, 2, dtype=F32) / ROPE_DIM)
    ang = jnp.arange(t_len, dtype=F32)[:, None] * inv_freq[None, :]
    cos, sin = jnp.cos(ang), jnp.sin(ang)
    pad = NSA_HEAD_DIM - ROPE_DIM
    head = lambda a, b, fill: jnp.concatenate([a, b, jnp.full((t_len, pad), fill, F32)], axis=1)
    zeros = jnp.zeros((t_len, half), F32)
    reps = LANES // NSA_HEAD_DIM
    return (jnp.tile(head(cos, cos, 1.0), (1, reps)),
            jnp.tile(head(-sin, zeros, 0.0), (1, reps)),
            jnp.tile(head(zeros, sin, 0.0), (1, reps)))


def _odd_layer(x, gain, w_in, cmp_pos, k_w1, k_w2, v_w1, v_w2, w_out, batch, t_len, mlp, final):
    n_g, dh, hpg = NSA_KV_GROUPS, NSA_HEAD_DIM, NSA_HPG
    dq = NSA_HEADS * dh
    kv = NSA_KV_WIDTH
    gate_w = w_in[:, dq + 6 * kv:].reshape(-1, n_g, 3 * hpg)
    gate_w = jnp.pad(gate_w, ((0, 0), (0, 0), (0, LANES - 3 * hpg))).reshape(-1, n_g * LANES)
    w_all = jnp.concatenate([w_in[:, :dq + 6 * kv], gate_w], axis=1).astype(BF16)
    qn, qr, kvc, ks, vs_t, kw, vw_t, gl = _odd_proj(x, gain, w_all, _rope_tables(t_len), batch, t_len)

    n_chunk = t_len // CMP_STRIDE

    def chunked(a):
        a = a.reshape(batch, n_chunk, CMP_STRIDE, n_g, dh)
        return jnp.moveaxis(a, 3, 1).reshape(batch * n_g, n_chunk, CMP_STRIDE * dh)

    pos = jnp.zeros((SUBLANES, CMP_BLOCK * dh), F32).at[0].set(cmp_pos.reshape(-1)).astype(BF16)
    vc_t, kc = _compress(chunked(kvc[:, kv:]), chunked(kvc[:, :kv]), pos, v_w1.astype(BF16),
                         v_w2.T.astype(BF16), k_w1.astype(BF16), k_w2.astype(BF16))

    def with_ones(a):
        n = a.shape[-1]
        return jnp.concatenate([a, jnp.ones((a.shape[0], 1, n), a.dtype),
                                jnp.zeros((a.shape[0], VALUE_ROWS - dh - 1, n), a.dtype)], axis=1)

    n_blk = _round_up(t_len // SEL_BLOCK, LANES)
    c_start = np.arange(n_chunk)[None, :] * CMP_STRIDE
    s_start = np.arange(n_blk)[:, None] * SEL_BLOCK
    overlap_t = jnp.asarray((c_start < s_start + SEL_BLOCK) & (c_start + CMP_BLOCK > s_start), BF16)
    expand_t = jnp.asarray((np.arange(t_len)[:, None] // SEL_BLOCK) == np.arange(n_blk)[None, :], BF16)
    vo = jnp.concatenate([with_ones(vc_t), jnp.broadcast_to(overlap_t, (batch * n_g,) + overlap_t.shape)], axis=1)
    o = _nsa_attention(qn, qr, gl, kc, vo, ks, vs_t, kw, vw_t, expand_t, batch, t_len)
    return _odd_tail(x, o, w_out.astype(BF16), mlp, final)


def kernel(x, mix_norm, mlp_norm, w_up, w_down, final_norm, ev_w_in, ev_qkv_conv, ev_a_log, ev_dt_bias,
           ev_o_norm, ev_sc_conv, ev_w_out, od_w_in, od_cmp_pos, od_cmp_k_w1, od_cmp_k_w2, od_cmp_v_w1,
           od_cmp_v_w2, od_w_out):
    batch, t_len, d = x.shape
    depth = mix_norm.shape[0]
    xs = x.reshape(batch * t_len, d)
    for layer in range(depth):
        i = layer // 2
        mlp = (mlp_norm[layer], w_up[layer].astype(BF16), w_down[layer].astype(BF16), final_norm)
        final = layer == depth - 1
        if layer % 2 == 0:
            xs = _even_layer(xs, mix_norm[layer], ev_w_in[i], ev_qkv_conv[i], ev_a_log[i], ev_dt_bias[i],
                             ev_o_norm[i], ev_sc_conv[i], ev_w_out[i], batch, t_len, mlp, final)
        else:
            xs = _odd_layer(xs, mix_norm[layer], od_w_in[i], od_cmp_pos[i], od_cmp_k_w1[i], od_cmp_k_w2[i],
                            od_cmp_v_w1[i], od_cmp_v_w2[i], od_w_out[i], batch, t_len, mlp, final)
    return xs.reshape(batch, t_len, d)
```

```python
import functools

import jax
import jax.numpy as jnp
import numpy as np
from jax import lax
from jax.experimental import pallas as pl
from jax.experimental.pallas import tpu as pltpu

F32 = jnp.float32
BF16 = jnp.bfloat16
HIGHEST = lax.Precision.HIGHEST

NORM_EPS = 1e-6
GDN_HEADS = 8
GDN_HEAD_DIM = 64
GDN_WIDTH = GDN_HEADS * GDN_HEAD_DIM
GDN_CONV = 4
GDN_CHUNK = 64
SOLVE_BLOCK = 16
SC_CONV = 3
NSA_HEADS = 16
NSA_HEAD_DIM = 64
NSA_KV_GROUPS = 4
NSA_HPG = NSA_HEADS // NSA_KV_GROUPS
NSA_GROUP_WIDTH = NSA_HPG * NSA_HEAD_DIM
NSA_KV_WIDTH = NSA_KV_GROUPS * NSA_HEAD_DIM
CMP_BLOCK = 32
CMP_STRIDE = 16
SEL_BLOCK = 64
N_SELECT = 16
N_FORCED = 3
WINDOW = 512
Q_BLOCK = 256
SEL_KEY_TILE = 512
VALUE_ROWS = NSA_HEAD_DIM + 16
ROPE_THETA = 500000.0
ROPE_DIM = NSA_HEAD_DIM // 4
LOG2_E = 1.4426950408889634
LANES = 128
SUBLANES = 8
BF16_ROWS = 16
VMEM_LIMIT = 56 * 1024 * 1024


def _round_up(n, m):
    return (n + m - 1) // m * m


def _params(*semantics):
    return pltpu.CompilerParams(dimension_semantics=semantics, vmem_limit_bytes=VMEM_LIMIT)


def _rms(x, gain):
    return x * lax.rsqrt(jnp.mean(x * x, axis=-1, keepdims=True) + NORM_EPS) * gain


def _sigmoid(x):
    return 1.0 / (1.0 + jnp.exp(-x))


def _silu(x):
    return x * _sigmoid(x)


def _softplus(x):
    return jnp.maximum(x, 0.0) + jnp.log(1.0 + jnp.exp(-jnp.abs(x)))


def _dot(a, b):
    return jnp.dot(a.astype(BF16), b.astype(BF16), preferred_element_type=F32)


def _dot_nt(a, b):
    return lax.dot_general(a.astype(BF16), b.astype(BF16), (((1,), (1,)), ((), ())),
                           preferred_element_type=F32)


def _dot_f32(a, b):
    return jnp.dot(a, b, precision=HIGHEST, preferred_element_type=F32)


def _norm_matmul_kernel(x_ref, g_ref, w_ref, o_ref, ob_ref):
    h = _rms(x_ref[...], g_ref[...])
    y = jnp.dot(h.astype(BF16), w_ref[...], preferred_element_type=F32)
    n = o_ref.shape[1]
    o_ref[...] = y[:, :n]
    ob_ref[...] = y[:, n:].astype(BF16)


def _norm_matmul(x, gain, w, n_f32, tm=512):
    m, d = x.shape
    n = w.shape[1]
    return pl.pallas_call(
        _norm_matmul_kernel,
        grid=(m // tm,),
        in_specs=[pl.BlockSpec((tm, d), lambda i: (i, 0)),
                  pl.BlockSpec((1, d), lambda i: (0, 0)),
                  pl.BlockSpec((d, n), lambda i: (0, 0))],
        out_specs=[pl.BlockSpec((tm, n_f32), lambda i: (i, 0)), pl.BlockSpec((tm, n - n_f32), lambda i: (i, 0))],
        out_shape=[jax.ShapeDtypeStruct((m, n_f32), F32), jax.ShapeDtypeStruct((m, n - n_f32), BF16)],
        compiler_params=_params("parallel"),
        name="norm_proj",
    )(x, gain.reshape(1, d), w)


def _split(a):
    hi = a.astype(BF16)
    return hi, (a - hi.astype(F32)).astype(BF16)


def _dot3(a, b):
    (ah, al), (bh, bl) = a, b
    return (jnp.dot(ah, bh, preferred_element_type=F32) + jnp.dot(ah, bl, preferred_element_type=F32)
            + jnp.dot(al, bh, preferred_element_type=F32))


def _gdn_kernel(qkv_ref, z_ref, ab_ref, cw_ref, gp_ref, on_ref, o_ref, xbuf, act, state):
    c = GDN_CHUNK
    dh = GDN_HEAD_DIM
    pw = 2 * dh
    n_pairs = GDN_HEADS // 2
    n_batch = qkv_ref.shape[0]
    hist = SUBLANES
    chains = [(b, p) for b in range(n_batch) for p in range(n_pairs)]

    @pl.when(pl.program_id(0) == 0)
    def _():
        xbuf[:, 0:hist, :] = jnp.zeros((n_batch, hist, 3 * GDN_WIDTH), F32)
        state[...] = jnp.zeros(state.shape, F32)

    row = lax.broadcasted_iota(jnp.int32, (c, pw), 0)
    lane = lax.broadcasted_iota(jnp.int32, (c, pw), 1)
    first = lane < dh
    col = jnp.where(first, lane, lane - dh)
    incl = row >= col
    strict = row > col
    eye = (row == col).astype(F32)
    diag_blk = strict & ((row // SOLVE_BLOCK) == (col // SOLVE_BLOCK))
    r2 = lax.broadcasted_iota(jnp.int32, (pw, pw), 0)
    c2 = lax.broadcasted_iota(jnp.int32, (pw, pw), 1)
    same_head = (r2 < dh) == (c2 < dh)
    ones_bd = same_head.astype(BF16)
    tri = (lax.broadcasted_iota(jnp.int32, (c, c), 0) >= lax.broadcasted_iota(jnp.int32, (c, c), 1)).astype(F32)

    def blockdiag(x):
        zero = jnp.zeros_like(x)
        return jnp.concatenate([jnp.where(first, x, zero), jnp.where(first, zero, x)], axis=0)

    def head_sums(x):
        return jnp.dot(x.astype(BF16), ones_bd, preferred_element_type=F32)

    def pdot(xs, y):
        hi, lo = _split(y)
        return _dot3(xs, (blockdiag(hi), blockdiag(lo)))

    gcs, betas = [], []
    for b in range(n_batch):
        x = qkv_ref[b]
        xbuf[b, hist:hist + c, :] = x
        y = x * cw_ref[GDN_CONV - 1:GDN_CONV, :]
        for j in range(GDN_CONV - 1):
            shift = GDN_CONV - 1 - j
            y = y + xbuf[b, hist - shift:hist - shift + c, :] * cw_ref[j:j + 1, :]
        xbuf[b, 0:hist, :] = x[c - hist:c, :]
        act[b] = _silu(y)
        ab = ab_ref[b]
        g_all = -jnp.exp(gp_ref[0:1, :]) * _softplus(ab + gp_ref[1:2, :])
        gcs.append(_dot_f32(tri, g_all))
        betas.append(_sigmoid(ab))

    def pair_cols(a, p, offset):
        return jnp.where(first, a[:, offset + 2 * p:offset + 2 * p + 1], a[:, offset + 2 * p + 1:offset + 2 * p + 2])

    scale = dh ** -0.5
    q = [act[b, :, p * pw:(p + 1) * pw] for b, p in chains]
    k = [act[b, :, GDN_WIDTH + p * pw:GDN_WIDTH + (p + 1) * pw] for b, p in chains]
    v = [act[b, :, 2 * GDN_WIDTH + p * pw:2 * GDN_WIDTH + (p + 1) * pw] for b, p in chains]
    q = [x * lax.rsqrt(head_sums(x * x) + NORM_EPS) * scale for x in q]
    k = [x * lax.rsqrt(head_sums(x * x) + NORM_EPS) for x in k]
    gcol = [pair_cols(gcs[b], p, 0) for b, p in chains]
    bcol = [pair_cols(betas[b], p, GDN_HEADS) for b, p in chains]
    grow = [jnp.sum(g * eye, axis=0, keepdims=True) for g in gcol]
    g_last = [g[c - 1:c, :] for g in gcol]
    decay = [jnp.exp(jnp.where(incl, gc_ - gr_, -jnp.inf)) for gc_, gr_ in zip(gcol, grow)]
    kb = [k_ * b_ for k_, b_ in zip(k, bcol)]
    k_bd = [blockdiag(k_).astype(BF16) for k_ in k]
    both = [_dot_nt(jnp.concatenate([kb_, q_], axis=0), kd_) for kb_, q_, kd_ in zip(kb, q, k_bd)]
    m_low = [jnp.where(strict, x[0:c] * d_, 0.0) for x, d_ in zip(both, decay)]
    attn = [x[c:2 * c] * d_ for x, d_ in zip(both, decay)]

    d_pow = [jnp.where(diag_blk, m_, 0.0) for m_ in m_low]
    l_mat = [m_ - d_ for m_, d_ in zip(m_low, d_pow)]
    p_inv = [eye - d_ for d_ in d_pow]
    d_pow = [pdot(_split(d_), d_) for d_ in d_pow]
    n_steps = int(np.log2(SOLVE_BLOCK)) - 1
    for step in range(n_steps):
        if step == n_steps - 1:
            p_inv = [p_ + pdot(_split(p_), d_) for p_, d_ in zip(p_inv, d_pow)]
        else:
            prod = [pdot(_split(jnp.concatenate([p_, d_], axis=0)), d_) for p_, d_ in zip(p_inv, d_pow)]
            p_inv = [p_ + x[0:c] for p_, x in zip(p_inv, prod)]
            d_pow = [x[c:2 * c] for x in prod]
    n_pow = [pdot(_split(p_), l_) for p_, l_ in zip(p_inv, l_mat)]
    q_inv = [eye - n_ for n_ in n_pow]
    for _ in range(int(np.log2(c // SOLVE_BLOCK)) - 1):
        n_pow = [pdot(_split(n_), n_) for n_ in n_pow]
        q_inv = [q_ + pdot(_split(q_), n_) for q_, n_ in zip(q_inv, n_pow)]
    a_inv = [pdot(_split(q_), p_) for q_, p_ in zip(q_inv, p_inv)]
    u = [_dot(a_, blockdiag(v_ * b_)) for a_, v_, b_ in zip(a_inv, v, bcol)]
    w = [_dot(a_, blockdiag(kb_ * jnp.exp(g_))) for a_, kb_, g_ in zip(a_inv, kb, gcol)]

    s_old = [state[b * n_pairs + p] for b, p in chains]
    on_s = [_dot(jnp.concatenate([w_, q_ * jnp.exp(g_)], axis=0), s_) for w_, q_, g_, s_ in zip(w, q, gcol, s_old)]
    v_new = [u_ - x[0:c] for u_, x in zip(u, on_s)]
    o = [x[c:2 * c] + _dot(a_, blockdiag(vn_)) for x, a_, vn_ in zip(on_s, attn, v_new)]
    k_dec_t = [(k_ * jnp.exp(gl_ - g_)).T for k_, gl_, g_ in zip(k, g_last, gcol)]
    for (b, p), s_, gl_, kt_, vn_ in zip(chains, s_old, g_last, k_dec_t, v_new):
        state[b * n_pairs + p] = s_ * jnp.exp(gl_) + jnp.where(same_head, _dot(kt_, vn_), 0.0)

    inv_dh = 1.0 / dh
    for (b, p), o_ in zip(chains, o):
        zp = z_ref[b, :, p * pw:(p + 1) * pw].astype(F32)
        y = o_ * lax.rsqrt(head_sums(o_ * o_) * inv_dh + NORM_EPS) * on_ref[...]
        o_ref[b, :, p * pw:(p + 1) * pw] = (y * _silu(zp)).astype(o_ref.dtype)


def _gdn(proj, gates, conv_w, gate_params, o_norm, batch, t_len):
    c = GDN_CHUNK
    n_chunks = t_len // c
    w3 = 3 * GDN_WIDTH
    ab_col = w3 // LANES
    proj = proj.reshape(batch, t_len, proj.shape[-1])
    gates = gates.reshape(batch, t_len, gates.shape[-1])
    on_pair = jnp.tile(o_norm.reshape(1, GDN_HEAD_DIM), (1, 2))
    out = pl.pallas_call(
        _gdn_kernel,
        grid=(n_chunks,),
        in_specs=[pl.BlockSpec((batch, c, w3), lambda i: (0, i, 0)),
                  pl.BlockSpec((batch, c, GDN_WIDTH), lambda i: (0, i, 0)),
                  pl.BlockSpec((batch, c, LANES), lambda i: (0, i, ab_col)),
                  pl.BlockSpec((GDN_CONV, w3), lambda i: (0, 0)),
                  pl.BlockSpec((SUBLANES, LANES), lambda i: (0, 0)),
                  pl.BlockSpec((1, 2 * GDN_HEAD_DIM), lambda i: (0, 0))],
        out_specs=pl.BlockSpec((batch, c, GDN_WIDTH), lambda i: (0, i, 0)),
        out_shape=jax.ShapeDtypeStruct((batch, t_len, GDN_WIDTH), BF16),
        scratch_shapes=[pltpu.VMEM((batch, c + SUBLANES, w3), F32),
                        pltpu.VMEM((batch, c, w3), F32),
                        pltpu.VMEM((batch * GDN_HEADS // 2, 2 * GDN_HEAD_DIM, 2 * GDN_HEAD_DIM), F32)],
        compiler_params=_params("arbitrary"),
        name="gdn",
    )(proj, gates, proj, conv_w, gate_params, on_pair)
    return out.reshape(batch * t_len, GDN_WIDTH)


def _mlp_tail(final, x, g_ref, wu_ref, wd_ref, fg_ref, o_ref):
    u = jnp.dot(_rms(x, g_ref[...]).astype(BF16), wu_ref[...], preferred_element_type=F32)
    u = jnp.square(jnp.maximum(u, 0.0))
    y = x + jnp.dot(u.astype(BF16), wd_ref[...], preferred_element_type=F32)
    if final:
        y = _rms(y, fg_ref[...])
    o_ref[...] = y


def _even_tail_kernel(tiles_per_batch, final, x_ref, ya_ref, bg_ref, cg_ref, hs_ref, cgp_ref, hsp_ref,
                      cw_ref, wa_ref, wb_ref, g_ref, wu_ref, wd_ref, fg_ref, o_ref, ubuf):
    tm = x_ref.shape[0]
    hist = cgp_ref.shape[0]
    first = (pl.program_id(0) % tiles_per_batch) == 0
    prev = cgp_ref[...].astype(F32) * hsp_ref[...].astype(F32)
    ubuf[0:hist, :] = jnp.where(first, 0.0, prev)
    u = cg_ref[...].astype(F32) * hs_ref[...].astype(F32)
    ubuf[hist:hist + tm, :] = u
    conv = u * cw_ref[SC_CONV - 1:SC_CONV, :]
    for j in range(SC_CONV - 1):
        shift = SC_CONV - 1 - j
        conv = conv + ubuf[hist - shift:hist - shift + tm, :] * cw_ref[j:j + 1, :]
    yb = bg_ref[...].astype(F32) * conv
    mix = jnp.dot(ya_ref[...], wa_ref[...], preferred_element_type=F32)
    mix = mix + jnp.dot(yb.astype(BF16), wb_ref[...], preferred_element_type=F32)
    _mlp_tail(final, x_ref[...] + mix, g_ref, wu_ref, wd_ref, fg_ref, o_ref)


def _odd_tail_kernel(final, x_ref, a_ref, w_ref, g_ref, wu_ref, wd_ref, fg_ref, o_ref):
    mix = jnp.dot(a_ref[...], w_ref[...], preferred_element_type=F32)
    _mlp_tail(final, x_ref[...] + mix, g_ref, wu_ref, wd_ref, fg_ref, o_ref)


def _resident(a):
    return pl.BlockSpec(a.shape, lambda i: (0, 0), pipeline_mode=pl.Buffered(1))


def _layer_tail(body, x, mixer_args, mixer_specs, mlp, final, scratch=(), tm=512):
    gain, w_up, w_down, final_gain = mlp
    m, d = x.shape
    row = pl.BlockSpec((tm, d), lambda i: (i, 0))
    vec = pl.BlockSpec((1, d), lambda i: (0, 0))
    return pl.pallas_call(
        functools.partial(body, final),
        grid=(m // tm,),
        in_specs=[row, *mixer_specs, vec, _resident(w_up), _resident(w_down), vec],
        out_specs=row,
        out_shape=jax.ShapeDtypeStruct((m, d), F32),
        scratch_shapes=list(scratch),
        compiler_params=_params("parallel"),
        name="layer_tail",
    )(x, *mixer_args, gain.reshape(1, d), w_up, w_down, final_gain.reshape(1, d))


def _even_tail(x, ya, gates, sc_conv, w_a, w_b, t_len, mlp, final, tm=512):
    wd = GDN_WIDTH
    hist = BF16_ROWS
    hb = tm // hist
    prev = lambda col: pl.BlockSpec((hist, wd), lambda i: (jnp.maximum(i * hb - 1, 0), col))
    cur = lambda col: pl.BlockSpec((tm, wd), lambda i: (i, col))
    specs = [cur(0), cur(1), cur(2), cur(3), prev(2), prev(3), _resident(sc_conv), _resident(w_a), _resident(w_b)]
    return _layer_tail(functools.partial(_even_tail_kernel, t_len // tm), x,
                       (ya, gates, gates, gates, gates, gates, sc_conv, w_a, w_b), specs, mlp, final,
                       scratch=[pltpu.VMEM((tm + hist, wd), F32)], tm=tm)


def _odd_tail(x, a, w, mlp, final, tm=512):
    specs = [pl.BlockSpec((tm, a.shape[1]), lambda i: (i, 0)), _resident(w)]
    return _layer_tail(_odd_tail_kernel, x, (a, w), specs, mlp, final, tm=tm)


def _rope(x, cos_t, sin_lo, sin_hi):
    half = ROPE_DIM // 2
    outs = []
    for j in range(x.shape[1] // LANES):
        xs = x[:, j * LANES:(j + 1) * LANES]
        up = pltpu.roll(xs, LANES - half, axis=1)
        down = pltpu.roll(xs, half, axis=1)
        outs.append(xs * cos_t + up * sin_lo + down * sin_hi)
    return jnp.concatenate(outs, axis=-1)


def _odd_proj_kernel(x_ref, g_ref, w_ref, cos_ref, slo_ref, shi_ref,
                     qn_ref, qr_ref, kvc_ref, ks_ref, vs_ref, kw_ref, vw_ref, gl_ref):
    h = _rms(x_ref[...], g_ref[...])
    y = jnp.dot(h.astype(BF16), w_ref[...], preferred_element_type=F32)
    cos_t, sin_lo, sin_hi = cos_ref[...], slo_ref[...], shi_ref[...]
    dq = NSA_HEADS * NSA_HEAD_DIM
    kv = NSA_KV_WIDTH
    q = y[:, :dq] * (LOG2_E * NSA_HEAD_DIM ** -0.5)
    qn_ref[...] = q.astype(BF16)
    qr_ref[...] = _rope(q, cos_t, sin_lo, sin_hi).astype(BF16)
    kvc_ref[...] = y[:, dq:dq + 2 * kv].astype(BF16)
    dh = NSA_HEAD_DIM
    tm = y.shape[0]
    ones_rows = (lax.broadcasted_iota(jnp.int32, (VALUE_ROWS - dh, tm), 0) == 0).astype(BF16)
    for k_ref, v_ref, col in ((ks_ref, vs_ref, dq + 2 * kv), (kw_ref, vw_ref, dq + 4 * kv)):
        k = _rope(y[:, col:col + kv], cos_t, sin_lo, sin_hi).astype(BF16)
        v_t = y[:, col + kv:col + 2 * kv].T
        for g in range(NSA_KV_GROUPS):
            k_ref[0, g] = k[:, g * dh:(g + 1) * dh]
            v_ref[0, g, 0:dh, :] = v_t[g * dh:(g + 1) * dh].astype(BF16)
            v_ref[0, g, dh:VALUE_ROWS, :] = ones_rows
    gl_ref[...] = y[:, dq + 6 * kv:]


def _odd_proj(x, gain, w, tables, batch, t_len, tm=512):
    m, d = x.shape
    n = w.shape[1]
    n_g, dh = NSA_KV_GROUPS, NSA_HEAD_DIM
    dq = NSA_HEADS * dh
    kv = NSA_KV_WIDTH
    ng = n - dq - 6 * kv
    tpb = t_len // tm
    row = lambda width: pl.BlockSpec((tm, width), lambda i: (i, 0))
    tab = pl.BlockSpec((tm, LANES), lambda i: (i % tpb, 0))
    keys = (pl.BlockSpec((1, n_g, tm, dh), lambda i: (i // tpb, 0, i % tpb, 0)),
            jax.ShapeDtypeStruct((batch, n_g, t_len, dh), BF16))
    values_t = (pl.BlockSpec((1, n_g, VALUE_ROWS, tm), lambda i: (i // tpb, 0, 0, i % tpb)),
                jax.ShapeDtypeStruct((batch, n_g, VALUE_ROWS, t_len), BF16))
    flat = lambda width, dt: (row(width), jax.ShapeDtypeStruct((m, width), dt))
    outs = [flat(dq, BF16), flat(dq, BF16), flat(2 * kv, BF16), keys, values_t, keys, values_t, flat(ng, F32)]
    return pl.pallas_call(
        _odd_proj_kernel,
        grid=(m // tm,),
        in_specs=[row(d), pl.BlockSpec((1, d), lambda i: (0, 0)),
                  pl.BlockSpec((d, n), lambda i: (0, 0)), tab, tab, tab],
        out_specs=[spec for spec, _ in outs],
        out_shape=[shape for _, shape in outs],
        compiler_params=_params("parallel"),
        name="odd_proj",
    )(x, gain.reshape(1, d), w, *tables)


def _compress_kernel(xa_ref, xb_ref, pos_ref, a1_ref, a2t_ref, b1_ref, b2_ref, at_ref, b_ref):
    half = xa_ref.shape[2]
    n = xa_ref.shape[1]

    def hidden(x_ref, w1_ref):
        x = x_ref[0]
        top = jnp.dot(x, w1_ref[0:half, :], preferred_element_type=F32)
        bot = jnp.dot(x, w1_ref[half:2 * half, :], preferred_element_type=F32)
        bias = (jnp.dot(pos_ref[:, 0:half], w1_ref[0:half, :], preferred_element_type=F32)
                + jnp.dot(pos_ref[:, half:2 * half], w1_ref[half:2 * half, :], preferred_element_type=F32))
        nxt = pltpu.roll(bot, n - 1, axis=0)
        return _silu(top + nxt + bias[0:1, :]).astype(BF16)

    at_ref[0] = lax.dot_general(a2t_ref[...], hidden(xa_ref, a1_ref), (((1,), (1,)), ((), ())),
                                preferred_element_type=F32).astype(BF16)
    b_ref[0] = jnp.dot(hidden(xb_ref, b1_ref), b2_ref[...], preferred_element_type=F32).astype(BF16)


def _compress(xa, xb, pos, a1, a2t, b1, b2):
    bg, n, half = xa.shape
    dh = NSA_HEAD_DIM
    full = lambda a: pl.BlockSpec(a.shape, lambda i: (0,) * a.ndim)
    return pl.pallas_call(
        _compress_kernel,
        grid=(bg,),
        in_specs=[pl.BlockSpec((1, n, half), lambda i: (i, 0, 0)),
                  pl.BlockSpec((1, n, half), lambda i: (i, 0, 0)),
                  full(pos), full(a1), full(a2t), full(b1), full(b2)],
        out_specs=[pl.BlockSpec((1, dh, n), lambda i: (i, 0, 0)),
                   pl.BlockSpec((1, n, dh), lambda i: (i, 0, 0))],
        out_shape=[jax.ShapeDtypeStruct((bg, dh, n), BF16), jax.ShapeDtypeStruct((bg, n, dh), BF16)],
        compiler_params=_params("parallel"),
        name="compress",
    )(xa, xb, pos, a1, a2t, b1, b2)


MASKED = -1e30


def _nsa_kernel(qn_ref, qr_ref, gl_ref, kc_ref, vo_ref, ks_ref, vst_ref, kw_ref, vwt_ref, expt_ref,
                o_ref, s_even, s_odd):
    qb = Q_BLOCK
    dh = NSA_HEAD_DIM
    hpg = NSA_HPG
    n_cmp = kc_ref.shape[1]
    n_blk = expt_ref.shape[1]
    q0 = pl.program_id(2) * qb
    t_row = q0 + lax.broadcasted_iota(jnp.int32, (1, qb), 1)

    def heads_t(ref):
        xt = ref[...].astype(F32).T
        return jnp.concatenate([xt[h * dh:(h + 1) * dh] for h in range(hpg)], axis=1).astype(BF16)

    qn_t = heads_t(qn_ref)
    qr_t = heads_t(qr_ref)

    def masked(s, mask, fill):
        return jnp.concatenate([jnp.where(mask, s[:, h * qb:(h + 1) * qb], fill) for h in range(hpg)], axis=1)

    def exp2_cols(s):
        m = jnp.max(s, axis=0, keepdims=True)
        return jnp.exp2(s - jnp.where(m == -jnp.inf, 0.0, m))

    cmp_end = lax.broadcasted_iota(jnp.int32, (n_cmp, 1), 0) * CMP_STRIDE + (CMP_BLOCK - 1)
    s_c = jnp.dot(kc_ref[0], qn_t, preferred_element_type=F32)
    e_c = exp2_cols(masked(s_c, cmp_end <= t_row, -jnp.inf)).astype(BF16)
    both = jnp.dot(vo_ref[0], e_c, preferred_element_type=F32)
    acc_c = both[0:VALUE_ROWS]
    den_c = acc_c[dh:dh + 1, :]
    imp_all = both[VALUE_ROWS:VALUE_ROWS + n_blk] * (1.0 / jnp.where(den_c > 0.0, den_c, 1.0))
    imp_t = imp_all[:, 0:qb]
    for h in range(1, hpg):
        imp_t = imp_t + imp_all[:, h * qb:(h + 1) * qb]

    span = WINDOW + qb
    w_start = pl.multiple_of(jnp.maximum(q0 - WINDOW, 0), qb)
    dist = t_row - (w_start + lax.broadcasted_iota(jnp.int32, (span, 1), 0))
    s_w = jnp.dot(kw_ref[0, 0, pl.ds(w_start, span), :], qr_t, preferred_element_type=F32)
    e_w = exp2_cols(masked(s_w, (dist >= 0) & (dist < WINDOW), -jnp.inf))
    acc_w = jnp.dot(vwt_ref[0, 0, :, pl.ds(w_start, span)], e_w.astype(BF16), preferred_element_type=F32)

    js = lax.broadcasted_iota(jnp.int32, (n_blk, qb), 0).astype(F32)
    cur = (t_row // SEL_BLOCK).astype(F32)
    forced = (js == 0.0) | (js == cur) | (js == cur - 1.0)
    val = jnp.where((js > cur) | forced, -jnp.inf, imp_t)
    for _ in range(N_SELECT - N_FORCED):
        best = jnp.max(val, axis=0, keepdims=True)
        first = jnp.min(jnp.where(val == best, js, float(n_blk)), axis=0, keepdims=True)
        val = jnp.where(js == first, -jnp.inf, val)
    chosen = val == -jnp.inf
    kt = SEL_KEY_TILE
    n_tiles = ks_ref.shape[2] // kt
    first_own = (q0 // SEL_BLOCK).astype(F32)

    def extended(bias):
        return jnp.concatenate([jnp.concatenate([bias.astype(BF16)] * hpg, axis=1), qr_t], axis=0)

    q_ext_t = extended(jnp.where(chosen & (js < first_own), 0.0, MASKED))
    q_own_t = extended(jnp.where(chosen & (js >= first_own) & (js <= cur), 0.0, MASKED))

    def scores(i, s_ref):
        start = pl.multiple_of(jnp.minimum(i, n_tiles - 1) * kt, kt)
        k_ext = jnp.concatenate([expt_ref[pl.ds(start, kt), :], ks_ref[0, 0, pl.ds(start, kt), :]], axis=1)
        s = jnp.dot(k_ext, q_ext_t, preferred_element_type=F32)
        s_ref[...] = s
        return jnp.max(s, axis=0, keepdims=True)

    def accumulate(s, m_tile, v_t, carry):
        m_run, acc = carry
        m_new = jnp.maximum(m_run, m_tile)
        p = jnp.exp2(s - m_new)
        acc = jnp.exp2(m_run - m_new) * acc + jnp.dot(v_t, p.astype(BF16), preferred_element_type=F32)
        return m_new, acc

    def consume(i, s_ref, m_tile, carry):
        start = pl.multiple_of(i * kt, kt)
        return accumulate(s_ref[...], m_tile, vst_ref[0, 0, :, pl.ds(start, kt)], carry)

    def pair_step(j, carry):
        m_run, acc, m_even = carry
        m_odd = scores(2 * j + 1, s_odd)
        m_run, acc = consume(2 * j, s_even, m_even, (m_run, acc))
        m_even = scores(2 * j + 2, s_even)
        m_run, acc = consume(2 * j + 1, s_odd, m_odd, (m_run, acc))
        return m_run, acc, m_even

    cols = hpg * qb
    init = (jnp.full((1, cols), MASKED, F32), jnp.zeros((VALUE_ROWS, cols), F32), scores(0, s_even))
    n_past = (q0 + kt - 1) // kt
    m_run, acc_s, _ = lax.fori_loop(0, (n_past + 1) // 2, pair_step, init)
    kpos = q0 + lax.broadcasted_iota(jnp.int32, (qb, 1), 0)
    k_own = jnp.concatenate([expt_ref[pl.ds(q0, qb), :], ks_ref[0, 0, pl.ds(q0, qb), :]], axis=1)
    s_own = masked(jnp.dot(k_own, q_own_t, preferred_element_type=F32), kpos <= t_row, MASKED)
    _, acc_s = accumulate(s_own, jnp.max(s_own, axis=0, keepdims=True), vst_ref[0, 0, :, pl.ds(q0, qb)],
                          (m_run, acc_s))

    gates_t = _sigmoid(gl_ref[...].T)
    outs = []
    for h in range(hpg):
        c = slice(h * qb, (h + 1) * qb)
        o_t = jnp.zeros((dh, qb), F32)
        for j, acc in enumerate((acc_c, acc_s, acc_w)):
            den = acc[dh:dh + 1, c]
            o_t = o_t + (gates_t[3 * h + j:3 * h + j + 1, :] / jnp.where(den > 0.0, den, 1.0)) * acc[0:dh, c]
        outs.append(o_t)
    o_ref[...] = jnp.concatenate(outs, axis=0).T.astype(o_ref.dtype)


def _nsa_attention(qn, qr, gl, kc, vo, ks, vst, kw, vwt, expand_t, batch, t_len):
    assert t_len % (2 * SEL_KEY_TILE) == 0, "key tiles of the selected branch are processed in pairs"
    m = qn.shape[0]
    n_g = NSA_KV_GROUPS
    gw = NSA_GROUP_WIDTH
    dh = NSA_HEAD_DIM
    nq = t_len // Q_BLOCK
    n_cmp = kc.shape[1]
    qspec = pl.BlockSpec((Q_BLOCK, gw), lambda b, g, i: (b * nq + i, g))
    keys = pl.BlockSpec((1, 1, t_len, dh), lambda b, g, i: (b, g, 0, 0))
    values_t = pl.BlockSpec((1, 1, VALUE_ROWS, t_len), lambda b, g, i: (b, g, 0, 0))
    const = lambda a: pl.BlockSpec(a.shape, lambda b, g, i: (0, 0))
    return pl.pallas_call(
        _nsa_kernel,
        grid=(batch, n_g, nq),
        in_specs=[qspec, qspec,
                  pl.BlockSpec((Q_BLOCK, LANES), lambda b, g, i: (b * nq + i, g)),
                  pl.BlockSpec((1, n_cmp, dh), lambda b, g, i: (b * n_g + g, 0, 0)),
                  pl.BlockSpec((1,) + vo.shape[1:], lambda b, g, i: (b * n_g + g, 0, 0)),
                  keys, values_t, keys, values_t, const(expand_t)],
        out_specs=qspec,
        out_shape=jax.ShapeDtypeStruct((m, n_g * gw), BF16),
        scratch_shapes=[pltpu.VMEM((SEL_KEY_TILE, NSA_HPG * Q_BLOCK), F32)] * 2,
        compiler_params=_params("parallel", "parallel", "arbitrary"),
        name="nsa_attention",
    )(qn, qr, gl, kc, vo, ks, vst, kw, vwt, expand_t)


def _pad_cols(w, n):
    return jnp.pad(w, ((0, 0), (0, n - w.shape[1])))


def _even_layer(x, gain, w_in, qkv_conv, a_log, dt_bias, o_norm, sc_conv, w_out, batch, t_len, mlp, final):
    gw = GDN_WIDTH
    ab = _pad_cols(w_in[:, 4 * gw:4 * gw + 2 * GDN_HEADS], LANES)
    w_all = jnp.concatenate([w_in[:, :3 * gw], ab, w_in[:, 3 * gw:4 * gw], w_in[:, 4 * gw + 2 * GDN_HEADS:]],
                            axis=1).astype(BF16)
    proj, gates = _norm_matmul(x, gain, w_all, 3 * gw + LANES)
    gate_params = jnp.zeros((SUBLANES, LANES), F32)
    gate_params = gate_params.at[0, :GDN_HEADS].set(a_log).at[1, :GDN_HEADS].set(dt_bias)
    ya = _gdn(proj, gates, qkv_conv, gate_params, o_norm, batch, t_len)
    return _even_tail(x, ya, gates, sc_conv, w_out[:gw].astype(BF16), w_out[gw:].astype(BF16), t_len, mlp, final)


def _rope_tables(t_len):
    half = ROPE_DIM // 2
    inv_freq = ROPE_THETA ** (-jnp.arange(0, ROPE_DIM, 2, dtype=F32) / ROPE_DIM)
    ang = jnp.arange(t_len, dtype=F32)[:, None] * inv_freq[None, :]
    cos, sin = jnp.cos(ang), jnp.sin(ang)
    pad = NSA_HEAD_DIM - ROPE_DIM
    head = lambda a, b, fill: jnp.concatenate([a, b, jnp.full((t_len, pad), fill, F32)], axis=1)
    zeros = jnp.zeros((t_len, half), F32)
    reps = LANES // NSA_HEAD_DIM
    return (jnp.tile(head(cos, cos, 1.0), (1, reps)),
            jnp.tile(head(-sin, zeros, 0.0), (1, reps)),
            jnp.tile(head(zeros, sin, 0.0), (1, reps)))


def _odd_layer(x, gain, w_in, cmp_pos, k_w1, k_w2, v_w1, v_w2, w_out, batch, t_len, mlp, final):
    n_g, dh, hpg = NSA_KV_GROUPS, NSA_HEAD_DIM, NSA_HPG
    dq = NSA_HEADS * dh
    kv = NSA_KV_WIDTH
    gate_w = w_in[:, dq + 6 * kv:].reshape(-1, n_g, 3 * hpg)
    gate_w = jnp.pad(gate_w, ((0, 0), (0, 0), (0, LANES - 3 * hpg))).reshape(-1, n_g * LANES)
    w_all = jnp.concatenate([w_in[:, :dq + 6 * kv], gate_w], axis=1).astype(BF16)
    qn, qr, kvc, ks, vs_t, kw, vw_t, gl = _odd_proj(x, gain, w_all, _rope_tables(t_len), batch, t_len)

    n_chunk = t_len // CMP_STRIDE

    def chunked(a):
        a = a.reshape(batch, n_chunk, CMP_STRIDE, n_g, dh)
        return jnp.moveaxis(a, 3, 1).reshape(batch * n_g, n_chunk, CMP_STRIDE * dh)

    pos = jnp.zeros((SUBLANES, CMP_BLOCK * dh), F32).at[0].set(cmp_pos.reshape(-1)).astype(BF16)
    vc_t, kc = _compress(chunked(kvc[:, kv:]), chunked(kvc[:, :kv]), pos, v_w1.astype(BF16),
                         v_w2.T.astype(BF16), k_w1.astype(BF16), k_w2.astype(BF16))

    def with_ones(a):
        n = a.shape[-1]
        return jnp.concatenate([a, jnp.ones((a.shape[0], 1, n), a.dtype),
                                jnp.zeros((a.shape[0], VALUE_ROWS - dh - 1, n), a.dtype)], axis=1)

    n_blk = _round_up(t_len // SEL_BLOCK, LANES)
    c_start = np.arange(n_chunk)[None, :] * CMP_STRIDE
    s_start = np.arange(n_blk)[:, None] * SEL_BLOCK
    overlap_t = jnp.asarray((c_start < s_start + SEL_BLOCK) & (c_start + CMP_BLOCK > s_start), BF16)
    expand_t = jnp.asarray((np.arange(t_len)[:, None] // SEL_BLOCK) == np.arange(n_blk)[None, :], BF16)
    vo = jnp.concatenate([with_ones(vc_t), jnp.broadcast_to(overlap_t, (batch * n_g,) + overlap_t.shape)], axis=1)
    o = _nsa_attention(qn, qr, gl, kc, vo, ks, vs_t, kw, vw_t, expand_t, batch, t_len)
    return _odd_tail(x, o, w_out.astype(BF16), mlp, final)


def kernel(x, mix_norm, mlp_norm, w_up, w_down, final_norm, ev_w_in, ev_qkv_conv, ev_a_log, ev_dt_bias,
           ev_o_norm, ev_sc_conv, ev_w_out, od_w_in, od_cmp_pos, od_cmp_k_w1, od_cmp_k_w2, od_cmp_v_w1,
           od_cmp_v_w2, od_w_out):
    batch, t_len, d = x.shape
    depth = mix_norm.shape[0]
    xs = x.reshape(batch * t_len, d)
    for layer in range(depth):
        i = layer // 2
        mlp = (mlp_norm[layer], w_up[layer].astype(BF16), w_down[layer].astype(BF16), final_norm)
        final = layer == depth - 1
        if layer % 2 == 0:
            xs = _even_layer(xs, mix_norm[layer], ev_w_in[i], ev_qkv_conv[i], ev_a_log[i], ev_dt_bias[i],
                             ev_o_norm[i], ev_sc_conv[i], ev_w_out[i], batch, t_len, mlp, final)
        else:
            xs = _odd_layer(xs, mix_norm[layer], od_w_in[i], od_cmp_pos[i], od_cmp_k_w1[i], od_cmp_k_w2[i],
                            od_cmp_v_w1[i], od_cmp_v_w2[i], od_w_out[i], batch, t_len, mlp, final)
    return xs.reshape(batch, t_len, d)
```

```python
import functools

import jax
import jax.numpy as jnp
import numpy as np
from jax import lax
from jax.experimental import pallas as pl
from jax.experimental.pallas import tpu as pltpu

F32 = jnp.float32
BF16 = jnp.bfloat16
HIGHEST = lax.Precision.HIGHEST

NORM_EPS = 1e-6
GDN_HEADS = 8
GDN_HEAD_DIM = 64
GDN_WIDTH = GDN_HEADS * GDN_HEAD_DIM
GDN_CONV = 4
GDN_CHUNK = 64
SOLVE_BLOCK = 16
SC_CONV = 3
NSA_HEADS = 16
NSA_HEAD_DIM = 64
NSA_KV_GROUPS = 4
NSA_HPG = NSA_HEADS // NSA_KV_GROUPS
NSA_GROUP_WIDTH = NSA_HPG * NSA_HEAD_DIM
NSA_KV_WIDTH = NSA_KV_GROUPS * NSA_HEAD_DIM
CMP_BLOCK = 32
CMP_STRIDE = 16
SEL_BLOCK = 64
N_SELECT = 16
N_FORCED = 3
WINDOW = 512
Q_BLOCK = 256
SEL_KEY_TILE = 512
VALUE_ROWS = NSA_HEAD_DIM + 16
ROPE_THETA = 500000.0
ROPE_DIM = NSA_HEAD_DIM // 4
LOG2_E = 1.4426950408889634
LANES = 128
SUBLANES = 8
BF16_ROWS = 16
VMEM_LIMIT = 56 * 1024 * 1024


def _round_up(n, m):
    return (n + m - 1) // m * m


def _params(*semantics):
    return pltpu.CompilerParams(dimension_semantics=semantics, vmem_limit_bytes=VMEM_LIMIT)


def _rms(x, gain):
    return x * lax.rsqrt(jnp.mean(x * x, axis=-1, keepdims=True) + NORM_EPS) * gain


def _sigmoid(x):
    return 1.0 / (1.0 + jnp.exp(-x))


def _silu(x):
    return x * _sigmoid(x)


def _softplus(x):
    return jnp.maximum(x, 0.0) + jnp.log(1.0 + jnp.exp(-jnp.abs(x)))


def _dot(a, b):
    return jnp.dot(a.astype(BF16), b.astype(BF16), preferred_element_type=F32)


def _dot_nt(a, b):
    return lax.dot_general(a.astype(BF16), b.astype(BF16), (((1,), (1,)), ((), ())),
                           preferred_element_type=F32)


def _dot_f32(a, b):
    return jnp.dot(a, b, precision=HIGHEST, preferred_element_type=F32)


def _norm_matmul_kernel(x_ref, g_ref, w_ref, o_ref, ob_ref):
    h = _rms(x_ref[...], g_ref[...])
    y = jnp.dot(h.astype(BF16), w_ref[...], preferred_element_type=F32)
    n = o_ref.shape[1]
    o_ref[...] = y[:, :n]
    ob_ref[...] = y[:, n:].astype(BF16)


def _norm_matmul(x, gain, w, n_f32, tm=512):
    m, d = x.shape
    n = w.shape[1]
    return pl.pallas_call(
        _norm_matmul_kernel,
        grid=(m // tm,),
        in_specs=[pl.BlockSpec((tm, d), lambda i: (i, 0)),
                  pl.BlockSpec((1, d), lambda i: (0, 0)),
                  pl.BlockSpec((d, n), lambda i: (0, 0))],
        out_specs=[pl.BlockSpec((tm, n_f32), lambda i: (i, 0)), pl.BlockSpec((tm, n - n_f32), lambda i: (i, 0))],
        out_shape=[jax.ShapeDtypeStruct((m, n_f32), F32), jax.ShapeDtypeStruct((m, n - n_f32), BF16)],
        compiler_params=_params("parallel"),
        name="norm_proj",
    )(x, gain.reshape(1, d), w)


def _split(a):
    hi = a.astype(BF16)
    return hi, (a - hi.astype(F32)).astype(BF16)


def _dot3(a, b):
    (ah, al), (bh, bl) = a, b
    return (jnp.dot(ah, bh, preferred_element_type=F32) + jnp.dot(ah, bl, preferred_element_type=F32)
            + jnp.dot(al, bh, preferred_element_type=F32))


def _gdn_kernel(qkv_ref, z_ref, ab_ref, cw_ref, gp_ref, on_ref, o_ref, xbuf, act, state):
    c = GDN_CHUNK
    dh = GDN_HEAD_DIM
    pw = 2 * dh
    n_pairs = GDN_HEADS // 2
    n_batch = qkv_ref.shape[0]
    hist = SUBLANES
    chains = [(b, p) for b in range(n_batch) for p in range(n_pairs)]

    @pl.when(pl.program_id(0) == 0)
    def _():
        xbuf[:, 0:hist, :] = jnp.zeros((n_batch, hist, 3 * GDN_WIDTH), F32)
        state[...] = jnp.zeros(state.shape, F32)

    row = lax.broadcasted_iota(jnp.int32, (c, pw), 0)
    lane = lax.broadcasted_iota(jnp.int32, (c, pw), 1)
    first = lane < dh
    col = jnp.where(first, lane, lane - dh)
    incl = row >= col
    strict = row > col
    eye = (row == col).astype(F32)
    diag_blk = strict & ((row // SOLVE_BLOCK) == (col // SOLVE_BLOCK))
    r2 = lax.broadcasted_iota(jnp.int32, (pw, pw), 0)
    c2 = lax.broadcasted_iota(jnp.int32, (pw, pw), 1)
    same_head = (r2 < dh) == (c2 < dh)
    ones_bd = same_head.astype(BF16)
    tri = (lax.broadcasted_iota(jnp.int32, (c, c), 0) >= lax.broadcasted_iota(jnp.int32, (c, c), 1)).astype(F32)

    def blockdiag(x):
        zero = jnp.zeros_like(x)
        return jnp.concatenate([jnp.where(first, x, zero), jnp.where(first, zero, x)], axis=0)

    def head_sums(x):
        return jnp.dot(x.astype(BF16), ones_bd, preferred_element_type=F32)

    def pdot(xs, y):
        hi, lo = _split(y)
        return _dot3(xs, (blockdiag(hi), blockdiag(lo)))

    gcs, betas = [], []
    for b in range(n_batch):
        x = qkv_ref[b]
        xbuf[b, hist:hist + c, :] = x
        y = x * cw_ref[GDN_CONV - 1:GDN_CONV, :]
        for j in range(GDN_CONV - 1):
            shift = GDN_CONV - 1 - j
            y = y + xbuf[b, hist - shift:hist - shift + c, :] * cw_ref[j:j + 1, :]
        xbuf[b, 0:hist, :] = x[c - hist:c, :]
        act[b] = _silu(y)
        ab = ab_ref[b]
        g_all = -jnp.exp(gp_ref[0:1, :]) * _softplus(ab + gp_ref[1:2, :])
        gcs.append(_dot_f32(tri, g_all))
        betas.append(_sigmoid(ab))

    def pair_cols(a, p, offset):
        return jnp.where(first, a[:, offset + 2 * p:offset + 2 * p + 1], a[:, offset + 2 * p + 1:offset + 2 * p + 2])

    scale = dh ** -0.5
    q = [act[b, :, p * pw:(p + 1) * pw] for b, p in chains]
    k = [act[b, :, GDN_WIDTH + p * pw:GDN_WIDTH + (p + 1) * pw] for b, p in chains]
    v = [act[b, :, 2 * GDN_WIDTH + p * pw:2 * GDN_WIDTH + (p + 1) * pw] for b, p in chains]
    q = [x * lax.rsqrt(head_sums(x * x) + NORM_EPS) * scale for x in q]
    k = [x * lax.rsqrt(head_sums(x * x) + NORM_EPS) for x in k]
    gcol = [pair_cols(gcs[b], p, 0) for b, p in chains]
    bcol = [pair_cols(betas[b], p, GDN_HEADS) for b, p in chains]
    grow = [jnp.sum(g * eye, axis=0, keepdims=True) for g in gcol]
    g_last = [g[c - 1:c, :] for g in gcol]
    decay = [jnp.exp(jnp.where(incl, gc_ - gr_, -jnp.inf)) for gc_, gr_ in zip(gcol, grow)]
    kb = [k_ * b_ for k_, b_ in zip(k, bcol)]
    k_bd = [blockdiag(k_).astype(BF16) for k_ in k]
    both = [_dot_nt(jnp.concatenate([kb_, q_], axis=0), kd_) for kb_, q_, kd_ in zip(kb, q, k_bd)]
    m_low = [jnp.where(strict, x[0:c] * d_, 0.0) for x, d_ in zip(both, decay)]
    attn = [x[c:2 * c] * d_ for x, d_ in zip(both, decay)]

    d_pow = [jnp.where(diag_blk, m_, 0.0) for m_ in m_low]
    l_mat = [m_ - d_ for m_, d_ in zip(m_low, d_pow)]
    p_inv = [eye - d_ for d_ in d_pow]
    d_pow = [pdot(_split(d_), d_) for d_ in d_pow]
    n_steps = int(np.log2(SOLVE_BLOCK)) - 1
    for step in range(n_steps):
        if step == n_steps - 1:
            p_inv = [p_ + pdot(_split(p_), d_) for p_, d_ in zip(p_inv, d_pow)]
        else:
            prod = [pdot(_split(jnp.concatenate([p_, d_], axis=0)), d_) for p_, d_ in zip(p_inv, d_pow)]
            p_inv = [p_ + x[0:c] for p_, x in zip(p_inv, prod)]
            d_pow = [x[c:2 * c] for x in prod]
    n_pow = [pdot(_split(p_), l_) for p_, l_ in zip(p_inv, l_mat)]
    q_inv = [eye - n_ for n_ in n_pow]
    for _ in range(int(np.log2(c // SOLVE_BLOCK)) - 1):
        n_pow = [pdot(_split(n_), n_) for n_ in n_pow]
        q_inv = [q_ + pdot(_split(q_), n_) for q_, n_ in zip(q_inv, n_pow)]
    a_inv = [pdot(_split(q_), p_) for q_, p_ in zip(q_inv, p_inv)]
    u = [_dot(a_, blockdiag(v_ * b_)) for a_, v_, b_ in zip(a_inv, v, bcol)]
    w = [_dot(a_, blockdiag(kb_ * jnp.exp(g_))) for a_, kb_, g_ in zip(a_inv, kb, gcol)]

    s_old = [state[b * n_pairs + p] for b, p in chains]
    on_s = [_dot(jnp.concatenate([w_, q_ * jnp.exp(g_)], axis=0), s_) for w_, q_, g_, s_ in zip(w, q, gcol, s_old)]
    v_new = [u_ - x[0:c] for u_, x in zip(u, on_s)]
    o = [x[c:2 * c] + _dot(a_, blockdiag(vn_)) for x, a_, vn_ in zip(on_s, attn, v_new)]
    k_dec_t = [(k_ * jnp.exp(gl_ - g_)).T for k_, gl_, g_ in zip(k, g_last, gcol)]
    for (b, p), s_, gl_, kt_, vn_ in zip(chains, s_old, g_last, k_dec_t, v_new):
        state[b * n_pairs + p] = s_ * jnp.exp(gl_) + jnp.where(same_head, _dot(kt_, vn_), 0.0)

    inv_dh = 1.0 / dh
    for (b, p), o_ in zip(chains, o):
        zp = z_ref[b, :, p * pw:(p + 1) * pw].astype(F32)
        y = o_ * lax.rsqrt(head_sums(o_ * o_) * inv_dh + NORM_EPS) * on_ref[...]
        o_ref[b, :, p * pw:(p + 1) * pw] = (y * _silu(zp)).astype(o_ref.dtype)


def _gdn(proj, gates, conv_w, gate_params, o_norm, batch, t_len):
    c = GDN_CHUNK
    n_chunks = t_len // c
    w3 = 3 * GDN_WIDTH
    ab_col = w3 // LANES
    proj = proj.reshape(batch, t_len, proj.shape[-1])
    gates = gates.reshape(batch, t_len, gates.shape[-1])
    on_pair = jnp.tile(o_norm.reshape(1, GDN_HEAD_DIM), (1, 2))
    out = pl.pallas_call(
        _gdn_kernel,
        grid=(n_chunks,),
        in_specs=[pl.BlockSpec((batch, c, w3), lambda i: (0, i, 0)),
                  pl.BlockSpec((batch, c, GDN_WIDTH), lambda i: (0, i, 0)),
                  pl.BlockSpec((batch, c, LANES), lambda i: (0, i, ab_col)),
                  pl.BlockSpec((GDN_CONV, w3), lambda i: (0, 0)),
                  pl.BlockSpec((SUBLANES, LANES), lambda i: (0, 0)),
                  pl.BlockSpec((1, 2 * GDN_HEAD_DIM), lambda i: (0, 0))],
        out_specs=pl.BlockSpec((batch, c, GDN_WIDTH), lambda i: (0, i, 0)),
        out_shape=jax.ShapeDtypeStruct((batch, t_len, GDN_WIDTH), BF16),
        scratch_shapes=[pltpu.VMEM((batch, c + SUBLANES, w3), F32),
                        pltpu.VMEM((batch, c, w3), F32),
                        pltpu.VMEM((batch * GDN_HEADS // 2, 2 * GDN_HEAD_DIM, 2 * GDN_HEAD_DIM), F32)],
        compiler_params=_params("arbitrary"),
        name="gdn",
    )(proj, gates, proj, conv_w, gate_params, on_pair)
    return out.reshape(batch * t_len, GDN_WIDTH)


def _mlp_tail(final, x, g_ref, wu_ref, wd_ref, fg_ref, o_ref):
    u = jnp.dot(_rms(x, g_ref[...]).astype(BF16), wu_ref[...], preferred_element_type=F32)
    u = jnp.square(jnp.maximum(u, 0.0))
    y = x + jnp.dot(u.astype(BF16), wd_ref[...], preferred_element_type=F32)
    if final:
        y = _rms(y, fg_ref[...])
    o_ref[...] = y


def _even_tail_kernel(tiles_per_batch, final, x_ref, ya_ref, bg_ref, cg_ref, hs_ref, cgp_ref, hsp_ref,
                      cw_ref, wa_ref, wb_ref, g_ref, wu_ref, wd_ref, fg_ref, o_ref, ubuf):
    tm = x_ref.shape[0]
    hist = cgp_ref.shape[0]
    first = (pl.program_id(0) % tiles_per_batch) == 0
    prev = cgp_ref[...].astype(F32) * hsp_ref[...].astype(F32)
    ubuf[0:hist, :] = jnp.where(first, 0.0, prev)
    u = cg_ref[...].astype(F32) * hs_ref[...].astype(F32)
    ubuf[hist:hist + tm, :] = u
    conv = u * cw_ref[SC_CONV - 1:SC_CONV, :]
    for j in range(SC_CONV - 1):
        shift = SC_CONV - 1 - j
        conv = conv + ubuf[hist - shift:hist - shift + tm, :] * cw_ref[j:j + 1, :]
    yb = bg_ref[...].astype(F32) * conv
    mix = jnp.dot(ya_ref[...], wa_ref[...], preferred_element_type=F32)
    mix = mix + jnp.dot(yb.astype(BF16), wb_ref[...], preferred_element_type=F32)
    _mlp_tail(final, x_ref[...] + mix, g_ref, wu_ref, wd_ref, fg_ref, o_ref)


def _odd_tail_kernel(final, x_ref, a_ref, w_ref, g_ref, wu_ref, wd_ref, fg_ref, o_ref):
    mix = jnp.dot(a_ref[...], w_ref[...], preferred_element_type=F32)
    _mlp_tail(final, x_ref[...] + mix, g_ref, wu_ref, wd_ref, fg_ref, o_ref)


def _resident(a):
    return pl.BlockSpec(a.shape, lambda i: (0, 0), pipeline_mode=pl.Buffered(1))


def _layer_tail(body, x, mixer_args, mixer_specs, mlp, final, scratch=(), tm=512):
    gain, w_up, w_down, final_gain = mlp
    m, d = x.shape
    row = pl.BlockSpec((tm, d), lambda i: (i, 0))
    vec = pl.BlockSpec((1, d), lambda i: (0, 0))
    return pl.pallas_call(
        functools.partial(body, final),
        grid=(m // tm,),
        in_specs=[row, *mixer_specs, vec, _resident(w_up), _resident(w_down), vec],
        out_specs=row,
        out_shape=jax.ShapeDtypeStruct((m, d), F32),
        scratch_shapes=list(scratch),
        compiler_params=_params("parallel"),
        name="layer_tail",
    )(x, *mixer_args, gain.reshape(1, d), w_up, w_down, final_gain.reshape(1, d))


def _even_tail(x, ya, gates, sc_conv, w_a, w_b, t_len, mlp, final, tm=512):
    wd = GDN_WIDTH
    hist = BF16_ROWS
    hb = tm // hist
    prev = lambda col: pl.BlockSpec((hist, wd), lambda i: (jnp.maximum(i * hb - 1, 0), col))
    cur = lambda col: pl.BlockSpec((tm, wd), lambda i: (i, col))
    specs = [cur(0), cur(1), cur(2), cur(3), prev(2), prev(3), _resident(sc_conv), _resident(w_a), _resident(w_b)]
    return _layer_tail(functools.partial(_even_tail_kernel, t_len // tm), x,
                       (ya, gates, gates, gates, gates, gates, sc_conv, w_a, w_b), specs, mlp, final,
                       scratch=[pltpu.VMEM((tm + hist, wd), F32)], tm=tm)


def _odd_tail(x, a, w, mlp, final, tm=512):
    specs = [pl.BlockSpec((tm, a.shape[1]), lambda i: (i, 0)), _resident(w)]
    return _layer_tail(_odd_tail_kernel, x, (a, w), specs, mlp, final, tm=tm)


def _rope(x, cos_t, sin_lo, sin_hi):
    half = ROPE_DIM // 2
    outs = []
    for j in range(x.shape[1] // LANES):
        xs = x[:, j * LANES:(j + 1) * LANES]
        up = pltpu.roll(xs, LANES - half, axis=1)
        down = pltpu.roll(xs, half, axis=1)
        outs.append(xs * cos_t + up * sin_lo + down * sin_hi)
    return jnp.concatenate(outs, axis=-1)


def _odd_proj_kernel(x_ref, g_ref, w_ref, cos_ref, slo_ref, shi_ref,
                     qn_ref, qr_ref, kvc_ref, ks_ref, vs_ref, kw_ref, vw_ref, gl_ref):
    h = _rms(x_ref[...], g_ref[...])
    y = jnp.dot(h.astype(BF16), w_ref[...], preferred_element_type=F32)
    cos_t, sin_lo, sin_hi = cos_ref[...], slo_ref[...], shi_ref[...]
    dq = NSA_HEADS * NSA_HEAD_DIM
    kv = NSA_KV_WIDTH
    q = y[:, :dq] * (LOG2_E * NSA_HEAD_DIM ** -0.5)
    qn_ref[...] = q.astype(BF16)
    qr_ref[...] = _rope(q, cos_t, sin_lo, sin_hi).astype(BF16)
    kvc_ref[...] = y[:, dq:dq + 2 * kv].astype(BF16)
    dh = NSA_HEAD_DIM
    tm = y.shape[0]
    ones_rows = (lax.broadcasted_iota(jnp.int32, (VALUE_ROWS - dh, tm), 0) == 0).astype(BF16)
    for k_ref, v_ref, col in ((ks_ref, vs_ref, dq + 2 * kv), (kw_ref, vw_ref, dq + 4 * kv)):
        k = _rope(y[:, col:col + kv], cos_t, sin_lo, sin_hi).astype(BF16)
        v_t = y[:, col + kv:col + 2 * kv].T
        for g in range(NSA_KV_GROUPS):
            k_ref[0, g] = k[:, g * dh:(g + 1) * dh]
            v_ref[0, g, 0:dh, :] = v_t[g * dh:(g + 1) * dh].astype(BF16)
            v_ref[0, g, dh:VALUE_ROWS, :] = ones_rows
    gl_ref[...] = y[:, dq + 6 * kv:]


def _odd_proj(x, gain, w, tables, batch, t_len, tm=512):
    m, d = x.shape
    n = w.shape[1]
    n_g, dh = NSA_KV_GROUPS, NSA_HEAD_DIM
    dq = NSA_HEADS * dh
    kv = NSA_KV_WIDTH
    ng = n - dq - 6 * kv
    tpb = t_len // tm
    row = lambda width: pl.BlockSpec((tm, width), lambda i: (i, 0))
    tab = pl.BlockSpec((tm, LANES), lambda i: (i % tpb, 0))
    keys = (pl.BlockSpec((1, n_g, tm, dh), lambda i: (i // tpb, 0, i % tpb, 0)),
            jax.ShapeDtypeStruct((batch, n_g, t_len, dh), BF16))
    values_t = (pl.BlockSpec((1, n_g, VALUE_ROWS, tm), lambda i: (i // tpb, 0, 0, i % tpb)),
                jax.ShapeDtypeStruct((batch, n_g, VALUE_ROWS, t_len), BF16))
    flat = lambda width, dt: (row(width), jax.ShapeDtypeStruct((m, width), dt))
    outs = [flat(dq, BF16), flat(dq, BF16), flat(2 * kv, BF16), keys, values_t, keys, values_t, flat(ng, F32)]
    return pl.pallas_call(
        _odd_proj_kernel,
        grid=(m // tm,),
        in_specs=[row(d), pl.BlockSpec((1, d), lambda i: (0, 0)),
                  pl.BlockSpec((d, n), lambda i: (0, 0)), tab, tab, tab],
        out_specs=[spec for spec, _ in outs],
        out_shape=[shape for _, shape in outs],
        compiler_params=_params("parallel"),
        name="odd_proj",
    )(x, gain.reshape(1, d), w, *tables)


def _compress_kernel(xa_ref, xb_ref, pos_ref, a1_ref, a2t_ref, b1_ref, b2_ref, at_ref, b_ref):
    half = xa_ref.shape[2]
    n = xa_ref.shape[1]

    def hidden(x_ref, w1_ref):
        x = x_ref[0]
        top = jnp.dot(x, w1_ref[0:half, :], preferred_element_type=F32)
        bot = jnp.dot(x, w1_ref[half:2 * half, :], preferred_element_type=F32)
        bias = (jnp.dot(pos_ref[:, 0:half], w1_ref[0:half, :], preferred_element_type=F32)
                + jnp.dot(pos_ref[:, half:2 * half], w1_ref[half:2 * half, :], preferred_element_type=F32))
        nxt = pltpu.roll(bot, n - 1, axis=0)
        return _silu(top + nxt + bias[0:1, :]).astype(BF16)

    at_ref[0] = lax.dot_general(a2t_ref[...], hidden(xa_ref, a1_ref), (((1,), (1,)), ((), ())),
                                preferred_element_type=F32).astype(BF16)
    b_ref[0] = jnp.dot(hidden(xb_ref, b1_ref), b2_ref[...], preferred_element_type=F32).astype(BF16)


def _compress(xa, xb, pos, a1, a2t, b1, b2):
    bg, n, half = xa.shape
    dh = NSA_HEAD_DIM
    full = lambda a: pl.BlockSpec(a.shape, lambda i: (0,) * a.ndim)
    return pl.pallas_call(
        _compress_kernel,
        grid=(bg,),
        in_specs=[pl.BlockSpec((1, n, half), lambda i: (i, 0, 0)),
                  pl.BlockSpec((1, n, half), lambda i: (i, 0, 0)),
                  full(pos), full(a1), full(a2t), full(b1), full(b2)],
        out_specs=[pl.BlockSpec((1, dh, n), lambda i: (i, 0, 0)),
                   pl.BlockSpec((1, n, dh), lambda i: (i, 0, 0))],
        out_shape=[jax.ShapeDtypeStruct((bg, dh, n), BF16), jax.ShapeDtypeStruct((bg, n, dh), BF16)],
        compiler_params=_params("parallel"),
        name="compress",
    )(xa, xb, pos, a1, a2t, b1, b2)


MASKED = -1e30


def _nsa_kernel(qn_ref, qr_ref, gl_ref, kc_ref, vo_ref, ks_ref, vst_ref, kw_ref, vwt_ref, expt_ref,
                o_ref, s_even, s_odd, stats, acc_buf):
    qb = Q_BLOCK
    dh = NSA_HEAD_DIM
    hpg = NSA_HPG
    n_cmp = kc_ref.shape[1]
    n_blk = expt_ref.shape[1]
    q0 = pl.program_id(2) * qb
    t_row = q0 + lax.broadcasted_iota(jnp.int32, (1, qb), 1)

    def heads_t(ref):
        xt = ref[...].astype(F32).T
        return jnp.concatenate([xt[h * dh:(h + 1) * dh] for h in range(hpg)], axis=1).astype(BF16)

    qn_t = heads_t(qn_ref)
    qr_t = heads_t(qr_ref)

    def masked(s, mask, fill):
        return jnp.concatenate([jnp.where(mask, s[:, h * qb:(h + 1) * qb], fill) for h in range(hpg)], axis=1)

    def exp2_cols(s):
        m = jnp.max(s, axis=0, keepdims=True)
        return jnp.exp2(s - jnp.where(m == -jnp.inf, 0.0, m))

    cmp_end = lax.broadcasted_iota(jnp.int32, (n_cmp, 1), 0) * CMP_STRIDE + (CMP_BLOCK - 1)
    s_c = jnp.dot(kc_ref[0], qn_t, preferred_element_type=F32)
    e_c = exp2_cols(masked(s_c, cmp_end <= t_row, -jnp.inf)).astype(BF16)
    both = jnp.dot(vo_ref[0], e_c, preferred_element_type=F32)
    acc_c = both[0:VALUE_ROWS]
    den_c = acc_c[dh:dh + 1, :]
    imp_all = both[VALUE_ROWS:VALUE_ROWS + n_blk] * (1.0 / jnp.where(den_c > 0.0, den_c, 1.0))
    imp_t = imp_all[:, 0:qb]
    for h in range(1, hpg):
        imp_t = imp_t + imp_all[:, h * qb:(h + 1) * qb]

    span = WINDOW + qb
    w_start = pl.multiple_of(jnp.maximum(q0 - WINDOW, 0), qb)
    dist = t_row - (w_start + lax.broadcasted_iota(jnp.int32, (span, 1), 0))
    s_w = jnp.dot(kw_ref[0, 0, pl.ds(w_start, span), :], qr_t, preferred_element_type=F32)
    e_w = exp2_cols(masked(s_w, (dist >= 0) & (dist < WINDOW), -jnp.inf))
    acc_w = jnp.dot(vwt_ref[0, 0, :, pl.ds(w_start, span)], e_w.astype(BF16), preferred_element_type=F32)

    js = lax.broadcasted_iota(jnp.int32, (n_blk, qb), 0).astype(F32)
    cur = (t_row // SEL_BLOCK).astype(F32)
    forced = (js == 0.0) | (js == cur) | (js == cur - 1.0)
    val = jnp.where((js > cur) | forced, -jnp.inf, imp_t)
    for _ in range(N_SELECT - N_FORCED):
        best = jnp.max(val, axis=0, keepdims=True)
        first = jnp.min(jnp.where(val == best, js, float(n_blk)), axis=0, keepdims=True)
        val = jnp.where(js == first, -jnp.inf, val)
    chosen = val == -jnp.inf
    kt = SEL_KEY_TILE
    n_tiles = ks_ref.shape[2] // kt
    first_own = (q0 // SEL_BLOCK).astype(F32)

    def extended(bias):
        return jnp.concatenate([jnp.concatenate([bias.astype(BF16)] * hpg, axis=1), qr_t], axis=0)

    q_ext_t = extended(jnp.where(chosen & (js < first_own), 0.0, MASKED))
    q_own_t = extended(jnp.where(chosen & (js >= first_own) & (js <= cur), 0.0, MASKED))

    def scores(i, s_ref):
        start = pl.multiple_of(jnp.minimum(i, n_tiles - 1) * kt, kt)
        k_ext = jnp.concatenate([expt_ref[pl.ds(start, kt), :], ks_ref[0, 0, pl.ds(start, kt), :]], axis=1)
        s = jnp.dot(k_ext, q_ext_t, preferred_element_type=F32)
        s_ref[...] = s
        return jnp.max(s, axis=0, keepdims=True)

    def accumulate(s, m_tile, v_t, carry):
        m_run, acc = carry
        m_new = jnp.maximum(m_run, m_tile)
        p = jnp.exp2(s - m_new)
        acc = jnp.exp2(m_run - m_new) * acc + jnp.dot(v_t, p.astype(BF16), preferred_element_type=F32)
        return m_new, acc

    def consume(i, s_ref, m_tile, carry):
        start = pl.multiple_of(i * kt, kt)
        return accumulate(s_ref[...], m_tile, vst_ref[0, 0, :, pl.ds(start, kt)], carry)

    def pair_step(j, carry):
        m_run, acc, m_even = carry
        m_odd = scores(2 * j + 1, s_odd)
        m_run, acc = consume(2 * j, s_even, m_even, (m_run, acc))
        m_even = scores(2 * j + 2, s_even)
        m_run, acc = consume(2 * j + 1, s_odd, m_odd, (m_run, acc))
        return m_run, acc, m_even

    cols = hpg * qb
    init = (jnp.full((1, cols), MASKED, F32), jnp.zeros((VALUE_ROWS, cols), F32), scores(0, s_even))
    n_past = (q0 + kt - 1) // kt
    m_run, acc_s, m_even = lax.fori_loop(0, n_past // 2, pair_step, init)
    stats[0:1, :] = m_run
    stats[1:2, :] = m_even
    acc_buf[...] = acc_s

    @pl.when(n_past % 2 == 1)
    def _():
        m_last, acc_last = consume(n_past - 1, s_even, stats[1:2, :], (stats[0:1, :], acc_buf[...]))
        stats[0:1, :] = m_last
        acc_buf[...] = acc_last

    m_run, acc_s = stats[0:1, :], acc_buf[...]
    kpos = q0 + lax.broadcasted_iota(jnp.int32, (qb, 1), 0)
    k_own = jnp.concatenate([expt_ref[pl.ds(q0, qb), :], ks_ref[0, 0, pl.ds(q0, qb), :]], axis=1)
    s_own = masked(jnp.dot(k_own, q_own_t, preferred_element_type=F32), kpos <= t_row, MASKED)
    _, acc_s = accumulate(s_own, jnp.max(s_own, axis=0, keepdims=True), vst_ref[0, 0, :, pl.ds(q0, qb)],
                          (m_run, acc_s))

    gates_t = _sigmoid(gl_ref[...].T)
    outs = []
    for h in range(hpg):
        c = slice(h * qb, (h + 1) * qb)
        o_t = jnp.zeros((dh, qb), F32)
        for j, acc in enumerate((acc_c, acc_s, acc_w)):
            den = acc[dh:dh + 1, c]
            o_t = o_t + (gates_t[3 * h + j:3 * h + j + 1, :] / jnp.where(den > 0.0, den, 1.0)) * acc[0:dh, c]
        outs.append(o_t)
    o_ref[...] = jnp.concatenate(outs, axis=0).T.astype(o_ref.dtype)


def _nsa_attention(qn, qr, gl, kc, vo, ks, vst, kw, vwt, expand_t, batch, t_len):
    assert t_len % (2 * SEL_KEY_TILE) == 0, "key tiles of the selected branch are processed in pairs"
    m = qn.shape[0]
    n_g = NSA_KV_GROUPS
    gw = NSA_GROUP_WIDTH
    dh = NSA_HEAD_DIM
    nq = t_len // Q_BLOCK
    n_cmp = kc.shape[1]
    qspec = pl.BlockSpec((Q_BLOCK, gw), lambda b, g, i: (b * nq + i, g))
    keys = pl.BlockSpec((1, 1, t_len, dh), lambda b, g, i: (b, g, 0, 0))
    values_t = pl.BlockSpec((1, 1, VALUE_ROWS, t_len), lambda b, g, i: (b, g, 0, 0))
    const = lambda a: pl.BlockSpec(a.shape, lambda b, g, i: (0, 0))
    return pl.pallas_call(
        _nsa_kernel,
        grid=(batch, n_g, nq),
        in_specs=[qspec, qspec,
                  pl.BlockSpec((Q_BLOCK, LANES), lambda b, g, i: (b * nq + i, g)),
                  pl.BlockSpec((1, n_cmp, dh), lambda b, g, i: (b * n_g + g, 0, 0)),
                  pl.BlockSpec((1,) + vo.shape[1:], lambda b, g, i: (b * n_g + g, 0, 0)),
                  keys, values_t, keys, values_t, const(expand_t)],
        out_specs=qspec,
        out_shape=jax.ShapeDtypeStruct((m, n_g * gw), BF16),
        scratch_shapes=[pltpu.VMEM((SEL_KEY_TILE, NSA_HPG * Q_BLOCK), F32)] * 2
        + [pltpu.VMEM((SUBLANES, NSA_HPG * Q_BLOCK), F32), pltpu.VMEM((VALUE_ROWS, NSA_HPG * Q_BLOCK), F32)],
        compiler_params=_params("parallel", "parallel", "arbitrary"),
        name="nsa_attention",
    )(qn, qr, gl, kc, vo, ks, vst, kw, vwt, expand_t)


def _pad_cols(w, n):
    return jnp.pad(w, ((0, 0), (0, n - w.shape[1])))


def _even_layer(x, gain, w_in, qkv_conv, a_log, dt_bias, o_norm, sc_conv, w_out, batch, t_len, mlp, final):
    gw = GDN_WIDTH
    ab = _pad_cols(w_in[:, 4 * gw:4 * gw + 2 * GDN_HEADS], LANES)
    w_all = jnp.concatenate([w_in[:, :3 * gw], ab, w_in[:, 3 * gw:4 * gw], w_in[:, 4 * gw + 2 * GDN_HEADS:]],
                            axis=1).astype(BF16)
    proj, gates = _norm_matmul(x, gain, w_all, 3 * gw + LANES)
    gate_params = jnp.zeros((SUBLANES, LANES), F32)
    gate_params = gate_params.at[0, :GDN_HEADS].set(a_log).at[1, :GDN_HEADS].set(dt_bias)
    ya = _gdn(proj, gates, qkv_conv, gate_params, o_norm, batch, t_len)
    return _even_tail(x, ya, gates, sc_conv, w_out[:gw].astype(BF16), w_out[gw:].astype(BF16), t_len, mlp, final)


def _rope_tables(t_len):
    half = ROPE_DIM // 2
    inv_freq = ROPE_THETA ** (-jnp.arange(0, ROPE_DIM, 2, dtype=F32) / ROPE_DIM)
    ang = jnp.arange(t_len, dtype=F32)[:, None] * inv_freq[None, :]
    cos, sin = jnp.cos(ang), jnp.sin(ang)
    pad = NSA_HEAD_DIM - ROPE_DIM
    head = lambda a, b, fill: jnp.concatenate([a, b, jnp.full((t_len, pad), fill, F32)], axis=1)
    zeros = jnp.zeros((t_len, half), F32)
    reps = LANES // NSA_HEAD_DIM
    return (jnp.tile(head(cos, cos, 1.0), (1, reps)),
            jnp.tile(head(-sin, zeros, 0.0), (1, reps)),
            jnp.tile(head(zeros, sin, 0.0), (1, reps)))


def _odd_layer(x, gain, w_in, cmp_pos, k_w1, k_w2, v_w1, v_w2, w_out, batch, t_len, mlp, final):
    n_g, dh, hpg = NSA_KV_GROUPS, NSA_HEAD_DIM, NSA_HPG
    dq = NSA_HEADS * dh
    kv = NSA_KV_WIDTH
    gate_w = w_in[:, dq + 6 * kv:].reshape(-1, n_g, 3 * hpg)
    gate_w = jnp.pad(gate_w, ((0, 0), (0, 0), (0, LANES - 3 * hpg))).reshape(-1, n_g * LANES)
    w_all = jnp.concatenate([w_in[:, :dq + 6 * kv], gate_w], axis=1).astype(BF16)
    qn, qr, kvc, ks, vs_t, kw, vw_t, gl = _odd_proj(x, gain, w_all, _rope_tables(t_len), batch, t_len)

    n_chunk = t_len // CMP_STRIDE

    def chunked(a):
        a = a.reshape(batch, n_chunk, CMP_STRIDE, n_g, dh)
        return jnp.moveaxis(a, 3, 1).reshape(batch * n_g, n_chunk, CMP_STRIDE * dh)

    pos = jnp.zeros((SUBLANES, CMP_BLOCK * dh), F32).at[0].set(cmp_pos.reshape(-1)).astype(BF16)
    vc_t, kc = _compress(chunked(kvc[:, kv:]), chunked(kvc[:, :kv]), pos, v_w1.astype(BF16),
                         v_w2.T.astype(BF16), k_w1.astype(BF16), k_w2.astype(BF16))

    def with_ones(a):
        n = a.shape[-1]
        return jnp.concatenate([a, jnp.ones((a.shape[0], 1, n), a.dtype),
                                jnp.zeros((a.shape[0], VALUE_ROWS - dh - 1, n), a.dtype)], axis=1)

    n_blk = _round_up(t_len // SEL_BLOCK, LANES)
    c_start = np.arange(n_chunk)[None, :] * CMP_STRIDE
    s_start = np.arange(n_blk)[:, None] * SEL_BLOCK
    overlap_t = jnp.asarray((c_start < s_start + SEL_BLOCK) & (c_start + CMP_BLOCK > s_start), BF16)
    expand_t = jnp.asarray((np.arange(t_len)[:, None] // SEL_BLOCK) == np.arange(n_blk)[None, :], BF16)
    vo = jnp.concatenate([with_ones(vc_t), jnp.broadcast_to(overlap_t, (batch * n_g,) + overlap_t.shape)], axis=1)
    o = _nsa_attention(qn, qr, gl, kc, vo, ks, vs_t, kw, vw_t, expand_t, batch, t_len)
    return _odd_tail(x, o, w_out.astype(BF16), mlp, final)


def kernel(x, mix_norm, mlp_norm, w_up, w_down, final_norm, ev_w_in, ev_qkv_conv, ev_a_log, ev_dt_bias,
           ev_o_norm, ev_sc_conv, ev_w_out, od_w_in, od_cmp_pos, od_cmp_k_w1, od_cmp_k_w2, od_cmp_v_w1,
           od_cmp_v_w2, od_w_out):
    batch, t_len, d = x.shape
    depth = mix_norm.shape[0]
    xs = x.reshape(batch * t_len, d)
    for layer in range(depth):
        i = layer // 2
        mlp = (mlp_norm[layer], w_up[layer].astype(BF16), w_down[layer].astype(BF16), final_norm)
        final = layer == depth - 1
        if layer % 2 == 0:
            xs = _even_layer(xs, mix_norm[layer], ev_w_in[i], ev_qkv_conv[i], ev_a_log[i], ev_dt_bias[i],
                             ev_o_norm[i], ev_sc_conv[i], ev_w_out[i], batch, t_len, mlp, final)
        else:
            xs = _odd_layer(xs, mix_norm[layer], od_w_in[i], od_cmp_pos[i], od_cmp_k_w1[i], od_cmp_k_w2[i],
                            od_cmp_v_w1[i], od_cmp_v_w2[i], od_w_out[i], batch, t_len, mlp, final)
    return xs.reshape(batch, t_len, d)
```

```python
import functools

import jax
import jax.numpy as jnp
import numpy as np
from jax import lax
from jax.experimental import pallas as pl
from jax.experimental.pallas import tpu as pltpu

F32 = jnp.float32
BF16 = jnp.bfloat16
HIGHEST = lax.Precision.HIGHEST

NORM_EPS = 1e-6
GDN_HEADS = 8
GDN_HEAD_DIM = 64
GDN_WIDTH = GDN_HEADS * GDN_HEAD_DIM
GDN_CONV = 4
GDN_CHUNK = 64
SOLVE_BLOCK = 16
SC_CONV = 3
NSA_HEADS = 16
NSA_HEAD_DIM = 64
NSA_KV_GROUPS = 4
NSA_HPG = NSA_HEADS // NSA_KV_GROUPS
NSA_GROUP_WIDTH = NSA_HPG * NSA_HEAD_DIM
NSA_KV_WIDTH = NSA_KV_GROUPS * NSA_HEAD_DIM
CMP_BLOCK = 32
CMP_STRIDE = 16
SEL_BLOCK = 64
N_SELECT = 16
N_FORCED = 3
WINDOW = 512
Q_BLOCK = 512
SEL_KEY_TILE = 512
VALUE_ROWS = NSA_HEAD_DIM + 16
ROPE_THETA = 500000.0
ROPE_DIM = NSA_HEAD_DIM // 4
LOG2_E = 1.4426950408889634
LANES = 128
SUBLANES = 8
BF16_ROWS = 16
VMEM_LIMIT = 56 * 1024 * 1024


def _round_up(n, m):
    return (n + m - 1) // m * m


def _params(*semantics):
    return pltpu.CompilerParams(dimension_semantics=semantics, vmem_limit_bytes=VMEM_LIMIT)


def _rms(x, gain):
    return x * lax.rsqrt(jnp.mean(x * x, axis=-1, keepdims=True) + NORM_EPS) * gain


def _sigmoid(x):
    return 1.0 / (1.0 + jnp.exp(-x))


def _silu(x):
    return x * _sigmoid(x)


def _softplus(x):
    return jnp.maximum(x, 0.0) + jnp.log(1.0 + jnp.exp(-jnp.abs(x)))


def _dot(a, b):
    return jnp.dot(a.astype(BF16), b.astype(BF16), preferred_element_type=F32)


def _dot_nt(a, b):
    return lax.dot_general(a.astype(BF16), b.astype(BF16), (((1,), (1,)), ((), ())),
                           preferred_element_type=F32)


def _dot_f32(a, b):
    return jnp.dot(a, b, precision=HIGHEST, preferred_element_type=F32)


def _norm_matmul_kernel(x_ref, g_ref, w_ref, o_ref, ob_ref):
    h = _rms(x_ref[...], g_ref[...])
    y = jnp.dot(h.astype(BF16), w_ref[...], preferred_element_type=F32)
    n = o_ref.shape[1]
    o_ref[...] = y[:, :n]
    ob_ref[...] = y[:, n:].astype(BF16)


def _norm_matmul(x, gain, w, n_f32, tm=512):
    m, d = x.shape
    n = w.shape[1]
    return pl.pallas_call(
        _norm_matmul_kernel,
        grid=(m // tm,),
        in_specs=[pl.BlockSpec((tm, d), lambda i: (i, 0)),
                  pl.BlockSpec((1, d), lambda i: (0, 0)),
                  pl.BlockSpec((d, n), lambda i: (0, 0))],
        out_specs=[pl.BlockSpec((tm, n_f32), lambda i: (i, 0)), pl.BlockSpec((tm, n - n_f32), lambda i: (i, 0))],
        out_shape=[jax.ShapeDtypeStruct((m, n_f32), F32), jax.ShapeDtypeStruct((m, n - n_f32), BF16)],
        compiler_params=_params("parallel"),
        name="norm_proj",
    )(x, gain.reshape(1, d), w)


def _split(a):
    hi = a.astype(BF16)
    return hi, (a - hi.astype(F32)).astype(BF16)


def _dot3(a, b):
    (ah, al), (bh, bl) = a, b
    return (jnp.dot(ah, bh, preferred_element_type=F32) + jnp.dot(ah, bl, preferred_element_type=F32)
            + jnp.dot(al, bh, preferred_element_type=F32))


def _gdn_kernel(qkv_ref, z_ref, ab_ref, cw_ref, gp_ref, on_ref, o_ref, xbuf, act, state):
    c = GDN_CHUNK
    dh = GDN_HEAD_DIM
    pw = 2 * dh
    n_pairs = GDN_HEADS // 2
    n_batch = qkv_ref.shape[0]
    hist = SUBLANES
    chains = [(b, p) for b in range(n_batch) for p in range(n_pairs)]

    @pl.when(pl.program_id(0) == 0)
    def _():
        xbuf[:, 0:hist, :] = jnp.zeros((n_batch, hist, 3 * GDN_WIDTH), F32)
        state[...] = jnp.zeros(state.shape, F32)

    row = lax.broadcasted_iota(jnp.int32, (c, pw), 0)
    lane = lax.broadcasted_iota(jnp.int32, (c, pw), 1)
    first = lane < dh
    col = jnp.where(first, lane, lane - dh)
    incl = row >= col
    strict = row > col
    eye = (row == col).astype(F32)
    diag_blk = strict & ((row // SOLVE_BLOCK) == (col // SOLVE_BLOCK))
    r2 = lax.broadcasted_iota(jnp.int32, (pw, pw), 0)
    c2 = lax.broadcasted_iota(jnp.int32, (pw, pw), 1)
    same_head = (r2 < dh) == (c2 < dh)
    ones_bd = same_head.astype(BF16)
    tri = (lax.broadcasted_iota(jnp.int32, (c, c), 0) >= lax.broadcasted_iota(jnp.int32, (c, c), 1)).astype(F32)

    def blockdiag(x):
        zero = jnp.zeros_like(x)
        return jnp.concatenate([jnp.where(first, x, zero), jnp.where(first, zero, x)], axis=0)

    def head_sums(x):
        return jnp.dot(x.astype(BF16), ones_bd, preferred_element_type=F32)

    def pdot(xs, y):
        hi, lo = _split(y)
        return _dot3(xs, (blockdiag(hi), blockdiag(lo)))

    gcs, betas = [], []
    for b in range(n_batch):
        x = qkv_ref[b]
        xbuf[b, hist:hist + c, :] = x
        y = x * cw_ref[GDN_CONV - 1:GDN_CONV, :]
        for j in range(GDN_CONV - 1):
            shift = GDN_CONV - 1 - j
            y = y + xbuf[b, hist - shift:hist - shift + c, :] * cw_ref[j:j + 1, :]
        xbuf[b, 0:hist, :] = x[c - hist:c, :]
        act[b] = _silu(y)
        ab = ab_ref[b]
        g_all = -jnp.exp(gp_ref[0:1, :]) * _softplus(ab + gp_ref[1:2, :])
        gcs.append(_dot_f32(tri, g_all))
        betas.append(_sigmoid(ab))

    def pair_cols(a, p, offset):
        return jnp.where(first, a[:, offset + 2 * p:offset + 2 * p + 1], a[:, offset + 2 * p + 1:offset + 2 * p + 2])

    scale = dh ** -0.5
    q = [act[b, :, p * pw:(p + 1) * pw] for b, p in chains]
    k = [act[b, :, GDN_WIDTH + p * pw:GDN_WIDTH + (p + 1) * pw] for b, p in chains]
    v = [act[b, :, 2 * GDN_WIDTH + p * pw:2 * GDN_WIDTH + (p + 1) * pw] for b, p in chains]
    q = [x * lax.rsqrt(head_sums(x * x) + NORM_EPS) * scale for x in q]
    k = [x * lax.rsqrt(head_sums(x * x) + NORM_EPS) for x in k]
    gcol = [pair_cols(gcs[b], p, 0) for b, p in chains]
    bcol = [pair_cols(betas[b], p, GDN_HEADS) for b, p in chains]
    grow = [jnp.sum(g * eye, axis=0, keepdims=True) for g in gcol]
    g_last = [g[c - 1:c, :] for g in gcol]
    decay = [jnp.exp(jnp.where(incl, gc_ - gr_, -jnp.inf)) for gc_, gr_ in zip(gcol, grow)]
    kb = [k_ * b_ for k_, b_ in zip(k, bcol)]
    k_bd = [blockdiag(k_).astype(BF16) for k_ in k]
    both = [_dot_nt(jnp.concatenate([kb_, q_], axis=0), kd_) for kb_, q_, kd_ in zip(kb, q, k_bd)]
    m_low = [jnp.where(strict, x[0:c] * d_, 0.0) for x, d_ in zip(both, decay)]
    attn = [x[c:2 * c] * d_ for x, d_ in zip(both, decay)]

    d_pow = [jnp.where(diag_blk, m_, 0.0) for m_ in m_low]
    l_mat = [m_ - d_ for m_, d_ in zip(m_low, d_pow)]
    p_inv = [eye - d_ for d_ in d_pow]
    d_pow = [pdot(_split(d_), d_) for d_ in d_pow]
    n_steps = int(np.log2(SOLVE_BLOCK)) - 1
    for step in range(n_steps):
        if step == n_steps - 1:
            p_inv = [p_ + pdot(_split(p_), d_) for p_, d_ in zip(p_inv, d_pow)]
        else:
            prod = [pdot(_split(jnp.concatenate([p_, d_], axis=0)), d_) for p_, d_ in zip(p_inv, d_pow)]
            p_inv = [p_ + x[0:c] for p_, x in zip(p_inv, prod)]
            d_pow = [x[c:2 * c] for x in prod]
    n_pow = [pdot(_split(p_), l_) for p_, l_ in zip(p_inv, l_mat)]
    q_inv = [eye - n_ for n_ in n_pow]
    for _ in range(int(np.log2(c // SOLVE_BLOCK)) - 1):
        n_pow = [pdot(_split(n_), n_) for n_ in n_pow]
        q_inv = [q_ + pdot(_split(q_), n_) for q_, n_ in zip(q_inv, n_pow)]
    a_inv = [pdot(_split(q_), p_) for q_, p_ in zip(q_inv, p_inv)]
    u = [_dot(a_, blockdiag(v_ * b_)) for a_, v_, b_ in zip(a_inv, v, bcol)]
    w = [_dot(a_, blockdiag(kb_ * jnp.exp(g_))) for a_, kb_, g_ in zip(a_inv, kb, gcol)]

    s_old = [state[b * n_pairs + p] for b, p in chains]
    on_s = [_dot(jnp.concatenate([w_, q_ * jnp.exp(g_)], axis=0), s_) for w_, q_, g_, s_ in zip(w, q, gcol, s_old)]
    v_new = [u_ - x[0:c] for u_, x in zip(u, on_s)]
    o = [x[c:2 * c] + _dot(a_, blockdiag(vn_)) for x, a_, vn_ in zip(on_s, attn, v_new)]
    k_dec_t = [(k_ * jnp.exp(gl_ - g_)).T for k_, gl_, g_ in zip(k, g_last, gcol)]
    for (b, p), s_, gl_, kt_, vn_ in zip(chains, s_old, g_last, k_dec_t, v_new):
        state[b * n_pairs + p] = s_ * jnp.exp(gl_) + jnp.where(same_head, _dot(kt_, vn_), 0.0)

    inv_dh = 1.0 / dh
    for (b, p), o_ in zip(chains, o):
        zp = z_ref[b, :, p * pw:(p + 1) * pw].astype(F32)
        y = o_ * lax.rsqrt(head_sums(o_ * o_) * inv_dh + NORM_EPS) * on_ref[...]
        o_ref[b, :, p * pw:(p + 1) * pw] = (y * _silu(zp)).astype(o_ref.dtype)


def _gdn(proj, gates, conv_w, gate_params, o_norm, batch, t_len):
    c = GDN_CHUNK
    n_chunks = t_len // c
    w3 = 3 * GDN_WIDTH
    ab_col = w3 // LANES
    proj = proj.reshape(batch, t_len, proj.shape[-1])
    gates = gates.reshape(batch, t_len, gates.shape[-1])
    on_pair = jnp.tile(o_norm.reshape(1, GDN_HEAD_DIM), (1, 2))
    out = pl.pallas_call(
        _gdn_kernel,
        grid=(n_chunks,),
        in_specs=[pl.BlockSpec((batch, c, w3), lambda i: (0, i, 0)),
                  pl.BlockSpec((batch, c, GDN_WIDTH), lambda i: (0, i, 0)),
                  pl.BlockSpec((batch, c, LANES), lambda i: (0, i, ab_col)),
                  pl.BlockSpec((GDN_CONV, w3), lambda i: (0, 0)),
                  pl.BlockSpec((SUBLANES, LANES), lambda i: (0, 0)),
                  pl.BlockSpec((1, 2 * GDN_HEAD_DIM), lambda i: (0, 0))],
        out_specs=pl.BlockSpec((batch, c, GDN_WIDTH), lambda i: (0, i, 0)),
        out_shape=jax.ShapeDtypeStruct((batch, t_len, GDN_WIDTH), BF16),
        scratch_shapes=[pltpu.VMEM((batch, c + SUBLANES, w3), F32),
                        pltpu.VMEM((batch, c, w3), F32),
                        pltpu.VMEM((batch * GDN_HEADS // 2, 2 * GDN_HEAD_DIM, 2 * GDN_HEAD_DIM), F32)],
        compiler_params=_params("arbitrary"),
        name="gdn",
    )(proj, gates, proj, conv_w, gate_params, on_pair)
    return out.reshape(batch * t_len, GDN_WIDTH)


def _mlp_tail(final, x, g_ref, wu_ref, wd_ref, fg_ref, o_ref):
    u = jnp.dot(_rms(x, g_ref[...]).astype(BF16), wu_ref[...], preferred_element_type=F32)
    u = jnp.square(jnp.maximum(u, 0.0))
    y = x + jnp.dot(u.astype(BF16), wd_ref[...], preferred_element_type=F32)
    if final:
        y = _rms(y, fg_ref[...])
    o_ref[...] = y


def _even_tail_kernel(tiles_per_batch, final, x_ref, ya_ref, bg_ref, cg_ref, hs_ref, cgp_ref, hsp_ref,
                      cw_ref, wa_ref, wb_ref, g_ref, wu_ref, wd_ref, fg_ref, o_ref, ubuf):
    tm = x_ref.shape[0]
    hist = cgp_ref.shape[0]
    first = (pl.program_id(0) % tiles_per_batch) == 0
    prev = cgp_ref[...].astype(F32) * hsp_ref[...].astype(F32)
    ubuf[0:hist, :] = jnp.where(first, 0.0, prev)
    u = cg_ref[...].astype(F32) * hs_ref[...].astype(F32)
    ubuf[hist:hist + tm, :] = u
    conv = u * cw_ref[SC_CONV - 1:SC_CONV, :]
    for j in range(SC_CONV - 1):
        shift = SC_CONV - 1 - j
        conv = conv + ubuf[hist - shift:hist - shift + tm, :] * cw_ref[j:j + 1, :]
    yb = bg_ref[...].astype(F32) * conv
    mix = jnp.dot(ya_ref[...], wa_ref[...], preferred_element_type=F32)
    mix = mix + jnp.dot(yb.astype(BF16), wb_ref[...], preferred_element_type=F32)
    _mlp_tail(final, x_ref[...] + mix, g_ref, wu_ref, wd_ref, fg_ref, o_ref)


def _odd_tail_kernel(final, x_ref, a_ref, w_ref, g_ref, wu_ref, wd_ref, fg_ref, o_ref):
    mix = jnp.dot(a_ref[...], w_ref[...], preferred_element_type=F32)
    _mlp_tail(final, x_ref[...] + mix, g_ref, wu_ref, wd_ref, fg_ref, o_ref)


def _resident(a):
    return pl.BlockSpec(a.shape, lambda i: (0, 0), pipeline_mode=pl.Buffered(1))


def _layer_tail(body, x, mixer_args, mixer_specs, mlp, final, scratch=(), tm=512):
    gain, w_up, w_down, final_gain = mlp
    m, d = x.shape
    row = pl.BlockSpec((tm, d), lambda i: (i, 0))
    vec = pl.BlockSpec((1, d), lambda i: (0, 0))
    return pl.pallas_call(
        functools.partial(body, final),
        grid=(m // tm,),
        in_specs=[row, *mixer_specs, vec, _resident(w_up), _resident(w_down), vec],
        out_specs=row,
        out_shape=jax.ShapeDtypeStruct((m, d), F32),
        scratch_shapes=list(scratch),
        compiler_params=_params("parallel"),
        name="layer_tail",
    )(x, *mixer_args, gain.reshape(1, d), w_up, w_down, final_gain.reshape(1, d))


def _even_tail(x, ya, gates, sc_conv, w_a, w_b, t_len, mlp, final, tm=512):
    wd = GDN_WIDTH
    hist = BF16_ROWS
    hb = tm // hist
    prev = lambda col: pl.BlockSpec((hist, wd), lambda i: (jnp.maximum(i * hb - 1, 0), col))
    cur = lambda col: pl.BlockSpec((tm, wd), lambda i: (i, col))
    specs = [cur(0), cur(1), cur(2), cur(3), prev(2), prev(3), _resident(sc_conv), _resident(w_a), _resident(w_b)]
    return _layer_tail(functools.partial(_even_tail_kernel, t_len // tm), x,
                       (ya, gates, gates, gates, gates, gates, sc_conv, w_a, w_b), specs, mlp, final,
                       scratch=[pltpu.VMEM((tm + hist, wd), F32)], tm=tm)


def _odd_tail(x, a, w, mlp, final, tm=512):
    specs = [pl.BlockSpec((tm, a.shape[1]), lambda i: (i, 0)), _resident(w)]
    return _layer_tail(_odd_tail_kernel, x, (a, w), specs, mlp, final, tm=tm)


def _rope(x, cos_t, sin_lo, sin_hi):
    half = ROPE_DIM // 2
    outs = []
    for j in range(x.shape[1] // LANES):
        xs = x[:, j * LANES:(j + 1) * LANES]
        up = pltpu.roll(xs, LANES - half, axis=1)
        down = pltpu.roll(xs, half, axis=1)
        outs.append(xs * cos_t + up * sin_lo + down * sin_hi)
    return jnp.concatenate(outs, axis=-1)


def _odd_proj_kernel(x_ref, g_ref, w_ref, cos_ref, slo_ref, shi_ref,
                     qn_ref, qr_ref, kvc_ref, ks_ref, vs_ref, kw_ref, vw_ref, gl_ref):
    h = _rms(x_ref[...], g_ref[...])
    y = jnp.dot(h.astype(BF16), w_ref[...], preferred_element_type=F32)
    cos_t, sin_lo, sin_hi = cos_ref[...], slo_ref[...], shi_ref[...]
    dq = NSA_HEADS * NSA_HEAD_DIM
    kv = NSA_KV_WIDTH
    q = y[:, :dq] * (LOG2_E * NSA_HEAD_DIM ** -0.5)
    qn_ref[...] = q.astype(BF16)
    qr_ref[...] = _rope(q, cos_t, sin_lo, sin_hi).astype(BF16)
    kvc_ref[...] = y[:, dq:dq + 2 * kv].astype(BF16)
    dh = NSA_HEAD_DIM
    tm = y.shape[0]
    ones_rows = (lax.broadcasted_iota(jnp.int32, (VALUE_ROWS - dh, tm), 0) == 0).astype(BF16)
    for k_ref, v_ref, col in ((ks_ref, vs_ref, dq + 2 * kv), (kw_ref, vw_ref, dq + 4 * kv)):
        k = _rope(y[:, col:col + kv], cos_t, sin_lo, sin_hi).astype(BF16)
        v_t = y[:, col + kv:col + 2 * kv].T
        for g in range(NSA_KV_GROUPS):
            k_ref[0, g] = k[:, g * dh:(g + 1) * dh]
            v_ref[0, g, 0:dh, :] = v_t[g * dh:(g + 1) * dh].astype(BF16)
            v_ref[0, g, dh:VALUE_ROWS, :] = ones_rows
    gl_ref[...] = y[:, dq + 6 * kv:]


def _odd_proj(x, gain, w, tables, batch, t_len, tm=512):
    m, d = x.shape
    n = w.shape[1]
    n_g, dh = NSA_KV_GROUPS, NSA_HEAD_DIM
    dq = NSA_HEADS * dh
    kv = NSA_KV_WIDTH
    ng = n - dq - 6 * kv
    tpb = t_len // tm
    row = lambda width: pl.BlockSpec((tm, width), lambda i: (i, 0))
    tab = pl.BlockSpec((tm, LANES), lambda i: (i % tpb, 0))
    keys = (pl.BlockSpec((1, n_g, tm, dh), lambda i: (i // tpb, 0, i % tpb, 0)),
            jax.ShapeDtypeStruct((batch, n_g, t_len, dh), BF16))
    values_t = (pl.BlockSpec((1, n_g, VALUE_ROWS, tm), lambda i: (i // tpb, 0, 0, i % tpb)),
                jax.ShapeDtypeStruct((batch, n_g, VALUE_ROWS, t_len), BF16))
    flat = lambda width, dt: (row(width), jax.ShapeDtypeStruct((m, width), dt))
    outs = [flat(dq, BF16), flat(dq, BF16), flat(2 * kv, BF16), keys, values_t, keys, values_t, flat(ng, F32)]
    return pl.pallas_call(
        _odd_proj_kernel,
        grid=(m // tm,),
        in_specs=[row(d), pl.BlockSpec((1, d), lambda i: (0, 0)),
                  pl.BlockSpec((d, n), lambda i: (0, 0)), tab, tab, tab],
        out_specs=[spec for spec, _ in outs],
        out_shape=[shape for _, shape in outs],
        compiler_params=_params("parallel"),
        name="odd_proj",
    )(x, gain.reshape(1, d), w, *tables)


def _compress_kernel(xa_ref, xb_ref, pos_ref, a1_ref, a2t_ref, b1_ref, b2_ref, at_ref, b_ref):
    half = xa_ref.shape[2]
    n = xa_ref.shape[1]

    def hidden(x_ref, w1_ref):
        x = x_ref[0]
        top = jnp.dot(x, w1_ref[0:half, :], preferred_element_type=F32)
        bot = jnp.dot(x, w1_ref[half:2 * half, :], preferred_element_type=F32)
        bias = (jnp.dot(pos_ref[:, 0:half], w1_ref[0:half, :], preferred_element_type=F32)
                + jnp.dot(pos_ref[:, half:2 * half], w1_ref[half:2 * half, :], preferred_element_type=F32))
        nxt = pltpu.roll(bot, n - 1, axis=0)
        return _silu(top + nxt + bias[0:1, :]).astype(BF16)

    at_ref[0] = lax.dot_general(a2t_ref[...], hidden(xa_ref, a1_ref), (((1,), (1,)), ((), ())),
                                preferred_element_type=F32).astype(BF16)
    b_ref[0] = jnp.dot(hidden(xb_ref, b1_ref), b2_ref[...], preferred_element_type=F32).astype(BF16)


def _compress(xa, xb, pos, a1, a2t, b1, b2):
    bg, n, half = xa.shape
    dh = NSA_HEAD_DIM
    full = lambda a: pl.BlockSpec(a.shape, lambda i: (0,) * a.ndim)
    return pl.pallas_call(
        _compress_kernel,
        grid=(bg,),
        in_specs=[pl.BlockSpec((1, n, half), lambda i: (i, 0, 0)),
                  pl.BlockSpec((1, n, half), lambda i: (i, 0, 0)),
                  full(pos), full(a1), full(a2t), full(b1), full(b2)],
        out_specs=[pl.BlockSpec((1, dh, n), lambda i: (i, 0, 0)),
                   pl.BlockSpec((1, n, dh), lambda i: (i, 0, 0))],
        out_shape=[jax.ShapeDtypeStruct((bg, dh, n), BF16), jax.ShapeDtypeStruct((bg, n, dh), BF16)],
        compiler_params=_params("parallel"),
        name="compress",
    )(xa, xb, pos, a1, a2t, b1, b2)


MASKED = -1e30


def _nsa_kernel(qn_ref, qr_ref, gl_ref, kc_ref, vo_ref, ks_ref, vst_ref, kw_ref, vwt_ref, expt_ref,
                o_ref, s_even, s_odd, stats, acc_buf):
    qb = Q_BLOCK
    dh = NSA_HEAD_DIM
    hpg = NSA_HPG
    n_cmp = kc_ref.shape[1]
    n_blk = expt_ref.shape[1]
    q0 = pl.program_id(2) * qb
    t_row = q0 + lax.broadcasted_iota(jnp.int32, (1, qb), 1)

    def heads_t(ref):
        xt = ref[...].astype(F32).T
        return jnp.concatenate([xt[h * dh:(h + 1) * dh] for h in range(hpg)], axis=1).astype(BF16)

    qn_t = heads_t(qn_ref)
    qr_t = heads_t(qr_ref)

    def masked(s, mask, fill):
        return jnp.concatenate([jnp.where(mask, s[:, h * qb:(h + 1) * qb], fill) for h in range(hpg)], axis=1)

    def exp2_cols(s):
        m = jnp.max(s, axis=0, keepdims=True)
        return jnp.exp2(s - jnp.where(m == -jnp.inf, 0.0, m))

    cmp_end = lax.broadcasted_iota(jnp.int32, (n_cmp, 1), 0) * CMP_STRIDE + (CMP_BLOCK - 1)
    s_c = jnp.dot(kc_ref[0], qn_t, preferred_element_type=F32)
    e_c = exp2_cols(masked(s_c, cmp_end <= t_row, -jnp.inf)).astype(BF16)
    both = jnp.dot(vo_ref[0], e_c, preferred_element_type=F32)
    acc_c = both[0:VALUE_ROWS]
    den_c = acc_c[dh:dh + 1, :]
    imp_all = both[VALUE_ROWS:VALUE_ROWS + n_blk] * (1.0 / jnp.where(den_c > 0.0, den_c, 1.0))
    imp_t = imp_all[:, 0:qb]
    for h in range(1, hpg):
        imp_t = imp_t + imp_all[:, h * qb:(h + 1) * qb]

    span = WINDOW + qb
    w_start = pl.multiple_of(jnp.maximum(q0 - WINDOW, 0), qb)
    dist = t_row - (w_start + lax.broadcasted_iota(jnp.int32, (span, 1), 0))
    s_w = jnp.dot(kw_ref[0, 0, pl.ds(w_start, span), :], qr_t, preferred_element_type=F32)
    e_w = exp2_cols(masked(s_w, (dist >= 0) & (dist < WINDOW), -jnp.inf))
    acc_w = jnp.dot(vwt_ref[0, 0, :, pl.ds(w_start, span)], e_w.astype(BF16), preferred_element_type=F32)

    js = lax.broadcasted_iota(jnp.int32, (n_blk, qb), 0).astype(F32)
    cur = (t_row // SEL_BLOCK).astype(F32)
    forced = (js == 0.0) | (js == cur) | (js == cur - 1.0)
    val = jnp.where((js > cur) | forced, -jnp.inf, imp_t)
    for _ in range(N_SELECT - N_FORCED):
        best = jnp.max(val, axis=0, keepdims=True)
        first = jnp.min(jnp.where(val == best, js, float(n_blk)), axis=0, keepdims=True)
        val = jnp.where(js == first, -jnp.inf, val)
    chosen = val == -jnp.inf
    kt = SEL_KEY_TILE
    n_tiles = ks_ref.shape[2] // kt
    first_own = (q0 // SEL_BLOCK).astype(F32)

    def extended(bias):
        return jnp.concatenate([jnp.concatenate([bias.astype(BF16)] * hpg, axis=1), qr_t], axis=0)

    q_ext_t = extended(jnp.where(chosen & (js < first_own), 0.0, MASKED))
    q_own_t = extended(jnp.where(chosen & (js >= first_own) & (js <= cur), 0.0, MASKED))

    def scores(i, s_ref):
        start = pl.multiple_of(jnp.minimum(i, n_tiles - 1) * kt, kt)
        k_ext = jnp.concatenate([expt_ref[pl.ds(start, kt), :], ks_ref[0, 0, pl.ds(start, kt), :]], axis=1)
        s = jnp.dot(k_ext, q_ext_t, preferred_element_type=F32)
        s_ref[...] = s
        return jnp.max(s, axis=0, keepdims=True)

    def accumulate(s, m_tile, v_t, carry):
        m_run, acc = carry
        m_new = jnp.maximum(m_run, m_tile)
        p = jnp.exp2(s - m_new)
        acc = jnp.exp2(m_run - m_new) * acc + jnp.dot(v_t, p.astype(BF16), preferred_element_type=F32)
        return m_new, acc

    def consume(i, s_ref, m_tile, carry):
        start = pl.multiple_of(i * kt, kt)
        return accumulate(s_ref[...], m_tile, vst_ref[0, 0, :, pl.ds(start, kt)], carry)

    def pair_step(j, carry):
        m_run, acc, m_even = carry
        m_odd = scores(2 * j + 1, s_odd)
        m_run, acc = consume(2 * j, s_even, m_even, (m_run, acc))
        m_even = scores(2 * j + 2, s_even)
        m_run, acc = consume(2 * j + 1, s_odd, m_odd, (m_run, acc))
        return m_run, acc, m_even

    cols = hpg * qb
    init = (jnp.full((1, cols), MASKED, F32), jnp.zeros((VALUE_ROWS, cols), F32), scores(0, s_even))
    n_past = (q0 + kt - 1) // kt
    m_run, acc_s, m_even = lax.fori_loop(0, n_past // 2, pair_step, init)
    stats[0:1, :] = m_run
    stats[1:2, :] = m_even
    acc_buf[...] = acc_s

    @pl.when(n_past % 2 == 1)
    def _():
        m_last, acc_last = consume(n_past - 1, s_even, stats[1:2, :], (stats[0:1, :], acc_buf[...]))
        stats[0:1, :] = m_last
        acc_buf[...] = acc_last

    m_run, acc_s = stats[0:1, :], acc_buf[...]
    kpos = q0 + lax.broadcasted_iota(jnp.int32, (qb, 1), 0)
    k_own = jnp.concatenate([expt_ref[pl.ds(q0, qb), :], ks_ref[0, 0, pl.ds(q0, qb), :]], axis=1)
    s_own = masked(jnp.dot(k_own, q_own_t, preferred_element_type=F32), kpos <= t_row, MASKED)
    _, acc_s = accumulate(s_own, jnp.max(s_own, axis=0, keepdims=True), vst_ref[0, 0, :, pl.ds(q0, qb)],
                          (m_run, acc_s))

    gates_t = _sigmoid(gl_ref[...].T)
    outs = []
    for h in range(hpg):
        c = slice(h * qb, (h + 1) * qb)
        o_t = jnp.zeros((dh, qb), F32)
        for j, acc in enumerate((acc_c, acc_s, acc_w)):
            den = acc[dh:dh + 1, c]
            o_t = o_t + (gates_t[3 * h + j:3 * h + j + 1, :] / jnp.where(den > 0.0, den, 1.0)) * acc[0:dh, c]
        outs.append(o_t)
    o_ref[...] = jnp.concatenate(outs, axis=0).T.astype(o_ref.dtype)


def _nsa_attention(qn, qr, gl, kc, vo, ks, vst, kw, vwt, expand_t, batch, t_len):
    assert t_len % (2 * SEL_KEY_TILE) == 0, "key tiles of the selected branch are processed in pairs"
    m = qn.shape[0]
    n_g = NSA_KV_GROUPS
    gw = NSA_GROUP_WIDTH
    dh = NSA_HEAD_DIM
    nq = t_len // Q_BLOCK
    n_cmp = kc.shape[1]
    qspec = pl.BlockSpec((Q_BLOCK, gw), lambda b, g, i: (b * nq + i, g))
    keys = pl.BlockSpec((1, 1, t_len, dh), lambda b, g, i: (b, g, 0, 0))
    values_t = pl.BlockSpec((1, 1, VALUE_ROWS, t_len), lambda b, g, i: (b, g, 0, 0))
    const = lambda a: pl.BlockSpec(a.shape, lambda b, g, i: (0, 0))
    return pl.pallas_call(
        _nsa_kernel,
        grid=(batch, n_g, nq),
        in_specs=[qspec, qspec,
                  pl.BlockSpec((Q_BLOCK, LANES), lambda b, g, i: (b * nq + i, g)),
                  pl.BlockSpec((1, n_cmp, dh), lambda b, g, i: (b * n_g + g, 0, 0)),
                  pl.BlockSpec((1,) + vo.shape[1:], lambda b, g, i: (b * n_g + g, 0, 0)),
                  keys, values_t, keys, values_t, const(expand_t)],
        out_specs=qspec,
        out_shape=jax.ShapeDtypeStruct((m, n_g * gw), BF16),
        scratch_shapes=[pltpu.VMEM((SEL_KEY_TILE, NSA_HPG * Q_BLOCK), F32)] * 2
        + [pltpu.VMEM((SUBLANES, NSA_HPG * Q_BLOCK), F32), pltpu.VMEM((VALUE_ROWS, NSA_HPG * Q_BLOCK), F32)],
        compiler_params=_params("parallel", "parallel", "arbitrary"),
        name="nsa_attention",
    )(qn, qr, gl, kc, vo, ks, vst, kw, vwt, expand_t)


def _pad_cols(w, n):
    return jnp.pad(w, ((0, 0), (0, n - w.shape[1])))


def _even_layer(x, gain, w_in, qkv_conv, a_log, dt_bias, o_norm, sc_conv, w_out, batch, t_len, mlp, final):
    gw = GDN_WIDTH
    ab = _pad_cols(w_in[:, 4 * gw:4 * gw + 2 * GDN_HEADS], LANES)
    w_all = jnp.concatenate([w_in[:, :3 * gw], ab, w_in[:, 3 * gw:4 * gw], w_in[:, 4 * gw + 2 * GDN_HEADS:]],
                            axis=1).astype(BF16)
    proj, gates = _norm_matmul(x, gain, w_all, 3 * gw + LANES)
    gate_params = jnp.zeros((SUBLANES, LANES), F32)
    gate_params = gate_params.at[0, :GDN_HEADS].set(a_log).at[1, :GDN_HEADS].set(dt_bias)
    ya = _gdn(proj, gates, qkv_conv, gate_params, o_norm, batch, t_len)
    return _even_tail(x, ya, gates, sc_conv, w_out[:gw].astype(BF16), w_out[gw:].astype(BF16), t_len, mlp, final)


def _rope_tables(t_len):
    half = ROPE_DIM // 2
    inv_freq = ROPE_THETA ** (-jnp.arange(0, ROPE_DIM, 2, dtype=F32) / ROPE_DIM)
    ang = jnp.arange(t_len, dtype=F32)[:, None] * inv_freq[None, :]
    cos, sin = jnp.cos(ang), jnp.sin(ang)
    pad = NSA_HEAD_DIM - ROPE_DIM
    head = lambda a, b, fill: jnp.concatenate([a, b, jnp.full((t_len, pad), fill, F32)], axis=1)
    zeros = jnp.zeros((t_len, half), F32)
    reps = LANES // NSA_HEAD_DIM
    return (jnp.tile(head(cos, cos, 1.0), (1, reps)),
            jnp.tile(head(-sin, zeros, 0.0), (1, reps)),
            jnp.tile(head(zeros, sin, 0.0), (1, reps)))


def _odd_layer(x, gain, w_in, cmp_pos, k_w1, k_w2, v_w1, v_w2, w_out, batch, t_len, mlp, final):
    n_g, dh, hpg = NSA_KV_GROUPS, NSA_HEAD_DIM, NSA_HPG
    dq = NSA_HEADS * dh
    kv = NSA_KV_WIDTH
    gate_w = w_in[:, dq + 6 * kv:].reshape(-1, n_g, 3 * hpg)
    gate_w = jnp.pad(gate_w, ((0, 0), (0, 0), (0, LANES - 3 * hpg))).reshape(-1, n_g * LANES)
    w_all = jnp.concatenate([w_in[:, :dq + 6 * kv], gate_w], axis=1).astype(BF16)
    qn, qr, kvc, ks, vs_t, kw, vw_t, gl = _odd_proj(x, gain, w_all, _rope_tables(t_len), batch, t_len)

    n_chunk = t_len // CMP_STRIDE

    def chunked(a):
        a = a.reshape(batch, n_chunk, CMP_STRIDE, n_g, dh)
        return jnp.moveaxis(a, 3, 1).reshape(batch * n_g, n_chunk, CMP_STRIDE * dh)

    pos = jnp.zeros((SUBLANES, CMP_BLOCK * dh), F32).at[0].set(cmp_pos.reshape(-1)).astype(BF16)
    vc_t, kc = _compress(chunked(kvc[:, kv:]), chunked(kvc[:, :kv]), pos, v_w1.astype(BF16),
                         v_w2.T.astype(BF16), k_w1.astype(BF16), k_w2.astype(BF16))

    def with_ones(a):
        n = a.shape[-1]
        return jnp.concatenate([a, jnp.ones((a.shape[0], 1, n), a.dtype),
                                jnp.zeros((a.shape[0], VALUE_ROWS - dh - 1, n), a.dtype)], axis=1)

    n_blk = _round_up(t_len // SEL_BLOCK, LANES)
    c_start = np.arange(n_chunk)[None, :] * CMP_STRIDE
    s_start = np.arange(n_blk)[:, None] * SEL_BLOCK
    overlap_t = jnp.asarray((c_start < s_start + SEL_BLOCK) & (c_start + CMP_BLOCK > s_start), BF16)
    expand_t = jnp.asarray((np.arange(t_len)[:, None] // SEL_BLOCK) == np.arange(n_blk)[None, :], BF16)
    vo = jnp.concatenate([with_ones(vc_t), jnp.broadcast_to(overlap_t, (batch * n_g,) + overlap_t.shape)], axis=1)
    o = _nsa_attention(qn, qr, gl, kc, vo, ks, vs_t, kw, vw_t, expand_t, batch, t_len)
    return _odd_tail(x, o, w_out.astype(BF16), mlp, final)


def kernel(x, mix_norm, mlp_norm, w_up, w_down, final_norm, ev_w_in, ev_qkv_conv, ev_a_log, ev_dt_bias,
           ev_o_norm, ev_sc_conv, ev_w_out, od_w_in, od_cmp_pos, od_cmp_k_w1, od_cmp_k_w2, od_cmp_v_w1,
           od_cmp_v_w2, od_w_out):
    batch, t_len, d = x.shape
    depth = mix_norm.shape[0]
    xs = x.reshape(batch * t_len, d)
    for layer in range(depth):
        i = layer // 2
        mlp = (mlp_norm[layer], w_up[layer].astype(BF16), w_down[layer].astype(BF16), final_norm)
        final = layer == depth - 1
        if layer % 2 == 0:
            xs = _even_layer(xs, mix_norm[layer], ev_w_in[i], ev_qkv_conv[i], ev_a_log[i], ev_dt_bias[i],
                             ev_o_norm[i], ev_sc_conv[i], ev_w_out[i], batch, t_len, mlp, final)
        else:
            xs = _odd_layer(xs, mix_norm[layer], od_w_in[i], od_cmp_pos[i], od_cmp_k_w1[i], od_cmp_k_w2[i],
                            od_cmp_v_w1[i], od_cmp_v_w2[i], od_w_out[i], batch, t_len, mlp, final)
    return xs.reshape(batch, t_len, d)
```

```python
import functools

import jax
import jax.numpy as jnp
import numpy as np
from jax import lax
from jax.experimental import pallas as pl
from jax.experimental.pallas import tpu as pltpu

F32 = jnp.float32
BF16 = jnp.bfloat16
HIGHEST = lax.Precision.HIGHEST

NORM_EPS = 1e-6
GDN_HEADS = 8
GDN_HEAD_DIM = 64
GDN_WIDTH = GDN_HEADS * GDN_HEAD_DIM
GDN_CONV = 4
GDN_CHUNK = 64
SOLVE_BLOCK = 16
SC_CONV = 3
NSA_HEADS = 16
NSA_HEAD_DIM = 64
NSA_KV_GROUPS = 4
NSA_HPG = NSA_HEADS // NSA_KV_GROUPS
NSA_GROUP_WIDTH = NSA_HPG * NSA_HEAD_DIM
NSA_KV_WIDTH = NSA_KV_GROUPS * NSA_HEAD_DIM
CMP_BLOCK = 32
CMP_STRIDE = 16
SEL_BLOCK = 64
N_SELECT = 16
N_FORCED = 3
WINDOW = 512
Q_BLOCK = 512
CMP_PREFIX_STEP = 128
N_Q_PARTS = 2
SEL_KEY_TILE = 512
VALUE_ROWS = NSA_HEAD_DIM + 16
ROPE_THETA = 500000.0
ROPE_DIM = NSA_HEAD_DIM // 4
LOG2_E = 1.4426950408889634
LANES = 128
SUBLANES = 8
BF16_ROWS = 16
VMEM_LIMIT = 56 * 1024 * 1024


def _round_up(n, m):
    return (n + m - 1) // m * m


def _params(*semantics):
    return pltpu.CompilerParams(dimension_semantics=semantics, vmem_limit_bytes=VMEM_LIMIT)


def _rms(x, gain):
    return x * lax.rsqrt(jnp.mean(x * x, axis=-1, keepdims=True) + NORM_EPS) * gain


def _sigmoid(x):
    return 1.0 / (1.0 + jnp.exp(-x))


def _silu(x):
    return x * _sigmoid(x)


def _softplus(x):
    return jnp.maximum(x, 0.0) + jnp.log(1.0 + jnp.exp(-jnp.abs(x)))


def _dot(a, b):
    return jnp.dot(a.astype(BF16), b.astype(BF16), preferred_element_type=F32)


def _dot_nt(a, b):
    return lax.dot_general(a.astype(BF16), b.astype(BF16), (((1,), (1,)), ((), ())),
                           preferred_element_type=F32)


def _dot_f32(a, b):
    return jnp.dot(a, b, precision=HIGHEST, preferred_element_type=F32)


def _norm_matmul_kernel(x_ref, g_ref, w_ref, o_ref, ob_ref):
    h = _rms(x_ref[...], g_ref[...])
    y = jnp.dot(h.astype(BF16), w_ref[...], preferred_element_type=F32)
    n = o_ref.shape[1]
    o_ref[...] = y[:, :n]
    ob_ref[...] = y[:, n:].astype(BF16)


def _norm_matmul(x, gain, w, n_f32, tm=512):
    m, d = x.shape
    n = w.shape[1]
    return pl.pallas_call(
        _norm_matmul_kernel,
        grid=(m // tm,),
        in_specs=[pl.BlockSpec((tm, d), lambda i: (i, 0)),
                  pl.BlockSpec((1, d), lambda i: (0, 0)),
                  pl.BlockSpec((d, n), lambda i: (0, 0))],
        out_specs=[pl.BlockSpec((tm, n_f32), lambda i: (i, 0)), pl.BlockSpec((tm, n - n_f32), lambda i: (i, 0))],
        out_shape=[jax.ShapeDtypeStruct((m, n_f32), F32), jax.ShapeDtypeStruct((m, n - n_f32), BF16)],
        compiler_params=_params("parallel"),
        name="norm_proj",
    )(x, gain.reshape(1, d), w)


def _split(a):
    hi = a.astype(BF16)
    return hi, (a - hi.astype(F32)).astype(BF16)


def _dot3(a, b):
    (ah, al), (bh, bl) = a, b
    return (jnp.dot(ah, bh, preferred_element_type=F32) + jnp.dot(ah, bl, preferred_element_type=F32)
            + jnp.dot(al, bh, preferred_element_type=F32))


def _gdn_kernel(qkv_ref, z_ref, ab_ref, cw_ref, gp_ref, on_ref, o_ref, xbuf, act, state):
    c = GDN_CHUNK
    dh = GDN_HEAD_DIM
    pw = 2 * dh
    n_pairs = GDN_HEADS // 2
    n_batch = qkv_ref.shape[0]
    hist = SUBLANES
    chains = [(b, p) for b in range(n_batch) for p in range(n_pairs)]

    @pl.when(pl.program_id(0) == 0)
    def _():
        xbuf[:, 0:hist, :] = jnp.zeros((n_batch, hist, 3 * GDN_WIDTH), F32)
        state[...] = jnp.zeros(state.shape, F32)

    row = lax.broadcasted_iota(jnp.int32, (c, pw), 0)
    lane = lax.broadcasted_iota(jnp.int32, (c, pw), 1)
    first = lane < dh
    col = jnp.where(first, lane, lane - dh)
    incl = row >= col
    strict = row > col
    eye = (row == col).astype(F32)
    diag_blk = strict & ((row // SOLVE_BLOCK) == (col // SOLVE_BLOCK))
    r2 = lax.broadcasted_iota(jnp.int32, (pw, pw), 0)
    c2 = lax.broadcasted_iota(jnp.int32, (pw, pw), 1)
    same_head = (r2 < dh) == (c2 < dh)
    ones_bd = same_head.astype(BF16)
    tri = (lax.broadcasted_iota(jnp.int32, (c, c), 0) >= lax.broadcasted_iota(jnp.int32, (c, c), 1)).astype(F32)

    def blockdiag(x):
        zero = jnp.zeros_like(x)
        return jnp.concatenate([jnp.where(first, x, zero), jnp.where(first, zero, x)], axis=0)

    def head_sums(x):
        return jnp.dot(x.astype(BF16), ones_bd, preferred_element_type=F32)

    def pdot(xs, y):
        hi, lo = _split(y)
        return _dot3(xs, (blockdiag(hi), blockdiag(lo)))

    gcs, betas = [], []
    for b in range(n_batch):
        x = qkv_ref[b]
        xbuf[b, hist:hist + c, :] = x
        y = x * cw_ref[GDN_CONV - 1:GDN_CONV, :]
        for j in range(GDN_CONV - 1):
            shift = GDN_CONV - 1 - j
            y = y + xbuf[b, hist - shift:hist - shift + c, :] * cw_ref[j:j + 1, :]
        xbuf[b, 0:hist, :] = x[c - hist:c, :]
        act[b] = _silu(y)
        ab = ab_ref[b]
        g_all = -jnp.exp(gp_ref[0:1, :]) * _softplus(ab + gp_ref[1:2, :])
        gcs.append(_dot_f32(tri, g_all))
        betas.append(_sigmoid(ab))

    def pair_cols(a, p, offset):
        return jnp.where(first, a[:, offset + 2 * p:offset + 2 * p + 1], a[:, offset + 2 * p + 1:offset + 2 * p + 2])

    scale = dh ** -0.5
    q = [act[b, :, p * pw:(p + 1) * pw] for b, p in chains]
    k = [act[b, :, GDN_WIDTH + p * pw:GDN_WIDTH + (p + 1) * pw] for b, p in chains]
    v = [act[b, :, 2 * GDN_WIDTH + p * pw:2 * GDN_WIDTH + (p + 1) * pw] for b, p in chains]
    q = [x * lax.rsqrt(head_sums(x * x) + NORM_EPS) * scale for x in q]
    k = [x * lax.rsqrt(head_sums(x * x) + NORM_EPS) for x in k]
    gcol = [pair_cols(gcs[b], p, 0) for b, p in chains]
    bcol = [pair_cols(betas[b], p, GDN_HEADS) for b, p in chains]
    grow = [jnp.sum(g * eye, axis=0, keepdims=True) for g in gcol]
    g_last = [g[c - 1:c, :] for g in gcol]
    decay = [jnp.exp(jnp.where(incl, gc_ - gr_, -jnp.inf)) for gc_, gr_ in zip(gcol, grow)]
    kb = [k_ * b_ for k_, b_ in zip(k, bcol)]
    k_bd = [blockdiag(k_).astype(BF16) for k_ in k]
    both = [_dot_nt(jnp.concatenate([kb_, q_], axis=0), kd_) for kb_, q_, kd_ in zip(kb, q, k_bd)]
    m_low = [jnp.where(strict, x[0:c] * d_, 0.0) for x, d_ in zip(both, decay)]
    attn = [x[c:2 * c] * d_ for x, d_ in zip(both, decay)]

    d_pow = [jnp.where(diag_blk, m_, 0.0) for m_ in m_low]
    l_mat = [m_ - d_ for m_, d_ in zip(m_low, d_pow)]
    p_inv = [eye - d_ for d_ in d_pow]
    d_pow = [pdot(_split(d_), d_) for d_ in d_pow]
    n_steps = int(np.log2(SOLVE_BLOCK)) - 1
    for step in range(n_steps):
        if step == n_steps - 1:
            p_inv = [p_ + pdot(_split(p_), d_) for p_, d_ in zip(p_inv, d_pow)]
        else:
            prod = [pdot(_split(jnp.concatenate([p_, d_], axis=0)), d_) for p_, d_ in zip(p_inv, d_pow)]
            p_inv = [p_ + x[0:c] for p_, x in zip(p_inv, prod)]
            d_pow = [x[c:2 * c] for x in prod]
    n_pow = [pdot(_split(p_), l_) for p_, l_ in zip(p_inv, l_mat)]
    q_inv = [eye - n_ for n_ in n_pow]
    for _ in range(int(np.log2(c // SOLVE_BLOCK)) - 1):
        n_pow = [pdot(_split(n_), n_) for n_ in n_pow]
        q_inv = [q_ + pdot(_split(q_), n_) for q_, n_ in zip(q_inv, n_pow)]
    a_inv = [pdot(_split(q_), p_) for q_, p_ in zip(q_inv, p_inv)]
    u = [_dot(a_, blockdiag(v_ * b_)) for a_, v_, b_ in zip(a_inv, v, bcol)]
    w = [_dot(a_, blockdiag(kb_ * jnp.exp(g_))) for a_, kb_, g_ in zip(a_inv, kb, gcol)]

    s_old = [state[b * n_pairs + p] for b, p in chains]
    on_s = [_dot(jnp.concatenate([w_, q_ * jnp.exp(g_)], axis=0), s_) for w_, q_, g_, s_ in zip(w, q, gcol, s_old)]
    v_new = [u_ - x[0:c] for u_, x in zip(u, on_s)]
    o = [x[c:2 * c] + _dot(a_, blockdiag(vn_)) for x, a_, vn_ in zip(on_s, attn, v_new)]
    k_dec_t = [(k_ * jnp.exp(gl_ - g_)).T for k_, gl_, g_ in zip(k, g_last, gcol)]
    for (b, p), s_, gl_, kt_, vn_ in zip(chains, s_old, g_last, k_dec_t, v_new):
        state[b * n_pairs + p] = s_ * jnp.exp(gl_) + jnp.where(same_head, _dot(kt_, vn_), 0.0)

    inv_dh = 1.0 / dh
    for (b, p), o_ in zip(chains, o):
        zp = z_ref[b, :, p * pw:(p + 1) * pw].astype(F32)
        y = o_ * lax.rsqrt(head_sums(o_ * o_) * inv_dh + NORM_EPS) * on_ref[...]
        o_ref[b, :, p * pw:(p + 1) * pw] = (y * _silu(zp)).astype(o_ref.dtype)


def _gdn(proj, gates, conv_w, gate_params, o_norm, batch, t_len):
    c = GDN_CHUNK
    n_chunks = t_len // c
    w3 = 3 * GDN_WIDTH
    ab_col = w3 // LANES
    proj = proj.reshape(batch, t_len, proj.shape[-1])
    gates = gates.reshape(batch, t_len, gates.shape[-1])
    on_pair = jnp.tile(o_norm.reshape(1, GDN_HEAD_DIM), (1, 2))
    out = pl.pallas_call(
        _gdn_kernel,
        grid=(n_chunks,),
        in_specs=[pl.BlockSpec((batch, c, w3), lambda i: (0, i, 0)),
                  pl.BlockSpec((batch, c, GDN_WIDTH), lambda i: (0, i, 0)),
                  pl.BlockSpec((batch, c, LANES), lambda i: (0, i, ab_col)),
                  pl.BlockSpec((GDN_CONV, w3), lambda i: (0, 0)),
                  pl.BlockSpec((SUBLANES, LANES), lambda i: (0, 0)),
                  pl.BlockSpec((1, 2 * GDN_HEAD_DIM), lambda i: (0, 0))],
        out_specs=pl.BlockSpec((batch, c, GDN_WIDTH), lambda i: (0, i, 0)),
        out_shape=jax.ShapeDtypeStruct((batch, t_len, GDN_WIDTH), BF16),
        scratch_shapes=[pltpu.VMEM((batch, c + SUBLANES, w3), F32),
                        pltpu.VMEM((batch, c, w3), F32),
                        pltpu.VMEM((batch * GDN_HEADS // 2, 2 * GDN_HEAD_DIM, 2 * GDN_HEAD_DIM), F32)],
        compiler_params=_params("arbitrary"),
        name="gdn",
    )(proj, gates, proj, conv_w, gate_params, on_pair)
    return out.reshape(batch * t_len, GDN_WIDTH)


def _mlp_tail(final, x, g_ref, wu_ref, wd_ref, fg_ref, o_ref):
    u = jnp.dot(_rms(x, g_ref[...]).astype(BF16), wu_ref[...], preferred_element_type=F32)
    u = jnp.square(jnp.maximum(u, 0.0))
    y = x + jnp.dot(u.astype(BF16), wd_ref[...], preferred_element_type=F32)
    if final:
        y = _rms(y, fg_ref[...])
    o_ref[...] = y


def _even_tail_kernel(tiles_per_batch, final, x_ref, ya_ref, bg_ref, cg_ref, hs_ref, cgp_ref, hsp_ref,
                      cw_ref, wa_ref, wb_ref, g_ref, wu_ref, wd_ref, fg_ref, o_ref, ubuf):
    tm = x_ref.shape[0]
    hist = cgp_ref.shape[0]
    first = (pl.program_id(0) % tiles_per_batch) == 0
    prev = cgp_ref[...].astype(F32) * hsp_ref[...].astype(F32)
    ubuf[0:hist, :] = jnp.where(first, 0.0, prev)
    u = cg_ref[...].astype(F32) * hs_ref[...].astype(F32)
    ubuf[hist:hist + tm, :] = u
    conv = u * cw_ref[SC_CONV - 1:SC_CONV, :]
    for j in range(SC_CONV - 1):
        shift = SC_CONV - 1 - j
        conv = conv + ubuf[hist - shift:hist - shift + tm, :] * cw_ref[j:j + 1, :]
    yb = bg_ref[...].astype(F32) * conv
    mix = jnp.dot(ya_ref[...], wa_ref[...], preferred_element_type=F32)
    mix = mix + jnp.dot(yb.astype(BF16), wb_ref[...], preferred_element_type=F32)
    _mlp_tail(final, x_ref[...] + mix, g_ref, wu_ref, wd_ref, fg_ref, o_ref)


def _odd_tail_kernel(final, x_ref, a_ref, w_ref, g_ref, wu_ref, wd_ref, fg_ref, o_ref):
    mix = jnp.dot(a_ref[...], w_ref[...], preferred_element_type=F32)
    _mlp_tail(final, x_ref[...] + mix, g_ref, wu_ref, wd_ref, fg_ref, o_ref)


def _resident(a):
    return pl.BlockSpec(a.shape, lambda i: (0, 0), pipeline_mode=pl.Buffered(1))


def _layer_tail(body, x, mixer_args, mixer_specs, mlp, final, scratch=(), tm=512):
    gain, w_up, w_down, final_gain = mlp
    m, d = x.shape
    row = pl.BlockSpec((tm, d), lambda i: (i, 0))
    vec = pl.BlockSpec((1, d), lambda i: (0, 0))
    return pl.pallas_call(
        functools.partial(body, final),
        grid=(m // tm,),
        in_specs=[row, *mixer_specs, vec, _resident(w_up), _resident(w_down), vec],
        out_specs=row,
        out_shape=jax.ShapeDtypeStruct((m, d), F32),
        scratch_shapes=list(scratch),
        compiler_params=_params("parallel"),
        name="layer_tail",
    )(x, *mixer_args, gain.reshape(1, d), w_up, w_down, final_gain.reshape(1, d))


def _even_tail(x, ya, gates, sc_conv, w_a, w_b, t_len, mlp, final, tm=512):
    wd = GDN_WIDTH
    hist = BF16_ROWS
    hb = tm // hist
    prev = lambda col: pl.BlockSpec((hist, wd), lambda i: (jnp.maximum(i * hb - 1, 0), col))
    cur = lambda col: pl.BlockSpec((tm, wd), lambda i: (i, col))
    specs = [cur(0), cur(1), cur(2), cur(3), prev(2), prev(3), _resident(sc_conv), _resident(w_a), _resident(w_b)]
    return _layer_tail(functools.partial(_even_tail_kernel, t_len // tm), x,
                       (ya, gates, gates, gates, gates, gates, sc_conv, w_a, w_b), specs, mlp, final,
                       scratch=[pltpu.VMEM((tm + hist, wd), F32)], tm=tm)


def _odd_tail(x, a, w, mlp, final, tm=512):
    specs = [pl.BlockSpec((tm, a.shape[1]), lambda i: (i, 0)), _resident(w)]
    return _layer_tail(_odd_tail_kernel, x, (a, w), specs, mlp, final, tm=tm)


def _rope(x, cos_t, sin_lo, sin_hi):
    half = ROPE_DIM // 2
    outs = []
    for j in range(x.shape[1] // LANES):
        xs = x[:, j * LANES:(j + 1) * LANES]
        up = pltpu.roll(xs, LANES - half, axis=1)
        down = pltpu.roll(xs, half, axis=1)
        outs.append(xs * cos_t + up * sin_lo + down * sin_hi)
    return jnp.concatenate(outs, axis=-1)


def _odd_proj_kernel(x_ref, g_ref, w_ref, cos_ref, slo_ref, shi_ref,
                     qn_ref, qr_ref, kvc_ref, ks_ref, vs_ref, kw_ref, vw_ref, gl_ref):
    h = _rms(x_ref[...], g_ref[...])
    y = jnp.dot(h.astype(BF16), w_ref[...], preferred_element_type=F32)
    cos_t, sin_lo, sin_hi = cos_ref[...], slo_ref[...], shi_ref[...]
    dq = NSA_HEADS * NSA_HEAD_DIM
    kv = NSA_KV_WIDTH
    q = y[:, :dq] * (LOG2_E * NSA_HEAD_DIM ** -0.5)
    qn_ref[...] = q.astype(BF16)
    qr_ref[...] = _rope(q, cos_t, sin_lo, sin_hi).astype(BF16)
    kvc_ref[...] = y[:, dq:dq + 2 * kv].astype(BF16)
    dh = NSA_HEAD_DIM
    tm = y.shape[0]
    ones_rows = (lax.broadcasted_iota(jnp.int32, (VALUE_ROWS - dh, tm), 0) == 0).astype(BF16)
    for k_ref, v_ref, col in ((ks_ref, vs_ref, dq + 2 * kv), (kw_ref, vw_ref, dq + 4 * kv)):
        k = _rope(y[:, col:col + kv], cos_t, sin_lo, sin_hi).astype(BF16)
        v_t = y[:, col + kv:col + 2 * kv].T
        for g in range(NSA_KV_GROUPS):
            k_ref[0, g] = k[:, g * dh:(g + 1) * dh]
            v_ref[0, g, 0:dh, :] = v_t[g * dh:(g + 1) * dh].astype(BF16)
            v_ref[0, g, dh:VALUE_ROWS, :] = ones_rows
    gl_ref[...] = y[:, dq + 6 * kv:]


def _odd_proj(x, gain, w, tables, batch, t_len, tm=512):
    m, d = x.shape
    n = w.shape[1]
    n_g, dh = NSA_KV_GROUPS, NSA_HEAD_DIM
    dq = NSA_HEADS * dh
    kv = NSA_KV_WIDTH
    ng = n - dq - 6 * kv
    tpb = t_len // tm
    row = lambda width: pl.BlockSpec((tm, width), lambda i: (i, 0))
    tab = pl.BlockSpec((tm, LANES), lambda i: (i % tpb, 0))
    keys = (pl.BlockSpec((1, n_g, tm, dh), lambda i: (i // tpb, 0, i % tpb, 0)),
            jax.ShapeDtypeStruct((batch, n_g, t_len, dh), BF16))
    values_t = (pl.BlockSpec((1, n_g, VALUE_ROWS, tm), lambda i: (i // tpb, 0, 0, i % tpb)),
                jax.ShapeDtypeStruct((batch, n_g, VALUE_ROWS, t_len), BF16))
    flat = lambda width, dt: (row(width), jax.ShapeDtypeStruct((m, width), dt))
    outs = [flat(dq, BF16), flat(dq, BF16), flat(2 * kv, BF16), keys, values_t, keys, values_t, flat(ng, F32)]
    return pl.pallas_call(
        _odd_proj_kernel,
        grid=(m // tm,),
        in_specs=[row(d), pl.BlockSpec((1, d), lambda i: (0, 0)),
                  pl.BlockSpec((d, n), lambda i: (0, 0)), tab, tab, tab],
        out_specs=[spec for spec, _ in outs],
        out_shape=[shape for _, shape in outs],
        compiler_params=_params("parallel"),
        name="odd_proj",
    )(x, gain.reshape(1, d), w, *tables)


def _compress_kernel(xa_ref, xb_ref, pos_ref, a1_ref, a2t_ref, b1_ref, b2_ref, at_ref, b_ref):
    half = xa_ref.shape[2]
    n = xa_ref.shape[1]

    def hidden(x_ref, w1_ref):
        x = x_ref[0]
        top = jnp.dot(x, w1_ref[0:half, :], preferred_element_type=F32)
        bot = jnp.dot(x, w1_ref[half:2 * half, :], preferred_element_type=F32)
        bias = (jnp.dot(pos_ref[:, 0:half], w1_ref[0:half, :], preferred_element_type=F32)
                + jnp.dot(pos_ref[:, half:2 * half], w1_ref[half:2 * half, :], preferred_element_type=F32))
        nxt = pltpu.roll(bot, n - 1, axis=0)
        return _silu(top + nxt + bias[0:1, :]).astype(BF16)

    at_ref[0] = lax.dot_general(a2t_ref[...], hidden(xa_ref, a1_ref), (((1,), (1,)), ((), ())),
                                preferred_element_type=F32).astype(BF16)
    b_ref[0] = jnp.dot(hidden(xb_ref, b1_ref), b2_ref[...], preferred_element_type=F32).astype(BF16)


def _compress(xa, xb, pos, a1, a2t, b1, b2):
    bg, n, half = xa.shape
    dh = NSA_HEAD_DIM
    full = lambda a: pl.BlockSpec(a.shape, lambda i: (0,) * a.ndim)
    return pl.pallas_call(
        _compress_kernel,
        grid=(bg,),
        in_specs=[pl.BlockSpec((1, n, half), lambda i: (i, 0, 0)),
                  pl.BlockSpec((1, n, half), lambda i: (i, 0, 0)),
                  full(pos), full(a1), full(a2t), full(b1), full(b2)],
        out_specs=[pl.BlockSpec((1, dh, n), lambda i: (i, 0, 0)),
                   pl.BlockSpec((1, n, dh), lambda i: (i, 0, 0))],
        out_shape=[jax.ShapeDtypeStruct((bg, dh, n), BF16), jax.ShapeDtypeStruct((bg, n, dh), BF16)],
        compiler_params=_params("parallel"),
        name="compress",
    )(xa, xb, pos, a1, a2t, b1, b2)


MASKED = -1e30


def _nsa_kernel(qn_ref, qr_ref, gl_ref, kc_ref, vo_ref, ks_ref, vst_ref, kw_ref, vwt_ref, expt_ref,
                o_ref, s_even, s_odd, stats, acc_buf):
    qb = Q_BLOCK
    dh = NSA_HEAD_DIM
    hpg = NSA_HPG
    n_cmp = kc_ref.shape[1]
    n_blk = expt_ref.shape[1]
    q0 = pl.program_id(2) * qb
    t_row = q0 + lax.broadcasted_iota(jnp.int32, (1, qb), 1)

    def heads_t(ref):
        xt = ref[...].astype(F32).T
        return jnp.concatenate([xt[h * dh:(h + 1) * dh] for h in range(hpg)], axis=1).astype(BF16)

    qn_t = heads_t(qn_ref)
    qr_t = heads_t(qr_ref)

    def masked(s, mask, fill):
        w = mask.shape[1]
        return jnp.concatenate([jnp.where(mask, s[:, h * w:(h + 1) * w], fill) for h in range(hpg)], axis=1)

    half = qb // N_Q_PARTS

    def part_cols(x, part):
        return jnp.concatenate([x[:, h * qb + part * half:h * qb + (part + 1) * half] for h in range(hpg)], axis=1)

    def join_parts(parts):
        return jnp.concatenate([p[:, h * half:(h + 1) * half] for h in range(hpg) for p in parts], axis=1)

    def exp2_cols(s):
        m = jnp.max(s, axis=0, keepdims=True)
        return jnp.exp2(s - jnp.where(m == -jnp.inf, 0.0, m))

    def compressed(n_vis):
        cmp_end = lax.broadcasted_iota(jnp.int32, (n_vis, 1), 0) * CMP_STRIDE + (CMP_BLOCK - 1)
        s_c = jnp.dot(kc_ref[0, 0:n_vis, :], qn_t, preferred_element_type=F32)
        e_c = exp2_cols(masked(s_c, cmp_end <= t_row, -jnp.inf)).astype(BF16)
        both = jnp.dot(vo_ref[0, :, 0:n_vis], e_c, preferred_element_type=F32)
        acc = both[0:VALUE_ROWS]
        den = acc[dh:dh + 1, :]
        imp_all = both[VALUE_ROWS:VALUE_ROWS + n_blk] * (1.0 / jnp.where(den > 0.0, den, 1.0))
        imp = imp_all[:, 0:qb]
        for h in range(1, hpg):
            imp = imp + imp_all[:, h * qb:(h + 1) * qb]
        return acc, imp

    sizes = sorted({min(n_cmp, n) for n in range(CMP_PREFIX_STEP, n_cmp + CMP_PREFIX_STEP, CMP_PREFIX_STEP)})
    n_vis_max = (q0 + qb - CMP_BLOCK) // CMP_STRIDE + 1
    case = jnp.minimum((n_vis_max + CMP_PREFIX_STEP - 1) // CMP_PREFIX_STEP, len(sizes)) - 1
    acc_c, imp_t = lax.switch(case, [functools.partial(compressed, n) for n in sizes])

    span = WINDOW + half
    acc_w = []
    for part in range(N_Q_PARTS):
        w_start = pl.multiple_of(jnp.maximum(q0 + part * half - WINDOW, 0), half)
        dist = t_row[:, part * half:(part + 1) * half] - (w_start + lax.broadcasted_iota(jnp.int32, (span, 1), 0))
        s_w = jnp.dot(kw_ref[0, 0, pl.ds(w_start, span), :], part_cols(qr_t, part), preferred_element_type=F32)
        e_w = exp2_cols(masked(s_w, (dist >= 0) & (dist < WINDOW), -jnp.inf))
        acc_w.append(jnp.dot(vwt_ref[0, 0, :, pl.ds(w_start, span)], e_w.astype(BF16),
                             preferred_element_type=F32))
    acc_w = join_parts(acc_w)

    js = lax.broadcasted_iota(jnp.int32, (n_blk, qb), 0).astype(F32)
    cur = (t_row // SEL_BLOCK).astype(F32)
    forced = (js == 0.0) | (js == cur) | (js == cur - 1.0)
    val = jnp.where((js > cur) | forced, -jnp.inf, imp_t)
    for _ in range(N_SELECT - N_FORCED):
        best = jnp.max(val, axis=0, keepdims=True)
        first = jnp.min(jnp.where(val == best, js, float(n_blk)), axis=0, keepdims=True)
        val = jnp.where(js == first, -jnp.inf, val)
    chosen = val == -jnp.inf
    kt = SEL_KEY_TILE
    n_tiles = ks_ref.shape[2] // kt
    first_own = (q0 // SEL_BLOCK).astype(F32)

    def extended(bias):
        return jnp.concatenate([jnp.concatenate([bias.astype(BF16)] * hpg, axis=1), qr_t], axis=0)

    q_ext_t = extended(jnp.where(chosen & (js < first_own), 0.0, MASKED))
    q_own_t = extended(jnp.where(chosen & (js >= first_own) & (js <= cur), 0.0, MASKED))

    def scores(i, s_ref):
        start = pl.multiple_of(jnp.minimum(i, n_tiles - 1) * kt, kt)
        k_ext = jnp.concatenate([expt_ref[pl.ds(start, kt), :], ks_ref[0, 0, pl.ds(start, kt), :]], axis=1)
        s = jnp.dot(k_ext, q_ext_t, preferred_element_type=F32)
        s_ref[...] = s
        return jnp.max(s, axis=0, keepdims=True)

    def accumulate(s, m_tile, v_t, carry):
        m_run, acc = carry
        m_new = jnp.maximum(m_run, m_tile)
        p = jnp.exp2(s - m_new)
        acc = jnp.exp2(m_run - m_new) * acc + jnp.dot(v_t, p.astype(BF16), preferred_element_type=F32)
        return m_new, acc

    def consume(i, s_ref, m_tile, carry):
        start = pl.multiple_of(i * kt, kt)
        return accumulate(s_ref[...], m_tile, vst_ref[0, 0, :, pl.ds(start, kt)], carry)

    def pair_step(j, carry):
        m_run, acc, m_even = carry
        m_odd = scores(2 * j + 1, s_odd)
        m_run, acc = consume(2 * j, s_even, m_even, (m_run, acc))
        m_even = scores(2 * j + 2, s_even)
        m_run, acc = consume(2 * j + 1, s_odd, m_odd, (m_run, acc))
        return m_run, acc, m_even

    cols = hpg * qb
    init = (jnp.full((1, cols), MASKED, F32), jnp.zeros((VALUE_ROWS, cols), F32), scores(0, s_even))
    n_past = (q0 + kt - 1) // kt
    m_run, acc_s, m_even = lax.fori_loop(0, (n_past - 1) // 2, pair_step, init)
    stats[0:1, :] = m_run
    stats[1:2, :] = m_even
    acc_buf[...] = acc_s
    carried = lambda: (stats[0:1, :], acc_buf[...])

    def hand_over(carry):
        stats[0:1, :], acc_buf[...] = carry

    @pl.when(n_past % 2 == 1)
    def _():
        hand_over(consume(n_past - 1, s_even, stats[1:2, :], carried()))

    @pl.when((n_past % 2 == 0) & (n_past > 0))
    def _():
        m_odd = scores(n_past - 1, s_odd)
        carry = consume(n_past - 2, s_even, stats[1:2, :], carried())
        hand_over(consume(n_past - 1, s_odd, m_odd, carry))

    m_run, acc_s = stats[0:1, :], acc_buf[...]
    acc_parts = []
    for part in range(N_Q_PARTS):
        n_own = (part + 1) * half
        kpos = q0 + lax.broadcasted_iota(jnp.int32, (n_own, 1), 0)
        k_own = jnp.concatenate([expt_ref[pl.ds(q0, n_own), :], ks_ref[0, 0, pl.ds(q0, n_own), :]], axis=1)
        s_own = masked(jnp.dot(k_own, part_cols(q_own_t, part), preferred_element_type=F32),
                       kpos <= t_row[:, part * half:(part + 1) * half], MASKED)
        _, acc_part = accumulate(s_own, jnp.max(s_own, axis=0, keepdims=True), vst_ref[0, 0, :, pl.ds(q0, n_own)],
                                 (part_cols(m_run, part), part_cols(acc_s, part)))
        acc_parts.append(acc_part)
    acc_s = join_parts(acc_parts)

    gates_t = _sigmoid(gl_ref[...].T)
    outs = []
    for h in range(hpg):
        c = slice(h * qb, (h + 1) * qb)
        o_t = jnp.zeros((dh, qb), F32)
        for j, acc in enumerate((acc_c, acc_s, acc_w)):
            den = acc[dh:dh + 1, c]
            o_t = o_t + (gates_t[3 * h + j:3 * h + j + 1, :] / jnp.where(den > 0.0, den, 1.0)) * acc[0:dh, c]
        outs.append(o_t)
    o_ref[...] = jnp.concatenate(outs, axis=0).T.astype(o_ref.dtype)


def _nsa_attention(qn, qr, gl, kc, vo, ks, vst, kw, vwt, expand_t, batch, t_len):
    assert t_len % (2 * SEL_KEY_TILE) == 0, "key tiles of the selected branch are processed in pairs"
    m = qn.shape[0]
    n_g = NSA_KV_GROUPS
    gw = NSA_GROUP_WIDTH
    dh = NSA_HEAD_DIM
    nq = t_len // Q_BLOCK
    n_cmp = kc.shape[1]
    qspec = pl.BlockSpec((Q_BLOCK, gw), lambda b, g, i: (b * nq + i, g))
    keys = pl.BlockSpec((1, 1, t_len, dh), lambda b, g, i: (b, g, 0, 0))
    values_t = pl.BlockSpec((1, 1, VALUE_ROWS, t_len), lambda b, g, i: (b, g, 0, 0))
    const = lambda a: pl.BlockSpec(a.shape, lambda b, g, i: (0, 0))
    return pl.pallas_call(
        _nsa_kernel,
        grid=(batch, n_g, nq),
        in_specs=[qspec, qspec,
                  pl.BlockSpec((Q_BLOCK, LANES), lambda b, g, i: (b * nq + i, g)),
                  pl.BlockSpec((1, n_cmp, dh), lambda b, g, i: (b * n_g + g, 0, 0)),
                  pl.BlockSpec((1,) + vo.shape[1:], lambda b, g, i: (b * n_g + g, 0, 0)),
                  keys, values_t, keys, values_t, const(expand_t)],
        out_specs=qspec,
        out_shape=jax.ShapeDtypeStruct((m, n_g * gw), BF16),
        scratch_shapes=[pltpu.VMEM((SEL_KEY_TILE, NSA_HPG * Q_BLOCK), F32)] * 2
        + [pltpu.VMEM((SUBLANES, NSA_HPG * Q_BLOCK), F32), pltpu.VMEM((VALUE_ROWS, NSA_HPG * Q_BLOCK), F32)],
        compiler_params=_params("parallel", "parallel", "arbitrary"),
        name="nsa_attention",
    )(qn, qr, gl, kc, vo, ks, vst, kw, vwt, expand_t)


def _pad_cols(w, n):
    return jnp.pad(w, ((0, 0), (0, n - w.shape[1])))


def _even_layer(x, gain, w_in, qkv_conv, a_log, dt_bias, o_norm, sc_conv, w_out, batch, t_len, mlp, final):
    gw = GDN_WIDTH
    ab = _pad_cols(w_in[:, 4 * gw:4 * gw + 2 * GDN_HEADS], LANES)
    w_all = jnp.concatenate([w_in[:, :3 * gw], ab, w_in[:, 3 * gw:4 * gw], w_in[:, 4 * gw + 2 * GDN_HEADS:]],
                            axis=1).astype(BF16)
    proj, gates = _norm_matmul(x, gain, w_all, 3 * gw + LANES)
    gate_params = jnp.zeros((SUBLANES, LANES), F32)
    gate_params = gate_params.at[0, :GDN_HEADS].set(a_log).at[1, :GDN_HEADS].set(dt_bias)
    ya = _gdn(proj, gates, qkv_conv, gate_params, o_norm, batch, t_len)
    return _even_tail(x, ya, gates, sc_conv, w_out[:gw].astype(BF16), w_out[gw:].astype(BF16), t_len, mlp, final)


def _rope_tables(t_len):
    half = ROPE_DIM // 2
    inv_freq = ROPE_THETA ** (-jnp.arange(0, ROPE_DIM, 2, dtype=F32) / ROPE_DIM)
    ang = jnp.arange(t_len, dtype=F32)[:, None] * inv_freq[None, :]
    cos, sin = jnp.cos(ang), jnp.sin(ang)
    pad = NSA_HEAD_DIM - ROPE_DIM
    head = lambda a, b, fill: jnp.concatenate([a, b, jnp.full((t_len, pad), fill, F32)], axis=1)
    zeros = jnp.zeros((t_len, half), F32)
    reps = LANES // NSA_HEAD_DIM
    return (jnp.tile(head(cos, cos, 1.0), (1, reps)),
            jnp.tile(head(-sin, zeros, 0.0), (1, reps)),
            jnp.tile(head(zeros, sin, 0.0), (1, reps)))


def _odd_layer(x, gain, w_in, cmp_pos, k_w1, k_w2, v_w1, v_w2, w_out, batch, t_len, mlp, final):
    n_g, dh, hpg = NSA_KV_GROUPS, NSA_HEAD_DIM, NSA_HPG
    dq = NSA_HEADS * dh
    kv = NSA_KV_WIDTH
    gate_w = w_in[:, dq + 6 * kv:].reshape(-1, n_g, 3 * hpg)
    gate_w = jnp.pad(gate_w, ((0, 0), (0, 0), (0, LANES - 3 * hpg))).reshape(-1, n_g * LANES)
    w_all = jnp.concatenate([w_in[:, :dq + 6 * kv], gate_w], axis=1).astype(BF16)
    qn, qr, kvc, ks, vs_t, kw, vw_t, gl = _odd_proj(x, gain, w_all, _rope_tables(t_len), batch, t_len)

    n_chunk = t_len // CMP_STRIDE

    def chunked(a):
        a = a.reshape(batch, n_chunk, CMP_STRIDE, n_g, dh)
        return jnp.moveaxis(a, 3, 1).reshape(batch * n_g, n_chunk, CMP_STRIDE * dh)

    pos = jnp.zeros((SUBLANES, CMP_BLOCK * dh), F32).at[0].set(cmp_pos.reshape(-1)).astype(BF16)
    vc_t, kc = _compress(chunked(kvc[:, kv:]), chunked(kvc[:, :kv]), pos, v_w1.astype(BF16),
                         v_w2.T.astype(BF16), k_w1.astype(BF16), k_w2.astype(BF16))

    def with_ones(a):
        n = a.shape[-1]
        return jnp.concatenate([a, jnp.ones((a.shape[0], 1, n), a.dtype),
                                jnp.zeros((a.shape[0], VALUE_ROWS - dh - 1, n), a.dtype)], axis=1)

    n_blk = _round_up(t_len // SEL_BLOCK, LANES)
    c_start = np.arange(n_chunk)[None, :] * CMP_STRIDE
    s_start = np.arange(n_blk)[:, None] * SEL_BLOCK
    overlap_t = jnp.asarray((c_start < s_start + SEL_BLOCK) & (c_start + CMP_BLOCK > s_start), BF16)
    expand_t = jnp.asarray((np.arange(t_len)[:, None] // SEL_BLOCK) == np.arange(n_blk)[None, :], BF16)
    vo = jnp.concatenate([with_ones(vc_t), jnp.broadcast_to(overlap_t, (batch * n_g,) + overlap_t.shape)], axis=1)
    o = _nsa_attention(qn, qr, gl, kc, vo, ks, vs_t, kw, vw_t, expand_t, batch, t_len)
    return _odd_tail(x, o, w_out.astype(BF16), mlp, final)


def kernel(x, mix_norm, mlp_norm, w_up, w_down, final_norm, ev_w_in, ev_qkv_conv, ev_a_log, ev_dt_bias,
           ev_o_norm, ev_sc_conv, ev_w_out, od_w_in, od_cmp_pos, od_cmp_k_w1, od_cmp_k_w2, od_cmp_v_w1,
           od_cmp_v_w2, od_w_out):
    batch, t_len, d = x.shape
    depth = mix_norm.shape[0]
    xs = x.reshape(batch * t_len, d)
    for layer in range(depth):
        i = layer // 2
        mlp = (mlp_norm[layer], w_up[layer].astype(BF16), w_down[layer].astype(BF16), final_norm)
        final = layer == depth - 1
        if layer % 2 == 0:
            xs = _even_layer(xs, mix_norm[layer], ev_w_in[i], ev_qkv_conv[i], ev_a_log[i], ev_dt_bias[i],
                             ev_o_norm[i], ev_sc_conv[i], ev_w_out[i], batch, t_len, mlp, final)
        else:
            xs = _odd_layer(xs, mix_norm[layer], od_w_in[i], od_cmp_pos[i], od_cmp_k_w1[i], od_cmp_k_w2[i],
                            od_cmp_v_w1[i], od_cmp_v_w2[i], od_w_out[i], batch, t_len, mlp, final)
    return xs.reshape(batch, t_len, d)
```

```python
import functools

import jax
import jax.numpy as jnp
import numpy as np
from jax import lax
from jax.experimental import pallas as pl
from jax.experimental.pallas import tpu as pltpu

F32 = jnp.float32
BF16 = jnp.bfloat16
HIGHEST = lax.Precision.HIGHEST

NORM_EPS = 1e-6
GDN_HEADS = 8
GDN_HEAD_DIM = 64
GDN_WIDTH = GDN_HEADS * GDN_HEAD_DIM
GDN_CONV = 4
GDN_CHUNK = 64
SOLVE_BLOCK = 16
SC_CONV = 3
NSA_HEADS = 16
NSA_HEAD_DIM = 64
NSA_KV_GROUPS = 4
NSA_HPG = NSA_HEADS // NSA_KV_GROUPS
NSA_GROUP_WIDTH = NSA_HPG * NSA_HEAD_DIM
NSA_KV_WIDTH = NSA_KV_GROUPS * NSA_HEAD_DIM
CMP_BLOCK = 32
CMP_STRIDE = 16
SEL_BLOCK = 64
N_SELECT = 16
N_FORCED = 3
WINDOW = 512
Q_BLOCK = 512
CMP_PREFIX_STEP = 128
N_Q_PARTS = 2
SEL_KEY_TILE = 512
VALUE_ROWS = NSA_HEAD_DIM + 16
ROPE_THETA = 500000.0
ROPE_DIM = NSA_HEAD_DIM // 4
LOG2_E = 1.4426950408889634
LANES = 128
SUBLANES = 8
BF16_ROWS = 16
VMEM_LIMIT = 56 * 1024 * 1024


def _round_up(n, m):
    return (n + m - 1) // m * m


def _params(*semantics):
    return pltpu.CompilerParams(dimension_semantics=semantics, vmem_limit_bytes=VMEM_LIMIT)


def _rms(x, gain):
    return x * lax.rsqrt(jnp.mean(x * x, axis=-1, keepdims=True) + NORM_EPS) * gain


def _sigmoid(x):
    return 1.0 / (1.0 + jnp.exp(-x))


def _silu(x):
    return x * _sigmoid(x)


def _softplus(x):
    return jnp.maximum(x, 0.0) + jnp.log(1.0 + jnp.exp(-jnp.abs(x)))


def _dot(a, b):
    return jnp.dot(a.astype(BF16), b.astype(BF16), preferred_element_type=F32)


def _dot_nt(a, b):
    return lax.dot_general(a.astype(BF16), b.astype(BF16), (((1,), (1,)), ((), ())),
                           preferred_element_type=F32)


def _dot_f32(a, b):
    return jnp.dot(a, b, precision=HIGHEST, preferred_element_type=F32)


def _norm_matmul_kernel(x_ref, g_ref, w_ref, o_ref, ob_ref):
    h = _rms(x_ref[...], g_ref[...])
    y = jnp.dot(h.astype(BF16), w_ref[...], preferred_element_type=F32)
    n = o_ref.shape[1]
    o_ref[...] = y[:, :n]
    ob_ref[...] = y[:, n:].astype(BF16)


def _norm_matmul(x, gain, w, n_f32, tm=512):
    m, d = x.shape
    n = w.shape[1]
    return pl.pallas_call(
        _norm_matmul_kernel,
        grid=(m // tm,),
        in_specs=[pl.BlockSpec((tm, d), lambda i: (i, 0)),
                  pl.BlockSpec((1, d), lambda i: (0, 0)),
                  pl.BlockSpec((d, n), lambda i: (0, 0))],
        out_specs=[pl.BlockSpec((tm, n_f32), lambda i: (i, 0)), pl.BlockSpec((tm, n - n_f32), lambda i: (i, 0))],
        out_shape=[jax.ShapeDtypeStruct((m, n_f32), F32), jax.ShapeDtypeStruct((m, n - n_f32), BF16)],
        compiler_params=_params("parallel"),
        name="norm_proj",
    )(x, gain.reshape(1, d), w)


def _split(a):
    hi = a.astype(BF16)
    return hi, (a - hi.astype(F32)).astype(BF16)


def _dot3(a, b):
    (ah, al), (bh, bl) = a, b
    return (jnp.dot(ah, bh, preferred_element_type=F32) + jnp.dot(ah, bl, preferred_element_type=F32)
            + jnp.dot(al, bh, preferred_element_type=F32))


def _gdn_kernel(qkv_ref, z_ref, ab_ref, cw_ref, gp_ref, on_ref, o_ref, xbuf, act, state):
    c = GDN_CHUNK
    dh = GDN_HEAD_DIM
    pw = 2 * dh
    n_pairs = GDN_HEADS // 2
    n_batch = qkv_ref.shape[0]
    hist = SUBLANES
    chains = [(b, p) for b in range(n_batch) for p in range(n_pairs)]

    @pl.when(pl.program_id(0) == 0)
    def _():
        xbuf[:, 0:hist, :] = jnp.zeros((n_batch, hist, 3 * GDN_WIDTH), F32)
        state[...] = jnp.zeros(state.shape, F32)

    row = lax.broadcasted_iota(jnp.int32, (c, pw), 0)
    lane = lax.broadcasted_iota(jnp.int32, (c, pw), 1)
    first = lane < dh
    col = jnp.where(first, lane, lane - dh)
    incl = row >= col
    strict = row > col
    eye = (row == col).astype(F32)
    diag_blk = strict & ((row // SOLVE_BLOCK) == (col // SOLVE_BLOCK))
    r2 = lax.broadcasted_iota(jnp.int32, (pw, pw), 0)
    c2 = lax.broadcasted_iota(jnp.int32, (pw, pw), 1)
    same_head = (r2 < dh) == (c2 < dh)
    ones_bd = same_head.astype(BF16)
    tri = (lax.broadcasted_iota(jnp.int32, (c, c), 0) >= lax.broadcasted_iota(jnp.int32, (c, c), 1)).astype(F32)

    def blockdiag(x):
        zero = jnp.zeros_like(x)
        return jnp.concatenate([jnp.where(first, x, zero), jnp.where(first, zero, x)], axis=0)

    def head_sums(x):
        return jnp.dot(x.astype(BF16), ones_bd, preferred_element_type=F32)

    def pdot(xs, y):
        hi, lo = _split(y)
        return _dot3(xs, (blockdiag(hi), blockdiag(lo)))

    gcs, betas = [], []
    for b in range(n_batch):
        x = qkv_ref[b]
        xbuf[b, hist:hist + c, :] = x
        y = x * cw_ref[GDN_CONV - 1:GDN_CONV, :]
        for j in range(GDN_CONV - 1):
            shift = GDN_CONV - 1 - j
            y = y + xbuf[b, hist - shift:hist - shift + c, :] * cw_ref[j:j + 1, :]
        xbuf[b, 0:hist, :] = x[c - hist:c, :]
        act[b] = _silu(y)
        ab = ab_ref[b]
        g_all = -jnp.exp(gp_ref[0:1, :]) * _softplus(ab + gp_ref[1:2, :])
        gcs.append(_dot_f32(tri, g_all))
        betas.append(_sigmoid(ab))

    def pair_cols(a, p, offset):
        return jnp.where(first, a[:, offset + 2 * p:offset + 2 * p + 1], a[:, offset + 2 * p + 1:offset + 2 * p + 2])

    scale = dh ** -0.5
    q = [act[b, :, p * pw:(p + 1) * pw] for b, p in chains]
    k = [act[b, :, GDN_WIDTH + p * pw:GDN_WIDTH + (p + 1) * pw] for b, p in chains]
    v = [act[b, :, 2 * GDN_WIDTH + p * pw:2 * GDN_WIDTH + (p + 1) * pw] for b, p in chains]
    q = [x * lax.rsqrt(head_sums(x * x) + NORM_EPS) * scale for x in q]
    k = [x * lax.rsqrt(head_sums(x * x) + NORM_EPS) for x in k]
    gcol = [pair_cols(gcs[b], p, 0) for b, p in chains]
    bcol = [pair_cols(betas[b], p, GDN_HEADS) for b, p in chains]
    grow = [jnp.sum(g * eye, axis=0, keepdims=True) for g in gcol]
    g_last = [g[c - 1:c, :] for g in gcol]
    decay = [jnp.exp(jnp.where(incl, gc_ - gr_, -jnp.inf)) for gc_, gr_ in zip(gcol, grow)]
    kb = [k_ * b_ for k_, b_ in zip(k, bcol)]
    k_bd = [blockdiag(k_).astype(BF16) for k_ in k]
    both = [_dot_nt(jnp.concatenate([kb_, q_], axis=0), kd_) for kb_, q_, kd_ in zip(kb, q, k_bd)]
    m_low = [jnp.where(strict, x[0:c] * d_, 0.0) for x, d_ in zip(both, decay)]
    attn = [x[c:2 * c] * d_ for x, d_ in zip(both, decay)]

    d_pow = [jnp.where(diag_blk, m_, 0.0) for m_ in m_low]
    l_mat = [m_ - d_ for m_, d_ in zip(m_low, d_pow)]
    p_inv = [eye - d_ for d_ in d_pow]
    d_pow = [pdot(_split(d_), d_) for d_ in d_pow]
    n_steps = int(np.log2(SOLVE_BLOCK)) - 1
    for step in range(n_steps):
        if step == n_steps - 1:
            p_inv = [p_ + pdot(_split(p_), d_) for p_, d_ in zip(p_inv, d_pow)]
        else:
            prod = [pdot(_split(jnp.concatenate([p_, d_], axis=0)), d_) for p_, d_ in zip(p_inv, d_pow)]
            p_inv = [p_ + x[0:c] for p_, x in zip(p_inv, prod)]
            d_pow = [x[c:2 * c] for x in prod]
    n_pow = [pdot(_split(p_), l_) for p_, l_ in zip(p_inv, l_mat)]
    q_inv = [eye - n_ for n_ in n_pow]
    for _ in range(int(np.log2(c // SOLVE_BLOCK)) - 1):
        n_pow = [pdot(_split(n_), n_) for n_ in n_pow]
        q_inv = [q_ + pdot(_split(q_), n_) for q_, n_ in zip(q_inv, n_pow)]
    a_inv = [pdot(_split(q_), p_) for q_, p_ in zip(q_inv, p_inv)]
    u = [_dot(a_, blockdiag(v_ * b_)) for a_, v_, b_ in zip(a_inv, v, bcol)]
    w = [_dot(a_, blockdiag(kb_ * jnp.exp(g_))) for a_, kb_, g_ in zip(a_inv, kb, gcol)]

    s_old = [state[b * n_pairs + p] for b, p in chains]
    on_s = [_dot(jnp.concatenate([w_, q_ * jnp.exp(g_)], axis=0), s_) for w_, q_, g_, s_ in zip(w, q, gcol, s_old)]
    v_new = [u_ - x[0:c] for u_, x in zip(u, on_s)]
    o = [x[c:2 * c] + _dot(a_, blockdiag(vn_)) for x, a_, vn_ in zip(on_s, attn, v_new)]
    k_dec_t = [(k_ * jnp.exp(gl_ - g_)).T for k_, gl_, g_ in zip(k, g_last, gcol)]
    for (b, p), s_, gl_, kt_, vn_ in zip(chains, s_old, g_last, k_dec_t, v_new):
        state[b * n_pairs + p] = s_ * jnp.exp(gl_) + jnp.where(same_head, _dot(kt_, vn_), 0.0)

    inv_dh = 1.0 / dh
    for (b, p), o_ in zip(chains, o):
        zp = z_ref[b, :, p * pw:(p + 1) * pw].astype(F32)
        y = o_ * lax.rsqrt(head_sums(o_ * o_) * inv_dh + NORM_EPS) * on_ref[...]
        o_ref[b, :, p * pw:(p + 1) * pw] = (y * _silu(zp)).astype(o_ref.dtype)


def _gdn(proj, gates, conv_w, gate_params, o_norm, batch, t_len):
    c = GDN_CHUNK
    n_chunks = t_len // c
    w3 = 3 * GDN_WIDTH
    ab_col = w3 // LANES
    proj = proj.reshape(batch, t_len, proj.shape[-1])
    gates = gates.reshape(batch, t_len, gates.shape[-1])
    on_pair = jnp.tile(o_norm.reshape(1, GDN_HEAD_DIM), (1, 2))
    out = pl.pallas_call(
        _gdn_kernel,
        grid=(n_chunks,),
        in_specs=[pl.BlockSpec((batch, c, w3), lambda i: (0, i, 0)),
                  pl.BlockSpec((batch, c, GDN_WIDTH), lambda i: (0, i, 0)),
                  pl.BlockSpec((batch, c, LANES), lambda i: (0, i, ab_col)),
                  pl.BlockSpec((GDN_CONV, w3), lambda i: (0, 0)),
                  pl.BlockSpec((SUBLANES, LANES), lambda i: (0, 0)),
                  pl.BlockSpec((1, 2 * GDN_HEAD_DIM), lambda i: (0, 0))],
        out_specs=pl.BlockSpec((batch, c, GDN_WIDTH), lambda i: (0, i, 0)),
        out_shape=jax.ShapeDtypeStruct((batch, t_len, GDN_WIDTH), BF16),
        scratch_shapes=[pltpu.VMEM((batch, c + SUBLANES, w3), F32),
                        pltpu.VMEM((batch, c, w3), F32),
                        pltpu.VMEM((batch * GDN_HEADS // 2, 2 * GDN_HEAD_DIM, 2 * GDN_HEAD_DIM), F32)],
        compiler_params=_params("arbitrary"),
        name="gdn",
    )(proj, gates, proj, conv_w, gate_params, on_pair)
    return out.reshape(batch * t_len, GDN_WIDTH)


def _mlp_tail(final, x, g_ref, wu_ref, wd_ref, fg_ref, o_ref):
    u = jnp.dot(_rms(x, g_ref[...]).astype(BF16), wu_ref[...], preferred_element_type=F32)
    u = jnp.square(jnp.maximum(u, 0.0))
    y = x + jnp.dot(u.astype(BF16), wd_ref[...], preferred_element_type=F32)
    if final:
        y = _rms(y, fg_ref[...])
    o_ref[...] = y


def _even_tail_kernel(tiles_per_batch, final, x_ref, ya_ref, bg_ref, cg_ref, hs_ref, cgp_ref, hsp_ref,
                      cw_ref, wa_ref, wb_ref, g_ref, wu_ref, wd_ref, fg_ref, o_ref, ubuf):
    tm = x_ref.shape[0]
    hist = cgp_ref.shape[0]
    first = (pl.program_id(0) % tiles_per_batch) == 0
    prev = cgp_ref[...].astype(F32) * hsp_ref[...].astype(F32)
    ubuf[0:hist, :] = jnp.where(first, 0.0, prev)
    u = cg_ref[...].astype(F32) * hs_ref[...].astype(F32)
    ubuf[hist:hist + tm, :] = u
    conv = u * cw_ref[SC_CONV - 1:SC_CONV, :]
    for j in range(SC_CONV - 1):
        shift = SC_CONV - 1 - j
        conv = conv + ubuf[hist - shift:hist - shift + tm, :] * cw_ref[j:j + 1, :]
    yb = bg_ref[...].astype(F32) * conv
    mix = jnp.dot(ya_ref[...], wa_ref[...], preferred_element_type=F32)
    mix = mix + jnp.dot(yb.astype(BF16), wb_ref[...], preferred_element_type=F32)
    _mlp_tail(final, x_ref[...] + mix, g_ref, wu_ref, wd_ref, fg_ref, o_ref)


def _odd_tail_kernel(final, x_ref, a_ref, w_ref, g_ref, wu_ref, wd_ref, fg_ref, o_ref):
    mix = jnp.dot(a_ref[...], w_ref[...], preferred_element_type=F32)
    _mlp_tail(final, x_ref[...] + mix, g_ref, wu_ref, wd_ref, fg_ref, o_ref)


def _resident(a):
    return pl.BlockSpec(a.shape, lambda i: (0, 0), pipeline_mode=pl.Buffered(1))


def _layer_tail(body, x, mixer_args, mixer_specs, mlp, final, scratch=(), tm=512):
    gain, w_up, w_down, final_gain = mlp
    m, d = x.shape
    row = pl.BlockSpec((tm, d), lambda i: (i, 0))
    vec = pl.BlockSpec((1, d), lambda i: (0, 0))
    return pl.pallas_call(
        functools.partial(body, final),
        grid=(m // tm,),
        in_specs=[row, *mixer_specs, vec, _resident(w_up), _resident(w_down), vec],
        out_specs=row,
        out_shape=jax.ShapeDtypeStruct((m, d), F32),
        scratch_shapes=list(scratch),
        compiler_params=_params("parallel"),
        name="layer_tail",
    )(x, *mixer_args, gain.reshape(1, d), w_up, w_down, final_gain.reshape(1, d))


def _even_tail(x, ya, gates, sc_conv, w_a, w_b, t_len, mlp, final, tm=512):
    wd = GDN_WIDTH
    hist = BF16_ROWS
    hb = tm // hist
    prev = lambda col: pl.BlockSpec((hist, wd), lambda i: (jnp.maximum(i * hb - 1, 0), col))
    cur = lambda col: pl.BlockSpec((tm, wd), lambda i: (i, col))
    specs = [cur(0), cur(1), cur(2), cur(3), prev(2), prev(3), _resident(sc_conv), _resident(w_a), _resident(w_b)]
    return _layer_tail(functools.partial(_even_tail_kernel, t_len // tm), x,
                       (ya, gates, gates, gates, gates, gates, sc_conv, w_a, w_b), specs, mlp, final,
                       scratch=[pltpu.VMEM((tm + hist, wd), F32)], tm=tm)


def _odd_tail(x, a, w, mlp, final, tm=512):
    specs = [pl.BlockSpec((tm, a.shape[1]), lambda i: (i, 0)), _resident(w)]
    return _layer_tail(_odd_tail_kernel, x, (a, w), specs, mlp, final, tm=tm)


def _rope(x, cos_t, sin_lo, sin_hi):
    half = ROPE_DIM // 2
    outs = []
    for j in range(x.shape[1] // LANES):
        xs = x[:, j * LANES:(j + 1) * LANES]
        up = pltpu.roll(xs, LANES - half, axis=1)
        down = pltpu.roll(xs, half, axis=1)
        outs.append(xs * cos_t + up * sin_lo + down * sin_hi)
    return jnp.concatenate(outs, axis=-1)


def _odd_proj_kernel(x_ref, g_ref, w_ref, cos_ref, slo_ref, shi_ref,
                     qn_ref, qr_ref, kc_ref, vc_ref, ks_ref, vs_ref, kw_ref, vw_ref, gl_ref):
    h = _rms(x_ref[...], g_ref[...])
    y = jnp.dot(h.astype(BF16), w_ref[...], preferred_element_type=F32)
    cos_t, sin_lo, sin_hi = cos_ref[...], slo_ref[...], shi_ref[...]
    dq = NSA_HEADS * NSA_HEAD_DIM
    kv = NSA_KV_WIDTH
    q = y[:, :dq] * (LOG2_E * NSA_HEAD_DIM ** -0.5)
    qn_ref[...] = q.astype(BF16)
    qr_ref[...] = _rope(q, cos_t, sin_lo, sin_hi).astype(BF16)
    dh = NSA_HEAD_DIM
    tm = y.shape[0]
    for g in range(NSA_KV_GROUPS):
        kc_ref[0, g] = y[:, dq + g * dh:dq + (g + 1) * dh].astype(BF16)
        vc_ref[0, g] = y[:, dq + kv + g * dh:dq + kv + (g + 1) * dh].astype(BF16)
    ones_rows = (lax.broadcasted_iota(jnp.int32, (VALUE_ROWS - dh, tm), 0) == 0).astype(BF16)
    for k_ref, v_ref, col in ((ks_ref, vs_ref, dq + 2 * kv), (kw_ref, vw_ref, dq + 4 * kv)):
        k = _rope(y[:, col:col + kv], cos_t, sin_lo, sin_hi).astype(BF16)
        v_t = y[:, col + kv:col + 2 * kv].T
        for g in range(NSA_KV_GROUPS):
            k_ref[0, g] = k[:, g * dh:(g + 1) * dh]
            v_ref[0, g, 0:dh, :] = v_t[g * dh:(g + 1) * dh].astype(BF16)
            v_ref[0, g, dh:VALUE_ROWS, :] = ones_rows
    gl_ref[...] = y[:, dq + 6 * kv:]


def _odd_proj(x, gain, w, tables, batch, t_len, tm=512):
    m, d = x.shape
    n = w.shape[1]
    n_g, dh = NSA_KV_GROUPS, NSA_HEAD_DIM
    dq = NSA_HEADS * dh
    kv = NSA_KV_WIDTH
    ng = n - dq - 6 * kv
    tpb = t_len // tm
    row = lambda width: pl.BlockSpec((tm, width), lambda i: (i, 0))
    tab = pl.BlockSpec((tm, LANES), lambda i: (i % tpb, 0))
    keys = (pl.BlockSpec((1, n_g, tm, dh), lambda i: (i // tpb, 0, i % tpb, 0)),
            jax.ShapeDtypeStruct((batch, n_g, t_len, dh), BF16))
    values_t = (pl.BlockSpec((1, n_g, VALUE_ROWS, tm), lambda i: (i // tpb, 0, 0, i % tpb)),
                jax.ShapeDtypeStruct((batch, n_g, VALUE_ROWS, t_len), BF16))
    flat = lambda width, dt: (row(width), jax.ShapeDtypeStruct((m, width), dt))
    outs = [flat(dq, BF16), flat(dq, BF16), keys, keys, keys, values_t, keys, values_t, flat(ng, F32)]
    return pl.pallas_call(
        _odd_proj_kernel,
        grid=(m // tm,),
        in_specs=[row(d), pl.BlockSpec((1, d), lambda i: (0, 0)),
                  pl.BlockSpec((d, n), lambda i: (0, 0)), tab, tab, tab],
        out_specs=[spec for spec, _ in outs],
        out_shape=[shape for _, shape in outs],
        compiler_params=_params("parallel"),
        name="odd_proj",
    )(x, gain.reshape(1, d), w, *tables)


def _compress_kernel(xa_ref, xb_ref, pos_ref, a1_ref, a2t_ref, b1_ref, b2_ref, at_ref, b_ref):
    half = xa_ref.shape[2]
    n = xa_ref.shape[1]

    def hidden(x_ref, w1_ref):
        x = x_ref[0]
        top = jnp.dot(x, w1_ref[0:half, :], preferred_element_type=F32)
        bot = jnp.dot(x, w1_ref[half:2 * half, :], preferred_element_type=F32)
        bias = (jnp.dot(pos_ref[:, 0:half], w1_ref[0:half, :], preferred_element_type=F32)
                + jnp.dot(pos_ref[:, half:2 * half], w1_ref[half:2 * half, :], preferred_element_type=F32))
        nxt = pltpu.roll(bot, n - 1, axis=0)
        return _silu(top + nxt + bias[0:1, :]).astype(BF16)

    at_ref[0] = lax.dot_general(a2t_ref[...], hidden(xa_ref, a1_ref), (((1,), (1,)), ((), ())),
                                preferred_element_type=F32).astype(BF16)
    b_ref[0] = jnp.dot(hidden(xb_ref, b1_ref), b2_ref[...], preferred_element_type=F32).astype(BF16)


def _compress(xa, xb, pos, a1, a2t, b1, b2):
    bg, n, half = xa.shape
    dh = NSA_HEAD_DIM
    full = lambda a: pl.BlockSpec(a.shape, lambda i: (0,) * a.ndim)
    return pl.pallas_call(
        _compress_kernel,
        grid=(bg,),
        in_specs=[pl.BlockSpec((1, n, half), lambda i: (i, 0, 0)),
                  pl.BlockSpec((1, n, half), lambda i: (i, 0, 0)),
                  full(pos), full(a1), full(a2t), full(b1), full(b2)],
        out_specs=[pl.BlockSpec((1, dh, n), lambda i: (i, 0, 0)),
                   pl.BlockSpec((1, n, dh), lambda i: (i, 0, 0))],
        out_shape=[jax.ShapeDtypeStruct((bg, dh, n), BF16), jax.ShapeDtypeStruct((bg, n, dh), BF16)],
        compiler_params=_params("parallel"),
        name="compress",
    )(xa, xb, pos, a1, a2t, b1, b2)


MASKED = -1e30


def _nsa_kernel(qn_ref, qr_ref, gl_ref, kc_ref, vo_ref, ks_ref, vst_ref, kw_ref, vwt_ref, expt_ref,
                o_ref, s_even, s_odd, stats, acc_buf):
    qb = Q_BLOCK
    dh = NSA_HEAD_DIM
    hpg = NSA_HPG
    n_cmp = kc_ref.shape[1]
    n_blk = expt_ref.shape[1]
    q0 = pl.program_id(2) * qb
    t_row = q0 + lax.broadcasted_iota(jnp.int32, (1, qb), 1)

    def heads_t(ref):
        xt = ref[...].astype(F32).T
        return jnp.concatenate([xt[h * dh:(h + 1) * dh] for h in range(hpg)], axis=1).astype(BF16)

    qn_t = heads_t(qn_ref)
    qr_t = heads_t(qr_ref)

    def masked(s, mask, fill):
        w = mask.shape[1]
        return jnp.concatenate([jnp.where(mask, s[:, h * w:(h + 1) * w], fill) for h in range(hpg)], axis=1)

    half = qb // N_Q_PARTS

    def part_cols(x, part):
        return jnp.concatenate([x[:, h * qb + part * half:h * qb + (part + 1) * half] for h in range(hpg)], axis=1)

    def join_parts(parts):
        return jnp.concatenate([p[:, h * half:(h + 1) * half] for h in range(hpg) for p in parts], axis=1)

    def exp2_cols(s):
        m = jnp.max(s, axis=0, keepdims=True)
        return jnp.exp2(s - jnp.where(m == -jnp.inf, 0.0, m))

    def compressed(n_vis):
        cmp_end = lax.broadcasted_iota(jnp.int32, (n_vis, 1), 0) * CMP_STRIDE + (CMP_BLOCK - 1)
        s_c = jnp.dot(kc_ref[0, 0:n_vis, :], qn_t, preferred_element_type=F32)
        e_c = exp2_cols(masked(s_c, cmp_end <= t_row, -jnp.inf)).astype(BF16)
        both = jnp.dot(vo_ref[0, :, 0:n_vis], e_c, preferred_element_type=F32)
        acc = both[0:VALUE_ROWS]
        den = acc[dh:dh + 1, :]
        imp_all = both[VALUE_ROWS:VALUE_ROWS + n_blk] * (1.0 / jnp.where(den > 0.0, den, 1.0))
        imp = imp_all[:, 0:qb]
        for h in range(1, hpg):
            imp = imp + imp_all[:, h * qb:(h + 1) * qb]
        return acc, imp

    sizes = sorted({min(n_cmp, n) for n in range(CMP_PREFIX_STEP, n_cmp + CMP_PREFIX_STEP, CMP_PREFIX_STEP)})
    n_vis_max = (q0 + qb - CMP_BLOCK) // CMP_STRIDE + 1
    case = jnp.minimum((n_vis_max + CMP_PREFIX_STEP - 1) // CMP_PREFIX_STEP, len(sizes)) - 1
    acc_c, imp_t = lax.switch(case, [functools.partial(compressed, n) for n in sizes])

    span = WINDOW + half
    acc_w = []
    for part in range(N_Q_PARTS):
        w_start = pl.multiple_of(jnp.maximum(q0 + part * half - WINDOW, 0), half)
        dist = t_row[:, part * half:(part + 1) * half] - (w_start + lax.broadcasted_iota(jnp.int32, (span, 1), 0))
        s_w = jnp.dot(kw_ref[0, 0, pl.ds(w_start, span), :], part_cols(qr_t, part), preferred_element_type=F32)
        e_w = exp2_cols(masked(s_w, (dist >= 0) & (dist < WINDOW), -jnp.inf))
        acc_w.append(jnp.dot(vwt_ref[0, 0, :, pl.ds(w_start, span)], e_w.astype(BF16),
                             preferred_element_type=F32))
    acc_w = join_parts(acc_w)

    js = lax.broadcasted_iota(jnp.int32, (n_blk, qb), 0).astype(F32)
    cur = (t_row // SEL_BLOCK).astype(F32)
    forced = (js == 0.0) | (js == cur) | (js == cur - 1.0)
    val = jnp.where((js > cur) | forced, -jnp.inf, imp_t)
    for _ in range(N_SELECT - N_FORCED):
        best = jnp.max(val, axis=0, keepdims=True)
        first = jnp.min(jnp.where(val == best, js, float(n_blk)), axis=0, keepdims=True)
        val = jnp.where(js == first, -jnp.inf, val)
    chosen = val == -jnp.inf
    kt = SEL_KEY_TILE
    n_tiles = ks_ref.shape[2] // kt
    first_own = (q0 // SEL_BLOCK).astype(F32)

    def extended(bias):
        return jnp.concatenate([jnp.concatenate([bias.astype(BF16)] * hpg, axis=1), qr_t], axis=0)

    q_ext_t = extended(jnp.where(chosen & (js < first_own), 0.0, MASKED))
    q_own_t = extended(jnp.where(chosen & (js >= first_own) & (js <= cur), 0.0, MASKED))

    def scores(i, s_ref):
        start = pl.multiple_of(jnp.minimum(i, n_tiles - 1) * kt, kt)
        k_ext = jnp.concatenate([expt_ref[pl.ds(start, kt), :], ks_ref[0, 0, pl.ds(start, kt), :]], axis=1)
        s = jnp.dot(k_ext, q_ext_t, preferred_element_type=F32)
        s_ref[...] = s
        return jnp.max(s, axis=0, keepdims=True)

    def accumulate(s, m_tile, v_t, carry):
        m_run, acc = carry
        m_new = jnp.maximum(m_run, m_tile)
        p = jnp.exp2(s - m_new)
        acc = jnp.exp2(m_run - m_new) * acc + jnp.dot(v_t, p.astype(BF16), preferred_element_type=F32)
        return m_new, acc

    def consume(i, s_ref, m_tile, carry):
        start = pl.multiple_of(i * kt, kt)
        return accumulate(s_ref[...], m_tile, vst_ref[0, 0, :, pl.ds(start, kt)], carry)

    def pair_step(j, carry):
        m_run, acc, m_even = carry
        m_odd = scores(2 * j + 1, s_odd)
        m_run, acc = consume(2 * j, s_even, m_even, (m_run, acc))
        m_even = scores(2 * j + 2, s_even)
        m_run, acc = consume(2 * j + 1, s_odd, m_odd, (m_run, acc))
        return m_run, acc, m_even

    cols = hpg * qb
    init = (jnp.full((1, cols), MASKED, F32), jnp.zeros((VALUE_ROWS, cols), F32), scores(0, s_even))
    n_past = (q0 + kt - 1) // kt
    m_run, acc_s, m_even = lax.fori_loop(0, (n_past - 1) // 2, pair_step, init)
    stats[0:1, :] = m_run
    stats[1:2, :] = m_even
    acc_buf[...] = acc_s
    carried = lambda: (stats[0:1, :], acc_buf[...])

    def hand_over(carry):
        stats[0:1, :], acc_buf[...] = carry

    @pl.when(n_past % 2 == 1)
    def _():
        hand_over(consume(n_past - 1, s_even, stats[1:2, :], carried()))

    @pl.when((n_past % 2 == 0) & (n_past > 0))
    def _():
        m_odd = scores(n_past - 1, s_odd)
        carry = consume(n_past - 2, s_even, stats[1:2, :], carried())
        hand_over(consume(n_past - 1, s_odd, m_odd, carry))

    m_run, acc_s = stats[0:1, :], acc_buf[...]
    acc_parts = []
    for part in range(N_Q_PARTS):
        n_own = (part + 1) * half
        kpos = q0 + lax.broadcasted_iota(jnp.int32, (n_own, 1), 0)
        k_own = jnp.concatenate([expt_ref[pl.ds(q0, n_own), :], ks_ref[0, 0, pl.ds(q0, n_own), :]], axis=1)
        s_own = masked(jnp.dot(k_own, part_cols(q_own_t, part), preferred_element_type=F32),
                       kpos <= t_row[:, part * half:(part + 1) * half], MASKED)
        _, acc_part = accumulate(s_own, jnp.max(s_own, axis=0, keepdims=True), vst_ref[0, 0, :, pl.ds(q0, n_own)],
                                 (part_cols(m_run, part), part_cols(acc_s, part)))
        acc_parts.append(acc_part)
    acc_s = join_parts(acc_parts)

    gates_t = _sigmoid(gl_ref[...].T)
    outs = []
    for h in range(hpg):
        c = slice(h * qb, (h + 1) * qb)
        o_t = jnp.zeros((dh, qb), F32)
        for j, acc in enumerate((acc_c, acc_s, acc_w)):
            den = acc[dh:dh + 1, c]
            o_t = o_t + (gates_t[3 * h + j:3 * h + j + 1, :] / jnp.where(den > 0.0, den, 1.0)) * acc[0:dh, c]
        outs.append(o_t)
    o_ref[...] = jnp.concatenate(outs, axis=0).T.astype(o_ref.dtype)


def _nsa_attention(qn, qr, gl, kc, vo, ks, vst, kw, vwt, expand_t, batch, t_len):
    assert t_len % (2 * SEL_KEY_TILE) == 0, "key tiles of the selected branch are processed in pairs"
    m = qn.shape[0]
    n_g = NSA_KV_GROUPS
    gw = NSA_GROUP_WIDTH
    dh = NSA_HEAD_DIM
    nq = t_len // Q_BLOCK
    n_cmp = kc.shape[1]
    qspec = pl.BlockSpec((Q_BLOCK, gw), lambda b, g, i: (b * nq + i, g))
    keys = pl.BlockSpec((1, 1, t_len, dh), lambda b, g, i: (b, g, 0, 0))
    values_t = pl.BlockSpec((1, 1, VALUE_ROWS, t_len), lambda b, g, i: (b, g, 0, 0))
    const = lambda a: pl.BlockSpec(a.shape, lambda b, g, i: (0, 0))
    return pl.pallas_call(
        _nsa_kernel,
        grid=(batch, n_g, nq),
        in_specs=[qspec, qspec,
                  pl.BlockSpec((Q_BLOCK, LANES), lambda b, g, i: (b * nq + i, g)),
                  pl.BlockSpec((1, n_cmp, dh), lambda b, g, i: (b * n_g + g, 0, 0)),
                  pl.BlockSpec((1,) + vo.shape[1:], lambda b, g, i: (b * n_g + g, 0, 0)),
                  keys, values_t, keys, values_t, const(expand_t)],
        out_specs=qspec,
        out_shape=jax.ShapeDtypeStruct((m, n_g * gw), BF16),
        scratch_shapes=[pltpu.VMEM((SEL_KEY_TILE, NSA_HPG * Q_BLOCK), F32)] * 2
        + [pltpu.VMEM((SUBLANES, NSA_HPG * Q_BLOCK), F32), pltpu.VMEM((VALUE_ROWS, NSA_HPG * Q_BLOCK), F32)],
        compiler_params=_params("parallel", "parallel", "arbitrary"),
        name="nsa_attention",
    )(qn, qr, gl, kc, vo, ks, vst, kw, vwt, expand_t)


def _pad_cols(w, n):
    return jnp.pad(w, ((0, 0), (0, n - w.shape[1])))


def _even_layer(x, gain, w_in, qkv_conv, a_log, dt_bias, o_norm, sc_conv, w_out, batch, t_len, mlp, final):
    gw = GDN_WIDTH
    ab = _pad_cols(w_in[:, 4 * gw:4 * gw + 2 * GDN_HEADS], LANES)
    w_all = jnp.concatenate([w_in[:, :3 * gw], ab, w_in[:, 3 * gw:4 * gw], w_in[:, 4 * gw + 2 * GDN_HEADS:]],
                            axis=1).astype(BF16)
    proj, gates = _norm_matmul(x, gain, w_all, 3 * gw + LANES)
    gate_params = jnp.zeros((SUBLANES, LANES), F32)
    gate_params = gate_params.at[0, :GDN_HEADS].set(a_log).at[1, :GDN_HEADS].set(dt_bias)
    ya = _gdn(proj, gates, qkv_conv, gate_params, o_norm, batch, t_len)
    return _even_tail(x, ya, gates, sc_conv, w_out[:gw].astype(BF16), w_out[gw:].astype(BF16), t_len, mlp, final)


def _rope_tables(t_len):
    half = ROPE_DIM // 2
    inv_freq = ROPE_THETA ** (-jnp.arange(0, ROPE_DIM, 2, dtype=F32) / ROPE_DIM)
    ang = jnp.arange(t_len, dtype=F32)[:, None] * inv_freq[None, :]
    cos, sin = jnp.cos(ang), jnp.sin(ang)
    pad = NSA_HEAD_DIM - ROPE_DIM
    head = lambda a, b, fill: jnp.concatenate([a, b, jnp.full((t_len, pad), fill, F32)], axis=1)
    zeros = jnp.zeros((t_len, half), F32)
    reps = LANES // NSA_HEAD_DIM
    return (jnp.tile(head(cos, cos, 1.0), (1, reps)),
            jnp.tile(head(-sin, zeros, 0.0), (1, reps)),
            jnp.tile(head(zeros, sin, 0.0), (1, reps)))


def _odd_layer(x, gain, w_in, cmp_pos, k_w1, k_w2, v_w1, v_w2, w_out, batch, t_len, mlp, final):
    n_g, dh, hpg = NSA_KV_GROUPS, NSA_HEAD_DIM, NSA_HPG
    dq = NSA_HEADS * dh
    kv = NSA_KV_WIDTH
    gate_w = w_in[:, dq + 6 * kv:].reshape(-1, n_g, 3 * hpg)
    gate_w = jnp.pad(gate_w, ((0, 0), (0, 0), (0, LANES - 3 * hpg))).reshape(-1, n_g * LANES)
    w_all = jnp.concatenate([w_in[:, :dq + 6 * kv], gate_w], axis=1).astype(BF16)
    qn, qr, kc_in, vc_in, ks, vs_t, kw, vw_t, gl = _odd_proj(x, gain, w_all, _rope_tables(t_len), batch, t_len)

    n_chunk = t_len // CMP_STRIDE
    chunked = lambda a: a.reshape(batch * n_g, n_chunk, CMP_STRIDE * dh)

    pos = jnp.zeros((SUBLANES, CMP_BLOCK * dh), F32).at[0].set(cmp_pos.reshape(-1)).astype(BF16)
    vc_t, kc = _compress(chunked(vc_in), chunked(kc_in), pos, v_w1.astype(BF16),
                         v_w2.T.astype(BF16), k_w1.astype(BF16), k_w2.astype(BF16))

    def with_ones(a):
        n = a.shape[-1]
        return jnp.concatenate([a, jnp.ones((a.shape[0], 1, n), a.dtype),
                                jnp.zeros((a.shape[0], VALUE_ROWS - dh - 1, n), a.dtype)], axis=1)

    n_blk = _round_up(t_len // SEL_BLOCK, LANES)
    c_start = np.arange(n_chunk)[None, :] * CMP_STRIDE
    s_start = np.arange(n_blk)[:, None] * SEL_BLOCK
    overlap_t = jnp.asarray((c_start < s_start + SEL_BLOCK) & (c_start + CMP_BLOCK > s_start), BF16)
    expand_t = jnp.asarray((np.arange(t_len)[:, None] // SEL_BLOCK) == np.arange(n_blk)[None, :], BF16)
    vo = jnp.concatenate([with_ones(vc_t), jnp.broadcast_to(overlap_t, (batch * n_g,) + overlap_t.shape)], axis=1)
    o = _nsa_attention(qn, qr, gl, kc, vo, ks, vs_t, kw, vw_t, expand_t, batch, t_len)
    return _odd_tail(x, o, w_out.astype(BF16), mlp, final)


def kernel(x, mix_norm, mlp_norm, w_up, w_down, final_norm, ev_w_in, ev_qkv_conv, ev_a_log, ev_dt_bias,
           ev_o_norm, ev_sc_conv, ev_w_out, od_w_in, od_cmp_pos, od_cmp_k_w1, od_cmp_k_w2, od_cmp_v_w1,
           od_cmp_v_w2, od_w_out):
    batch, t_len, d = x.shape
    depth = mix_norm.shape[0]
    xs = x.reshape(batch * t_len, d)
    for layer in range(depth):
        i = layer // 2
        mlp = (mlp_norm[layer], w_up[layer].astype(BF16), w_down[layer].astype(BF16), final_norm)
        final = layer == depth - 1
        if layer % 2 == 0:
            xs = _even_layer(xs, mix_norm[layer], ev_w_in[i], ev_qkv_conv[i], ev_a_log[i], ev_dt_bias[i],
                             ev_o_norm[i], ev_sc_conv[i], ev_w_out[i], batch, t_len, mlp, final)
        else:
            xs = _odd_layer(xs, mix_norm[layer], od_w_in[i], od_cmp_pos[i], od_cmp_k_w1[i], od_cmp_k_w2[i],
                            od_cmp_v_w1[i], od_cmp_v_w2[i], od_w_out[i], batch, t_len, mlp, final)
    return xs.reshape(batch, t_len, d)
```

```python
import functools

import jax
import jax.numpy as jnp
import numpy as np
from jax import lax
from jax.experimental import pallas as pl
from jax.experimental.pallas import tpu as pltpu

F32 = jnp.float32
BF16 = jnp.bfloat16
HIGHEST = lax.Precision.HIGHEST

NORM_EPS = 1e-6
GDN_HEADS = 8
GDN_HEAD_DIM = 64
GDN_WIDTH = GDN_HEADS * GDN_HEAD_DIM
GDN_CONV = 4
GDN_CHUNK = 64
SOLVE_BLOCK = 16
SC_CONV = 3
NSA_HEADS = 16
NSA_HEAD_DIM = 64
NSA_KV_GROUPS = 4
NSA_HPG = NSA_HEADS // NSA_KV_GROUPS
NSA_GROUP_WIDTH = NSA_HPG * NSA_HEAD_DIM
NSA_KV_WIDTH = NSA_KV_GROUPS * NSA_HEAD_DIM
CMP_BLOCK = 32
CMP_STRIDE = 16
SEL_BLOCK = 64
N_SELECT = 16
N_FORCED = 3
WINDOW = 512
Q_BLOCK = 512
CMP_PREFIX_STEP = 128
N_Q_PARTS = 2
SEL_KEY_TILE = 512
VALUE_ROWS = NSA_HEAD_DIM + 16
ROPE_THETA = 500000.0
ROPE_DIM = NSA_HEAD_DIM // 4
LOG2_E = 1.4426950408889634
LANES = 128
SUBLANES = 8
BF16_ROWS = 16
VMEM_LIMIT = 56 * 1024 * 1024


def _round_up(n, m):
    return (n + m - 1) // m * m


def _params(*semantics):
    return pltpu.CompilerParams(dimension_semantics=semantics, vmem_limit_bytes=VMEM_LIMIT)


def _rms(x, gain):
    return x * lax.rsqrt(jnp.mean(x * x, axis=-1, keepdims=True) + NORM_EPS) * gain


def _sigmoid(x):
    return 1.0 / (1.0 + jnp.exp(-x))


def _silu(x):
    return x * _sigmoid(x)


def _softplus(x):
    return jnp.maximum(x, 0.0) + jnp.log(1.0 + jnp.exp(-jnp.abs(x)))


def _dot(a, b):
    return jnp.dot(a.astype(BF16), b.astype(BF16), preferred_element_type=F32)


def _dot_nt(a, b):
    return lax.dot_general(a.astype(BF16), b.astype(BF16), (((1,), (1,)), ((), ())),
                           preferred_element_type=F32)


def _dot_f32(a, b):
    return jnp.dot(a, b, precision=HIGHEST, preferred_element_type=F32)


def _norm_matmul_kernel(x_ref, g_ref, w_ref, o_ref, ob_ref):
    h = _rms(x_ref[...], g_ref[...])
    y = jnp.dot(h.astype(BF16), w_ref[...], preferred_element_type=F32)
    n = o_ref.shape[1]
    o_ref[...] = y[:, :n]
    ob_ref[...] = y[:, n:].astype(BF16)


def _norm_matmul(x, gain, w, n_f32, tm=512):
    m, d = x.shape
    n = w.shape[1]
    return pl.pallas_call(
        _norm_matmul_kernel,
        grid=(m // tm,),
        in_specs=[pl.BlockSpec((tm, d), lambda i: (i, 0)),
                  pl.BlockSpec((1, d), lambda i: (0, 0)),
                  pl.BlockSpec((d, n), lambda i: (0, 0))],
        out_specs=[pl.BlockSpec((tm, n_f32), lambda i: (i, 0)), pl.BlockSpec((tm, n - n_f32), lambda i: (i, 0))],
        out_shape=[jax.ShapeDtypeStruct((m, n_f32), F32), jax.ShapeDtypeStruct((m, n - n_f32), BF16)],
        compiler_params=_params("parallel"),
        name="norm_proj",
    )(x, gain.reshape(1, d), w)


def _split(a):
    hi = a.astype(BF16)
    return hi, (a - hi.astype(F32)).astype(BF16)


def _dot3(a, b):
    (ah, al), (bh, bl) = a, b
    m = ah.shape[0]
    with_bh = jnp.dot(jnp.concatenate([ah, al], axis=0), bh, preferred_element_type=F32)
    return with_bh[0:m] + with_bh[m:2 * m] + jnp.dot(ah, bl, preferred_element_type=F32)


def _gdn_kernel(qkv_ref, z_ref, ab_ref, cw_ref, gp_ref, on_ref, o_ref, xbuf, act, state):
    c = GDN_CHUNK
    dh = GDN_HEAD_DIM
    pw = 2 * dh
    n_pairs = GDN_HEADS // 2
    n_batch = qkv_ref.shape[0]
    hist = SUBLANES
    chains = [(b, p) for b in range(n_batch) for p in range(n_pairs)]

    @pl.when(pl.program_id(0) == 0)
    def _():
        xbuf[:, 0:hist, :] = jnp.zeros((n_batch, hist, 3 * GDN_WIDTH), F32)
        state[...] = jnp.zeros(state.shape, F32)

    row = lax.broadcasted_iota(jnp.int32, (c, pw), 0)
    lane = lax.broadcasted_iota(jnp.int32, (c, pw), 1)
    first = lane < dh
    col = jnp.where(first, lane, lane - dh)
    incl = row >= col
    strict = row > col
    eye = (row == col).astype(F32)
    diag_blk = strict & ((row // SOLVE_BLOCK) == (col // SOLVE_BLOCK))
    r2 = lax.broadcasted_iota(jnp.int32, (pw, pw), 0)
    c2 = lax.broadcasted_iota(jnp.int32, (pw, pw), 1)
    same_head = (r2 < dh) == (c2 < dh)
    ones_bd = same_head.astype(BF16)
    tri = (lax.broadcasted_iota(jnp.int32, (c, c), 0) >= lax.broadcasted_iota(jnp.int32, (c, c), 1)).astype(F32)

    def blockdiag(x):
        zero = jnp.zeros_like(x)
        return jnp.concatenate([jnp.where(first, x, zero), jnp.where(first, zero, x)], axis=0)

    def head_sums(x):
        return jnp.dot(x.astype(BF16), ones_bd, preferred_element_type=F32)

    def pdot(xs, y):
        hi, lo = _split(y)
        return _dot3(xs, (blockdiag(hi), blockdiag(lo)))

    gcs, betas = [], []
    for b in range(n_batch):
        x = qkv_ref[b]
        xbuf[b, hist:hist + c, :] = x
        y = x * cw_ref[GDN_CONV - 1:GDN_CONV, :]
        for j in range(GDN_CONV - 1):
            shift = GDN_CONV - 1 - j
            y = y + xbuf[b, hist - shift:hist - shift + c, :] * cw_ref[j:j + 1, :]
        xbuf[b, 0:hist, :] = x[c - hist:c, :]
        act[b] = _silu(y)
        ab = ab_ref[b]
        g_all = -jnp.exp(gp_ref[0:1, :]) * _softplus(ab + gp_ref[1:2, :])
        gcs.append(_dot_f32(tri, g_all))
        betas.append(_sigmoid(ab))

    def pair_cols(a, p, offset):
        return jnp.where(first, a[:, offset + 2 * p:offset + 2 * p + 1], a[:, offset + 2 * p + 1:offset + 2 * p + 2])

    scale = dh ** -0.5
    q = [act[b, :, p * pw:(p + 1) * pw] for b, p in chains]
    k = [act[b, :, GDN_WIDTH + p * pw:GDN_WIDTH + (p + 1) * pw] for b, p in chains]
    v = [act[b, :, 2 * GDN_WIDTH + p * pw:2 * GDN_WIDTH + (p + 1) * pw] for b, p in chains]
    q = [x * lax.rsqrt(head_sums(x * x) + NORM_EPS) * scale for x in q]
    k = [x * lax.rsqrt(head_sums(x * x) + NORM_EPS) for x in k]
    gcol = [pair_cols(gcs[b], p, 0) for b, p in chains]
    bcol = [pair_cols(betas[b], p, GDN_HEADS) for b, p in chains]
    grow = [jnp.sum(g * eye, axis=0, keepdims=True) for g in gcol]
    g_last = [g[c - 1:c, :] for g in gcol]
    decay = [jnp.exp(jnp.where(incl, gc_ - gr_, -jnp.inf)) for gc_, gr_ in zip(gcol, grow)]
    kb = [k_ * b_ for k_, b_ in zip(k, bcol)]
    k_bd = [blockdiag(k_).astype(BF16) for k_ in k]
    both = [_dot_nt(jnp.concatenate([kb_, q_], axis=0), kd_) for kb_, q_, kd_ in zip(kb, q, k_bd)]
    m_low = [jnp.where(strict, x[0:c] * d_, 0.0) for x, d_ in zip(both, decay)]
    attn = [x[c:2 * c] * d_ for x, d_ in zip(both, decay)]

    d_pow = [jnp.where(diag_blk, m_, 0.0) for m_ in m_low]
    l_mat = [m_ - d_ for m_, d_ in zip(m_low, d_pow)]
    p_inv = [eye - d_ for d_ in d_pow]
    d_pow = [pdot(_split(d_), d_) for d_ in d_pow]
    n_steps = int(np.log2(SOLVE_BLOCK)) - 1
    for step in range(n_steps):
        if step == n_steps - 1:
            p_inv = [p_ + pdot(_split(p_), d_) for p_, d_ in zip(p_inv, d_pow)]
        else:
            prod = [pdot(_split(jnp.concatenate([p_, d_], axis=0)), d_) for p_, d_ in zip(p_inv, d_pow)]
            p_inv = [p_ + x[0:c] for p_, x in zip(p_inv, prod)]
            d_pow = [x[c:2 * c] for x in prod]
    n_pow = [pdot(_split(p_), l_) for p_, l_ in zip(p_inv, l_mat)]
    q_inv = [eye - n_ for n_ in n_pow]
    for _ in range(int(np.log2(c // SOLVE_BLOCK)) - 1):
        n_pow = [pdot(_split(n_), n_) for n_ in n_pow]
        q_inv = [q_ + pdot(_split(q_), n_) for q_, n_ in zip(q_inv, n_pow)]
    a_inv = [pdot(_split(q_), p_) for q_, p_ in zip(q_inv, p_inv)]
    u = [_dot(a_, blockdiag(v_ * b_)) for a_, v_, b_ in zip(a_inv, v, bcol)]
    w = [_dot(a_, blockdiag(kb_ * jnp.exp(g_))) for a_, kb_, g_ in zip(a_inv, kb, gcol)]

    s_old = [state[b * n_pairs + p] for b, p in chains]
    on_s = [_dot(jnp.concatenate([w_, q_ * jnp.exp(g_)], axis=0), s_) for w_, q_, g_, s_ in zip(w, q, gcol, s_old)]
    v_new = [u_ - x[0:c] for u_, x in zip(u, on_s)]
    o = [x[c:2 * c] + _dot(a_, blockdiag(vn_)) for x, a_, vn_ in zip(on_s, attn, v_new)]
    k_dec_t = [(k_ * jnp.exp(gl_ - g_)).T for k_, gl_, g_ in zip(k, g_last, gcol)]
    for (b, p), s_, gl_, kt_, vn_ in zip(chains, s_old, g_last, k_dec_t, v_new):
        state[b * n_pairs + p] = s_ * jnp.exp(gl_) + jnp.where(same_head, _dot(kt_, vn_), 0.0)

    inv_dh = 1.0 / dh
    for (b, p), o_ in zip(chains, o):
        zp = z_ref[b, :, p * pw:(p + 1) * pw].astype(F32)
        y = o_ * lax.rsqrt(head_sums(o_ * o_) * inv_dh + NORM_EPS) * on_ref[...]
        o_ref[b, :, p * pw:(p + 1) * pw] = (y * _silu(zp)).astype(o_ref.dtype)


def _gdn(proj, gates, conv_w, gate_params, o_norm, batch, t_len):
    c = GDN_CHUNK
    n_chunks = t_len // c
    w3 = 3 * GDN_WIDTH
    ab_col = w3 // LANES
    proj = proj.reshape(batch, t_len, proj.shape[-1])
    gates = gates.reshape(batch, t_len, gates.shape[-1])
    on_pair = jnp.tile(o_norm.reshape(1, GDN_HEAD_DIM), (1, 2))
    out = pl.pallas_call(
        _gdn_kernel,
        grid=(n_chunks,),
        in_specs=[pl.BlockSpec((batch, c, w3), lambda i: (0, i, 0)),
                  pl.BlockSpec((batch, c, GDN_WIDTH), lambda i: (0, i, 0)),
                  pl.BlockSpec((batch, c, LANES), lambda i: (0, i, ab_col)),
                  pl.BlockSpec((GDN_CONV, w3), lambda i: (0, 0)),
                  pl.BlockSpec((SUBLANES, LANES), lambda i: (0, 0)),
                  pl.BlockSpec((1, 2 * GDN_HEAD_DIM), lambda i: (0, 0))],
        out_specs=pl.BlockSpec((batch, c, GDN_WIDTH), lambda i: (0, i, 0)),
        out_shape=jax.ShapeDtypeStruct((batch, t_len, GDN_WIDTH), BF16),
        scratch_shapes=[pltpu.VMEM((batch, c + SUBLANES, w3), F32),
                        pltpu.VMEM((batch, c, w3), F32),
                        pltpu.VMEM((batch * GDN_HEADS // 2, 2 * GDN_HEAD_DIM, 2 * GDN_HEAD_DIM), F32)],
        compiler_params=_params("arbitrary"),
        name="gdn",
    )(proj, gates, proj, conv_w, gate_params, on_pair)
    return out.reshape(batch * t_len, GDN_WIDTH)


def _mlp_tail(final, x, g_ref, wu_ref, wd_ref, fg_ref, o_ref):
    u = jnp.dot(_rms(x, g_ref[...]).astype(BF16), wu_ref[...], preferred_element_type=F32)
    u = jnp.square(jnp.maximum(u, 0.0))
    y = x + jnp.dot(u.astype(BF16), wd_ref[...], preferred_element_type=F32)
    if final:
        y = _rms(y, fg_ref[...])
    o_ref[...] = y


def _even_tail_kernel(tiles_per_batch, final, x_ref, ya_ref, bg_ref, cg_ref, hs_ref, cgp_ref, hsp_ref,
                      cw_ref, wa_ref, wb_ref, g_ref, wu_ref, wd_ref, fg_ref, o_ref, ubuf):
    tm = x_ref.shape[0]
    hist = cgp_ref.shape[0]
    first = (pl.program_id(0) % tiles_per_batch) == 0
    prev = cgp_ref[...].astype(F32) * hsp_ref[...].astype(F32)
    ubuf[0:hist, :] = jnp.where(first, 0.0, prev)
    u = cg_ref[...].astype(F32) * hs_ref[...].astype(F32)
    ubuf[hist:hist + tm, :] = u
    conv = u * cw_ref[SC_CONV - 1:SC_CONV, :]
    for j in range(SC_CONV - 1):
        shift = SC_CONV - 1 - j
        conv = conv + ubuf[hist - shift:hist - shift + tm, :] * cw_ref[j:j + 1, :]
    yb = bg_ref[...].astype(F32) * conv
    mix = jnp.dot(ya_ref[...], wa_ref[...], preferred_element_type=F32)
    mix = mix + jnp.dot(yb.astype(BF16), wb_ref[...], preferred_element_type=F32)
    _mlp_tail(final, x_ref[...] + mix, g_ref, wu_ref, wd_ref, fg_ref, o_ref)


def _odd_tail_kernel(final, x_ref, a_ref, w_ref, g_ref, wu_ref, wd_ref, fg_ref, o_ref):
    mix = jnp.dot(a_ref[...], w_ref[...], preferred_element_type=F32)
    _mlp_tail(final, x_ref[...] + mix, g_ref, wu_ref, wd_ref, fg_ref, o_ref)


def _resident(a):
    return pl.BlockSpec(a.shape, lambda i: (0, 0), pipeline_mode=pl.Buffered(1))


def _layer_tail(body, x, mixer_args, mixer_specs, mlp, final, scratch=(), tm=512):
    gain, w_up, w_down, final_gain = mlp
    m, d = x.shape
    row = pl.BlockSpec((tm, d), lambda i: (i, 0))
    vec = pl.BlockSpec((1, d), lambda i: (0, 0))
    return pl.pallas_call(
        functools.partial(body, final),
        grid=(m // tm,),
        in_specs=[row, *mixer_specs, vec, _resident(w_up), _resident(w_down), vec],
        out_specs=row,
        out_shape=jax.ShapeDtypeStruct((m, d), F32),
        scratch_shapes=list(scratch),
        compiler_params=_params("parallel"),
        name="layer_tail",
    )(x, *mixer_args, gain.reshape(1, d), w_up, w_down, final_gain.reshape(1, d))


def _even_tail(x, ya, gates, sc_conv, w_a, w_b, t_len, mlp, final, tm=512):
    wd = GDN_WIDTH
    hist = BF16_ROWS
    hb = tm // hist
    prev = lambda col: pl.BlockSpec((hist, wd), lambda i: (jnp.maximum(i * hb - 1, 0), col))
    cur = lambda col: pl.BlockSpec((tm, wd), lambda i: (i, col))
    specs = [cur(0), cur(1), cur(2), cur(3), prev(2), prev(3), _resident(sc_conv), _resident(w_a), _resident(w_b)]
    return _layer_tail(functools.partial(_even_tail_kernel, t_len // tm), x,
                       (ya, gates, gates, gates, gates, gates, sc_conv, w_a, w_b), specs, mlp, final,
                       scratch=[pltpu.VMEM((tm + hist, wd), F32)], tm=tm)


def _odd_tail(x, a, w, mlp, final, tm=512):
    specs = [pl.BlockSpec((tm, a.shape[1]), lambda i: (i, 0)), _resident(w)]
    return _layer_tail(_odd_tail_kernel, x, (a, w), specs, mlp, final, tm=tm)


def _rope(x, cos_t, sin_lo, sin_hi):
    half = ROPE_DIM // 2
    outs = []
    for j in range(x.shape[1] // LANES):
        xs = x[:, j * LANES:(j + 1) * LANES]
        up = pltpu.roll(xs, LANES - half, axis=1)
        down = pltpu.roll(xs, half, axis=1)
        outs.append(xs * cos_t + up * sin_lo + down * sin_hi)
    return jnp.concatenate(outs, axis=-1)


def _odd_proj_kernel(x_ref, g_ref, w_ref, cos_ref, slo_ref, shi_ref,
                     qn_ref, qr_ref, kc_ref, vc_ref, ks_ref, vs_ref, kw_ref, vw_ref, gl_ref):
    h = _rms(x_ref[...], g_ref[...])
    y = jnp.dot(h.astype(BF16), w_ref[...], preferred_element_type=F32)
    cos_t, sin_lo, sin_hi = cos_ref[...], slo_ref[...], shi_ref[...]
    dq = NSA_HEADS * NSA_HEAD_DIM
    kv = NSA_KV_WIDTH
    q = y[:, :dq] * (LOG2_E * NSA_HEAD_DIM ** -0.5)
    qn_ref[...] = q.astype(BF16)
    qr_ref[...] = _rope(q, cos_t, sin_lo, sin_hi).astype(BF16)
    dh = NSA_HEAD_DIM
    tm = y.shape[0]
    for g in range(NSA_KV_GROUPS):
        kc_ref[0, g] = y[:, dq + g * dh:dq + (g + 1) * dh].astype(BF16)
        vc_ref[0, g] = y[:, dq + kv + g * dh:dq + kv + (g + 1) * dh].astype(BF16)
    ones_rows = (lax.broadcasted_iota(jnp.int32, (VALUE_ROWS - dh, tm), 0) == 0).astype(BF16)
    for k_ref, v_ref, col in ((ks_ref, vs_ref, dq + 2 * kv), (kw_ref, vw_ref, dq + 4 * kv)):
        k = _rope(y[:, col:col + kv], cos_t, sin_lo, sin_hi).astype(BF16)
        v_t = y[:, col + kv:col + 2 * kv].T
        for g in range(NSA_KV_GROUPS):
            k_ref[0, g] = k[:, g * dh:(g + 1) * dh]
            v_ref[0, g, 0:dh, :] = v_t[g * dh:(g + 1) * dh].astype(BF16)
            v_ref[0, g, dh:VALUE_ROWS, :] = ones_rows
    gl_ref[...] = y[:, dq + 6 * kv:]


def _odd_proj(x, gain, w, tables, batch, t_len, tm=512):
    m, d = x.shape
    n = w.shape[1]
    n_g, dh = NSA_KV_GROUPS, NSA_HEAD_DIM
    dq = NSA_HEADS * dh
    kv = NSA_KV_WIDTH
    ng = n - dq - 6 * kv
    tpb = t_len // tm
    row = lambda width: pl.BlockSpec((tm, width), lambda i: (i, 0))
    tab = pl.BlockSpec((tm, LANES), lambda i: (i % tpb, 0))
    keys = (pl.BlockSpec((1, n_g, tm, dh), lambda i: (i // tpb, 0, i % tpb, 0)),
            jax.ShapeDtypeStruct((batch, n_g, t_len, dh), BF16))
    values_t = (pl.BlockSpec((1, n_g, VALUE_ROWS, tm), lambda i: (i // tpb, 0, 0, i % tpb)),
                jax.ShapeDtypeStruct((batch, n_g, VALUE_ROWS, t_len), BF16))
    flat = lambda width, dt: (row(width), jax.ShapeDtypeStruct((m, width), dt))
    outs = [flat(dq, BF16), flat(dq, BF16), keys, keys, keys, values_t, keys, values_t, flat(ng, F32)]
    return pl.pallas_call(
        _odd_proj_kernel,
        grid=(m // tm,),
        in_specs=[row(d), pl.BlockSpec((1, d), lambda i: (0, 0)),
                  pl.BlockSpec((d, n), lambda i: (0, 0)), tab, tab, tab],
        out_specs=[spec for spec, _ in outs],
        out_shape=[shape for _, shape in outs],
        compiler_params=_params("parallel"),
        name="odd_proj",
    )(x, gain.reshape(1, d), w, *tables)


def _compress_kernel(xa_ref, xb_ref, pos_ref, a1_ref, a2t_ref, b1_ref, b2_ref, at_ref, b_ref):
    half = xa_ref.shape[2]
    n = xa_ref.shape[1]

    def hidden(x_ref, w1_ref):
        x = x_ref[0]
        top = jnp.dot(x, w1_ref[0:half, :], preferred_element_type=F32)
        bot = jnp.dot(x, w1_ref[half:2 * half, :], preferred_element_type=F32)
        bias = (jnp.dot(pos_ref[:, 0:half], w1_ref[0:half, :], preferred_element_type=F32)
                + jnp.dot(pos_ref[:, half:2 * half], w1_ref[half:2 * half, :], preferred_element_type=F32))
        nxt = pltpu.roll(bot, n - 1, axis=0)
        return _silu(top + nxt + bias[0:1, :]).astype(BF16)

    at_ref[0] = lax.dot_general(a2t_ref[...], hidden(xa_ref, a1_ref), (((1,), (1,)), ((), ())),
                                preferred_element_type=F32).astype(BF16)
    b_ref[0] = jnp.dot(hidden(xb_ref, b1_ref), b2_ref[...], preferred_element_type=F32).astype(BF16)


def _compress(xa, xb, pos, a1, a2t, b1, b2):
    bg, n, half = xa.shape
    dh = NSA_HEAD_DIM
    full = lambda a: pl.BlockSpec(a.shape, lambda i: (0,) * a.ndim)
    return pl.pallas_call(
        _compress_kernel,
        grid=(bg,),
        in_specs=[pl.BlockSpec((1, n, half), lambda i: (i, 0, 0)),
                  pl.BlockSpec((1, n, half), lambda i: (i, 0, 0)),
                  full(pos), full(a1), full(a2t), full(b1), full(b2)],
        out_specs=[pl.BlockSpec((1, dh, n), lambda i: (i, 0, 0)),
                   pl.BlockSpec((1, n, dh), lambda i: (i, 0, 0))],
        out_shape=[jax.ShapeDtypeStruct((bg, dh, n), BF16), jax.ShapeDtypeStruct((bg, n, dh), BF16)],
        compiler_params=_params("parallel"),
        name="compress",
    )(xa, xb, pos, a1, a2t, b1, b2)


MASKED = -1e30


def _nsa_kernel(qn_ref, qr_ref, gl_ref, kc_ref, vo_ref, ks_ref, vst_ref, kw_ref, vwt_ref, expt_ref,
                o_ref, s_even, s_odd, stats, acc_buf):
    qb = Q_BLOCK
    dh = NSA_HEAD_DIM
    hpg = NSA_HPG
    n_cmp = kc_ref.shape[1]
    n_blk = expt_ref.shape[1]
    q0 = pl.program_id(2) * qb
    t_row = q0 + lax.broadcasted_iota(jnp.int32, (1, qb), 1)

    def heads_t(ref):
        xt = ref[...].astype(F32).T
        return jnp.concatenate([xt[h * dh:(h + 1) * dh] for h in range(hpg)], axis=1).astype(BF16)

    qn_t = heads_t(qn_ref)
    qr_t = heads_t(qr_ref)

    def masked(s, mask, fill):
        w = mask.shape[1]
        return jnp.concatenate([jnp.where(mask, s[:, h * w:(h + 1) * w], fill) for h in range(hpg)], axis=1)

    half = qb // N_Q_PARTS

    def part_cols(x, part):
        return jnp.concatenate([x[:, h * qb + part * half:h * qb + (part + 1) * half] for h in range(hpg)], axis=1)

    def join_parts(parts):
        return jnp.concatenate([p[:, h * half:(h + 1) * half] for h in range(hpg) for p in parts], axis=1)

    def exp2_cols(s):
        m = jnp.max(s, axis=0, keepdims=True)
        return jnp.exp2(s - jnp.where(m == -jnp.inf, 0.0, m))

    def compressed(n_vis):
        cmp_end = lax.broadcasted_iota(jnp.int32, (n_vis, 1), 0) * CMP_STRIDE + (CMP_BLOCK - 1)
        s_c = jnp.dot(kc_ref[0, 0:n_vis, :], qn_t, preferred_element_type=F32)
        e_c = exp2_cols(masked(s_c, cmp_end <= t_row, -jnp.inf)).astype(BF16)
        both = jnp.dot(vo_ref[0, :, 0:n_vis], e_c, preferred_element_type=F32)
        acc = both[0:VALUE_ROWS]
        den = acc[dh:dh + 1, :]
        imp_all = both[VALUE_ROWS:VALUE_ROWS + n_blk] * (1.0 / jnp.where(den > 0.0, den, 1.0))
        imp = imp_all[:, 0:qb]
        for h in range(1, hpg):
            imp = imp + imp_all[:, h * qb:(h + 1) * qb]
        return acc, imp

    sizes = sorted({min(n_cmp, n) for n in range(CMP_PREFIX_STEP, n_cmp + CMP_PREFIX_STEP, CMP_PREFIX_STEP)})
    n_vis_max = (q0 + qb - CMP_BLOCK) // CMP_STRIDE + 1
    case = jnp.minimum((n_vis_max + CMP_PREFIX_STEP - 1) // CMP_PREFIX_STEP, len(sizes)) - 1
    acc_c, imp_t = lax.switch(case, [functools.partial(compressed, n) for n in sizes])

    span = WINDOW + half
    acc_w = []
    for part in range(N_Q_PARTS):
        w_start = pl.multiple_of(jnp.maximum(q0 + part * half - WINDOW, 0), half)
        dist = t_row[:, part * half:(part + 1) * half] - (w_start + lax.broadcasted_iota(jnp.int32, (span, 1), 0))
        s_w = jnp.dot(kw_ref[0, 0, pl.ds(w_start, span), :], part_cols(qr_t, part), preferred_element_type=F32)
        e_w = exp2_cols(masked(s_w, (dist >= 0) & (dist < WINDOW), -jnp.inf))
        acc_w.append(jnp.dot(vwt_ref[0, 0, :, pl.ds(w_start, span)], e_w.astype(BF16),
                             preferred_element_type=F32))
    acc_w = join_parts(acc_w)

    js = lax.broadcasted_iota(jnp.int32, (n_blk, qb), 0).astype(F32)
    cur = (t_row // SEL_BLOCK).astype(F32)
    forced = (js == 0.0) | (js == cur) | (js == cur - 1.0)
    val = jnp.where((js > cur) | forced, -jnp.inf, imp_t)
    for _ in range(N_SELECT - N_FORCED):
        best = jnp.max(val, axis=0, keepdims=True)
        first = jnp.min(jnp.where(val == best, js, float(n_blk)), axis=0, keepdims=True)
        val = jnp.where(js == first, -jnp.inf, val)
    chosen = val == -jnp.inf
    kt = SEL_KEY_TILE
    n_tiles = ks_ref.shape[2] // kt
    first_own = (q0 // SEL_BLOCK).astype(F32)

    def extended(bias):
        return jnp.concatenate([jnp.concatenate([bias.astype(BF16)] * hpg, axis=1), qr_t], axis=0)

    q_ext_t = extended(jnp.where(chosen & (js < first_own), 0.0, MASKED))
    q_own_t = extended(jnp.where(chosen & (js >= first_own) & (js <= cur), 0.0, MASKED))

    def scores(i, s_ref):
        start = pl.multiple_of(jnp.minimum(i, n_tiles - 1) * kt, kt)
        k_ext = jnp.concatenate([expt_ref[pl.ds(start, kt), :], ks_ref[0, 0, pl.ds(start, kt), :]], axis=1)
        s = jnp.dot(k_ext, q_ext_t, preferred_element_type=F32)
        s_ref[...] = s
        return jnp.max(s, axis=0, keepdims=True)

    def accumulate(s, m_tile, v_t, carry):
        m_run, acc = carry
        m_new = jnp.maximum(m_run, m_tile)
        p = jnp.exp2(s - m_new)
        acc = jnp.exp2(m_run - m_new) * acc + jnp.dot(v_t, p.astype(BF16), preferred_element_type=F32)
        return m_new, acc

    def consume(i, s_ref, m_tile, carry):
        start = pl.multiple_of(i * kt, kt)
        return accumulate(s_ref[...], m_tile, vst_ref[0, 0, :, pl.ds(start, kt)], carry)

    def pair_step(j, carry):
        m_run, acc, m_even = carry
        m_odd = scores(2 * j + 1, s_odd)
        m_run, acc = consume(2 * j, s_even, m_even, (m_run, acc))
        m_even = scores(2 * j + 2, s_even)
        m_run, acc = consume(2 * j + 1, s_odd, m_odd, (m_run, acc))
        return m_run, acc, m_even

    cols = hpg * qb
    init = (jnp.full((1, cols), MASKED, F32), jnp.zeros((VALUE_ROWS, cols), F32), scores(0, s_even))
    n_past = (q0 + kt - 1) // kt
    m_run, acc_s, m_even = lax.fori_loop(0, (n_past - 1) // 2, pair_step, init)
    stats[0:1, :] = m_run
    stats[1:2, :] = m_even
    acc_buf[...] = acc_s
    carried = lambda: (stats[0:1, :], acc_buf[...])

    def hand_over(carry):
        stats[0:1, :], acc_buf[...] = carry

    @pl.when(n_past % 2 == 1)
    def _():
        hand_over(consume(n_past - 1, s_even, stats[1:2, :], carried()))

    @pl.when((n_past % 2 == 0) & (n_past > 0))
    def _():
        m_odd = scores(n_past - 1, s_odd)
        carry = consume(n_past - 2, s_even, stats[1:2, :], carried())
        hand_over(consume(n_past - 1, s_odd, m_odd, carry))

    m_run, acc_s = stats[0:1, :], acc_buf[...]
    acc_parts = []
    for part in range(N_Q_PARTS):
        n_own = (part + 1) * half
        kpos = q0 + lax.broadcasted_iota(jnp.int32, (n_own, 1), 0)
        k_own = jnp.concatenate([expt_ref[pl.ds(q0, n_own), :], ks_ref[0, 0, pl.ds(q0, n_own), :]], axis=1)
        s_own = masked(jnp.dot(k_own, part_cols(q_own_t, part), preferred_element_type=F32),
                       kpos <= t_row[:, part * half:(part + 1) * half], MASKED)
        _, acc_part = accumulate(s_own, jnp.max(s_own, axis=0, keepdims=True), vst_ref[0, 0, :, pl.ds(q0, n_own)],
                                 (part_cols(m_run, part), part_cols(acc_s, part)))
        acc_parts.append(acc_part)
    acc_s = join_parts(acc_parts)

    gates_t = _sigmoid(gl_ref[...].T)
    outs = []
    for h in range(hpg):
        c = slice(h * qb, (h + 1) * qb)
        o_t = jnp.zeros((dh, qb), F32)
        for j, acc in enumerate((acc_c, acc_s, acc_w)):
            den = acc[dh:dh + 1, c]
            o_t = o_t + (gates_t[3 * h + j:3 * h + j + 1, :] / jnp.where(den > 0.0, den, 1.0)) * acc[0:dh, c]
        outs.append(o_t)
    o_ref[...] = jnp.concatenate(outs, axis=0).T.astype(o_ref.dtype)


def _nsa_attention(qn, qr, gl, kc, vo, ks, vst, kw, vwt, expand_t, batch, t_len):
    assert t_len % (2 * SEL_KEY_TILE) == 0, "key tiles of the selected branch are processed in pairs"
    m = qn.shape[0]
    n_g = NSA_KV_GROUPS
    gw = NSA_GROUP_WIDTH
    dh = NSA_HEAD_DIM
    nq = t_len // Q_BLOCK
    n_cmp = kc.shape[1]
    qspec = pl.BlockSpec((Q_BLOCK, gw), lambda b, g, i: (b * nq + i, g))
    keys = pl.BlockSpec((1, 1, t_len, dh), lambda b, g, i: (b, g, 0, 0))
    values_t = pl.BlockSpec((1, 1, VALUE_ROWS, t_len), lambda b, g, i: (b, g, 0, 0))
    const = lambda a: pl.BlockSpec(a.shape, lambda b, g, i: (0, 0))
    return pl.pallas_call(
        _nsa_kernel,
        grid=(batch, n_g, nq),
        in_specs=[qspec, qspec,
                  pl.BlockSpec((Q_BLOCK, LANES), lambda b, g, i: (b * nq + i, g)),
                  pl.BlockSpec((1, n_cmp, dh), lambda b, g, i: (b * n_g + g, 0, 0)),
                  pl.BlockSpec((1,) + vo.shape[1:], lambda b, g, i: (b * n_g + g, 0, 0)),
                  keys, values_t, keys, values_t, const(expand_t)],
        out_specs=qspec,
        out_shape=jax.ShapeDtypeStruct((m, n_g * gw), BF16),
        scratch_shapes=[pltpu.VMEM((SEL_KEY_TILE, NSA_HPG * Q_BLOCK), F32)] * 2
        + [pltpu.VMEM((SUBLANES, NSA_HPG * Q_BLOCK), F32), pltpu.VMEM((VALUE_ROWS, NSA_HPG * Q_BLOCK), F32)],
        compiler_params=_params("parallel", "parallel", "arbitrary"),
        name="nsa_attention",
    )(qn, qr, gl, kc, vo, ks, vst, kw, vwt, expand_t)


def _pad_cols(w, n):
    return jnp.pad(w, ((0, 0), (0, n - w.shape[1])))


def _even_layer(x, gain, w_in, qkv_conv, a_log, dt_bias, o_norm, sc_conv, w_out, batch, t_len, mlp, final):
    gw = GDN_WIDTH
    ab = _pad_cols(w_in[:, 4 * gw:4 * gw + 2 * GDN_HEADS], LANES)
    w_all = jnp.concatenate([w_in[:, :3 * gw], ab, w_in[:, 3 * gw:4 * gw], w_in[:, 4 * gw + 2 * GDN_HEADS:]],
                            axis=1).astype(BF16)
    proj, gates = _norm_matmul(x, gain, w_all, 3 * gw + LANES)
    gate_params = jnp.zeros((SUBLANES, LANES), F32)
    gate_params = gate_params.at[0, :GDN_HEADS].set(a_log).at[1, :GDN_HEADS].set(dt_bias)
    ya = _gdn(proj, gates, qkv_conv, gate_params, o_norm, batch, t_len)
    return _even_tail(x, ya, gates, sc_conv, w_out[:gw].astype(BF16), w_out[gw:].astype(BF16), t_len, mlp, final)


def _rope_tables(t_len):
    half = ROPE_DIM // 2
    inv_freq = ROPE_THETA ** (-jnp.arange(0, ROPE_DIM, 2, dtype=F32) / ROPE_DIM)
    ang = jnp.arange(t_len, dtype=F32)[:, None] * inv_freq[None, :]
    cos, sin = jnp.cos(ang), jnp.sin(ang)
    pad = NSA_HEAD_DIM - ROPE_DIM
    head = lambda a, b, fill: jnp.concatenate([a, b, jnp.full((t_len, pad), fill, F32)], axis=1)
    zeros = jnp.zeros((t_len, half), F32)
    reps = LANES // NSA_HEAD_DIM
    return (jnp.tile(head(cos, cos, 1.0), (1, reps)),
            jnp.tile(head(-sin, zeros, 0.0), (1, reps)),
            jnp.tile(head(zeros, sin, 0.0), (1, reps)))


def _odd_layer(x, gain, w_in, cmp_pos, k_w1, k_w2, v_w1, v_w2, w_out, batch, t_len, mlp, final):
    n_g, dh, hpg = NSA_KV_GROUPS, NSA_HEAD_DIM, NSA_HPG
    dq = NSA_HEADS * dh
    kv = NSA_KV_WIDTH
    gate_w = w_in[:, dq + 6 * kv:].reshape(-1, n_g, 3 * hpg)
    gate_w = jnp.pad(gate_w, ((0, 0), (0, 0), (0, LANES - 3 * hpg))).reshape(-1, n_g * LANES)
    w_all = jnp.concatenate([w_in[:, :dq + 6 * kv], gate_w], axis=1).astype(BF16)
    qn, qr, kc_in, vc_in, ks, vs_t, kw, vw_t, gl = _odd_proj(x, gain, w_all, _rope_tables(t_len), batch, t_len)

    n_chunk = t_len // CMP_STRIDE
    chunked = lambda a: a.reshape(batch * n_g, n_chunk, CMP_STRIDE * dh)

    pos = jnp.zeros((SUBLANES, CMP_BLOCK * dh), F32).at[0].set(cmp_pos.reshape(-1)).astype(BF16)
    vc_t, kc = _compress(chunked(vc_in), chunked(kc_in), pos, v_w1.astype(BF16),
                         v_w2.T.astype(BF16), k_w1.astype(BF16), k_w2.astype(BF16))

    def with_ones(a):
        n = a.shape[-1]
        return jnp.concatenate([a, jnp.ones((a.shape[0], 1, n), a.dtype),
                                jnp.zeros((a.shape[0], VALUE_ROWS - dh - 1, n), a.dtype)], axis=1)

    n_blk = _round_up(t_len // SEL_BLOCK, LANES)
    c_start = np.arange(n_chunk)[None, :] * CMP_STRIDE
    s_start = np.arange(n_blk)[:, None] * SEL_BLOCK
    overlap_t = jnp.asarray((c_start < s_start + SEL_BLOCK) & (c_start + CMP_BLOCK > s_start), BF16)
    expand_t = jnp.asarray((np.arange(t_len)[:, None] // SEL_BLOCK) == np.arange(n_blk)[None, :], BF16)
    vo = jnp.concatenate([with_ones(vc_t), jnp.broadcast_to(overlap_t, (batch * n_g,) + overlap_t.shape)], axis=1)
    o = _nsa_attention(qn, qr, gl, kc, vo, ks, vs_t, kw, vw_t, expand_t, batch, t_len)
    return _odd_tail(x, o, w_out.astype(BF16), mlp, final)


def kernel(x, mix_norm, mlp_norm, w_up, w_down, final_norm, ev_w_in, ev_qkv_conv, ev_a_log, ev_dt_bias,
           ev_o_norm, ev_sc_conv, ev_w_out, od_w_in, od_cmp_pos, od_cmp_k_w1, od_cmp_k_w2, od_cmp_v_w1,
           od_cmp_v_w2, od_w_out):
    batch, t_len, d = x.shape
    depth = mix_norm.shape[0]
    xs = x.reshape(batch * t_len, d)
    for layer in range(depth):
        i = layer // 2
        mlp = (mlp_norm[layer], w_up[layer].astype(BF16), w_down[layer].astype(BF16), final_norm)
        final = layer == depth - 1
        if layer % 2 == 0:
            xs = _even_layer(xs, mix_norm[layer], ev_w_in[i], ev_qkv_conv[i], ev_a_log[i], ev_dt_bias[i],
                             ev_o_norm[i], ev_sc_conv[i], ev_w_out[i], batch, t_len, mlp, final)
        else:
            xs = _odd_layer(xs, mix_norm[layer], od_w_in[i], od_cmp_pos[i], od_cmp_k_w1[i], od_cmp_k_w2[i],
                            od_cmp_v_w1[i], od_cmp_v_w2[i], od_w_out[i], batch, t_len, mlp, final)
    return xs.reshape(batch, t_len, d)
```

```python
import functools

import jax
import jax.numpy as jnp
import numpy as np
from jax import lax
from jax.experimental import pallas as pl
from jax.experimental.pallas import tpu as pltpu

F32 = jnp.float32
BF16 = jnp.bfloat16

NORM_EPS = 1e-6
GDN_HEADS = 8
GDN_HEAD_DIM = 64
GDN_WIDTH = GDN_HEADS * GDN_HEAD_DIM
GDN_CONV = 4
GDN_CHUNK = 64
SOLVE_BLOCK = 16
SC_CONV = 3
NSA_HEADS = 16
NSA_HEAD_DIM = 64
NSA_KV_GROUPS = 4
NSA_HPG = NSA_HEADS // NSA_KV_GROUPS
NSA_GROUP_WIDTH = NSA_HPG * NSA_HEAD_DIM
NSA_KV_WIDTH = NSA_KV_GROUPS * NSA_HEAD_DIM
CMP_BLOCK = 32
CMP_STRIDE = 16
SEL_BLOCK = 64
N_SELECT = 16
N_FORCED = 3
WINDOW = 512
Q_BLOCK = 512
CMP_PREFIX_STEP = 128
N_Q_PARTS = 2
SEL_KEY_TILE = 512
VALUE_ROWS = NSA_HEAD_DIM + 16
ROPE_THETA = 500000.0
ROPE_DIM = NSA_HEAD_DIM // 4
LOG2_E = 1.4426950408889634
LANES = 128
SUBLANES = 8
BF16_ROWS = 16
VMEM_LIMIT = 56 * 1024 * 1024


def _round_up(n, m):
    return (n + m - 1) // m * m


def _params(*semantics):
    return pltpu.CompilerParams(dimension_semantics=semantics, vmem_limit_bytes=VMEM_LIMIT)


def _rms(x, gain):
    return x * lax.rsqrt(jnp.mean(x * x, axis=-1, keepdims=True) + NORM_EPS) * gain


def _sigmoid(x):
    return 1.0 / (1.0 + jnp.exp(-x))


def _silu(x):
    return x * _sigmoid(x)


def _softplus(x):
    return jnp.maximum(x, 0.0) + jnp.log(1.0 + jnp.exp(-jnp.abs(x)))


def _dot(a, b):
    return jnp.dot(a.astype(BF16), b.astype(BF16), preferred_element_type=F32)


def _dot_nt(a, b):
    return lax.dot_general(a.astype(BF16), b.astype(BF16), (((1,), (1,)), ((), ())),
                           preferred_element_type=F32)


def _norm_matmul_kernel(x_ref, g_ref, w_ref, o_ref, ob_ref):
    h = _rms(x_ref[...], g_ref[...])
    y = jnp.dot(h.astype(BF16), w_ref[...], preferred_element_type=F32)
    n = o_ref.shape[1]
    o_ref[...] = y[:, :n]
    ob_ref[...] = y[:, n:].astype(BF16)


def _norm_matmul(x, gain, w, n_f32, tm=512):
    m, d = x.shape
    n = w.shape[1]
    return pl.pallas_call(
        _norm_matmul_kernel,
        grid=(m // tm,),
        in_specs=[pl.BlockSpec((tm, d), lambda i: (i, 0)),
                  pl.BlockSpec((1, d), lambda i: (0, 0)),
                  pl.BlockSpec((d, n), lambda i: (0, 0))],
        out_specs=[pl.BlockSpec((tm, n_f32), lambda i: (i, 0)), pl.BlockSpec((tm, n - n_f32), lambda i: (i, 0))],
        out_shape=[jax.ShapeDtypeStruct((m, n_f32), F32), jax.ShapeDtypeStruct((m, n - n_f32), BF16)],
        compiler_params=_params("parallel"),
        name="norm_proj",
    )(x, gain.reshape(1, d), w)


def _split(a):
    hi = a.astype(BF16)
    return hi, (a - hi.astype(F32)).astype(BF16)


def _dot3(a, b):
    (ah, al), (bh, bl) = a, b
    m = ah.shape[0]
    with_bh = jnp.dot(jnp.concatenate([ah, al], axis=0), bh, preferred_element_type=F32)
    return with_bh[0:m] + with_bh[m:2 * m] + jnp.dot(ah, bl, preferred_element_type=F32)


def _gdn_kernel(qkv_ref, z_ref, ab_ref, cw_ref, gp_ref, on_ref, o_ref, xbuf, act, state):
    c = GDN_CHUNK
    dh = GDN_HEAD_DIM
    pw = 2 * dh
    n_pairs = GDN_HEADS // 2
    n_batch = qkv_ref.shape[0]
    hist = SUBLANES
    chains = [(b, p) for b in range(n_batch) for p in range(n_pairs)]

    @pl.when(pl.program_id(0) == 0)
    def _():
        xbuf[:, 0:hist, :] = jnp.zeros((n_batch, hist, 3 * GDN_WIDTH), F32)
        state[...] = jnp.zeros(state.shape, F32)

    row = lax.broadcasted_iota(jnp.int32, (c, pw), 0)
    lane = lax.broadcasted_iota(jnp.int32, (c, pw), 1)
    first = lane < dh
    col = jnp.where(first, lane, lane - dh)
    incl = row >= col
    strict = row > col
    eye = (row == col).astype(F32)
    diag_blk = strict & ((row // SOLVE_BLOCK) == (col // SOLVE_BLOCK))
    r2 = lax.broadcasted_iota(jnp.int32, (pw, pw), 0)
    c2 = lax.broadcasted_iota(jnp.int32, (pw, pw), 1)
    same_head = (r2 < dh) == (c2 < dh)
    ones_bd = same_head.astype(BF16)
    tri = (lax.broadcasted_iota(jnp.int32, (c, c), 0) >= lax.broadcasted_iota(jnp.int32, (c, c), 1)).astype(BF16)

    def blockdiag(x):
        zero = jnp.zeros_like(x)
        return jnp.concatenate([jnp.where(first, x, zero), jnp.where(first, zero, x)], axis=0)

    def head_sums(x):
        return jnp.dot(x.astype(BF16), ones_bd, preferred_element_type=F32)

    def pdot(xs, y):
        hi, lo = _split(y)
        return _dot3(xs, (blockdiag(hi), blockdiag(lo)))

    gcs, betas = [], []
    for b in range(n_batch):
        x = qkv_ref[b]
        xbuf[b, hist:hist + c, :] = x
        y = x * cw_ref[GDN_CONV - 1:GDN_CONV, :]
        for j in range(GDN_CONV - 1):
            shift = GDN_CONV - 1 - j
            y = y + xbuf[b, hist - shift:hist - shift + c, :] * cw_ref[j:j + 1, :]
        xbuf[b, 0:hist, :] = x[c - hist:c, :]
        act[b] = _silu(y)
        ab = ab_ref[b]
        g_all = -jnp.exp(gp_ref[0:1, :]) * _softplus(ab + gp_ref[1:2, :])
        g_hi, g_rest = _split(g_all)
        g_mid = (g_all - g_hi.astype(F32) - g_rest.astype(F32)).astype(BF16)
        cum = jnp.dot(tri, jnp.concatenate([g_hi, g_rest, g_mid], axis=1), preferred_element_type=F32)
        gcs.append(cum[:, 0:LANES] + cum[:, LANES:2 * LANES] + cum[:, 2 * LANES:3 * LANES])
        betas.append(_sigmoid(ab))

    def pair_cols(a, p, offset):
        return jnp.where(first, a[:, offset + 2 * p:offset + 2 * p + 1], a[:, offset + 2 * p + 1:offset + 2 * p + 2])

    scale = dh ** -0.5
    q = [act[b, :, p * pw:(p + 1) * pw] for b, p in chains]
    k = [act[b, :, GDN_WIDTH + p * pw:GDN_WIDTH + (p + 1) * pw] for b, p in chains]
    v = [act[b, :, 2 * GDN_WIDTH + p * pw:2 * GDN_WIDTH + (p + 1) * pw] for b, p in chains]
    ssq = [head_sums(jnp.concatenate([q_ * q_, k_ * k_], axis=0)) for q_, k_ in zip(q, k)]
    q = [x * lax.rsqrt(s_[0:c] + NORM_EPS) * scale for x, s_ in zip(q, ssq)]
    k = [x * lax.rsqrt(s_[c:2 * c] + NORM_EPS) for x, s_ in zip(k, ssq)]
    gcol = [pair_cols(gcs[b], p, 0) for b, p in chains]
    bcol = [pair_cols(betas[b], p, GDN_HEADS) for b, p in chains]
    grow = [jnp.sum(g * eye, axis=0, keepdims=True) for g in gcol]
    g_last = [g[c - 1:c, :] for g in gcol]
    decay = [jnp.exp(jnp.where(incl, gc_ - gr_, -jnp.inf)) for gc_, gr_ in zip(gcol, grow)]
    kb = [k_ * b_ for k_, b_ in zip(k, bcol)]
    k_bd = [blockdiag(k_).astype(BF16) for k_ in k]
    both = [_dot_nt(jnp.concatenate([kb_, q_], axis=0), kd_) for kb_, q_, kd_ in zip(kb, q, k_bd)]
    m_low = [jnp.where(strict, x[0:c] * d_, 0.0) for x, d_ in zip(both, decay)]
    attn = [x[c:2 * c] * d_ for x, d_ in zip(both, decay)]

    d_pow = [jnp.where(diag_blk, m_, 0.0) for m_ in m_low]
    l_mat = [m_ - d_ for m_, d_ in zip(m_low, d_pow)]
    p_inv = [eye - d_ for d_ in d_pow]
    d_pow = [pdot(_split(d_), d_) for d_ in d_pow]
    n_steps = int(np.log2(SOLVE_BLOCK)) - 1
    for step in range(n_steps):
        if step == n_steps - 1:
            p_inv = [p_ + pdot(_split(p_), d_) for p_, d_ in zip(p_inv, d_pow)]
        else:
            prod = [pdot(_split(jnp.concatenate([p_, d_], axis=0)), d_) for p_, d_ in zip(p_inv, d_pow)]
            p_inv = [p_ + x[0:c] for p_, x in zip(p_inv, prod)]
            d_pow = [x[c:2 * c] for x in prod]
    n_pow = [pdot(_split(p_), l_) for p_, l_ in zip(p_inv, l_mat)]
    q_inv = [eye - n_ for n_ in n_pow]
    for _ in range(int(np.log2(c // SOLVE_BLOCK)) - 1):
        n_pow = [pdot(_split(n_), n_) for n_ in n_pow]
        q_inv = [q_ + pdot(_split(q_), n_) for q_, n_ in zip(q_inv, n_pow)]
    a_inv = [pdot(_split(q_), p_) for q_, p_ in zip(q_inv, p_inv)]
    uw = [_dot(a_, jnp.concatenate([blockdiag(v_ * b_), blockdiag(kb_ * jnp.exp(g_))], axis=1))
          for a_, v_, b_, kb_, g_ in zip(a_inv, v, bcol, kb, gcol)]
    u = [x[:, 0:pw] for x in uw]
    w = [x[:, pw:2 * pw] for x in uw]

    s_old = [state[b * n_pairs + p] for b, p in chains]
    on_s = [_dot(jnp.concatenate([w_, q_ * jnp.exp(g_)], axis=0), s_) for w_, q_, g_, s_ in zip(w, q, gcol, s_old)]
    v_new = [u_ - x[0:c] for u_, x in zip(u, on_s)]
    o = [x[c:2 * c] + _dot(a_, blockdiag(vn_)) for x, a_, vn_ in zip(on_s, attn, v_new)]
    k_dec_t = [(k_ * jnp.exp(gl_ - g_)).T for k_, gl_, g_ in zip(k, g_last, gcol)]
    for (b, p), s_, gl_, kt_, vn_ in zip(chains, s_old, g_last, k_dec_t, v_new):
        state[b * n_pairs + p] = s_ * jnp.exp(gl_) + jnp.where(same_head, _dot(kt_, vn_), 0.0)

    inv_dh = 1.0 / dh
    for (b, p), o_ in zip(chains, o):
        zp = z_ref[b, :, p * pw:(p + 1) * pw].astype(F32)
        y = o_ * lax.rsqrt(head_sums(o_ * o_) * inv_dh + NORM_EPS) * on_ref[...]
        o_ref[b, :, p * pw:(p + 1) * pw] = (y * _silu(zp)).astype(o_ref.dtype)


def _gdn(proj, gates, conv_w, gate_params, o_norm, batch, t_len):
    c = GDN_CHUNK
    n_chunks = t_len // c
    w3 = 3 * GDN_WIDTH
    ab_col = w3 // LANES
    proj = proj.reshape(batch, t_len, proj.shape[-1])
    gates = gates.reshape(batch, t_len, gates.shape[-1])
    on_pair = jnp.tile(o_norm.reshape(1, GDN_HEAD_DIM), (1, 2))
    out = pl.pallas_call(
        _gdn_kernel,
        grid=(n_chunks,),
        in_specs=[pl.BlockSpec((batch, c, w3), lambda i: (0, i, 0)),
                  pl.BlockSpec((batch, c, GDN_WIDTH), lambda i: (0, i, 0)),
                  pl.BlockSpec((batch, c, LANES), lambda i: (0, i, ab_col)),
                  pl.BlockSpec((GDN_CONV, w3), lambda i: (0, 0)),
                  pl.BlockSpec((SUBLANES, LANES), lambda i: (0, 0)),
                  pl.BlockSpec((1, 2 * GDN_HEAD_DIM), lambda i: (0, 0))],
        out_specs=pl.BlockSpec((batch, c, GDN_WIDTH), lambda i: (0, i, 0)),
        out_shape=jax.ShapeDtypeStruct((batch, t_len, GDN_WIDTH), BF16),
        scratch_shapes=[pltpu.VMEM((batch, c + SUBLANES, w3), F32),
                        pltpu.VMEM((batch, c, w3), F32),
                        pltpu.VMEM((batch * GDN_HEADS // 2, 2 * GDN_HEAD_DIM, 2 * GDN_HEAD_DIM), F32)],
        compiler_params=_params("arbitrary"),
        name="gdn",
    )(proj, gates, proj, conv_w, gate_params, on_pair)
    return out.reshape(batch * t_len, GDN_WIDTH)


def _mlp_tail(final, x, g_ref, wu_ref, wd_ref, fg_ref, o_ref):
    u = jnp.dot(_rms(x, g_ref[...]).astype(BF16), wu_ref[...], preferred_element_type=F32)
    u = jnp.square(jnp.maximum(u, 0.0))
    y = x + jnp.dot(u.astype(BF16), wd_ref[...], preferred_element_type=F32)
    if final:
        y = _rms(y, fg_ref[...])
    o_ref[...] = y


def _even_tail_kernel(tiles_per_batch, final, x_ref, ya_ref, bg_ref, cg_ref, hs_ref, cgp_ref, hsp_ref,
                      cw_ref, wa_ref, wb_ref, g_ref, wu_ref, wd_ref, fg_ref, o_ref, ubuf):
    tm = x_ref.shape[0]
    hist = cgp_ref.shape[0]
    first = (pl.program_id(0) % tiles_per_batch) == 0
    prev = cgp_ref[...].astype(F32) * hsp_ref[...].astype(F32)
    ubuf[0:hist, :] = jnp.where(first, 0.0, prev)
    u = cg_ref[...].astype(F32) * hs_ref[...].astype(F32)
    ubuf[hist:hist + tm, :] = u
    conv = u * cw_ref[SC_CONV - 1:SC_CONV, :]
    for j in range(SC_CONV - 1):
        shift = SC_CONV - 1 - j
        conv = conv + ubuf[hist - shift:hist - shift + tm, :] * cw_ref[j:j + 1, :]
    yb = bg_ref[...].astype(F32) * conv
    mix = jnp.dot(ya_ref[...], wa_ref[...], preferred_element_type=F32)
    mix = mix + jnp.dot(yb.astype(BF16), wb_ref[...], preferred_element_type=F32)
    _mlp_tail(final, x_ref[...] + mix, g_ref, wu_ref, wd_ref, fg_ref, o_ref)


def _odd_tail_kernel(final, x_ref, a_ref, w_ref, g_ref, wu_ref, wd_ref, fg_ref, o_ref):
    mix = jnp.dot(a_ref[...], w_ref[...], preferred_element_type=F32)
    _mlp_tail(final, x_ref[...] + mix, g_ref, wu_ref, wd_ref, fg_ref, o_ref)


def _resident(a):
    return pl.BlockSpec(a.shape, lambda i: (0, 0), pipeline_mode=pl.Buffered(1))


def _layer_tail(body, x, mixer_args, mixer_specs, mlp, final, scratch=(), tm=512):
    gain, w_up, w_down, final_gain = mlp
    m, d = x.shape
    row = pl.BlockSpec((tm, d), lambda i: (i, 0))
    vec = pl.BlockSpec((1, d), lambda i: (0, 0))
    return pl.pallas_call(
        functools.partial(body, final),
        grid=(m // tm,),
        in_specs=[row, *mixer_specs, vec, _resident(w_up), _resident(w_down), vec],
        out_specs=row,
        out_shape=jax.ShapeDtypeStruct((m, d), F32),
        scratch_shapes=list(scratch),
        compiler_params=_params("parallel"),
        name="layer_tail",
    )(x, *mixer_args, gain.reshape(1, d), w_up, w_down, final_gain.reshape(1, d))


def _even_tail(x, ya, gates, sc_conv, w_a, w_b, t_len, mlp, final, tm=512):
    wd = GDN_WIDTH
    hist = BF16_ROWS
    hb = tm // hist
    prev = lambda col: pl.BlockSpec((hist, wd), lambda i: (jnp.maximum(i * hb - 1, 0), col))
    cur = lambda col: pl.BlockSpec((tm, wd), lambda i: (i, col))
    specs = [cur(0), cur(1), cur(2), cur(3), prev(2), prev(3), _resident(sc_conv), _resident(w_a), _resident(w_b)]
    return _layer_tail(functools.partial(_even_tail_kernel, t_len // tm), x,
                       (ya, gates, gates, gates, gates, gates, sc_conv, w_a, w_b), specs, mlp, final,
                       scratch=[pltpu.VMEM((tm + hist, wd), F32)], tm=tm)


def _odd_tail(x, a, w, mlp, final, tm=512):
    specs = [pl.BlockSpec((tm, a.shape[1]), lambda i: (i, 0)), _resident(w)]
    return _layer_tail(_odd_tail_kernel, x, (a, w), specs, mlp, final, tm=tm)


def _rope(x, cos_t, sin_lo, sin_hi):
    half = ROPE_DIM // 2
    outs = []
    for j in range(x.shape[1] // LANES):
        xs = x[:, j * LANES:(j + 1) * LANES]
        up = pltpu.roll(xs, LANES - half, axis=1)
        down = pltpu.roll(xs, half, axis=1)
        outs.append(xs * cos_t + up * sin_lo + down * sin_hi)
    return jnp.concatenate(outs, axis=-1)


def _odd_proj_kernel(x_ref, g_ref, w_ref, cos_ref, slo_ref, shi_ref,
                     qn_ref, qr_ref, kc_ref, vc_ref, ks_ref, vs_ref, kw_ref, vw_ref, gl_ref):
    h = _rms(x_ref[...], g_ref[...])
    y = jnp.dot(h.astype(BF16), w_ref[...], preferred_element_type=F32)
    cos_t, sin_lo, sin_hi = cos_ref[...], slo_ref[...], shi_ref[...]
    dq = NSA_HEADS * NSA_HEAD_DIM
    kv = NSA_KV_WIDTH
    q = y[:, :dq] * (LOG2_E * NSA_HEAD_DIM ** -0.5)
    qn_ref[...] = q.astype(BF16)
    qr_ref[...] = _rope(q, cos_t, sin_lo, sin_hi).astype(BF16)
    dh = NSA_HEAD_DIM
    tm = y.shape[0]
    for g in range(NSA_KV_GROUPS):
        kc_ref[0, g] = y[:, dq + g * dh:dq + (g + 1) * dh].astype(BF16)
        vc_ref[0, g] = y[:, dq + kv + g * dh:dq + kv + (g + 1) * dh].astype(BF16)
    ones_rows = (lax.broadcasted_iota(jnp.int32, (VALUE_ROWS - dh, tm), 0) == 0).astype(BF16)
    for k_ref, v_ref, col in ((ks_ref, vs_ref, dq + 2 * kv), (kw_ref, vw_ref, dq + 4 * kv)):
        k = _rope(y[:, col:col + kv], cos_t, sin_lo, sin_hi).astype(BF16)
        v_t = y[:, col + kv:col + 2 * kv].T
        for g in range(NSA_KV_GROUPS):
            k_ref[0, g] = k[:, g * dh:(g + 1) * dh]
            v_ref[0, g, 0:dh, :] = v_t[g * dh:(g + 1) * dh].astype(BF16)
            v_ref[0, g, dh:VALUE_ROWS, :] = ones_rows
    gl_ref[...] = y[:, dq + 6 * kv:]


def _odd_proj(x, gain, w, tables, batch, t_len, tm=512):
    m, d = x.shape
    n = w.shape[1]
    n_g, dh = NSA_KV_GROUPS, NSA_HEAD_DIM
    dq = NSA_HEADS * dh
    kv = NSA_KV_WIDTH
    ng = n - dq - 6 * kv
    tpb = t_len // tm
    row = lambda width: pl.BlockSpec((tm, width), lambda i: (i, 0))
    tab = pl.BlockSpec((tm, LANES), lambda i: (i % tpb, 0))
    keys = (pl.BlockSpec((1, n_g, tm, dh), lambda i: (i // tpb, 0, i % tpb, 0)),
            jax.ShapeDtypeStruct((batch, n_g, t_len, dh), BF16))
    values_t = (pl.BlockSpec((1, n_g, VALUE_ROWS, tm), lambda i: (i // tpb, 0, 0, i % tpb)),
                jax.ShapeDtypeStruct((batch, n_g, VALUE_ROWS, t_len), BF16))
    flat = lambda width, dt: (row(width), jax.ShapeDtypeStruct((m, width), dt))
    outs = [flat(dq, BF16), flat(dq, BF16), keys, keys, keys, values_t, keys, values_t, flat(ng, F32)]
    return pl.pallas_call(
        _odd_proj_kernel,
        grid=(m // tm,),
        in_specs=[row(d), pl.BlockSpec((1, d), lambda i: (0, 0)),
                  pl.BlockSpec((d, n), lambda i: (0, 0)), tab, tab, tab],
        out_specs=[spec for spec, _ in outs],
        out_shape=[shape for _, shape in outs],
        compiler_params=_params("parallel"),
        name="odd_proj",
    )(x, gain.reshape(1, d), w, *tables)


def _compress_kernel(xa_ref, xb_ref, pos_ref, a1_ref, a2t_ref, b1_ref, b2_ref, at_ref, b_ref):
    half = xa_ref.shape[2]
    n = xa_ref.shape[1]

    def hidden(x_ref, w1_ref):
        x = x_ref[0]
        top = jnp.dot(x, w1_ref[0:half, :], preferred_element_type=F32)
        bot = jnp.dot(x, w1_ref[half:2 * half, :], preferred_element_type=F32)
        bias = (jnp.dot(pos_ref[:, 0:half], w1_ref[0:half, :], preferred_element_type=F32)
                + jnp.dot(pos_ref[:, half:2 * half], w1_ref[half:2 * half, :], preferred_element_type=F32))
        nxt = pltpu.roll(bot, n - 1, axis=0)
        return _silu(top + nxt + bias[0:1, :]).astype(BF16)

    at_ref[0] = lax.dot_general(a2t_ref[...], hidden(xa_ref, a1_ref), (((1,), (1,)), ((), ())),
                                preferred_element_type=F32).astype(BF16)
    b_ref[0] = jnp.dot(hidden(xb_ref, b1_ref), b2_ref[...], preferred_element_type=F32).astype(BF16)


def _compress(xa, xb, pos, a1, a2t, b1, b2):
    bg, n, half = xa.shape
    dh = NSA_HEAD_DIM
    full = lambda a: pl.BlockSpec(a.shape, lambda i: (0,) * a.ndim)
    return pl.pallas_call(
        _compress_kernel,
        grid=(bg,),
        in_specs=[pl.BlockSpec((1, n, half), lambda i: (i, 0, 0)),
                  pl.BlockSpec((1, n, half), lambda i: (i, 0, 0)),
                  full(pos), full(a1), full(a2t), full(b1), full(b2)],
        out_specs=[pl.BlockSpec((1, dh, n), lambda i: (i, 0, 0)),
                   pl.BlockSpec((1, n, dh), lambda i: (i, 0, 0))],
        out_shape=[jax.ShapeDtypeStruct((bg, dh, n), BF16), jax.ShapeDtypeStruct((bg, n, dh), BF16)],
        compiler_params=_params("parallel"),
        name="compress",
    )(xa, xb, pos, a1, a2t, b1, b2)


MASKED = -1e30


def _nsa_kernel(qn_ref, qr_ref, gl_ref, kc_ref, vo_ref, ks_ref, vst_ref, kw_ref, vwt_ref, expt_ref,
                o_ref, s_even, s_odd, stats, acc_buf):
    qb = Q_BLOCK
    dh = NSA_HEAD_DIM
    hpg = NSA_HPG
    n_cmp = kc_ref.shape[1]
    n_blk = expt_ref.shape[1]
    q0 = pl.program_id(2) * qb
    t_row = q0 + lax.broadcasted_iota(jnp.int32, (1, qb), 1)

    def heads_t(ref):
        xt = ref[...].astype(F32).T
        return jnp.concatenate([xt[h * dh:(h + 1) * dh] for h in range(hpg)], axis=1).astype(BF16)

    qn_t = heads_t(qn_ref)
    qr_t = heads_t(qr_ref)

    def masked(s, mask, fill):
        w = mask.shape[1]
        return jnp.concatenate([jnp.where(mask, s[:, h * w:(h + 1) * w], fill) for h in range(hpg)], axis=1)

    half = qb // N_Q_PARTS

    def part_cols(x, part):
        return jnp.concatenate([x[:, h * qb + part * half:h * qb + (part + 1) * half] for h in range(hpg)], axis=1)

    def join_parts(parts):
        return jnp.concatenate([p[:, h * half:(h + 1) * half] for h in range(hpg) for p in parts], axis=1)

    def exp2_cols(s):
        m = jnp.max(s, axis=0, keepdims=True)
        return jnp.exp2(s - jnp.where(m == -jnp.inf, 0.0, m))

    def compressed(n_vis):
        cmp_end = lax.broadcasted_iota(jnp.int32, (n_vis, 1), 0) * CMP_STRIDE + (CMP_BLOCK - 1)
        s_c = jnp.dot(kc_ref[0, 0:n_vis, :], qn_t, preferred_element_type=F32)
        e_c = exp2_cols(masked(s_c, cmp_end <= t_row, -jnp.inf)).astype(BF16)
        both = jnp.dot(vo_ref[0, :, 0:n_vis], e_c, preferred_element_type=F32)
        acc = both[0:VALUE_ROWS]
        den = acc[dh:dh + 1, :]
        imp_all = both[VALUE_ROWS:VALUE_ROWS + n_blk] * (1.0 / jnp.where(den > 0.0, den, 1.0))
        imp = imp_all[:, 0:qb]
        for h in range(1, hpg):
            imp = imp + imp_all[:, h * qb:(h + 1) * qb]
        return acc, imp

    sizes = sorted({min(n_cmp, n) for n in range(CMP_PREFIX_STEP, n_cmp + CMP_PREFIX_STEP, CMP_PREFIX_STEP)})
    n_vis_max = (q0 + qb - CMP_BLOCK) // CMP_STRIDE + 1
    case = jnp.minimum((n_vis_max + CMP_PREFIX_STEP - 1) // CMP_PREFIX_STEP, len(sizes)) - 1
    acc_c, imp_t = lax.switch(case, [functools.partial(compressed, n) for n in sizes])

    span = WINDOW + half
    acc_w = []
    for part in range(N_Q_PARTS):
        w_start = pl.multiple_of(jnp.maximum(q0 + part * half - WINDOW, 0), half)
        dist = t_row[:, part * half:(part + 1) * half] - (w_start + lax.broadcasted_iota(jnp.int32, (span, 1), 0))
        s_w = jnp.dot(kw_ref[0, 0, pl.ds(w_start, span), :], part_cols(qr_t, part), preferred_element_type=F32)
        e_w = exp2_cols(masked(s_w, (dist >= 0) & (dist < WINDOW), -jnp.inf))
        acc_w.append(jnp.dot(vwt_ref[0, 0, :, pl.ds(w_start, span)], e_w.astype(BF16),
                             preferred_element_type=F32))
    acc_w = join_parts(acc_w)

    js = lax.broadcasted_iota(jnp.int32, (n_blk, qb), 0).astype(F32)
    cur = (t_row // SEL_BLOCK).astype(F32)
    forced = (js == 0.0) | (js == cur) | (js == cur - 1.0)
    val = jnp.where((js > cur) | forced, -jnp.inf, imp_t)
    for _ in range(N_SELECT - N_FORCED):
        best = jnp.max(val, axis=0, keepdims=True)
        first = jnp.min(jnp.where(val == best, js, float(n_blk)), axis=0, keepdims=True)
        val = jnp.where(js == first, -jnp.inf, val)
    chosen = val == -jnp.inf
    kt = SEL_KEY_TILE
    n_tiles = ks_ref.shape[2] // kt
    first_own = (q0 // SEL_BLOCK).astype(F32)

    def extended(bias):
        return jnp.concatenate([jnp.concatenate([bias.astype(BF16)] * hpg, axis=1), qr_t], axis=0)

    q_ext_t = extended(jnp.where(chosen & (js < first_own), 0.0, MASKED))
    q_own_t = extended(jnp.where(chosen & (js >= first_own) & (js <= cur), 0.0, MASKED))

    def scores(i, s_ref):
        start = pl.multiple_of(jnp.minimum(i, n_tiles - 1) * kt, kt)
        k_ext = jnp.concatenate([expt_ref[pl.ds(start, kt), :], ks_ref[0, 0, pl.ds(start, kt), :]], axis=1)
        s = jnp.dot(k_ext, q_ext_t, preferred_element_type=F32)
        s_ref[...] = s
        return jnp.max(s, axis=0, keepdims=True)

    def accumulate(s, m_tile, v_t, carry):
        m_run, acc = carry
        m_new = jnp.maximum(m_run, m_tile)
        p = jnp.exp2(s - m_new)
        acc = jnp.exp2(m_run - m_new) * acc + jnp.dot(v_t, p.astype(BF16), preferred_element_type=F32)
        return m_new, acc

    def consume(i, s_ref, m_tile, carry):
        start = pl.multiple_of(i * kt, kt)
        return accumulate(s_ref[...], m_tile, vst_ref[0, 0, :, pl.ds(start, kt)], carry)

    def pair_step(j, carry):
        m_run, acc, m_even = carry
        m_odd = scores(2 * j + 1, s_odd)
        m_run, acc = consume(2 * j, s_even, m_even, (m_run, acc))
        m_even = scores(2 * j + 2, s_even)
        m_run, acc = consume(2 * j + 1, s_odd, m_odd, (m_run, acc))
        return m_run, acc, m_even

    cols = hpg * qb
    init = (jnp.full((1, cols), MASKED, F32), jnp.zeros((VALUE_ROWS, cols), F32), scores(0, s_even))
    n_past = (q0 + kt - 1) // kt
    m_run, acc_s, m_even = lax.fori_loop(0, (n_past - 1) // 2, pair_step, init)
    stats[0:1, :] = m_run
    stats[1:2, :] = m_even
    acc_buf[...] = acc_s
    carried = lambda: (stats[0:1, :], acc_buf[...])

    def hand_over(carry):
        stats[0:1, :], acc_buf[...] = carry

    @pl.when(n_past % 2 == 1)
    def _():
        hand_over(consume(n_past - 1, s_even, stats[1:2, :], carried()))

    @pl.when((n_past % 2 == 0) & (n_past > 0))
    def _():
        m_odd = scores(n_past - 1, s_odd)
        carry = consume(n_past - 2, s_even, stats[1:2, :], carried())
        hand_over(consume(n_past - 1, s_odd, m_odd, carry))

    m_run, acc_s = stats[0:1, :], acc_buf[...]
    acc_parts = []
    for part in range(N_Q_PARTS):
        n_own = (part + 1) * half
        kpos = q0 + lax.broadcasted_iota(jnp.int32, (n_own, 1), 0)
        k_own = jnp.concatenate([expt_ref[pl.ds(q0, n_own), :], ks_ref[0, 0, pl.ds(q0, n_own), :]], axis=1)
        s_own = masked(jnp.dot(k_own, part_cols(q_own_t, part), preferred_element_type=F32),
                       kpos <= t_row[:, part * half:(part + 1) * half], MASKED)
        _, acc_part = accumulate(s_own, jnp.max(s_own, axis=0, keepdims=True), vst_ref[0, 0, :, pl.ds(q0, n_own)],
                                 (part_cols(m_run, part), part_cols(acc_s, part)))
        acc_parts.append(acc_part)
    acc_s = join_parts(acc_parts)

    gates_t = _sigmoid(gl_ref[...].T)
    outs = []
    for h in range(hpg):
        c = slice(h * qb, (h + 1) * qb)
        o_t = jnp.zeros((dh, qb), F32)
        for j, acc in enumerate((acc_c, acc_s, acc_w)):
            den = acc[dh:dh + 1, c]
            o_t = o_t + (gates_t[3 * h + j:3 * h + j + 1, :] / jnp.where(den > 0.0, den, 1.0)) * acc[0:dh, c]
        outs.append(o_t)
    o_ref[...] = jnp.concatenate(outs, axis=0).T.astype(o_ref.dtype)


def _nsa_attention(qn, qr, gl, kc, vo, ks, vst, kw, vwt, expand_t, batch, t_len):
    assert t_len % (2 * SEL_KEY_TILE) == 0, "key tiles of the selected branch are processed in pairs"
    m = qn.shape[0]
    n_g = NSA_KV_GROUPS
    gw = NSA_GROUP_WIDTH
    dh = NSA_HEAD_DIM
    nq = t_len // Q_BLOCK
    n_cmp = kc.shape[1]
    qspec = pl.BlockSpec((Q_BLOCK, gw), lambda b, g, i: (b * nq + i, g))
    keys = pl.BlockSpec((1, 1, t_len, dh), lambda b, g, i: (b, g, 0, 0))
    values_t = pl.BlockSpec((1, 1, VALUE_ROWS, t_len), lambda b, g, i: (b, g, 0, 0))
    const = lambda a: pl.BlockSpec(a.shape, lambda b, g, i: (0, 0))
    return pl.pallas_call(
        _nsa_kernel,
        grid=(batch, n_g, nq),
        in_specs=[qspec, qspec,
                  pl.BlockSpec((Q_BLOCK, LANES), lambda b, g, i: (b * nq + i, g)),
                  pl.BlockSpec((1, n_cmp, dh), lambda b, g, i: (b * n_g + g, 0, 0)),
                  pl.BlockSpec((1,) + vo.shape[1:], lambda b, g, i: (b * n_g + g, 0, 0)),
                  keys, values_t, keys, values_t, const(expand_t)],
        out_specs=qspec,
        out_shape=jax.ShapeDtypeStruct((m, n_g * gw), BF16),
        scratch_shapes=[pltpu.VMEM((SEL_KEY_TILE, NSA_HPG * Q_BLOCK), F32)] * 2
        + [pltpu.VMEM((SUBLANES, NSA_HPG * Q_BLOCK), F32), pltpu.VMEM((VALUE_ROWS, NSA_HPG * Q_BLOCK), F32)],
        compiler_params=_params("parallel", "parallel", "arbitrary"),
        name="nsa_attention",
    )(qn, qr, gl, kc, vo, ks, vst, kw, vwt, expand_t)


def _pad_cols(w, n):
    return jnp.pad(w, ((0, 0), (0, n - w.shape[1])))


def _even_layer(x, gain, w_in, qkv_conv, a_log, dt_bias, o_norm, sc_conv, w_out, batch, t_len, mlp, final):
    gw = GDN_WIDTH
    ab = _pad_cols(w_in[:, 4 * gw:4 * gw + 2 * GDN_HEADS], LANES)
    w_all = jnp.concatenate([w_in[:, :3 * gw], ab, w_in[:, 3 * gw:4 * gw], w_in[:, 4 * gw + 2 * GDN_HEADS:]],
                            axis=1).astype(BF16)
    proj, gates = _norm_matmul(x, gain, w_all, 3 * gw + LANES)
    gate_params = jnp.zeros((SUBLANES, LANES), F32)
    gate_params = gate_params.at[0, :GDN_HEADS].set(a_log).at[1, :GDN_HEADS].set(dt_bias)
    ya = _gdn(proj, gates, qkv_conv, gate_params, o_norm, batch, t_len)
    return _even_tail(x, ya, gates, sc_conv, w_out[:gw].astype(BF16), w_out[gw:].astype(BF16), t_len, mlp, final)


def _rope_tables(t_len):
    half = ROPE_DIM // 2
    inv_freq = ROPE_THETA ** (-jnp.arange(0, ROPE_DIM, 2, dtype=F32) / ROPE_DIM)
    ang = jnp.arange(t_len, dtype=F32)[:, None] * inv_freq[None, :]
    cos, sin = jnp.cos(ang), jnp.sin(ang)
    pad = NSA_HEAD_DIM - ROPE_DIM
    head = lambda a, b, fill: jnp.concatenate([a, b, jnp.full((t_len, pad), fill, F32)], axis=1)
    zeros = jnp.zeros((t_len, half), F32)
    reps = LANES // NSA_HEAD_DIM
    return (jnp.tile(head(cos, cos, 1.0), (1, reps)),
            jnp.tile(head(-sin, zeros, 0.0), (1, reps)),
            jnp.tile(head(zeros, sin, 0.0), (1, reps)))


def _odd_layer(x, gain, w_in, cmp_pos, k_w1, k_w2, v_w1, v_w2, w_out, batch, t_len, mlp, final):
    n_g, dh, hpg = NSA_KV_GROUPS, NSA_HEAD_DIM, NSA_HPG
    dq = NSA_HEADS * dh
    kv = NSA_KV_WIDTH
    gate_w = w_in[:, dq + 6 * kv:].reshape(-1, n_g, 3 * hpg)
    gate_w = jnp.pad(gate_w, ((0, 0), (0, 0), (0, LANES - 3 * hpg))).reshape(-1, n_g * LANES)
    w_all = jnp.concatenate([w_in[:, :dq + 6 * kv], gate_w], axis=1).astype(BF16)
    qn, qr, kc_in, vc_in, ks, vs_t, kw, vw_t, gl = _odd_proj(x, gain, w_all, _rope_tables(t_len), batch, t_len)

    n_chunk = t_len // CMP_STRIDE
    chunked = lambda a: a.reshape(batch * n_g, n_chunk, CMP_STRIDE * dh)

    pos = jnp.zeros((SUBLANES, CMP_BLOCK * dh), F32).at[0].set(cmp_pos.reshape(-1)).astype(BF16)
    vc_t, kc = _compress(chunked(vc_in), chunked(kc_in), pos, v_w1.astype(BF16),
                         v_w2.T.astype(BF16), k_w1.astype(BF16), k_w2.astype(BF16))

    def with_ones(a):
        n = a.shape[-1]
        return jnp.concatenate([a, jnp.ones((a.shape[0], 1, n), a.dtype),
                                jnp.zeros((a.shape[0], VALUE_ROWS - dh - 1, n), a.dtype)], axis=1)

    n_blk = _round_up(t_len // SEL_BLOCK, LANES)
    c_start = np.arange(n_chunk)[None, :] * CMP_STRIDE
    s_start = np.arange(n_blk)[:, None] * SEL_BLOCK
    overlap_t = jnp.asarray((c_start < s_start + SEL_BLOCK) & (c_start + CMP_BLOCK > s_start), BF16)
    expand_t = jnp.asarray((np.arange(t_len)[:, None] // SEL_BLOCK) == np.arange(n_blk)[None, :], BF16)
    vo = jnp.concatenate([with_ones(vc_t), jnp.broadcast_to(overlap_t, (batch * n_g,) + overlap_t.shape)], axis=1)
    o = _nsa_attention(qn, qr, gl, kc, vo, ks, vs_t, kw, vw_t, expand_t, batch, t_len)
    return _odd_tail(x, o, w_out.astype(BF16), mlp, final)


def kernel(x, mix_norm, mlp_norm, w_up, w_down, final_norm, ev_w_in, ev_qkv_conv, ev_a_log, ev_dt_bias,
           ev_o_norm, ev_sc_conv, ev_w_out, od_w_in, od_cmp_pos, od_cmp_k_w1, od_cmp_k_w2, od_cmp_v_w1,
           od_cmp_v_w2, od_w_out):
    batch, t_len, d = x.shape
    depth = mix_norm.shape[0]
    xs = x.reshape(batch * t_len, d)
    for layer in range(depth):
        i = layer // 2
        mlp = (mlp_norm[layer], w_up[layer].astype(BF16), w_down[layer].astype(BF16), final_norm)
        final = layer == depth - 1
        if layer % 2 == 0:
            xs = _even_layer(xs, mix_norm[layer], ev_w_in[i], ev_qkv_conv[i], ev_a_log[i], ev_dt_bias[i],
                             ev_o_norm[i], ev_sc_conv[i], ev_w_out[i], batch, t_len, mlp, final)
        else:
            xs = _odd_layer(xs, mix_norm[layer], od_w_in[i], od_cmp_pos[i], od_cmp_k_w1[i], od_cmp_k_w2[i],
                            od_cmp_v_w1[i], od_cmp_v_w2[i], od_w_out[i], batch, t_len, mlp, final)
    return xs.reshape(batch, t_len, d)
```

```python
import functools

import jax
import jax.numpy as jnp
import numpy as np
from jax import lax
from jax.experimental import pallas as pl
from jax.experimental.pallas import tpu as pltpu

F32 = jnp.float32
BF16 = jnp.bfloat16

NORM_EPS = 1e-6
GDN_HEADS = 8
GDN_HEAD_DIM = 64
GDN_WIDTH = GDN_HEADS * GDN_HEAD_DIM
GDN_CONV = 4
GDN_CHUNK = 64
SOLVE_BLOCK = 16
SC_CONV = 3
NSA_HEADS = 16
NSA_HEAD_DIM = 64
NSA_KV_GROUPS = 4
NSA_HPG = NSA_HEADS // NSA_KV_GROUPS
NSA_GROUP_WIDTH = NSA_HPG * NSA_HEAD_DIM
NSA_KV_WIDTH = NSA_KV_GROUPS * NSA_HEAD_DIM
CMP_BLOCK = 32
CMP_STRIDE = 16
SEL_BLOCK = 64
N_SELECT = 16
N_FORCED = 3
WINDOW = 512
Q_BLOCK = 512
CMP_PREFIX_STEP = 128
N_Q_PARTS = 4
SEL_KEY_TILE = 512
VALUE_ROWS = NSA_HEAD_DIM + 16
ROPE_THETA = 500000.0
ROPE_DIM = NSA_HEAD_DIM // 4
LOG2_E = 1.4426950408889634
LANES = 128
SUBLANES = 8
BF16_ROWS = 16
VMEM_LIMIT = 56 * 1024 * 1024


def _round_up(n, m):
    return (n + m - 1) // m * m


def _params(*semantics):
    return pltpu.CompilerParams(dimension_semantics=semantics, vmem_limit_bytes=VMEM_LIMIT)


def _rms(x, gain):
    return x * lax.rsqrt(jnp.mean(x * x, axis=-1, keepdims=True) + NORM_EPS) * gain


def _sigmoid(x):
    return 1.0 / (1.0 + jnp.exp(-x))


def _silu(x):
    return x * _sigmoid(x)


def _softplus(x):
    return jnp.maximum(x, 0.0) + jnp.log(1.0 + jnp.exp(-jnp.abs(x)))


def _dot(a, b):
    return jnp.dot(a.astype(BF16), b.astype(BF16), preferred_element_type=F32)


def _dot_nt(a, b):
    return lax.dot_general(a.astype(BF16), b.astype(BF16), (((1,), (1,)), ((), ())),
                           preferred_element_type=F32)


def _norm_matmul_kernel(x_ref, g_ref, w_ref, o_ref, ob_ref):
    h = _rms(x_ref[...], g_ref[...])
    y = jnp.dot(h.astype(BF16), w_ref[...], preferred_element_type=F32)
    n = o_ref.shape[1]
    o_ref[...] = y[:, :n]
    ob_ref[...] = y[:, n:].astype(BF16)


def _norm_matmul(x, gain, w, n_f32, tm=1024):
    m, d = x.shape
    n = w.shape[1]
    return pl.pallas_call(
        _norm_matmul_kernel,
        grid=(m // tm,),
        in_specs=[pl.BlockSpec((tm, d), lambda i: (i, 0)),
                  pl.BlockSpec((1, d), lambda i: (0, 0)),
                  pl.BlockSpec((d, n), lambda i: (0, 0))],
        out_specs=[pl.BlockSpec((tm, n_f32), lambda i: (i, 0)), pl.BlockSpec((tm, n - n_f32), lambda i: (i, 0))],
        out_shape=[jax.ShapeDtypeStruct((m, n_f32), F32), jax.ShapeDtypeStruct((m, n - n_f32), BF16)],
        compiler_params=_params("parallel"),
        name="norm_proj",
    )(x, gain.reshape(1, d), w)


def _split(a):
    hi = a.astype(BF16)
    return hi, (a - hi.astype(F32)).astype(BF16)


def _dot3(a, b):
    (ah, al), (bh, bl) = a, b
    m = ah.shape[0]
    with_bh = jnp.dot(jnp.concatenate([ah, al], axis=0), bh, preferred_element_type=F32)
    return with_bh[0:m] + with_bh[m:2 * m] + jnp.dot(ah, bl, preferred_element_type=F32)


def _gdn_kernel(qkv_ref, z_ref, ab_ref, cw_ref, gp_ref, on_ref, o_ref, xbuf, act, state):
    c = GDN_CHUNK
    dh = GDN_HEAD_DIM
    pw = 2 * dh
    n_pairs = GDN_HEADS // 2
    n_batch = qkv_ref.shape[0]
    hist = SUBLANES
    chains = [(b, p) for b in range(n_batch) for p in range(n_pairs)]

    @pl.when(pl.program_id(0) == 0)
    def _():
        xbuf[:, 0:hist, :] = jnp.zeros((n_batch, hist, 3 * GDN_WIDTH), F32)
        state[...] = jnp.zeros(state.shape, F32)

    row = lax.broadcasted_iota(jnp.int32, (c, pw), 0)
    lane = lax.broadcasted_iota(jnp.int32, (c, pw), 1)
    first = lane < dh
    col = jnp.where(first, lane, lane - dh)
    incl = row >= col
    strict = row > col
    eye = (row == col).astype(F32)
    diag_blk = strict & ((row // SOLVE_BLOCK) == (col // SOLVE_BLOCK))
    r2 = lax.broadcasted_iota(jnp.int32, (pw, pw), 0)
    c2 = lax.broadcasted_iota(jnp.int32, (pw, pw), 1)
    same_head = (r2 < dh) == (c2 < dh)
    ones_bd = same_head.astype(BF16)
    tri = (lax.broadcasted_iota(jnp.int32, (c, c), 0) >= lax.broadcasted_iota(jnp.int32, (c, c), 1)).astype(BF16)

    def blockdiag(x):
        zero = jnp.zeros_like(x)
        return jnp.concatenate([jnp.where(first, x, zero), jnp.where(first, zero, x)], axis=0)

    def head_sums(x):
        return jnp.dot(x.astype(BF16), ones_bd, preferred_element_type=F32)

    def pdot(xs, y):
        hi, lo = _split(y)
        return _dot3(xs, (blockdiag(hi), blockdiag(lo)))

    gcs, betas = [], []
    for b in range(n_batch):
        x = qkv_ref[b]
        xbuf[b, hist:hist + c, :] = x
        y = x * cw_ref[GDN_CONV - 1:GDN_CONV, :]
        for j in range(GDN_CONV - 1):
            shift = GDN_CONV - 1 - j
            y = y + xbuf[b, hist - shift:hist - shift + c, :] * cw_ref[j:j + 1, :]
        xbuf[b, 0:hist, :] = x[c - hist:c, :]
        act[b] = _silu(y)
        ab = ab_ref[b]
        g_all = -jnp.exp(gp_ref[0:1, :]) * _softplus(ab + gp_ref[1:2, :])
        g_hi, g_rest = _split(g_all)
        g_mid = (g_all - g_hi.astype(F32) - g_rest.astype(F32)).astype(BF16)
        cum = jnp.dot(tri, jnp.concatenate([g_hi, g_rest, g_mid], axis=1), preferred_element_type=F32)
        gcs.append(cum[:, 0:LANES] + cum[:, LANES:2 * LANES] + cum[:, 2 * LANES:3 * LANES])
        betas.append(_sigmoid(ab))

    def pair_cols(a, p, offset):
        return jnp.where(first, a[:, offset + 2 * p:offset + 2 * p + 1], a[:, offset + 2 * p + 1:offset + 2 * p + 2])

    scale = dh ** -0.5
    q = [act[b, :, p * pw:(p + 1) * pw] for b, p in chains]
    k = [act[b, :, GDN_WIDTH + p * pw:GDN_WIDTH + (p + 1) * pw] for b, p in chains]
    v = [act[b, :, 2 * GDN_WIDTH + p * pw:2 * GDN_WIDTH + (p + 1) * pw] for b, p in chains]
    ssq = [head_sums(jnp.concatenate([q_ * q_, k_ * k_], axis=0)) for q_, k_ in zip(q, k)]
    q = [x * lax.rsqrt(s_[0:c] + NORM_EPS) * scale for x, s_ in zip(q, ssq)]
    k = [x * lax.rsqrt(s_[c:2 * c] + NORM_EPS) for x, s_ in zip(k, ssq)]
    gcol = [pair_cols(gcs[b], p, 0) for b, p in chains]
    bcol = [pair_cols(betas[b], p, GDN_HEADS) for b, p in chains]
    grow = [jnp.sum(g * eye, axis=0, keepdims=True) for g in gcol]
    g_last = [g[c - 1:c, :] for g in gcol]
    decay = [jnp.exp(jnp.where(incl, gc_ - gr_, -jnp.inf)) for gc_, gr_ in zip(gcol, grow)]
    kb = [k_ * b_ for k_, b_ in zip(k, bcol)]
    k_bd = [blockdiag(k_).astype(BF16) for k_ in k]
    both = [_dot_nt(jnp.concatenate([kb_, q_], axis=0), kd_) for kb_, q_, kd_ in zip(kb, q, k_bd)]
    m_low = [jnp.where(strict, x[0:c] * d_, 0.0) for x, d_ in zip(both, decay)]
    attn = [x[c:2 * c] * d_ for x, d_ in zip(both, decay)]

    d_pow = [jnp.where(diag_blk, m_, 0.0) for m_ in m_low]
    l_mat = [m_ - d_ for m_, d_ in zip(m_low, d_pow)]
    p_inv = [eye - d_ for d_ in d_pow]
    d_pow = [pdot(_split(d_), d_) for d_ in d_pow]
    n_steps = int(np.log2(SOLVE_BLOCK)) - 1
    for step in range(n_steps):
        if step == n_steps - 1:
            p_inv = [p_ + pdot(_split(p_), d_) for p_, d_ in zip(p_inv, d_pow)]
        else:
            prod = [pdot(_split(jnp.concatenate([p_, d_], axis=0)), d_) for p_, d_ in zip(p_inv, d_pow)]
            p_inv = [p_ + x[0:c] for p_, x in zip(p_inv, prod)]
            d_pow = [x[c:2 * c] for x in prod]
    n_pow = [pdot(_split(p_), l_) for p_, l_ in zip(p_inv, l_mat)]
    q_inv = [eye - n_ for n_ in n_pow]
    for _ in range(int(np.log2(c // SOLVE_BLOCK)) - 1):
        n_pow = [pdot(_split(n_), n_) for n_ in n_pow]
        q_inv = [q_ + pdot(_split(q_), n_) for q_, n_ in zip(q_inv, n_pow)]
    a_inv = [pdot(_split(q_), p_) for q_, p_ in zip(q_inv, p_inv)]
    uw = [_dot(a_, jnp.concatenate([blockdiag(v_ * b_), blockdiag(kb_ * jnp.exp(g_))], axis=1))
          for a_, v_, b_, kb_, g_ in zip(a_inv, v, bcol, kb, gcol)]
    u = [x[:, 0:pw] for x in uw]
    w = [x[:, pw:2 * pw] for x in uw]

    s_old = [state[b * n_pairs + p] for b, p in chains]
    on_s = [_dot(jnp.concatenate([w_, q_ * jnp.exp(g_)], axis=0), s_) for w_, q_, g_, s_ in zip(w, q, gcol, s_old)]
    v_new = [u_ - x[0:c] for u_, x in zip(u, on_s)]
    o = [x[c:2 * c] + _dot(a_, blockdiag(vn_)) for x, a_, vn_ in zip(on_s, attn, v_new)]
    k_dec_t = [(k_ * jnp.exp(gl_ - g_)).T for k_, gl_, g_ in zip(k, g_last, gcol)]
    for (b, p), s_, gl_, kt_, vn_ in zip(chains, s_old, g_last, k_dec_t, v_new):
        state[b * n_pairs + p] = s_ * jnp.exp(gl_) + jnp.where(same_head, _dot(kt_, vn_), 0.0)

    inv_dh = 1.0 / dh
    for (b, p), o_ in zip(chains, o):
        zp = z_ref[b, :, p * pw:(p + 1) * pw].astype(F32)
        y = o_ * lax.rsqrt(head_sums(o_ * o_) * inv_dh + NORM_EPS) * on_ref[...]
        o_ref[b, :, p * pw:(p + 1) * pw] = (y * _silu(zp)).astype(o_ref.dtype)


def _gdn(proj, gates, conv_w, gate_params, o_norm, batch, t_len):
    c = GDN_CHUNK
    n_chunks = t_len // c
    w3 = 3 * GDN_WIDTH
    ab_col = w3 // LANES
    proj = proj.reshape(batch, t_len, proj.shape[-1])
    gates = gates.reshape(batch, t_len, gates.shape[-1])
    on_pair = jnp.tile(o_norm.reshape(1, GDN_HEAD_DIM), (1, 2))
    out = pl.pallas_call(
        _gdn_kernel,
        grid=(n_chunks,),
        in_specs=[pl.BlockSpec((batch, c, w3), lambda i: (0, i, 0)),
                  pl.BlockSpec((batch, c, GDN_WIDTH), lambda i: (0, i, 0)),
                  pl.BlockSpec((batch, c, LANES), lambda i: (0, i, ab_col)),
                  pl.BlockSpec((GDN_CONV, w3), lambda i: (0, 0)),
                  pl.BlockSpec((SUBLANES, LANES), lambda i: (0, 0)),
                  pl.BlockSpec((1, 2 * GDN_HEAD_DIM), lambda i: (0, 0))],
        out_specs=pl.BlockSpec((batch, c, GDN_WIDTH), lambda i: (0, i, 0)),
        out_shape=jax.ShapeDtypeStruct((batch, t_len, GDN_WIDTH), BF16),
        scratch_shapes=[pltpu.VMEM((batch, c + SUBLANES, w3), F32),
                        pltpu.VMEM((batch, c, w3), F32),
                        pltpu.VMEM((batch * GDN_HEADS // 2, 2 * GDN_HEAD_DIM, 2 * GDN_HEAD_DIM), F32)],
        compiler_params=_params("arbitrary"),
        name="gdn",
    )(proj, gates, proj, conv_w, gate_params, on_pair)
    return out.reshape(batch * t_len, GDN_WIDTH)


def _mlp_tail(final, x, g_ref, wu_ref, wd_ref, fg_ref, o_ref):
    u = jnp.dot(_rms(x, g_ref[...]).astype(BF16), wu_ref[...], preferred_element_type=F32)
    u = jnp.square(jnp.maximum(u, 0.0))
    y = x + jnp.dot(u.astype(BF16), wd_ref[...], preferred_element_type=F32)
    if final:
        y = _rms(y, fg_ref[...])
    o_ref[...] = y


def _even_tail_kernel(tiles_per_batch, final, x_ref, ya_ref, bg_ref, cg_ref, hs_ref, cgp_ref, hsp_ref,
                      cw_ref, wa_ref, wb_ref, g_ref, wu_ref, wd_ref, fg_ref, o_ref, ubuf):
    tm = x_ref.shape[0]
    hist = cgp_ref.shape[0]
    first = (pl.program_id(0) % tiles_per_batch) == 0
    prev = cgp_ref[...].astype(F32) * hsp_ref[...].astype(F32)
    ubuf[0:hist, :] = jnp.where(first, 0.0, prev)
    u = cg_ref[...].astype(F32) * hs_ref[...].astype(F32)
    ubuf[hist:hist + tm, :] = u
    conv = u * cw_ref[SC_CONV - 1:SC_CONV, :]
    for j in range(SC_CONV - 1):
        shift = SC_CONV - 1 - j
        conv = conv + ubuf[hist - shift:hist - shift + tm, :] * cw_ref[j:j + 1, :]
    yb = bg_ref[...].astype(F32) * conv
    mix = jnp.dot(ya_ref[...], wa_ref[...], preferred_element_type=F32)
    mix = mix + jnp.dot(yb.astype(BF16), wb_ref[...], preferred_element_type=F32)
    _mlp_tail(final, x_ref[...] + mix, g_ref, wu_ref, wd_ref, fg_ref, o_ref)


def _odd_tail_kernel(final, x_ref, a_ref, w_ref, g_ref, wu_ref, wd_ref, fg_ref, o_ref):
    mix = jnp.dot(a_ref[...], w_ref[...], preferred_element_type=F32)
    _mlp_tail(final, x_ref[...] + mix, g_ref, wu_ref, wd_ref, fg_ref, o_ref)


def _resident(a):
    return pl.BlockSpec(a.shape, lambda i: (0, 0), pipeline_mode=pl.Buffered(1))


def _layer_tail(body, x, mixer_args, mixer_specs, mlp, final, scratch=(), tm=512):
    gain, w_up, w_down, final_gain = mlp
    m, d = x.shape
    row = pl.BlockSpec((tm, d), lambda i: (i, 0))
    vec = pl.BlockSpec((1, d), lambda i: (0, 0))
    return pl.pallas_call(
        functools.partial(body, final),
        grid=(m // tm,),
        in_specs=[row, *mixer_specs, vec, _resident(w_up), _resident(w_down), vec],
        out_specs=row,
        out_shape=jax.ShapeDtypeStruct((m, d), F32),
        scratch_shapes=list(scratch),
        compiler_params=_params("parallel"),
        name="layer_tail",
    )(x, *mixer_args, gain.reshape(1, d), w_up, w_down, final_gain.reshape(1, d))


def _even_tail(x, ya, gates, sc_conv, w_a, w_b, t_len, mlp, final, tm=512):
    wd = GDN_WIDTH
    hist = BF16_ROWS
    hb = tm // hist
    prev = lambda col: pl.BlockSpec((hist, wd), lambda i: (jnp.maximum(i * hb - 1, 0), col))
    cur = lambda col: pl.BlockSpec((tm, wd), lambda i: (i, col))
    specs = [cur(0), cur(1), cur(2), cur(3), prev(2), prev(3), _resident(sc_conv), _resident(w_a), _resident(w_b)]
    return _layer_tail(functools.partial(_even_tail_kernel, t_len // tm), x,
                       (ya, gates, gates, gates, gates, gates, sc_conv, w_a, w_b), specs, mlp, final,
                       scratch=[pltpu.VMEM((tm + hist, wd), F32)], tm=tm)


def _odd_tail(x, a, w, mlp, final, tm=512):
    specs = [pl.BlockSpec((tm, a.shape[1]), lambda i: (i, 0)), _resident(w)]
    return _layer_tail(_odd_tail_kernel, x, (a, w), specs, mlp, final, tm=tm)


def _rope(x, cos_t, sin_lo, sin_hi):
    half = ROPE_DIM // 2
    outs = []
    for j in range(x.shape[1] // LANES):
        xs = x[:, j * LANES:(j + 1) * LANES]
        up = pltpu.roll(xs, LANES - half, axis=1)
        down = pltpu.roll(xs, half, axis=1)
        outs.append(xs * cos_t + up * sin_lo + down * sin_hi)
    return jnp.concatenate(outs, axis=-1)


def _odd_proj_kernel(x_ref, g_ref, w_ref, cos_ref, slo_ref, shi_ref,
                     qn_ref, qr_ref, kc_ref, vc_ref, ks_ref, vs_ref, kw_ref, vw_ref, gl_ref):
    h = _rms(x_ref[...], g_ref[...])
    y = jnp.dot(h.astype(BF16), w_ref[...], preferred_element_type=F32)
    cos_t, sin_lo, sin_hi = cos_ref[...], slo_ref[...], shi_ref[...]
    dq = NSA_HEADS * NSA_HEAD_DIM
    kv = NSA_KV_WIDTH
    q = y[:, :dq] * (LOG2_E * NSA_HEAD_DIM ** -0.5)
    qn_ref[...] = q.astype(BF16)
    qr_ref[...] = _rope(q, cos_t, sin_lo, sin_hi).astype(BF16)
    dh = NSA_HEAD_DIM
    tm = y.shape[0]
    for g in range(NSA_KV_GROUPS):
        kc_ref[0, g] = y[:, dq + g * dh:dq + (g + 1) * dh].astype(BF16)
        vc_ref[0, g] = y[:, dq + kv + g * dh:dq + kv + (g + 1) * dh].astype(BF16)
    ones_rows = (lax.broadcasted_iota(jnp.int32, (VALUE_ROWS - dh, tm), 0) == 0).astype(BF16)
    for k_ref, v_ref, col in ((ks_ref, vs_ref, dq + 2 * kv), (kw_ref, vw_ref, dq + 4 * kv)):
        k = _rope(y[:, col:col + kv], cos_t, sin_lo, sin_hi).astype(BF16)
        v_t = y[:, col + kv:col + 2 * kv].T
        for g in range(NSA_KV_GROUPS):
            k_ref[0, g] = k[:, g * dh:(g + 1) * dh]
            v_ref[0, g, 0:dh, :] = v_t[g * dh:(g + 1) * dh].astype(BF16)
            v_ref[0, g, dh:VALUE_ROWS, :] = ones_rows
    gl_ref[...] = y[:, dq + 6 * kv:]


def _odd_proj(x, gain, w, tables, batch, t_len, tm=1024):
    m, d = x.shape
    n = w.shape[1]
    n_g, dh = NSA_KV_GROUPS, NSA_HEAD_DIM
    dq = NSA_HEADS * dh
    kv = NSA_KV_WIDTH
    ng = n - dq - 6 * kv
    tpb = t_len // tm
    row = lambda width: pl.BlockSpec((tm, width), lambda i: (i, 0))
    tab = pl.BlockSpec((tm, LANES), lambda i: (i % tpb, 0))
    keys = (pl.BlockSpec((1, n_g, tm, dh), lambda i: (i // tpb, 0, i % tpb, 0)),
            jax.ShapeDtypeStruct((batch, n_g, t_len, dh), BF16))
    values_t = (pl.BlockSpec((1, n_g, VALUE_ROWS, tm), lambda i: (i // tpb, 0, 0, i % tpb)),
                jax.ShapeDtypeStruct((batch, n_g, VALUE_ROWS, t_len), BF16))
    flat = lambda width, dt: (row(width), jax.ShapeDtypeStruct((m, width), dt))
    outs = [flat(dq, BF16), flat(dq, BF16), keys, keys, keys, values_t, keys, values_t, flat(ng, F32)]
    return pl.pallas_call(
        _odd_proj_kernel,
        grid=(m // tm,),
        in_specs=[row(d), pl.BlockSpec((1, d), lambda i: (0, 0)),
                  pl.BlockSpec((d, n), lambda i: (0, 0)), tab, tab, tab],
        out_specs=[spec for spec, _ in outs],
        out_shape=[shape for _, shape in outs],
        compiler_params=_params("parallel"),
        name="odd_proj",
    )(x, gain.reshape(1, d), w, *tables)


def _compress_kernel(xa_ref, xb_ref, pos_ref, a1_ref, a2t_ref, b1_ref, b2_ref, at_ref, b_ref):
    half = xa_ref.shape[2]
    n = xa_ref.shape[1]

    def hidden(x_ref, w1_ref):
        x = x_ref[0]
        top = jnp.dot(x, w1_ref[0:half, :], preferred_element_type=F32)
        bot = jnp.dot(x, w1_ref[half:2 * half, :], preferred_element_type=F32)
        bias = (jnp.dot(pos_ref[:, 0:half], w1_ref[0:half, :], preferred_element_type=F32)
                + jnp.dot(pos_ref[:, half:2 * half], w1_ref[half:2 * half, :], preferred_element_type=F32))
        nxt = pltpu.roll(bot, n - 1, axis=0)
        return _silu(top + nxt + bias[0:1, :]).astype(BF16)

    at_ref[0] = lax.dot_general(a2t_ref[...], hidden(xa_ref, a1_ref), (((1,), (1,)), ((), ())),
                                preferred_element_type=F32).astype(BF16)
    b_ref[0] = jnp.dot(hidden(xb_ref, b1_ref), b2_ref[...], preferred_element_type=F32).astype(BF16)


def _compress(xa, xb, pos, a1, a2t, b1, b2):
    bg, n, half = xa.shape
    dh = NSA_HEAD_DIM
    full = lambda a: pl.BlockSpec(a.shape, lambda i: (0,) * a.ndim)
    return pl.pallas_call(
        _compress_kernel,
        grid=(bg,),
        in_specs=[pl.BlockSpec((1, n, half), lambda i: (i, 0, 0)),
                  pl.BlockSpec((1, n, half), lambda i: (i, 0, 0)),
                  full(pos), full(a1), full(a2t), full(b1), full(b2)],
        out_specs=[pl.BlockSpec((1, dh, n), lambda i: (i, 0, 0)),
                   pl.BlockSpec((1, n, dh), lambda i: (i, 0, 0))],
        out_shape=[jax.ShapeDtypeStruct((bg, dh, n), BF16), jax.ShapeDtypeStruct((bg, n, dh), BF16)],
        compiler_params=_params("parallel"),
        name="compress",
    )(xa, xb, pos, a1, a2t, b1, b2)


MASKED = -1e30


def _nsa_kernel(qn_ref, qr_ref, gl_ref, kc_ref, vo_ref, ks_ref, vst_ref, kw_ref, vwt_ref, expt_ref,
                o_ref, s_even, s_odd, stats, acc_buf):
    qb = Q_BLOCK
    dh = NSA_HEAD_DIM
    hpg = NSA_HPG
    n_cmp = kc_ref.shape[1]
    n_blk = expt_ref.shape[1]
    q0 = pl.program_id(2) * qb
    t_row = q0 + lax.broadcasted_iota(jnp.int32, (1, qb), 1)

    def heads_t(ref):
        xt = ref[...].astype(F32).T
        return jnp.concatenate([xt[h * dh:(h + 1) * dh] for h in range(hpg)], axis=1).astype(BF16)

    qn_t = heads_t(qn_ref)
    qr_t = heads_t(qr_ref)

    def masked(s, mask, fill):
        w = mask.shape[1]
        return jnp.concatenate([jnp.where(mask, s[:, h * w:(h + 1) * w], fill) for h in range(hpg)], axis=1)

    half = qb // N_Q_PARTS

    def part_cols(x, part):
        return jnp.concatenate([x[:, h * qb + part * half:h * qb + (part + 1) * half] for h in range(hpg)], axis=1)

    def join_parts(parts):
        return jnp.concatenate([p[:, h * half:(h + 1) * half] for h in range(hpg) for p in parts], axis=1)

    def exp2_cols(s):
        m = jnp.max(s, axis=0, keepdims=True)
        return jnp.exp2(s - jnp.where(m == -jnp.inf, 0.0, m))

    def compressed(n_vis):
        cmp_end = lax.broadcasted_iota(jnp.int32, (n_vis, 1), 0) * CMP_STRIDE + (CMP_BLOCK - 1)
        s_c = jnp.dot(kc_ref[0, 0:n_vis, :], qn_t, preferred_element_type=F32)
        e_c = exp2_cols(masked(s_c, cmp_end <= t_row, -jnp.inf)).astype(BF16)
        both = jnp.dot(vo_ref[0, :, 0:n_vis], e_c, preferred_element_type=F32)
        acc = both[0:VALUE_ROWS]
        den = acc[dh:dh + 1, :]
        imp_all = both[VALUE_ROWS:VALUE_ROWS + n_blk] * (1.0 / jnp.where(den > 0.0, den, 1.0))
        imp = imp_all[:, 0:qb]
        for h in range(1, hpg):
            imp = imp + imp_all[:, h * qb:(h + 1) * qb]
        return acc, imp

    sizes = sorted({min(n_cmp, n) for n in range(CMP_PREFIX_STEP, n_cmp + CMP_PREFIX_STEP, CMP_PREFIX_STEP)})
    n_vis_max = (q0 + qb - CMP_BLOCK) // CMP_STRIDE + 1
    case = jnp.minimum((n_vis_max + CMP_PREFIX_STEP - 1) // CMP_PREFIX_STEP, len(sizes)) - 1
    acc_c, imp_t = lax.switch(case, [functools.partial(compressed, n) for n in sizes])

    span = WINDOW + half
    acc_w = []
    for part in range(N_Q_PARTS):
        w_start = pl.multiple_of(jnp.maximum(q0 + part * half - WINDOW, 0), half)
        dist = t_row[:, part * half:(part + 1) * half] - (w_start + lax.broadcasted_iota(jnp.int32, (span, 1), 0))
        s_w = jnp.dot(kw_ref[0, 0, pl.ds(w_start, span), :], part_cols(qr_t, part), preferred_element_type=F32)
        e_w = exp2_cols(masked(s_w, (dist >= 0) & (dist < WINDOW), -jnp.inf))
        acc_w.append(jnp.dot(vwt_ref[0, 0, :, pl.ds(w_start, span)], e_w.astype(BF16),
                             preferred_element_type=F32))
    acc_w = join_parts(acc_w)

    js = lax.broadcasted_iota(jnp.int32, (n_blk, qb), 0).astype(F32)
    cur = (t_row // SEL_BLOCK).astype(F32)
    forced = (js == 0.0) | (js == cur) | (js == cur - 1.0)
    val = jnp.where((js > cur) | forced, -jnp.inf, imp_t)
    for _ in range(N_SELECT - N_FORCED):
        best = jnp.max(val, axis=0, keepdims=True)
        first = jnp.min(jnp.where(val == best, js, float(n_blk)), axis=0, keepdims=True)
        val = jnp.where(js == first, -jnp.inf, val)
    chosen = val == -jnp.inf
    kt = SEL_KEY_TILE
    n_tiles = ks_ref.shape[2] // kt
    first_own = (q0 // SEL_BLOCK).astype(F32)

    def extended(bias):
        return jnp.concatenate([jnp.concatenate([bias.astype(BF16)] * hpg, axis=1), qr_t], axis=0)

    q_ext_t = extended(jnp.where(chosen & (js < first_own), 0.0, MASKED))
    q_own_t = extended(jnp.where(chosen & (js >= first_own) & (js <= cur), 0.0, MASKED))

    def scores(i, s_ref):
        start = pl.multiple_of(jnp.minimum(i, n_tiles - 1) * kt, kt)
        k_ext = jnp.concatenate([expt_ref[pl.ds(start, kt), :], ks_ref[0, 0, pl.ds(start, kt), :]], axis=1)
        s = jnp.dot(k_ext, q_ext_t, preferred_element_type=F32)
        s_ref[...] = s
        return jnp.max(s, axis=0, keepdims=True)

    def accumulate(s, m_tile, v_t, carry):
        m_run, acc = carry
        m_new = jnp.maximum(m_run, m_tile)
        p = jnp.exp2(s - m_new)
        acc = jnp.exp2(m_run - m_new) * acc + jnp.dot(v_t, p.astype(BF16), preferred_element_type=F32)
        return m_new, acc

    def consume(i, s_ref, m_tile, carry):
        start = pl.multiple_of(i * kt, kt)
        return accumulate(s_ref[...], m_tile, vst_ref[0, 0, :, pl.ds(start, kt)], carry)

    def pair_step(j, carry):
        m_run, acc, m_even = carry
        m_odd = scores(2 * j + 1, s_odd)
        m_run, acc = consume(2 * j, s_even, m_even, (m_run, acc))
        m_even = scores(2 * j + 2, s_even)
        m_run, acc = consume(2 * j + 1, s_odd, m_odd, (m_run, acc))
        return m_run, acc, m_even

    cols = hpg * qb
    init = (jnp.full((1, cols), MASKED, F32), jnp.zeros((VALUE_ROWS, cols), F32), scores(0, s_even))
    n_past = (q0 + kt - 1) // kt
    m_run, acc_s, m_even = lax.fori_loop(0, (n_past - 1) // 2, pair_step, init)
    stats[0:1, :] = m_run
    stats[1:2, :] = m_even
    acc_buf[...] = acc_s
    carried = lambda: (stats[0:1, :], acc_buf[...])

    def hand_over(carry):
        stats[0:1, :], acc_buf[...] = carry

    @pl.when(n_past % 2 == 1)
    def _():
        hand_over(consume(n_past - 1, s_even, stats[1:2, :], carried()))

    @pl.when((n_past % 2 == 0) & (n_past > 0))
    def _():
        m_odd = scores(n_past - 1, s_odd)
        carry = consume(n_past - 2, s_even, stats[1:2, :], carried())
        hand_over(consume(n_past - 1, s_odd, m_odd, carry))

    m_run, acc_s = stats[0:1, :], acc_buf[...]
    acc_parts = []
    for part in range(N_Q_PARTS):
        n_own = (part + 1) * half
        kpos = q0 + lax.broadcasted_iota(jnp.int32, (n_own, 1), 0)
        k_own = jnp.concatenate([expt_ref[pl.ds(q0, n_own), :], ks_ref[0, 0, pl.ds(q0, n_own), :]], axis=1)
        s_own = masked(jnp.dot(k_own, part_cols(q_own_t, part), preferred_element_type=F32),
                       kpos <= t_row[:, part * half:(part + 1) * half], MASKED)
        _, acc_part = accumulate(s_own, jnp.max(s_own, axis=0, keepdims=True), vst_ref[0, 0, :, pl.ds(q0, n_own)],
                                 (part_cols(m_run, part), part_cols(acc_s, part)))
        acc_parts.append(acc_part)
    acc_s = join_parts(acc_parts)

    gates_t = _sigmoid(gl_ref[...].T)
    outs = []
    for h in range(hpg):
        c = slice(h * qb, (h + 1) * qb)
        o_t = jnp.zeros((dh, qb), F32)
        for j, acc in enumerate((acc_c, acc_s, acc_w)):
            den = acc[dh:dh + 1, c]
            o_t = o_t + (gates_t[3 * h + j:3 * h + j + 1, :] / jnp.where(den > 0.0, den, 1.0)) * acc[0:dh, c]
        outs.append(o_t)
    o_ref[...] = jnp.concatenate(outs, axis=0).T.astype(o_ref.dtype)


def _nsa_attention(qn, qr, gl, kc, vo, ks, vst, kw, vwt, expand_t, batch, t_len):
    assert t_len % (2 * SEL_KEY_TILE) == 0, "key tiles of the selected branch are processed in pairs"
    m = qn.shape[0]
    n_g = NSA_KV_GROUPS
    gw = NSA_GROUP_WIDTH
    dh = NSA_HEAD_DIM
    nq = t_len // Q_BLOCK
    n_cmp = kc.shape[1]
    qspec = pl.BlockSpec((Q_BLOCK, gw), lambda b, g, i: (b * nq + i, g))
    keys = pl.BlockSpec((1, 1, t_len, dh), lambda b, g, i: (b, g, 0, 0))
    values_t = pl.BlockSpec((1, 1, VALUE_ROWS, t_len), lambda b, g, i: (b, g, 0, 0))
    const = lambda a: pl.BlockSpec(a.shape, lambda b, g, i: (0, 0))
    return pl.pallas_call(
        _nsa_kernel,
        grid=(batch, n_g, nq),
        in_specs=[qspec, qspec,
                  pl.BlockSpec((Q_BLOCK, LANES), lambda b, g, i: (b * nq + i, g)),
                  pl.BlockSpec((1, n_cmp, dh), lambda b, g, i: (b * n_g + g, 0, 0)),
                  pl.BlockSpec((1,) + vo.shape[1:], lambda b, g, i: (b * n_g + g, 0, 0)),
                  keys, values_t, keys, values_t, const(expand_t)],
        out_specs=qspec,
        out_shape=jax.ShapeDtypeStruct((m, n_g * gw), BF16),
        scratch_shapes=[pltpu.VMEM((SEL_KEY_TILE, NSA_HPG * Q_BLOCK), F32)] * 2
        + [pltpu.VMEM((SUBLANES, NSA_HPG * Q_BLOCK), F32), pltpu.VMEM((VALUE_ROWS, NSA_HPG * Q_BLOCK), F32)],
        compiler_params=_params("parallel", "parallel", "arbitrary"),
        name="nsa_attention",
    )(qn, qr, gl, kc, vo, ks, vst, kw, vwt, expand_t)


def _pad_cols(w, n):
    return jnp.pad(w, ((0, 0), (0, n - w.shape[1])))


def _even_layer(x, gain, w_in, qkv_conv, a_log, dt_bias, o_norm, sc_conv, w_out, batch, t_len, mlp, final):
    gw = GDN_WIDTH
    ab = _pad_cols(w_in[:, 4 * gw:4 * gw + 2 * GDN_HEADS], LANES)
    w_all = jnp.concatenate([w_in[:, :3 * gw], ab, w_in[:, 3 * gw:4 * gw], w_in[:, 4 * gw + 2 * GDN_HEADS:]],
                            axis=1).astype(BF16)
    proj, gates = _norm_matmul(x, gain, w_all, 3 * gw + LANES)
    gate_params = jnp.zeros((SUBLANES, LANES), F32)
    gate_params = gate_params.at[0, :GDN_HEADS].set(a_log).at[1, :GDN_HEADS].set(dt_bias)
    ya = _gdn(proj, gates, qkv_conv, gate_params, o_norm, batch, t_len)
    return _even_tail(x, ya, gates, sc_conv, w_out[:gw].astype(BF16), w_out[gw:].astype(BF16), t_len, mlp, final)


def _rope_tables(t_len):
    half = ROPE_DIM // 2
    inv_freq = ROPE_THETA ** (-jnp.arange(0, ROPE_DIM, 2, dtype=F32) / ROPE_DIM)
    ang = jnp.arange(t_len, dtype=F32)[:, None] * inv_freq[None, :]
    cos, sin = jnp.cos(ang), jnp.sin(ang)
    pad = NSA_HEAD_DIM - ROPE_DIM
    head = lambda a, b, fill: jnp.concatenate([a, b, jnp.full((t_len, pad), fill, F32)], axis=1)
    zeros = jnp.zeros((t_len, half), F32)
    reps = LANES // NSA_HEAD_DIM
    return (jnp.tile(head(cos, cos, 1.0), (1, reps)),
            jnp.tile(head(-sin, zeros, 0.0), (1, reps)),
            jnp.tile(head(zeros, sin, 0.0), (1, reps)))


def _odd_layer(x, gain, w_in, cmp_pos, k_w1, k_w2, v_w1, v_w2, w_out, batch, t_len, mlp, final):
    n_g, dh, hpg = NSA_KV_GROUPS, NSA_HEAD_DIM, NSA_HPG
    dq = NSA_HEADS * dh
    kv = NSA_KV_WIDTH
    gate_w = w_in[:, dq + 6 * kv:].reshape(-1, n_g, 3 * hpg)
    gate_w = jnp.pad(gate_w, ((0, 0), (0, 0), (0, LANES - 3 * hpg))).reshape(-1, n_g * LANES)
    w_all = jnp.concatenate([w_in[:, :dq + 6 * kv], gate_w], axis=1).astype(BF16)
    qn, qr, kc_in, vc_in, ks, vs_t, kw, vw_t, gl = _odd_proj(x, gain, w_all, _rope_tables(t_len), batch, t_len)

    n_chunk = t_len // CMP_STRIDE
    chunked = lambda a: a.reshape(batch * n_g, n_chunk, CMP_STRIDE * dh)

    pos = jnp.zeros((SUBLANES, CMP_BLOCK * dh), F32).at[0].set(cmp_pos.reshape(-1)).astype(BF16)
    vc_t, kc = _compress(chunked(vc_in), chunked(kc_in), pos, v_w1.astype(BF16),
                         v_w2.T.astype(BF16), k_w1.astype(BF16), k_w2.astype(BF16))

    def with_ones(a):
        n = a.shape[-1]
        return jnp.concatenate([a, jnp.ones((a.shape[0], 1, n), a.dtype),
                                jnp.zeros((a.shape[0], VALUE_ROWS - dh - 1, n), a.dtype)], axis=1)

    n_blk = _round_up(t_len // SEL_BLOCK, LANES)
    c_start = np.arange(n_chunk)[None, :] * CMP_STRIDE
    s_start = np.arange(n_blk)[:, None] * SEL_BLOCK
    overlap_t = jnp.asarray((c_start < s_start + SEL_BLOCK) & (c_start + CMP_BLOCK > s_start), BF16)
    expand_t = jnp.asarray((np.arange(t_len)[:, None] // SEL_BLOCK) == np.arange(n_blk)[None, :], BF16)
    vo = jnp.concatenate([with_ones(vc_t), jnp.broadcast_to(overlap_t, (batch * n_g,) + overlap_t.shape)], axis=1)
    o = _nsa_attention(qn, qr, gl, kc, vo, ks, vs_t, kw, vw_t, expand_t, batch, t_len)
    return _odd_tail(x, o, w_out.astype(BF16), mlp, final)


def kernel(x, mix_norm, mlp_norm, w_up, w_down, final_norm, ev_w_in, ev_qkv_conv, ev_a_log, ev_dt_bias,
           ev_o_norm, ev_sc_conv, ev_w_out, od_w_in, od_cmp_pos, od_cmp_k_w1, od_cmp_k_w2, od_cmp_v_w1,
           od_cmp_v_w2, od_w_out):
    batch, t_len, d = x.shape
    depth = mix_norm.shape[0]
    xs = x.reshape(batch * t_len, d)
    for layer in range(depth):
        i = layer // 2
        mlp = (mlp_norm[layer], w_up[layer].astype(BF16), w_down[layer].astype(BF16), final_norm)
        final = layer == depth - 1
        if layer % 2 == 0:
            xs = _even_layer(xs, mix_norm[layer], ev_w_in[i], ev_qkv_conv[i], ev_a_log[i], ev_dt_bias[i],
                             ev_o_norm[i], ev_sc_conv[i], ev_w_out[i], batch, t_len, mlp, final)
        else:
            xs = _odd_layer(xs, mix_norm[layer], od_w_in[i], od_cmp_pos[i], od_cmp_k_w1[i], od_cmp_k_w2[i],
                            od_cmp_v_w1[i], od_cmp_v_w2[i], od_w_out[i], batch, t_len, mlp, final)
    return xs.reshape(batch, t_len, d)
```

```python
import functools

import jax
import jax.numpy as jnp
import numpy as np
from jax import lax
from jax.experimental import pallas as pl
from jax.experimental.pallas import tpu as pltpu

F32 = jnp.float32
BF16 = jnp.bfloat16

NORM_EPS = 1e-6
GDN_HEADS = 8
GDN_HEAD_DIM = 64
GDN_WIDTH = GDN_HEADS * GDN_HEAD_DIM
GDN_CONV = 4
GDN_CHUNK = 64
SOLVE_BLOCK = 16
SC_CONV = 3
NSA_HEADS = 16
NSA_HEAD_DIM = 64
NSA_KV_GROUPS = 4
NSA_HPG = NSA_HEADS // NSA_KV_GROUPS
NSA_GROUP_WIDTH = NSA_HPG * NSA_HEAD_DIM
NSA_KV_WIDTH = NSA_KV_GROUPS * NSA_HEAD_DIM
CMP_BLOCK = 32
CMP_STRIDE = 16
SEL_BLOCK = 64
N_SELECT = 16
N_FORCED = 3
WINDOW = 512
Q_BLOCK = 512
CMP_PREFIX_STEP = 128
N_Q_PARTS = 2
SEL_KEY_TILE = 512
VALUE_ROWS = NSA_HEAD_DIM + 16
ROPE_THETA = 500000.0
ROPE_DIM = NSA_HEAD_DIM // 4
LOG2_E = 1.4426950408889634
LANES = 128
SUBLANES = 8
BF16_ROWS = 16
VMEM_LIMIT = 56 * 1024 * 1024


def _round_up(n, m):
    return (n + m - 1) // m * m


def _params(*semantics):
    return pltpu.CompilerParams(dimension_semantics=semantics, vmem_limit_bytes=VMEM_LIMIT)


def _rms(x, gain):
    return x * lax.rsqrt(jnp.mean(x * x, axis=-1, keepdims=True) + NORM_EPS) * gain


def _sigmoid(x):
    return 1.0 / (1.0 + jnp.exp(-x))


def _silu(x):
    return x * _sigmoid(x)


def _softplus(x):
    return jnp.maximum(x, 0.0) + jnp.log(1.0 + jnp.exp(-jnp.abs(x)))


def _dot(a, b):
    return jnp.dot(a.astype(BF16), b.astype(BF16), preferred_element_type=F32)


def _dot_nt(a, b):
    return lax.dot_general(a.astype(BF16), b.astype(BF16), (((1,), (1,)), ((), ())),
                           preferred_element_type=F32)


def _norm_matmul_kernel(x_ref, g_ref, w_ref, o_ref, ob_ref):
    h = _rms(x_ref[...], g_ref[...])
    y = jnp.dot(h.astype(BF16), w_ref[...], preferred_element_type=F32)
    n = o_ref.shape[1]
    o_ref[...] = y[:, :n]
    ob_ref[...] = y[:, n:].astype(BF16)


def _norm_matmul(x, gain, w, n_f32, tm=1024):
    m, d = x.shape
    n = w.shape[1]
    return pl.pallas_call(
        _norm_matmul_kernel,
        grid=(m // tm,),
        in_specs=[pl.BlockSpec((tm, d), lambda i: (i, 0)),
                  pl.BlockSpec((1, d), lambda i: (0, 0)),
                  pl.BlockSpec((d, n), lambda i: (0, 0))],
        out_specs=[pl.BlockSpec((tm, n_f32), lambda i: (i, 0)), pl.BlockSpec((tm, n - n_f32), lambda i: (i, 0))],
        out_shape=[jax.ShapeDtypeStruct((m, n_f32), F32), jax.ShapeDtypeStruct((m, n - n_f32), BF16)],
        compiler_params=_params("parallel"),
        name="norm_proj",
    )(x, gain.reshape(1, d), w)


def _split(a):
    hi = a.astype(BF16)
    return hi, (a - hi.astype(F32)).astype(BF16)


def _dot3(a, b):
    (ah, al), (bh, bl) = a, b
    m = ah.shape[0]
    with_bh = jnp.dot(jnp.concatenate([ah, al], axis=0), bh, preferred_element_type=F32)
    return with_bh[0:m] + with_bh[m:2 * m] + jnp.dot(ah, bl, preferred_element_type=F32)


def _gdn_kernel(qkv_ref, z_ref, ab_ref, cw_ref, gp_ref, on_ref, o_ref, xbuf, act, state):
    c = GDN_CHUNK
    dh = GDN_HEAD_DIM
    pw = 2 * dh
    n_pairs = GDN_HEADS // 2
    n_batch = qkv_ref.shape[0]
    hist = SUBLANES
    chains = [(b, p) for b in range(n_batch) for p in range(n_pairs)]

    @pl.when(pl.program_id(0) == 0)
    def _():
        xbuf[:, 0:hist, :] = jnp.zeros((n_batch, hist, 3 * GDN_WIDTH), F32)
        state[...] = jnp.zeros(state.shape, F32)

    row = lax.broadcasted_iota(jnp.int32, (c, pw), 0)
    lane = lax.broadcasted_iota(jnp.int32, (c, pw), 1)
    first = lane < dh
    col = jnp.where(first, lane, lane - dh)
    incl = row >= col
    strict = row > col
    eye = (row == col).astype(F32)
    diag_blk = strict & ((row // SOLVE_BLOCK) == (col // SOLVE_BLOCK))
    r2 = lax.broadcasted_iota(jnp.int32, (pw, pw), 0)
    c2 = lax.broadcasted_iota(jnp.int32, (pw, pw), 1)
    same_head = (r2 < dh) == (c2 < dh)
    ones_bd = same_head.astype(BF16)
    tri = (lax.broadcasted_iota(jnp.int32, (c, c), 0) >= lax.broadcasted_iota(jnp.int32, (c, c), 1)).astype(BF16)

    def blockdiag(x):
        zero = jnp.zeros_like(x)
        return jnp.concatenate([jnp.where(first, x, zero), jnp.where(first, zero, x)], axis=0)

    def head_sums(x):
        return jnp.dot(x.astype(BF16), ones_bd, preferred_element_type=F32)

    def pdot(xs, y):
        hi, lo = _split(y)
        return _dot3(xs, (blockdiag(hi), blockdiag(lo)))

    gcs, betas = [], []
    for b in range(n_batch):
        x = qkv_ref[b]
        xbuf[b, hist:hist + c, :] = x
        y = x * cw_ref[GDN_CONV - 1:GDN_CONV, :]
        for j in range(GDN_CONV - 1):
            shift = GDN_CONV - 1 - j
            y = y + xbuf[b, hist - shift:hist - shift + c, :] * cw_ref[j:j + 1, :]
        xbuf[b, 0:hist, :] = x[c - hist:c, :]
        act[b] = _silu(y)
        ab = ab_ref[b]
        g_all = -jnp.exp(gp_ref[0:1, :]) * _softplus(ab + gp_ref[1:2, :])
        g_hi, g_rest = _split(g_all)
        g_mid = (g_all - g_hi.astype(F32) - g_rest.astype(F32)).astype(BF16)
        cum = jnp.dot(tri, jnp.concatenate([g_hi, g_rest, g_mid], axis=1), preferred_element_type=F32)
        gcs.append(cum[:, 0:LANES] + cum[:, LANES:2 * LANES] + cum[:, 2 * LANES:3 * LANES])
        betas.append(_sigmoid(ab))

    def pair_cols(a, p, offset):
        return jnp.where(first, a[:, offset + 2 * p:offset + 2 * p + 1], a[:, offset + 2 * p + 1:offset + 2 * p + 2])

    scale = dh ** -0.5
    q = [act[b, :, p * pw:(p + 1) * pw] for b, p in chains]
    k = [act[b, :, GDN_WIDTH + p * pw:GDN_WIDTH + (p + 1) * pw] for b, p in chains]
    v = [act[b, :, 2 * GDN_WIDTH + p * pw:2 * GDN_WIDTH + (p + 1) * pw] for b, p in chains]
    ssq = [head_sums(jnp.concatenate([q_ * q_, k_ * k_], axis=0)) for q_, k_ in zip(q, k)]
    q = [x * lax.rsqrt(s_[0:c] + NORM_EPS) * scale for x, s_ in zip(q, ssq)]
    k = [x * lax.rsqrt(s_[c:2 * c] + NORM_EPS) for x, s_ in zip(k, ssq)]
    gcol = [pair_cols(gcs[b], p, 0) for b, p in chains]
    bcol = [pair_cols(betas[b], p, GDN_HEADS) for b, p in chains]
    grow = [jnp.sum(g * eye, axis=0, keepdims=True) for g in gcol]
    g_last = [g[c - 1:c, :] for g in gcol]
    decay = [jnp.exp(jnp.where(incl, gc_ - gr_, -jnp.inf)) for gc_, gr_ in zip(gcol, grow)]
    kb = [k_ * b_ for k_, b_ in zip(k, bcol)]
    k_bd = [blockdiag(k_).astype(BF16) for k_ in k]
    both = [_dot_nt(jnp.concatenate([kb_, q_], axis=0), kd_) for kb_, q_, kd_ in zip(kb, q, k_bd)]
    m_low = [jnp.where(strict, x[0:c] * d_, 0.0) for x, d_ in zip(both, decay)]
    attn = [x[c:2 * c] * d_ for x, d_ in zip(both, decay)]

    d_pow = [jnp.where(diag_blk, m_, 0.0) for m_ in m_low]
    l_mat = [m_ - d_ for m_, d_ in zip(m_low, d_pow)]
    p_inv = [eye - d_ for d_ in d_pow]
    d_pow = [pdot(_split(d_), d_) for d_ in d_pow]
    n_steps = int(np.log2(SOLVE_BLOCK)) - 1
    for step in range(n_steps):
        if step == n_steps - 1:
            p_inv = [p_ + pdot(_split(p_), d_) for p_, d_ in zip(p_inv, d_pow)]
        else:
            prod = [pdot(_split(jnp.concatenate([p_, d_], axis=0)), d_) for p_, d_ in zip(p_inv, d_pow)]
            p_inv = [p_ + x[0:c] for p_, x in zip(p_inv, prod)]
            d_pow = [x[c:2 * c] for x in prod]
    n_pow = [pdot(_split(p_), l_) for p_, l_ in zip(p_inv, l_mat)]
    q_inv = [eye - n_ for n_ in n_pow]
    for _ in range(int(np.log2(c // SOLVE_BLOCK)) - 1):
        n_pow = [pdot(_split(n_), n_) for n_ in n_pow]
        q_inv = [q_ + pdot(_split(q_), n_) for q_, n_ in zip(q_inv, n_pow)]
    a_inv = [pdot(_split(q_), p_) for q_, p_ in zip(q_inv, p_inv)]
    uw = [_dot(a_, jnp.concatenate([blockdiag(v_ * b_), blockdiag(kb_ * jnp.exp(g_))], axis=1))
          for a_, v_, b_, kb_, g_ in zip(a_inv, v, bcol, kb, gcol)]
    u = [x[:, 0:pw] for x in uw]
    w = [x[:, pw:2 * pw] for x in uw]

    s_old = [state[b * n_pairs + p] for b, p in chains]
    on_s = [_dot(jnp.concatenate([w_, q_ * jnp.exp(g_)], axis=0), s_) for w_, q_, g_, s_ in zip(w, q, gcol, s_old)]
    v_new = [u_ - x[0:c] for u_, x in zip(u, on_s)]
    o = [x[c:2 * c] + _dot(a_, blockdiag(vn_)) for x, a_, vn_ in zip(on_s, attn, v_new)]
    k_dec_t = [(k_ * jnp.exp(gl_ - g_)).T for k_, gl_, g_ in zip(k, g_last, gcol)]
    for (b, p), s_, gl_, kt_, vn_ in zip(chains, s_old, g_last, k_dec_t, v_new):
        state[b * n_pairs + p] = s_ * jnp.exp(gl_) + jnp.where(same_head, _dot(kt_, vn_), 0.0)

    inv_dh = 1.0 / dh
    for (b, p), o_ in zip(chains, o):
        zp = z_ref[b, :, p * pw:(p + 1) * pw].astype(F32)
        y = o_ * lax.rsqrt(head_sums(o_ * o_) * inv_dh + NORM_EPS) * on_ref[...]
        o_ref[b, :, p * pw:(p + 1) * pw] = (y * _silu(zp)).astype(o_ref.dtype)


def _gdn(proj, gates, conv_w, gate_params, o_norm, batch, t_len):
    c = GDN_CHUNK
    n_chunks = t_len // c
    w3 = 3 * GDN_WIDTH
    ab_col = w3 // LANES
    proj = proj.reshape(batch, t_len, proj.shape[-1])
    gates = gates.reshape(batch, t_len, gates.shape[-1])
    on_pair = jnp.tile(o_norm.reshape(1, GDN_HEAD_DIM), (1, 2))
    out = pl.pallas_call(
        _gdn_kernel,
        grid=(n_chunks,),
        in_specs=[pl.BlockSpec((batch, c, w3), lambda i: (0, i, 0)),
                  pl.BlockSpec((batch, c, GDN_WIDTH), lambda i: (0, i, 0)),
                  pl.BlockSpec((batch, c, LANES), lambda i: (0, i, ab_col)),
                  pl.BlockSpec((GDN_CONV, w3), lambda i: (0, 0)),
                  pl.BlockSpec((SUBLANES, LANES), lambda i: (0, 0)),
                  pl.BlockSpec((1, 2 * GDN_HEAD_DIM), lambda i: (0, 0))],
        out_specs=pl.BlockSpec((batch, c, GDN_WIDTH), lambda i: (0, i, 0)),
        out_shape=jax.ShapeDtypeStruct((batch, t_len, GDN_WIDTH), BF16),
        scratch_shapes=[pltpu.VMEM((batch, c + SUBLANES, w3), F32),
                        pltpu.VMEM((batch, c, w3), F32),
                        pltpu.VMEM((batch * GDN_HEADS // 2, 2 * GDN_HEAD_DIM, 2 * GDN_HEAD_DIM), F32)],
        compiler_params=_params("arbitrary"),
        name="gdn",
    )(proj, gates, proj, conv_w, gate_params, on_pair)
    return out.reshape(batch * t_len, GDN_WIDTH)


def _mlp_tail(final, x, g_ref, wu_ref, wd_ref, fg_ref, o_ref):
    u = jnp.dot(_rms(x, g_ref[...]).astype(BF16), wu_ref[...], preferred_element_type=F32)
    u = jnp.square(jnp.maximum(u, 0.0))
    y = x + jnp.dot(u.astype(BF16), wd_ref[...], preferred_element_type=F32)
    if final:
        y = _rms(y, fg_ref[...])
    o_ref[...] = y


def _even_tail_kernel(tiles_per_batch, final, x_ref, ya_ref, bg_ref, cg_ref, hs_ref, cgp_ref, hsp_ref,
                      cw_ref, wa_ref, wb_ref, g_ref, wu_ref, wd_ref, fg_ref, o_ref, ubuf):
    tm = x_ref.shape[0]
    hist = cgp_ref.shape[0]
    first = (pl.program_id(0) % tiles_per_batch) == 0
    prev = cgp_ref[...].astype(F32) * hsp_ref[...].astype(F32)
    ubuf[0:hist, :] = jnp.where(first, 0.0, prev)
    u = cg_ref[...].astype(F32) * hs_ref[...].astype(F32)
    ubuf[hist:hist + tm, :] = u
    conv = u * cw_ref[SC_CONV - 1:SC_CONV, :]
    for j in range(SC_CONV - 1):
        shift = SC_CONV - 1 - j
        conv = conv + ubuf[hist - shift:hist - shift + tm, :] * cw_ref[j:j + 1, :]
    yb = bg_ref[...].astype(F32) * conv
    mix = jnp.dot(ya_ref[...], wa_ref[...], preferred_element_type=F32)
    mix = mix + jnp.dot(yb.astype(BF16), wb_ref[...], preferred_element_type=F32)
    _mlp_tail(final, x_ref[...] + mix, g_ref, wu_ref, wd_ref, fg_ref, o_ref)


def _odd_tail_kernel(final, x_ref, a_ref, w_ref, g_ref, wu_ref, wd_ref, fg_ref, o_ref):
    mix = jnp.dot(a_ref[...], w_ref[...], preferred_element_type=F32)
    _mlp_tail(final, x_ref[...] + mix, g_ref, wu_ref, wd_ref, fg_ref, o_ref)


def _resident(a):
    return pl.BlockSpec(a.shape, lambda i: (0, 0), pipeline_mode=pl.Buffered(1))


def _layer_tail(body, x, mixer_args, mixer_specs, mlp, final, scratch=(), tm=512):
    gain, w_up, w_down, final_gain = mlp
    m, d = x.shape
    row = pl.BlockSpec((tm, d), lambda i: (i, 0))
    vec = pl.BlockSpec((1, d), lambda i: (0, 0))
    return pl.pallas_call(
        functools.partial(body, final),
        grid=(m // tm,),
        in_specs=[row, *mixer_specs, vec, _resident(w_up), _resident(w_down), vec],
        out_specs=row,
        out_shape=jax.ShapeDtypeStruct((m, d), F32),
        scratch_shapes=list(scratch),
        compiler_params=_params("parallel"),
        name="layer_tail",
    )(x, *mixer_args, gain.reshape(1, d), w_up, w_down, final_gain.reshape(1, d))


def _even_tail(x, ya, gates, sc_conv, w_a, w_b, t_len, mlp, final, tm=512):
    wd = GDN_WIDTH
    hist = BF16_ROWS
    hb = tm // hist
    prev = lambda col: pl.BlockSpec((hist, wd), lambda i: (jnp.maximum(i * hb - 1, 0), col))
    cur = lambda col: pl.BlockSpec((tm, wd), lambda i: (i, col))
    specs = [cur(0), cur(1), cur(2), cur(3), prev(2), prev(3), _resident(sc_conv), _resident(w_a), _resident(w_b)]
    return _layer_tail(functools.partial(_even_tail_kernel, t_len // tm), x,
                       (ya, gates, gates, gates, gates, gates, sc_conv, w_a, w_b), specs, mlp, final,
                       scratch=[pltpu.VMEM((tm + hist, wd), F32)], tm=tm)


def _odd_tail(x, a, w, mlp, final, tm=512):
    specs = [pl.BlockSpec((tm, a.shape[1]), lambda i: (i, 0)), _resident(w)]
    return _layer_tail(_odd_tail_kernel, x, (a, w), specs, mlp, final, tm=tm)


def _rope(x, cos_t, sin_lo, sin_hi):
    half = ROPE_DIM // 2
    outs = []
    for j in range(x.shape[1] // LANES):
        xs = x[:, j * LANES:(j + 1) * LANES]
        up = pltpu.roll(xs, LANES - half, axis=1)
        down = pltpu.roll(xs, half, axis=1)
        outs.append(xs * cos_t + up * sin_lo + down * sin_hi)
    return jnp.concatenate(outs, axis=-1)


def _odd_proj_kernel(x_ref, g_ref, w_ref, cos_ref, slo_ref, shi_ref,
                     qn_ref, qr_ref, kc_ref, vc_ref, ks_ref, vs_ref, kw_ref, vw_ref, gl_ref):
    h = _rms(x_ref[...], g_ref[...])
    y = jnp.dot(h.astype(BF16), w_ref[...], preferred_element_type=F32)
    cos_t, sin_lo, sin_hi = cos_ref[...], slo_ref[...], shi_ref[...]
    dq = NSA_HEADS * NSA_HEAD_DIM
    kv = NSA_KV_WIDTH
    q = y[:, :dq] * (LOG2_E * NSA_HEAD_DIM ** -0.5)
    qn_ref[...] = q.astype(BF16)
    qr_ref[...] = _rope(q, cos_t, sin_lo, sin_hi).astype(BF16)
    dh = NSA_HEAD_DIM
    tm = y.shape[0]
    for g in range(NSA_KV_GROUPS):
        kc_ref[0, g] = y[:, dq + g * dh:dq + (g + 1) * dh].astype(BF16)
        vc_ref[0, g] = y[:, dq + kv + g * dh:dq + kv + (g + 1) * dh].astype(BF16)
    ones_rows = (lax.broadcasted_iota(jnp.int32, (VALUE_ROWS - dh, tm), 0) == 0).astype(BF16)
    for k_ref, v_ref, col in ((ks_ref, vs_ref, dq + 2 * kv), (kw_ref, vw_ref, dq + 4 * kv)):
        k = _rope(y[:, col:col + kv], cos_t, sin_lo, sin_hi).astype(BF16)
        v_t = y[:, col + kv:col + 2 * kv].T
        for g in range(NSA_KV_GROUPS):
            k_ref[0, g] = k[:, g * dh:(g + 1) * dh]
            v_ref[0, g, 0:dh, :] = v_t[g * dh:(g + 1) * dh].astype(BF16)
            v_ref[0, g, dh:VALUE_ROWS, :] = ones_rows
    gl_ref[...] = y[:, dq + 6 * kv:]


def _odd_proj(x, gain, w, tables, batch, t_len, tm=1024):
    m, d = x.shape
    n = w.shape[1]
    n_g, dh = NSA_KV_GROUPS, NSA_HEAD_DIM
    dq = NSA_HEADS * dh
    kv = NSA_KV_WIDTH
    ng = n - dq - 6 * kv
    tpb = t_len // tm
    row = lambda width: pl.BlockSpec((tm, width), lambda i: (i, 0))
    tab = pl.BlockSpec((tm, LANES), lambda i: (i % tpb, 0))
    keys = (pl.BlockSpec((1, n_g, tm, dh), lambda i: (i // tpb, 0, i % tpb, 0)),
            jax.ShapeDtypeStruct((batch, n_g, t_len, dh), BF16))
    values_t = (pl.BlockSpec((1, n_g, VALUE_ROWS, tm), lambda i: (i // tpb, 0, 0, i % tpb)),
                jax.ShapeDtypeStruct((batch, n_g, VALUE_ROWS, t_len), BF16))
    flat = lambda width, dt: (row(width), jax.ShapeDtypeStruct((m, width), dt))
    outs = [flat(dq, BF16), flat(dq, BF16), keys, keys, keys, values_t, keys, values_t, flat(ng, F32)]
    return pl.pallas_call(
        _odd_proj_kernel,
        grid=(m // tm,),
        in_specs=[row(d), pl.BlockSpec((1, d), lambda i: (0, 0)),
                  pl.BlockSpec((d, n), lambda i: (0, 0)), tab, tab, tab],
        out_specs=[spec for spec, _ in outs],
        out_shape=[shape for _, shape in outs],
        compiler_params=_params("parallel"),
        name="odd_proj",
    )(x, gain.reshape(1, d), w, *tables)


def _compress_kernel(xa_ref, xb_ref, pos_ref, a1_ref, a2t_ref, b1_ref, b2_ref, at_ref, b_ref):
    half = xa_ref.shape[2]
    n = xa_ref.shape[1]

    def hidden(x_ref, w1_ref):
        x = x_ref[0]
        top = jnp.dot(x, w1_ref[0:half, :], preferred_element_type=F32)
        bot = jnp.dot(x, w1_ref[half:2 * half, :], preferred_element_type=F32)
        bias = (jnp.dot(pos_ref[:, 0:half], w1_ref[0:half, :], preferred_element_type=F32)
                + jnp.dot(pos_ref[:, half:2 * half], w1_ref[half:2 * half, :], preferred_element_type=F32))
        nxt = pltpu.roll(bot, n - 1, axis=0)
        return _silu(top + nxt + bias[0:1, :]).astype(BF16)

    at_ref[0] = lax.dot_general(a2t_ref[...], hidden(xa_ref, a1_ref), (((1,), (1,)), ((), ())),
                                preferred_element_type=F32).astype(BF16)
    b_ref[0] = jnp.dot(hidden(xb_ref, b1_ref), b2_ref[...], preferred_element_type=F32).astype(BF16)


def _compress(xa, xb, pos, a1, a2t, b1, b2):
    bg, n, half = xa.shape
    dh = NSA_HEAD_DIM
    full = lambda a: pl.BlockSpec(a.shape, lambda i: (0,) * a.ndim)
    return pl.pallas_call(
        _compress_kernel,
        grid=(bg,),
        in_specs=[pl.BlockSpec((1, n, half), lambda i: (i, 0, 0)),
                  pl.BlockSpec((1, n, half), lambda i: (i, 0, 0)),
                  full(pos), full(a1), full(a2t), full(b1), full(b2)],
        out_specs=[pl.BlockSpec((1, dh, n), lambda i: (i, 0, 0)),
                   pl.BlockSpec((1, n, dh), lambda i: (i, 0, 0))],
        out_shape=[jax.ShapeDtypeStruct((bg, dh, n), BF16), jax.ShapeDtypeStruct((bg, n, dh), BF16)],
        compiler_params=_params("parallel"),
        name="compress",
    )(xa, xb, pos, a1, a2t, b1, b2)


MASKED = -1e30


def _nsa_kernel(qn_ref, qr_ref, gl_ref, kc_ref, vo_ref, ks_ref, vst_ref, kw_ref, vwt_ref, expt_ref,
                o_ref, s_even, s_odd, stats, acc_buf):
    qb = Q_BLOCK
    dh = NSA_HEAD_DIM
    hpg = NSA_HPG
    n_cmp = kc_ref.shape[1]
    n_blk = expt_ref.shape[1]
    q0 = pl.program_id(2) * qb
    t_row = q0 + lax.broadcasted_iota(jnp.int32, (1, qb), 1)

    def heads_t(ref):
        xt = ref[...].astype(F32).T
        return jnp.concatenate([xt[h * dh:(h + 1) * dh] for h in range(hpg)], axis=1).astype(BF16)

    qn_t = heads_t(qn_ref)
    qr_t = heads_t(qr_ref)

    def masked(s, mask, fill):
        w = mask.shape[1]
        return jnp.concatenate([jnp.where(mask, s[:, h * w:(h + 1) * w], fill) for h in range(hpg)], axis=1)

    half = qb // N_Q_PARTS

    def part_cols(x, part):
        return jnp.concatenate([x[:, h * qb + part * half:h * qb + (part + 1) * half] for h in range(hpg)], axis=1)

    def join_parts(parts):
        return jnp.concatenate([p[:, h * half:(h + 1) * half] for h in range(hpg) for p in parts], axis=1)

    def exp2_cols(s):
        m = jnp.max(s, axis=0, keepdims=True)
        return jnp.exp2(s - jnp.where(m == -jnp.inf, 0.0, m))

    def compressed(n_vis):
        cmp_end = lax.broadcasted_iota(jnp.int32, (n_vis, 1), 0) * CMP_STRIDE + (CMP_BLOCK - 1)
        s_c = jnp.dot(kc_ref[0, 0:n_vis, :], qn_t, preferred_element_type=F32)
        e_c = exp2_cols(masked(s_c, cmp_end <= t_row, -jnp.inf)).astype(BF16)
        both = jnp.dot(vo_ref[0, :, 0:n_vis], e_c, preferred_element_type=F32)
        acc = both[0:VALUE_ROWS]
        den = acc[dh:dh + 1, :]
        imp_all = both[VALUE_ROWS:VALUE_ROWS + n_blk] * (1.0 / jnp.where(den > 0.0, den, 1.0))
        imp = imp_all[:, 0:qb]
        for h in range(1, hpg):
            imp = imp + imp_all[:, h * qb:(h + 1) * qb]
        return acc, imp

    sizes = sorted({min(n_cmp, n) for n in range(CMP_PREFIX_STEP, n_cmp + CMP_PREFIX_STEP, CMP_PREFIX_STEP)})
    n_vis_max = (q0 + qb - CMP_BLOCK) // CMP_STRIDE + 1
    case = jnp.minimum((n_vis_max + CMP_PREFIX_STEP - 1) // CMP_PREFIX_STEP, len(sizes)) - 1
    acc_c, imp_t = lax.switch(case, [functools.partial(compressed, n) for n in sizes])

    span = WINDOW + half
    acc_w = []
    for part in range(N_Q_PARTS):
        w_start = pl.multiple_of(jnp.maximum(q0 + part * half - WINDOW, 0), half)
        dist = t_row[:, part * half:(part + 1) * half] - (w_start + lax.broadcasted_iota(jnp.int32, (span, 1), 0))
        s_w = jnp.dot(kw_ref[0, 0, pl.ds(w_start, span), :], part_cols(qr_t, part), preferred_element_type=F32)
        e_w = exp2_cols(masked(s_w, (dist >= 0) & (dist < WINDOW), -jnp.inf))
        acc_w.append(jnp.dot(vwt_ref[0, 0, :, pl.ds(w_start, span)], e_w.astype(BF16),
                             preferred_element_type=F32))
    acc_w = join_parts(acc_w)

    js = lax.broadcasted_iota(jnp.int32, (n_blk, qb), 0).astype(F32)
    cur = (t_row // SEL_BLOCK).astype(F32)
    forced = (js == 0.0) | (js == cur) | (js == cur - 1.0)
    val = jnp.where((js > cur) | forced, -jnp.inf, imp_t)
    for _ in range(N_SELECT - N_FORCED):
        best = jnp.max(val, axis=0, keepdims=True)
        first = jnp.min(jnp.where(val == best, js, float(n_blk)), axis=0, keepdims=True)
        val = jnp.where(js == first, -jnp.inf, val)
    chosen = val == -jnp.inf
    kt = SEL_KEY_TILE
    n_tiles = ks_ref.shape[2] // kt
    first_own = (q0 // SEL_BLOCK).astype(F32)

    def extended(bias):
        return jnp.concatenate([jnp.concatenate([bias.astype(BF16)] * hpg, axis=1), qr_t], axis=0)

    q_ext_t = extended(jnp.where(chosen & (js < first_own), 0.0, MASKED))
    q_own_t = extended(jnp.where(chosen & (js >= first_own) & (js <= cur), 0.0, MASKED))

    def scores(i, s_ref):
        start = pl.multiple_of(jnp.minimum(i, n_tiles - 1) * kt, kt)
        k_ext = jnp.concatenate([expt_ref[pl.ds(start, kt), :], ks_ref[0, 0, pl.ds(start, kt), :]], axis=1)
        s = jnp.dot(k_ext, q_ext_t, preferred_element_type=F32)
        s_ref[...] = s
        return jnp.max(s, axis=0, keepdims=True)

    def accumulate(s, m_tile, v_t, carry):
        m_run, acc = carry
        m_new = jnp.maximum(m_run, m_tile)
        p = jnp.exp2(s - m_new)
        acc = jnp.exp2(m_run - m_new) * acc + jnp.dot(v_t, p.astype(BF16), preferred_element_type=F32)
        return m_new, acc

    def consume(i, s_ref, m_tile, carry):
        start = pl.multiple_of(i * kt, kt)
        return accumulate(s_ref[...], m_tile, vst_ref[0, 0, :, pl.ds(start, kt)], carry)

    def pair_step(j, carry):
        m_run, acc, m_even = carry
        m_odd = scores(2 * j + 1, s_odd)
        m_run, acc = consume(2 * j, s_even, m_even, (m_run, acc))
        m_even = scores(2 * j + 2, s_even)
        m_run, acc = consume(2 * j + 1, s_odd, m_odd, (m_run, acc))
        return m_run, acc, m_even

    cols = hpg * qb
    init = (jnp.full((1, cols), MASKED, F32), jnp.zeros((VALUE_ROWS, cols), F32), scores(0, s_even))
    n_past = (q0 + kt - 1) // kt
    m_run, acc_s, m_even = lax.fori_loop(0, (n_past - 1) // 2, pair_step, init)
    stats[0:1, :] = m_run
    stats[1:2, :] = m_even
    acc_buf[...] = acc_s
    carried = lambda: (stats[0:1, :], acc_buf[...])

    def hand_over(carry):
        stats[0:1, :], acc_buf[...] = carry

    @pl.when(n_past % 2 == 1)
    def _():
        hand_over(consume(n_past - 1, s_even, stats[1:2, :], carried()))

    @pl.when((n_past % 2 == 0) & (n_past > 0))
    def _():
        m_odd = scores(n_past - 1, s_odd)
        carry = consume(n_past - 2, s_even, stats[1:2, :], carried())
        hand_over(consume(n_past - 1, s_odd, m_odd, carry))

    m_run, acc_s = stats[0:1, :], acc_buf[...]
    acc_parts = []
    for part in range(N_Q_PARTS):
        n_own = (part + 1) * half
        kpos = q0 + lax.broadcasted_iota(jnp.int32, (n_own, 1), 0)
        k_own = jnp.concatenate([expt_ref[pl.ds(q0, n_own), :], ks_ref[0, 0, pl.ds(q0, n_own), :]], axis=1)
        s_own = masked(jnp.dot(k_own, part_cols(q_own_t, part), preferred_element_type=F32),
                       kpos <= t_row[:, part * half:(part + 1) * half], MASKED)
        _, acc_part = accumulate(s_own, jnp.max(s_own, axis=0, keepdims=True), vst_ref[0, 0, :, pl.ds(q0, n_own)],
                                 (part_cols(m_run, part), part_cols(acc_s, part)))
        acc_parts.append(acc_part)
    acc_s = join_parts(acc_parts)

    gates_t = _sigmoid(gl_ref[...].T)
    outs = []
    for h in range(hpg):
        c = slice(h * qb, (h + 1) * qb)
        o_t = jnp.zeros((dh, qb), F32)
        for j, acc in enumerate((acc_c, acc_s, acc_w)):
            den = acc[dh:dh + 1, c]
            o_t = o_t + (gates_t[3 * h + j:3 * h + j + 1, :] / jnp.where(den > 0.0, den, 1.0)) * acc[0:dh, c]
        outs.append(o_t)
    o_ref[...] = jnp.concatenate(outs, axis=0).T.astype(o_ref.dtype)


def _nsa_attention(qn, qr, gl, kc, vo, ks, vst, kw, vwt, expand_t, batch, t_len):
    assert t_len % (2 * SEL_KEY_TILE) == 0, "key tiles of the selected branch are processed in pairs"
    m = qn.shape[0]
    n_g = NSA_KV_GROUPS
    gw = NSA_GROUP_WIDTH
    dh = NSA_HEAD_DIM
    nq = t_len // Q_BLOCK
    n_cmp = kc.shape[1]
    qspec = pl.BlockSpec((Q_BLOCK, gw), lambda b, g, i: (b * nq + i, g))
    keys = pl.BlockSpec((1, 1, t_len, dh), lambda b, g, i: (b, g, 0, 0))
    values_t = pl.BlockSpec((1, 1, VALUE_ROWS, t_len), lambda b, g, i: (b, g, 0, 0))
    const = lambda a: pl.BlockSpec(a.shape, lambda b, g, i: (0, 0))
    return pl.pallas_call(
        _nsa_kernel,
        grid=(batch, n_g, nq),
        in_specs=[qspec, qspec,
                  pl.BlockSpec((Q_BLOCK, LANES), lambda b, g, i: (b * nq + i, g)),
                  pl.BlockSpec((1, n_cmp, dh), lambda b, g, i: (b * n_g + g, 0, 0)),
                  pl.BlockSpec((1,) + vo.shape[1:], lambda b, g, i: (b * n_g + g, 0, 0)),
                  keys, values_t, keys, values_t, const(expand_t)],
        out_specs=qspec,
        out_shape=jax.ShapeDtypeStruct((m, n_g * gw), BF16),
        scratch_shapes=[pltpu.VMEM((SEL_KEY_TILE, NSA_HPG * Q_BLOCK), F32)] * 2
        + [pltpu.VMEM((SUBLANES, NSA_HPG * Q_BLOCK), F32), pltpu.VMEM((VALUE_ROWS, NSA_HPG * Q_BLOCK), F32)],
        compiler_params=_params("parallel", "parallel", "arbitrary"),
        name="nsa_attention",
    )(qn, qr, gl, kc, vo, ks, vst, kw, vwt, expand_t)


def _pad_cols(w, n):
    return jnp.pad(w, ((0, 0), (0, n - w.shape[1])))


def _even_layer(x, gain, w_in, qkv_conv, a_log, dt_bias, o_norm, sc_conv, w_out, batch, t_len, mlp, final):
    gw = GDN_WIDTH
    ab = _pad_cols(w_in[:, 4 * gw:4 * gw + 2 * GDN_HEADS], LANES)
    w_all = jnp.concatenate([w_in[:, :3 * gw], ab, w_in[:, 3 * gw:4 * gw], w_in[:, 4 * gw + 2 * GDN_HEADS:]],
                            axis=1).astype(BF16)
    proj, gates = _norm_matmul(x, gain, w_all, 3 * gw + LANES)
    gate_params = jnp.zeros((SUBLANES, LANES), F32)
    gate_params = gate_params.at[0, :GDN_HEADS].set(a_log).at[1, :GDN_HEADS].set(dt_bias)
    ya = _gdn(proj, gates, qkv_conv, gate_params, o_norm, batch, t_len)
    return _even_tail(x, ya, gates, sc_conv, w_out[:gw].astype(BF16), w_out[gw:].astype(BF16), t_len, mlp, final)


def _rope_tables(t_len):
    half = ROPE_DIM // 2
    inv_freq = ROPE_THETA ** (-jnp.arange(0, ROPE_DIM, 2, dtype=F32) / ROPE_DIM)
    ang = jnp.arange(t_len, dtype=F32)[:, None] * inv_freq[None, :]
    cos, sin = jnp.cos(ang), jnp.sin(ang)
    pad = NSA_HEAD_DIM - ROPE_DIM
    head = lambda a, b, fill: jnp.concatenate([a, b, jnp.full((t_len, pad), fill, F32)], axis=1)
    zeros = jnp.zeros((t_len, half), F32)
    reps = LANES // NSA_HEAD_DIM
    return (jnp.tile(head(cos, cos, 1.0), (1, reps)),
            jnp.tile(head(-sin, zeros, 0.0), (1, reps)),
            jnp.tile(head(zeros, sin, 0.0), (1, reps)))


def _odd_layer(x, gain, w_in, cmp_pos, k_w1, k_w2, v_w1, v_w2, w_out, batch, t_len, mlp, final):
    n_g, dh, hpg = NSA_KV_GROUPS, NSA_HEAD_DIM, NSA_HPG
    dq = NSA_HEADS * dh
    kv = NSA_KV_WIDTH
    gate_w = w_in[:, dq + 6 * kv:].reshape(-1, n_g, 3 * hpg)
    gate_w = jnp.pad(gate_w, ((0, 0), (0, 0), (0, LANES - 3 * hpg))).reshape(-1, n_g * LANES)
    w_all = jnp.concatenate([w_in[:, :dq + 6 * kv], gate_w], axis=1).astype(BF16)
    qn, qr, kc_in, vc_in, ks, vs_t, kw, vw_t, gl = _odd_proj(x, gain, w_all, _rope_tables(t_len), batch, t_len)

    n_chunk = t_len // CMP_STRIDE
    chunked = lambda a: a.reshape(batch * n_g, n_chunk, CMP_STRIDE * dh)

    pos = jnp.zeros((SUBLANES, CMP_BLOCK * dh), F32).at[0].set(cmp_pos.reshape(-1)).astype(BF16)
    vc_t, kc = _compress(chunked(vc_in), chunked(kc_in), pos, v_w1.astype(BF16),
                         v_w2.T.astype(BF16), k_w1.astype(BF16), k_w2.astype(BF16))

    def with_ones(a):
        n = a.shape[-1]
        return jnp.concatenate([a, jnp.ones((a.shape[0], 1, n), a.dtype),
                                jnp.zeros((a.shape[0], VALUE_ROWS - dh - 1, n), a.dtype)], axis=1)

    n_blk = _round_up(t_len // SEL_BLOCK, LANES)
    c_start = np.arange(n_chunk)[None, :] * CMP_STRIDE
    s_start = np.arange(n_blk)[:, None] * SEL_BLOCK
    overlap_t = jnp.asarray((c_start < s_start + SEL_BLOCK) & (c_start + CMP_BLOCK > s_start), BF16)
    expand_t = jnp.asarray((np.arange(t_len)[:, None] // SEL_BLOCK) == np.arange(n_blk)[None, :], BF16)
    vo = jnp.concatenate([with_ones(vc_t), jnp.broadcast_to(overlap_t, (batch * n_g,) + overlap_t.shape)], axis=1)
    o = _nsa_attention(qn, qr, gl, kc, vo, ks, vs_t, kw, vw_t, expand_t, batch, t_len)
    return _odd_tail(x, o, w_out.astype(BF16), mlp, final)


def kernel(x, mix_norm, mlp_norm, w_up, w_down, final_norm, ev_w_in, ev_qkv_conv, ev_a_log, ev_dt_bias,
           ev_o_norm, ev_sc_conv, ev_w_out, od_w_in, od_cmp_pos, od_cmp_k_w1, od_cmp_k_w2, od_cmp_v_w1,
           od_cmp_v_w2, od_w_out):
    batch, t_len, d = x.shape
    depth = mix_norm.shape[0]
    xs = x.reshape(batch * t_len, d)
    for layer in range(depth):
        i = layer // 2
        mlp = (mlp_norm[layer], w_up[layer].astype(BF16), w_down[layer].astype(BF16), final_norm)
        final = layer == depth - 1
        if layer % 2 == 0:
            xs = _even_layer(xs, mix_norm[layer], ev_w_in[i], ev_qkv_conv[i], ev_a_log[i], ev_dt_bias[i],
                             ev_o_norm[i], ev_sc_conv[i], ev_w_out[i], batch, t_len, mlp, final)
        else:
            xs = _odd_layer(xs, mix_norm[layer], od_w_in[i], od_cmp_pos[i], od_cmp_k_w1[i], od_cmp_k_w2[i],
                            od_cmp_v_w1[i], od_cmp_v_w2[i], od_w_out[i], batch, t_len, mlp, final)
    return xs.reshape(batch, t_len, d)
```

```python
import functools

import jax
import jax.numpy as jnp
import numpy as np
from jax import lax
from jax.experimental import pallas as pl
from jax.experimental.pallas import tpu as pltpu

F32 = jnp.float32
BF16 = jnp.bfloat16

NORM_EPS = 1e-6
GDN_HEADS = 8
GDN_HEAD_DIM = 64
GDN_WIDTH = GDN_HEADS * GDN_HEAD_DIM
GDN_CONV = 4
GDN_CHUNK = 64
SOLVE_BLOCK = 16
SC_CONV = 3
NSA_HEADS = 16
NSA_HEAD_DIM = 64
NSA_KV_GROUPS = 4
NSA_HPG = NSA_HEADS // NSA_KV_GROUPS
NSA_GROUP_WIDTH = NSA_HPG * NSA_HEAD_DIM
NSA_KV_WIDTH = NSA_KV_GROUPS * NSA_HEAD_DIM
CMP_BLOCK = 32
CMP_STRIDE = 16
SEL_BLOCK = 64
N_SELECT = 16
N_FORCED = 3
WINDOW = 512
Q_BLOCK = 512
SEL_PREFIX_STEP = 32
CMP_PREFIX_STEP = 128
N_Q_PARTS = 2
SEL_KEY_TILE = 512
VALUE_ROWS = NSA_HEAD_DIM + 16
ROPE_THETA = 500000.0
ROPE_DIM = NSA_HEAD_DIM // 4
LOG2_E = 1.4426950408889634
LANES = 128
SUBLANES = 8
BF16_ROWS = 16
VMEM_LIMIT = 56 * 1024 * 1024


def _round_up(n, m):
    return (n + m - 1) // m * m


def _params(*semantics):
    return pltpu.CompilerParams(dimension_semantics=semantics, vmem_limit_bytes=VMEM_LIMIT)


def _rms(x, gain):
    return x * lax.rsqrt(jnp.mean(x * x, axis=-1, keepdims=True) + NORM_EPS) * gain


def _sigmoid(x):
    return 1.0 / (1.0 + jnp.exp(-x))


def _silu(x):
    return x * _sigmoid(x)


def _softplus(x):
    return jnp.maximum(x, 0.0) + jnp.log(1.0 + jnp.exp(-jnp.abs(x)))


def _dot(a, b):
    return jnp.dot(a.astype(BF16), b.astype(BF16), preferred_element_type=F32)


def _dot_nt(a, b):
    return lax.dot_general(a.astype(BF16), b.astype(BF16), (((1,), (1,)), ((), ())),
                           preferred_element_type=F32)


def _norm_matmul_kernel(x_ref, g_ref, w_ref, o_ref, ob_ref):
    h = _rms(x_ref[...], g_ref[...])
    y = jnp.dot(h.astype(BF16), w_ref[...], preferred_element_type=F32)
    n = o_ref.shape[1]
    o_ref[...] = y[:, :n]
    ob_ref[...] = y[:, n:].astype(BF16)


def _norm_matmul(x, gain, w, n_f32, tm=512):
    m, d = x.shape
    n = w.shape[1]
    return pl.pallas_call(
        _norm_matmul_kernel,
        grid=(m // tm,),
        in_specs=[pl.BlockSpec((tm, d), lambda i: (i, 0)),
                  pl.BlockSpec((1, d), lambda i: (0, 0)),
                  pl.BlockSpec((d, n), lambda i: (0, 0))],
        out_specs=[pl.BlockSpec((tm, n_f32), lambda i: (i, 0)), pl.BlockSpec((tm, n - n_f32), lambda i: (i, 0))],
        out_shape=[jax.ShapeDtypeStruct((m, n_f32), F32), jax.ShapeDtypeStruct((m, n - n_f32), BF16)],
        compiler_params=_params("parallel"),
        name="norm_proj",
    )(x, gain.reshape(1, d), w)


def _split(a):
    hi = a.astype(BF16)
    return hi, (a - hi.astype(F32)).astype(BF16)


def _dot3(a, b):
    (ah, al), (bh, bl) = a, b
    m = ah.shape[0]
    with_bh = jnp.dot(jnp.concatenate([ah, al], axis=0), bh, preferred_element_type=F32)
    return with_bh[0:m] + with_bh[m:2 * m] + jnp.dot(ah, bl, preferred_element_type=F32)


def _gdn_kernel(qkv_ref, z_ref, ab_ref, cw_ref, gp_ref, on_ref, o_ref, xbuf, act, state):
    c = GDN_CHUNK
    dh = GDN_HEAD_DIM
    pw = 2 * dh
    n_pairs = GDN_HEADS // 2
    n_batch = qkv_ref.shape[0]
    hist = SUBLANES
    chains = [(b, p) for b in range(n_batch) for p in range(n_pairs)]

    @pl.when(pl.program_id(0) == 0)
    def _():
        xbuf[:, 0:hist, :] = jnp.zeros((n_batch, hist, 3 * GDN_WIDTH), F32)
        state[...] = jnp.zeros(state.shape, F32)

    row = lax.broadcasted_iota(jnp.int32, (c, pw), 0)
    lane = lax.broadcasted_iota(jnp.int32, (c, pw), 1)
    first = lane < dh
    col = jnp.where(first, lane, lane - dh)
    incl = row >= col
    strict = row > col
    eye = (row == col).astype(F32)
    diag_blk = strict & ((row // SOLVE_BLOCK) == (col // SOLVE_BLOCK))
    r2 = lax.broadcasted_iota(jnp.int32, (pw, pw), 0)
    c2 = lax.broadcasted_iota(jnp.int32, (pw, pw), 1)
    same_head = (r2 < dh) == (c2 < dh)
    ones_bd = same_head.astype(BF16)
    tri = (lax.broadcasted_iota(jnp.int32, (c, c), 0) >= lax.broadcasted_iota(jnp.int32, (c, c), 1)).astype(BF16)

    def blockdiag(x):
        zero = jnp.zeros_like(x)
        return jnp.concatenate([jnp.where(first, x, zero), jnp.where(first, zero, x)], axis=0)

    def head_sums(x):
        return jnp.dot(x.astype(BF16), ones_bd, preferred_element_type=F32)

    def pdot(xs, y):
        hi, lo = _split(y)
        return _dot3(xs, (blockdiag(hi), blockdiag(lo)))

    gcs, betas = [], []
    for b in range(n_batch):
        x = qkv_ref[b]
        xbuf[b, hist:hist + c, :] = x
        y = x * cw_ref[GDN_CONV - 1:GDN_CONV, :]
        for j in range(GDN_CONV - 1):
            shift = GDN_CONV - 1 - j
            y = y + xbuf[b, hist - shift:hist - shift + c, :] * cw_ref[j:j + 1, :]
        xbuf[b, 0:hist, :] = x[c - hist:c, :]
        act[b] = _silu(y)
        ab = ab_ref[b]
        g_all = -jnp.exp(gp_ref[0:1, :]) * _softplus(ab + gp_ref[1:2, :])
        g_hi, g_rest = _split(g_all)
        g_mid = (g_all - g_hi.astype(F32) - g_rest.astype(F32)).astype(BF16)
        cum = jnp.dot(tri, jnp.concatenate([g_hi, g_rest, g_mid], axis=1), preferred_element_type=F32)
        gcs.append(cum[:, 0:LANES] + cum[:, LANES:2 * LANES] + cum[:, 2 * LANES:3 * LANES])
        betas.append(_sigmoid(ab))

    def pair_cols(a, p, offset):
        return jnp.where(first, a[:, offset + 2 * p:offset + 2 * p + 1], a[:, offset + 2 * p + 1:offset + 2 * p + 2])

    scale = dh ** -0.5
    q = [act[b, :, p * pw:(p + 1) * pw] for b, p in chains]
    k = [act[b, :, GDN_WIDTH + p * pw:GDN_WIDTH + (p + 1) * pw] for b, p in chains]
    v = [act[b, :, 2 * GDN_WIDTH + p * pw:2 * GDN_WIDTH + (p + 1) * pw] for b, p in chains]
    ssq = [head_sums(jnp.concatenate([q_ * q_, k_ * k_], axis=0)) for q_, k_ in zip(q, k)]
    q = [x * lax.rsqrt(s_[0:c] + NORM_EPS) * scale for x, s_ in zip(q, ssq)]
    k = [x * lax.rsqrt(s_[c:2 * c] + NORM_EPS) for x, s_ in zip(k, ssq)]
    gcol = [pair_cols(gcs[b], p, 0) for b, p in chains]
    bcol = [pair_cols(betas[b], p, GDN_HEADS) for b, p in chains]
    grow = [jnp.sum(g * eye, axis=0, keepdims=True) for g in gcol]
    g_last = [g[c - 1:c, :] for g in gcol]
    decay = [jnp.exp(jnp.where(incl, gc_ - gr_, -jnp.inf)) for gc_, gr_ in zip(gcol, grow)]
    kb = [k_ * b_ for k_, b_ in zip(k, bcol)]
    k_bd = [blockdiag(k_).astype(BF16) for k_ in k]
    both = [_dot_nt(jnp.concatenate([kb_, q_], axis=0), kd_) for kb_, q_, kd_ in zip(kb, q, k_bd)]
    m_low = [jnp.where(strict, x[0:c] * d_, 0.0) for x, d_ in zip(both, decay)]
    attn = [x[c:2 * c] * d_ for x, d_ in zip(both, decay)]

    d_pow = [jnp.where(diag_blk, m_, 0.0) for m_ in m_low]
    l_mat = [m_ - d_ for m_, d_ in zip(m_low, d_pow)]
    p_inv = [eye - d_ for d_ in d_pow]
    d_pow = [pdot(_split(d_), d_) for d_ in d_pow]
    n_steps = int(np.log2(SOLVE_BLOCK)) - 1
    for step in range(n_steps):
        if step == n_steps - 1:
            p_inv = [p_ + pdot(_split(p_), d_) for p_, d_ in zip(p_inv, d_pow)]
        else:
            prod = [pdot(_split(jnp.concatenate([p_, d_], axis=0)), d_) for p_, d_ in zip(p_inv, d_pow)]
            p_inv = [p_ + x[0:c] for p_, x in zip(p_inv, prod)]
            d_pow = [x[c:2 * c] for x in prod]
    n_pow = [pdot(_split(p_), l_) for p_, l_ in zip(p_inv, l_mat)]
    q_inv = [eye - n_ for n_ in n_pow]
    for _ in range(int(np.log2(c // SOLVE_BLOCK)) - 1):
        n_pow = [pdot(_split(n_), n_) for n_ in n_pow]
        q_inv = [q_ + pdot(_split(q_), n_) for q_, n_ in zip(q_inv, n_pow)]
    a_inv = [pdot(_split(q_), p_) for q_, p_ in zip(q_inv, p_inv)]
    uw = [_dot(a_, jnp.concatenate([blockdiag(v_ * b_), blockdiag(kb_ * jnp.exp(g_))], axis=1))
          for a_, v_, b_, kb_, g_ in zip(a_inv, v, bcol, kb, gcol)]
    u = [x[:, 0:pw] for x in uw]
    w = [x[:, pw:2 * pw] for x in uw]

    s_old = [state[b * n_pairs + p] for b, p in chains]
    on_s = [_dot(jnp.concatenate([w_, q_ * jnp.exp(g_)], axis=0), s_) for w_, q_, g_, s_ in zip(w, q, gcol, s_old)]
    v_new = [u_ - x[0:c] for u_, x in zip(u, on_s)]
    o = [x[c:2 * c] + _dot(a_, blockdiag(vn_)) for x, a_, vn_ in zip(on_s, attn, v_new)]
    k_dec_t = [(k_ * jnp.exp(gl_ - g_)).T for k_, gl_, g_ in zip(k, g_last, gcol)]
    for (b, p), s_, gl_, kt_, vn_ in zip(chains, s_old, g_last, k_dec_t, v_new):
        state[b * n_pairs + p] = s_ * jnp.exp(gl_) + jnp.where(same_head, _dot(kt_, vn_), 0.0)

    inv_dh = 1.0 / dh
    for (b, p), o_ in zip(chains, o):
        zp = z_ref[b, :, p * pw:(p + 1) * pw].astype(F32)
        y = o_ * lax.rsqrt(head_sums(o_ * o_) * inv_dh + NORM_EPS) * on_ref[...]
        o_ref[b, :, p * pw:(p + 1) * pw] = (y * _silu(zp)).astype(o_ref.dtype)


def _gdn(proj, gates, conv_w, gate_params, o_norm, batch, t_len):
    c = GDN_CHUNK
    n_chunks = t_len // c
    w3 = 3 * GDN_WIDTH
    ab_col = w3 // LANES
    proj = proj.reshape(batch, t_len, proj.shape[-1])
    gates = gates.reshape(batch, t_len, gates.shape[-1])
    on_pair = jnp.tile(o_norm.reshape(1, GDN_HEAD_DIM), (1, 2))
    out = pl.pallas_call(
        _gdn_kernel,
        grid=(n_chunks,),
        in_specs=[pl.BlockSpec((batch, c, w3), lambda i: (0, i, 0)),
                  pl.BlockSpec((batch, c, GDN_WIDTH), lambda i: (0, i, 0)),
                  pl.BlockSpec((batch, c, LANES), lambda i: (0, i, ab_col)),
                  pl.BlockSpec((GDN_CONV, w3), lambda i: (0, 0)),
                  pl.BlockSpec((SUBLANES, LANES), lambda i: (0, 0)),
                  pl.BlockSpec((1, 2 * GDN_HEAD_DIM), lambda i: (0, 0))],
        out_specs=pl.BlockSpec((batch, c, GDN_WIDTH), lambda i: (0, i, 0)),
        out_shape=jax.ShapeDtypeStruct((batch, t_len, GDN_WIDTH), BF16),
        scratch_shapes=[pltpu.VMEM((batch, c + SUBLANES, w3), F32),
                        pltpu.VMEM((batch, c, w3), F32),
                        pltpu.VMEM((batch * GDN_HEADS // 2, 2 * GDN_HEAD_DIM, 2 * GDN_HEAD_DIM), F32)],
        compiler_params=_params("arbitrary"),
        name="gdn",
    )(proj, gates, proj, conv_w, gate_params, on_pair)
    return out.reshape(batch * t_len, GDN_WIDTH)


def _mlp_tail(final, x, g_ref, wu_ref, wd_ref, fg_ref, o_ref):
    u = jnp.dot(_rms(x, g_ref[...]).astype(BF16), wu_ref[...], preferred_element_type=F32)
    u = jnp.square(jnp.maximum(u, 0.0))
    y = x + jnp.dot(u.astype(BF16), wd_ref[...], preferred_element_type=F32)
    if final:
        y = _rms(y, fg_ref[...])
    o_ref[...] = y


def _even_tail_kernel(tiles_per_batch, final, x_ref, ya_ref, bg_ref, cg_ref, hs_ref, cgp_ref, hsp_ref,
                      cw_ref, wa_ref, wb_ref, g_ref, wu_ref, wd_ref, fg_ref, o_ref, ubuf):
    tm = x_ref.shape[0]
    hist = cgp_ref.shape[0]
    first = (pl.program_id(0) % tiles_per_batch) == 0
    prev = cgp_ref[...].astype(F32) * hsp_ref[...].astype(F32)
    ubuf[0:hist, :] = jnp.where(first, 0.0, prev)
    u = cg_ref[...].astype(F32) * hs_ref[...].astype(F32)
    ubuf[hist:hist + tm, :] = u
    conv = u * cw_ref[SC_CONV - 1:SC_CONV, :]
    for j in range(SC_CONV - 1):
        shift = SC_CONV - 1 - j
        conv = conv + ubuf[hist - shift:hist - shift + tm, :] * cw_ref[j:j + 1, :]
    yb = bg_ref[...].astype(F32) * conv
    mix = jnp.dot(ya_ref[...], wa_ref[...], preferred_element_type=F32)
    mix = mix + jnp.dot(yb.astype(BF16), wb_ref[...], preferred_element_type=F32)
    _mlp_tail(final, x_ref[...] + mix, g_ref, wu_ref, wd_ref, fg_ref, o_ref)


def _odd_tail_kernel(final, x_ref, a_ref, w_ref, g_ref, wu_ref, wd_ref, fg_ref, o_ref):
    mix = jnp.dot(a_ref[...], w_ref[...], preferred_element_type=F32)
    _mlp_tail(final, x_ref[...] + mix, g_ref, wu_ref, wd_ref, fg_ref, o_ref)


def _resident(a):
    return pl.BlockSpec(a.shape, lambda i: (0, 0), pipeline_mode=pl.Buffered(1))


def _layer_tail(body, x, mixer_args, mixer_specs, mlp, final, scratch=(), tm=512):
    gain, w_up, w_down, final_gain = mlp
    m, d = x.shape
    row = pl.BlockSpec((tm, d), lambda i: (i, 0))
    vec = pl.BlockSpec((1, d), lambda i: (0, 0))
    return pl.pallas_call(
        functools.partial(body, final),
        grid=(m // tm,),
        in_specs=[row, *mixer_specs, vec, _resident(w_up), _resident(w_down), vec],
        out_specs=row,
        out_shape=jax.ShapeDtypeStruct((m, d), F32),
        scratch_shapes=list(scratch),
        compiler_params=_params("parallel"),
        name="layer_tail",
    )(x, *mixer_args, gain.reshape(1, d), w_up, w_down, final_gain.reshape(1, d))


def _even_tail(x, ya, gates, sc_conv, w_a, w_b, t_len, mlp, final, tm=512):
    wd = GDN_WIDTH
    hist = BF16_ROWS
    hb = tm // hist
    prev = lambda col: pl.BlockSpec((hist, wd), lambda i: (jnp.maximum(i * hb - 1, 0), col))
    cur = lambda col: pl.BlockSpec((tm, wd), lambda i: (i, col))
    specs = [cur(0), cur(1), cur(2), cur(3), prev(2), prev(3), _resident(sc_conv), _resident(w_a), _resident(w_b)]
    return _layer_tail(functools.partial(_even_tail_kernel, t_len // tm), x,
                       (ya, gates, gates, gates, gates, gates, sc_conv, w_a, w_b), specs, mlp, final,
                       scratch=[pltpu.VMEM((tm + hist, wd), F32)], tm=tm)


def _odd_tail(x, a, w, mlp, final, tm=512):
    specs = [pl.BlockSpec((tm, a.shape[1]), lambda i: (i, 0)), _resident(w)]
    return _layer_tail(_odd_tail_kernel, x, (a, w), specs, mlp, final, tm=tm)


def _rope(x, cos_t, sin_lo, sin_hi):
    half = ROPE_DIM // 2
    outs = []
    for j in range(x.shape[1] // LANES):
        xs = x[:, j * LANES:(j + 1) * LANES]
        up = pltpu.roll(xs, LANES - half, axis=1)
        down = pltpu.roll(xs, half, axis=1)
        outs.append(xs * cos_t + up * sin_lo + down * sin_hi)
    return jnp.concatenate(outs, axis=-1)


def _odd_proj_kernel(x_ref, g_ref, w_ref, cos_ref, slo_ref, shi_ref,
                     qn_ref, qr_ref, kc_ref, vc_ref, ks_ref, vs_ref, kw_ref, vw_ref, gl_ref):
    h = _rms(x_ref[...], g_ref[...])
    y = jnp.dot(h.astype(BF16), w_ref[...], preferred_element_type=F32)
    cos_t, sin_lo, sin_hi = cos_ref[...], slo_ref[...], shi_ref[...]
    dq = NSA_HEADS * NSA_HEAD_DIM
    kv = NSA_KV_WIDTH
    q = y[:, :dq] * (LOG2_E * NSA_HEAD_DIM ** -0.5)
    qn_ref[...] = q.astype(BF16)
    qr_ref[...] = _rope(q, cos_t, sin_lo, sin_hi).astype(BF16)
    dh = NSA_HEAD_DIM
    tm = y.shape[0]
    for g in range(NSA_KV_GROUPS):
        kc_ref[0, g] = y[:, dq + g * dh:dq + (g + 1) * dh].astype(BF16)
        vc_ref[0, g] = y[:, dq + kv + g * dh:dq + kv + (g + 1) * dh].astype(BF16)
    ones_rows = (lax.broadcasted_iota(jnp.int32, (VALUE_ROWS - dh, tm), 0) == 0).astype(BF16)
    for k_ref, v_ref, col in ((ks_ref, vs_ref, dq + 2 * kv), (kw_ref, vw_ref, dq + 4 * kv)):
        k = _rope(y[:, col:col + kv], cos_t, sin_lo, sin_hi).astype(BF16)
        v_t = y[:, col + kv:col + 2 * kv].T
        for g in range(NSA_KV_GROUPS):
            k_ref[0, g] = k[:, g * dh:(g + 1) * dh]
            v_ref[0, g, 0:dh, :] = v_t[g * dh:(g + 1) * dh].astype(BF16)
            v_ref[0, g, dh:VALUE_ROWS, :] = ones_rows
    gl_ref[...] = y[:, dq + 6 * kv:]


def _odd_proj(x, gain, w, tables, batch, t_len, tm=512):
    m, d = x.shape
    n = w.shape[1]
    n_g, dh = NSA_KV_GROUPS, NSA_HEAD_DIM
    dq = NSA_HEADS * dh
    kv = NSA_KV_WIDTH
    ng = n - dq - 6 * kv
    tpb = t_len // tm
    row = lambda width: pl.BlockSpec((tm, width), lambda i: (i, 0))
    tab = pl.BlockSpec((tm, LANES), lambda i: (i % tpb, 0))
    keys = (pl.BlockSpec((1, n_g, tm, dh), lambda i: (i // tpb, 0, i % tpb, 0)),
            jax.ShapeDtypeStruct((batch, n_g, t_len, dh), BF16))
    values_t = (pl.BlockSpec((1, n_g, VALUE_ROWS, tm), lambda i: (i // tpb, 0, 0, i % tpb)),
                jax.ShapeDtypeStruct((batch, n_g, VALUE_ROWS, t_len), BF16))
    flat = lambda width, dt: (row(width), jax.ShapeDtypeStruct((m, width), dt))
    outs = [flat(dq, BF16), flat(dq, BF16), keys, keys, keys, values_t, keys, values_t, flat(ng, F32)]
    return pl.pallas_call(
        _odd_proj_kernel,
        grid=(m // tm,),
        in_specs=[row(d), pl.BlockSpec((1, d), lambda i: (0, 0)),
                  pl.BlockSpec((d, n), lambda i: (0, 0)), tab, tab, tab],
        out_specs=[spec for spec, _ in outs],
        out_shape=[shape for _, shape in outs],
        compiler_params=_params("parallel"),
        name="odd_proj",
    )(x, gain.reshape(1, d), w, *tables)


def _compress_kernel(xa_ref, xb_ref, pos_ref, a1_ref, a2t_ref, b1_ref, b2_ref, at_ref, b_ref):
    half = xa_ref.shape[2]
    n = xa_ref.shape[1]

    def hidden(x_ref, w1_ref):
        x = x_ref[0]
        top = jnp.dot(x, w1_ref[0:half, :], preferred_element_type=F32)
        bot = jnp.dot(x, w1_ref[half:2 * half, :], preferred_element_type=F32)
        bias = (jnp.dot(pos_ref[:, 0:half], w1_ref[0:half, :], preferred_element_type=F32)
                + jnp.dot(pos_ref[:, half:2 * half], w1_ref[half:2 * half, :], preferred_element_type=F32))
        nxt = pltpu.roll(bot, n - 1, axis=0)
        return _silu(top + nxt + bias[0:1, :]).astype(BF16)

    at_ref[0] = lax.dot_general(a2t_ref[...], hidden(xa_ref, a1_ref), (((1,), (1,)), ((), ())),
                                preferred_element_type=F32).astype(BF16)
    b_ref[0] = jnp.dot(hidden(xb_ref, b1_ref), b2_ref[...], preferred_element_type=F32).astype(BF16)


def _compress(xa, xb, pos, a1, a2t, b1, b2):
    bg, n, half = xa.shape
    dh = NSA_HEAD_DIM
    full = lambda a: pl.BlockSpec(a.shape, lambda i: (0,) * a.ndim)
    return pl.pallas_call(
        _compress_kernel,
        grid=(bg,),
        in_specs=[pl.BlockSpec((1, n, half), lambda i: (i, 0, 0)),
                  pl.BlockSpec((1, n, half), lambda i: (i, 0, 0)),
                  full(pos), full(a1), full(a2t), full(b1), full(b2)],
        out_specs=[pl.BlockSpec((1, dh, n), lambda i: (i, 0, 0)),
                   pl.BlockSpec((1, n, dh), lambda i: (i, 0, 0))],
        out_shape=[jax.ShapeDtypeStruct((bg, dh, n), BF16), jax.ShapeDtypeStruct((bg, n, dh), BF16)],
        compiler_params=_params("parallel"),
        name="compress",
    )(xa, xb, pos, a1, a2t, b1, b2)


MASKED = -1e30


def _nsa_kernel(qn_ref, qr_ref, gl_ref, kc_ref, vo_ref, ks_ref, vst_ref, kw_ref, vwt_ref, expt_ref,
                o_ref, s_even, s_odd, stats, acc_buf):
    qb = Q_BLOCK
    dh = NSA_HEAD_DIM
    hpg = NSA_HPG
    n_cmp = kc_ref.shape[1]
    n_blk = expt_ref.shape[1]
    q0 = pl.program_id(2) * qb
    t_row = q0 + lax.broadcasted_iota(jnp.int32, (1, qb), 1)

    def heads_t(ref):
        xt = ref[...].astype(F32).T
        return jnp.concatenate([xt[h * dh:(h + 1) * dh] for h in range(hpg)], axis=1).astype(BF16)

    qn_t = heads_t(qn_ref)
    qr_t = heads_t(qr_ref)

    def masked(s, mask, fill):
        w = mask.shape[1]
        return jnp.concatenate([jnp.where(mask, s[:, h * w:(h + 1) * w], fill) for h in range(hpg)], axis=1)

    half = qb // N_Q_PARTS

    def part_cols(x, part):
        return jnp.concatenate([x[:, h * qb + part * half:h * qb + (part + 1) * half] for h in range(hpg)], axis=1)

    def join_parts(parts):
        return jnp.concatenate([p[:, h * half:(h + 1) * half] for h in range(hpg) for p in parts], axis=1)

    def exp2_cols(s):
        m = jnp.max(s, axis=0, keepdims=True)
        return jnp.exp2(s - jnp.where(m == -jnp.inf, 0.0, m))

    def compressed(n_vis):
        cmp_end = lax.broadcasted_iota(jnp.int32, (n_vis, 1), 0) * CMP_STRIDE + (CMP_BLOCK - 1)
        s_c = jnp.dot(kc_ref[0, 0:n_vis, :], qn_t, preferred_element_type=F32)
        e_c = exp2_cols(masked(s_c, cmp_end <= t_row, -jnp.inf)).astype(BF16)
        both = jnp.dot(vo_ref[0, :, 0:n_vis], e_c, preferred_element_type=F32)
        acc = both[0:VALUE_ROWS]
        den = acc[dh:dh + 1, :]
        imp_all = both[VALUE_ROWS:VALUE_ROWS + n_blk] * (1.0 / jnp.where(den > 0.0, den, 1.0))
        imp = imp_all[:, 0:qb]
        for h in range(1, hpg):
            imp = imp + imp_all[:, h * qb:(h + 1) * qb]
        return acc, imp

    sizes = sorted({min(n_cmp, n) for n in range(CMP_PREFIX_STEP, n_cmp + CMP_PREFIX_STEP, CMP_PREFIX_STEP)})
    n_vis_max = (q0 + qb - CMP_BLOCK) // CMP_STRIDE + 1
    case = jnp.minimum((n_vis_max + CMP_PREFIX_STEP - 1) // CMP_PREFIX_STEP, len(sizes)) - 1
    acc_c, imp_t = lax.switch(case, [functools.partial(compressed, n) for n in sizes])

    span = WINDOW + half
    acc_w = []
    for part in range(N_Q_PARTS):
        w_start = pl.multiple_of(jnp.maximum(q0 + part * half - WINDOW, 0), half)
        dist = t_row[:, part * half:(part + 1) * half] - (w_start + lax.broadcasted_iota(jnp.int32, (span, 1), 0))
        s_w = jnp.dot(kw_ref[0, 0, pl.ds(w_start, span), :], part_cols(qr_t, part), preferred_element_type=F32)
        e_w = exp2_cols(masked(s_w, (dist >= 0) & (dist < WINDOW), -jnp.inf))
        acc_w.append(jnp.dot(vwt_ref[0, 0, :, pl.ds(w_start, span)], e_w.astype(BF16),
                             preferred_element_type=F32))
    acc_w = join_parts(acc_w)

    cur = (t_row // SEL_BLOCK).astype(F32)

    def pick(n_rows):
        js = lax.broadcasted_iota(jnp.int32, (n_rows, qb), 0).astype(F32)
        forced = (js == 0.0) | (js == cur) | (js == cur - 1.0)
        val = jnp.where((js > cur) | forced, -jnp.inf, imp_t[0:n_rows, :])
        for _ in range(N_SELECT - N_FORCED):
            best = jnp.max(val, axis=0, keepdims=True)
            first = jnp.min(jnp.where(val == best, js, float(n_blk)), axis=0, keepdims=True)
            val = jnp.where(js == first, -jnp.inf, val)
        sel = jnp.where(val == -jnp.inf, 1.0, 0.0)
        return jnp.concatenate([sel, jnp.zeros((n_blk - n_rows, qb), F32)], axis=0) if n_rows < n_blk else sel

    row_sizes = list(range(SEL_PREFIX_STEP, n_blk + SEL_PREFIX_STEP, SEL_PREFIX_STEP))
    n_rows_max = (q0 + qb) // SEL_BLOCK
    row_case = jnp.minimum((n_rows_max + SEL_PREFIX_STEP - 1) // SEL_PREFIX_STEP, len(row_sizes)) - 1
    chosen = lax.switch(row_case, [functools.partial(pick, n) for n in row_sizes]) > 0.5
    js = lax.broadcasted_iota(jnp.int32, (n_blk, qb), 0).astype(F32)
    kt = SEL_KEY_TILE
    n_tiles = ks_ref.shape[2] // kt
    first_own = (q0 // SEL_BLOCK).astype(F32)

    def extended(bias):
        return jnp.concatenate([jnp.concatenate([bias.astype(BF16)] * hpg, axis=1), qr_t], axis=0)

    q_ext_t = extended(jnp.where(chosen & (js < first_own), 0.0, MASKED))
    q_own_t = extended(jnp.where(chosen & (js >= first_own) & (js <= cur), 0.0, MASKED))

    def scores(i, s_ref):
        start = pl.multiple_of(jnp.minimum(i, n_tiles - 1) * kt, kt)
        k_ext = jnp.concatenate([expt_ref[pl.ds(start, kt), :], ks_ref[0, 0, pl.ds(start, kt), :]], axis=1)
        s = jnp.dot(k_ext, q_ext_t, preferred_element_type=F32)
        s_ref[...] = s
        return jnp.max(s, axis=0, keepdims=True)

    def accumulate(s, m_tile, v_t, carry):
        m_run, acc = carry
        m_new = jnp.maximum(m_run, m_tile)
        p = jnp.exp2(s - m_new)
        acc = jnp.exp2(m_run - m_new) * acc + jnp.dot(v_t, p.astype(BF16), preferred_element_type=F32)
        return m_new, acc

    def consume(i, s_ref, m_tile, carry):
        start = pl.multiple_of(i * kt, kt)
        return accumulate(s_ref[...], m_tile, vst_ref[0, 0, :, pl.ds(start, kt)], carry)

    def pair_step(j, carry):
        m_run, acc, m_even = carry
        m_odd = scores(2 * j + 1, s_odd)
        m_run, acc = consume(2 * j, s_even, m_even, (m_run, acc))
        m_even = scores(2 * j + 2, s_even)
        m_run, acc = consume(2 * j + 1, s_odd, m_odd, (m_run, acc))
        return m_run, acc, m_even

    cols = hpg * qb
    init = (jnp.full((1, cols), MASKED, F32), jnp.zeros((VALUE_ROWS, cols), F32), scores(0, s_even))
    n_past = (q0 + kt - 1) // kt
    m_run, acc_s, m_even = lax.fori_loop(0, (n_past - 1) // 2, pair_step, init)
    stats[0:1, :] = m_run
    stats[1:2, :] = m_even
    acc_buf[...] = acc_s
    carried = lambda: (stats[0:1, :], acc_buf[...])

    def hand_over(carry):
        stats[0:1, :], acc_buf[...] = carry

    @pl.when(n_past % 2 == 1)
    def _():
        hand_over(consume(n_past - 1, s_even, stats[1:2, :], carried()))

    @pl.when((n_past % 2 == 0) & (n_past > 0))
    def _():
        m_odd = scores(n_past - 1, s_odd)
        carry = consume(n_past - 2, s_even, stats[1:2, :], carried())
        hand_over(consume(n_past - 1, s_odd, m_odd, carry))

    m_run, acc_s = stats[0:1, :], acc_buf[...]
    acc_parts = []
    for part in range(N_Q_PARTS):
        n_own = (part + 1) * half
        kpos = q0 + lax.broadcasted_iota(jnp.int32, (n_own, 1), 0)
        k_own = jnp.concatenate([expt_ref[pl.ds(q0, n_own), :], ks_ref[0, 0, pl.ds(q0, n_own), :]], axis=1)
        s_own = masked(jnp.dot(k_own, part_cols(q_own_t, part), preferred_element_type=F32),
                       kpos <= t_row[:, part * half:(part + 1) * half], MASKED)
        _, acc_part = accumulate(s_own, jnp.max(s_own, axis=0, keepdims=True), vst_ref[0, 0, :, pl.ds(q0, n_own)],
                                 (part_cols(m_run, part), part_cols(acc_s, part)))
        acc_parts.append(acc_part)
    acc_s = join_parts(acc_parts)

    gates_t = _sigmoid(gl_ref[...].T)
    outs = []
    for h in range(hpg):
        c = slice(h * qb, (h + 1) * qb)
        o_t = jnp.zeros((dh, qb), F32)
        for j, acc in enumerate((acc_c, acc_s, acc_w)):
            den = acc[dh:dh + 1, c]
            o_t = o_t + (gates_t[3 * h + j:3 * h + j + 1, :] / jnp.where(den > 0.0, den, 1.0)) * acc[0:dh, c]
        outs.append(o_t)
    o_ref[...] = jnp.concatenate(outs, axis=0).T.astype(o_ref.dtype)


def _nsa_attention(qn, qr, gl, kc, vo, ks, vst, kw, vwt, expand_t, batch, t_len):
    assert t_len % (2 * SEL_KEY_TILE) == 0, "key tiles of the selected branch are processed in pairs"
    m = qn.shape[0]
    n_g = NSA_KV_GROUPS
    gw = NSA_GROUP_WIDTH
    dh = NSA_HEAD_DIM
    nq = t_len // Q_BLOCK
    n_cmp = kc.shape[1]
    qspec = pl.BlockSpec((Q_BLOCK, gw), lambda b, g, i: (b * nq + i, g))
    keys = pl.BlockSpec((1, 1, t_len, dh), lambda b, g, i: (b, g, 0, 0))
    values_t = pl.BlockSpec((1, 1, VALUE_ROWS, t_len), lambda b, g, i: (b, g, 0, 0))
    const = lambda a: pl.BlockSpec(a.shape, lambda b, g, i: (0, 0))
    return pl.pallas_call(
        _nsa_kernel,
        grid=(batch, n_g, nq),
        in_specs=[qspec, qspec,
                  pl.BlockSpec((Q_BLOCK, LANES), lambda b, g, i: (b * nq + i, g)),
                  pl.BlockSpec((1, n_cmp, dh), lambda b, g, i: (b * n_g + g, 0, 0)),
                  pl.BlockSpec((1,) + vo.shape[1:], lambda b, g, i: (b * n_g + g, 0, 0)),
                  keys, values_t, keys, values_t, const(expand_t)],
        out_specs=qspec,
        out_shape=jax.ShapeDtypeStruct((m, n_g * gw), BF16),
        scratch_shapes=[pltpu.VMEM((SEL_KEY_TILE, NSA_HPG * Q_BLOCK), F32)] * 2
        + [pltpu.VMEM((SUBLANES, NSA_HPG * Q_BLOCK), F32), pltpu.VMEM((VALUE_ROWS, NSA_HPG * Q_BLOCK), F32)],
        compiler_params=_params("parallel", "parallel", "arbitrary"),
        name="nsa_attention",
    )(qn, qr, gl, kc, vo, ks, vst, kw, vwt, expand_t)


def _pad_cols(w, n):
    return jnp.pad(w, ((0, 0), (0, n - w.shape[1])))


def _even_layer(x, gain, w_in, qkv_conv, a_log, dt_bias, o_norm, sc_conv, w_out, batch, t_len, mlp, final):
    gw = GDN_WIDTH
    ab = _pad_cols(w_in[:, 4 * gw:4 * gw + 2 * GDN_HEADS], LANES)
    w_all = jnp.concatenate([w_in[:, :3 * gw], ab, w_in[:, 3 * gw:4 * gw], w_in[:, 4 * gw + 2 * GDN_HEADS:]],
                            axis=1).astype(BF16)
    proj, gates = _norm_matmul(x, gain, w_all, 3 * gw + LANES)
    gate_params = jnp.zeros((SUBLANES, LANES), F32)
    gate_params = gate_params.at[0, :GDN_HEADS].set(a_log).at[1, :GDN_HEADS].set(dt_bias)
    ya = _gdn(proj, gates, qkv_conv, gate_params, o_norm, batch, t_len)
    return _even_tail(x, ya, gates, sc_conv, w_out[:gw].astype(BF16), w_out[gw:].astype(BF16), t_len, mlp, final)


def _rope_tables(t_len):
    half = ROPE_DIM // 2
    inv_freq = ROPE_THETA ** (-jnp.arange(0, ROPE_DIM, 2, dtype=F32) / ROPE_DIM)
    ang = jnp.arange(t_len, dtype=F32)[:, None] * inv_freq[None, :]
    cos, sin = jnp.cos(ang), jnp.sin(ang)
    pad = NSA_HEAD_DIM - ROPE_DIM
    head = lambda a, b, fill: jnp.concatenate([a, b, jnp.full((t_len, pad), fill, F32)], axis=1)
    zeros = jnp.zeros((t_len, half), F32)
    reps = LANES // NSA_HEAD_DIM
    return (jnp.tile(head(cos, cos, 1.0), (1, reps)),
            jnp.tile(head(-sin, zeros, 0.0), (1, reps)),
            jnp.tile(head(zeros, sin, 0.0), (1, reps)))


def _odd_layer(x, gain, w_in, cmp_pos, k_w1, k_w2, v_w1, v_w2, w_out, batch, t_len, mlp, final):
    n_g, dh, hpg = NSA_KV_GROUPS, NSA_HEAD_DIM, NSA_HPG
    dq = NSA_HEADS * dh
    kv = NSA_KV_WIDTH
    gate_w = w_in[:, dq + 6 * kv:].reshape(-1, n_g, 3 * hpg)
    gate_w = jnp.pad(gate_w, ((0, 0), (0, 0), (0, LANES - 3 * hpg))).reshape(-1, n_g * LANES)
    w_all = jnp.concatenate([w_in[:, :dq + 6 * kv], gate_w], axis=1).astype(BF16)
    qn, qr, kc_in, vc_in, ks, vs_t, kw, vw_t, gl = _odd_proj(x, gain, w_all, _rope_tables(t_len), batch, t_len)

    n_chunk = t_len // CMP_STRIDE
    chunked = lambda a: a.reshape(batch * n_g, n_chunk, CMP_STRIDE * dh)

    pos = jnp.zeros((SUBLANES, CMP_BLOCK * dh), F32).at[0].set(cmp_pos.reshape(-1)).astype(BF16)
    vc_t, kc = _compress(chunked(vc_in), chunked(kc_in), pos, v_w1.astype(BF16),
                         v_w2.T.astype(BF16), k_w1.astype(BF16), k_w2.astype(BF16))

    def with_ones(a):
        n = a.shape[-1]
        return jnp.concatenate([a, jnp.ones((a.shape[0], 1, n), a.dtype),
                                jnp.zeros((a.shape[0], VALUE_ROWS - dh - 1, n), a.dtype)], axis=1)

    n_blk = _round_up(t_len // SEL_BLOCK, LANES)
    c_start = np.arange(n_chunk)[None, :] * CMP_STRIDE
    s_start = np.arange(n_blk)[:, None] * SEL_BLOCK
    overlap_t = jnp.asarray((c_start < s_start + SEL_BLOCK) & (c_start + CMP_BLOCK > s_start), BF16)
    expand_t = jnp.asarray((np.arange(t_len)[:, None] // SEL_BLOCK) == np.arange(n_blk)[None, :], BF16)
    vo = jnp.concatenate([with_ones(vc_t), jnp.broadcast_to(overlap_t, (batch * n_g,) + overlap_t.shape)], axis=1)
    o = _nsa_attention(qn, qr, gl, kc, vo, ks, vs_t, kw, vw_t, expand_t, batch, t_len)
    return _odd_tail(x, o, w_out.astype(BF16), mlp, final)


def kernel(x, mix_norm, mlp_norm, w_up, w_down, final_norm, ev_w_in, ev_qkv_conv, ev_a_log, ev_dt_bias,
           ev_o_norm, ev_sc_conv, ev_w_out, od_w_in, od_cmp_pos, od_cmp_k_w1, od_cmp_k_w2, od_cmp_v_w1,
           od_cmp_v_w2, od_w_out):
    batch, t_len, d = x.shape
    depth = mix_norm.shape[0]
    xs = x.reshape(batch * t_len, d)
    for layer in range(depth):
        i = layer // 2
        mlp = (mlp_norm[layer], w_up[layer].astype(BF16), w_down[layer].astype(BF16), final_norm)
        final = layer == depth - 1
        if layer % 2 == 0:
            xs = _even_layer(xs, mix_norm[layer], ev_w_in[i], ev_qkv_conv[i], ev_a_log[i], ev_dt_bias[i],
                             ev_o_norm[i], ev_sc_conv[i], ev_w_out[i], batch, t_len, mlp, final)
        else:
            xs = _odd_layer(xs, mix_norm[layer], od_w_in[i], od_cmp_pos[i], od_cmp_k_w1[i], od_cmp_k_w2[i],
                            od_cmp_v_w1[i], od_cmp_v_w2[i], od_w_out[i], batch, t_len, mlp, final)
    return xs.reshape(batch, t_len, d)
```

```python
import functools

import jax
import jax.numpy as jnp
import numpy as np
from jax import lax
from jax.experimental import pallas as pl
from jax.experimental.pallas import tpu as pltpu

F32 = jnp.float32
BF16 = jnp.bfloat16

NORM_EPS = 1e-6
GDN_HEADS = 8
GDN_HEAD_DIM = 64
GDN_WIDTH = GDN_HEADS * GDN_HEAD_DIM
GDN_CONV = 4
GDN_CHUNK = 64
SOLVE_BLOCK = 16
SC_CONV = 3
NSA_HEADS = 16
NSA_HEAD_DIM = 64
NSA_KV_GROUPS = 4
NSA_HPG = NSA_HEADS // NSA_KV_GROUPS
NSA_GROUP_WIDTH = NSA_HPG * NSA_HEAD_DIM
NSA_KV_WIDTH = NSA_KV_GROUPS * NSA_HEAD_DIM
CMP_BLOCK = 32
CMP_STRIDE = 16
SEL_BLOCK = 64
N_SELECT = 16
N_FORCED = 3
WINDOW = 512
Q_BLOCK = 512
CMP_PREFIX_STEP = 128
N_Q_PARTS = 2
SEL_KEY_TILE = 512
VALUE_ROWS = NSA_HEAD_DIM + 16
ROPE_THETA = 500000.0
ROPE_DIM = NSA_HEAD_DIM // 4
LOG2_E = 1.4426950408889634
LANES = 128
SUBLANES = 8
BF16_ROWS = 16
VMEM_LIMIT = 56 * 1024 * 1024


def _round_up(n, m):
    return (n + m - 1) // m * m


def _params(*semantics):
    return pltpu.CompilerParams(dimension_semantics=semantics, vmem_limit_bytes=VMEM_LIMIT)


def _rms(x, gain):
    return x * lax.rsqrt(jnp.mean(x * x, axis=-1, keepdims=True) + NORM_EPS) * gain


def _sigmoid(x):
    return 1.0 / (1.0 + jnp.exp(-x))


def _silu(x):
    return x * _sigmoid(x)


def _softplus(x):
    return jnp.maximum(x, 0.0) + jnp.log(1.0 + jnp.exp(-jnp.abs(x)))


def _dot(a, b):
    return jnp.dot(a.astype(BF16), b.astype(BF16), preferred_element_type=F32)


def _dot_nt(a, b):
    return lax.dot_general(a.astype(BF16), b.astype(BF16), (((1,), (1,)), ((), ())),
                           preferred_element_type=F32)


def _norm_matmul_kernel(x_ref, g_ref, w_ref, o_ref, ob_ref):
    h = _rms(x_ref[...], g_ref[...])
    y = jnp.dot(h.astype(BF16), w_ref[...], preferred_element_type=F32)
    n = o_ref.shape[1]
    o_ref[...] = y[:, :n]
    ob_ref[...] = y[:, n:].astype(BF16)


def _norm_matmul(x, gain, w, n_f32, tm=512):
    m, d = x.shape
    n = w.shape[1]
    return pl.pallas_call(
        _norm_matmul_kernel,
        grid=(m // tm,),
        in_specs=[pl.BlockSpec((tm, d), lambda i: (i, 0)),
                  pl.BlockSpec((1, d), lambda i: (0, 0)),
                  pl.BlockSpec((d, n), lambda i: (0, 0))],
        out_specs=[pl.BlockSpec((tm, n_f32), lambda i: (i, 0)), pl.BlockSpec((tm, n - n_f32), lambda i: (i, 0))],
        out_shape=[jax.ShapeDtypeStruct((m, n_f32), F32), jax.ShapeDtypeStruct((m, n - n_f32), BF16)],
        compiler_params=_params("parallel"),
        name="norm_proj",
    )(x, gain.reshape(1, d), w)


def _split(a):
    hi = a.astype(BF16)
    return hi, (a - hi.astype(F32)).astype(BF16)


def _dot3(a, b):
    (ah, al), (bh, bl) = a, b
    m = ah.shape[0]
    with_bh = jnp.dot(jnp.concatenate([ah, al], axis=0), bh, preferred_element_type=F32)
    return with_bh[0:m] + with_bh[m:2 * m] + jnp.dot(ah, bl, preferred_element_type=F32)


def _gdn_kernel(qkv_ref, z_ref, ab_ref, cw_ref, gp_ref, on_ref, o_ref, xbuf, act, state):
    c = GDN_CHUNK
    dh = GDN_HEAD_DIM
    pw = 2 * dh
    n_pairs = GDN_HEADS // 2
    n_batch = qkv_ref.shape[0]
    hist = SUBLANES
    chains = [(b, p) for b in range(n_batch) for p in range(n_pairs)]

    @pl.when(pl.program_id(0) == 0)
    def _():
        xbuf[:, 0:hist, :] = jnp.zeros((n_batch, hist, 3 * GDN_WIDTH), F32)
        state[...] = jnp.zeros(state.shape, F32)

    row = lax.broadcasted_iota(jnp.int32, (c, pw), 0)
    lane = lax.broadcasted_iota(jnp.int32, (c, pw), 1)
    first = lane < dh
    col = jnp.where(first, lane, lane - dh)
    incl = row >= col
    strict = row > col
    eye = (row == col).astype(F32)
    diag_blk = strict & ((row // SOLVE_BLOCK) == (col // SOLVE_BLOCK))
    r2 = lax.broadcasted_iota(jnp.int32, (pw, pw), 0)
    c2 = lax.broadcasted_iota(jnp.int32, (pw, pw), 1)
    same_head = (r2 < dh) == (c2 < dh)
    ones_bd = same_head.astype(BF16)
    tri = (lax.broadcasted_iota(jnp.int32, (c, c), 0) >= lax.broadcasted_iota(jnp.int32, (c, c), 1)).astype(BF16)

    def blockdiag(x):
        zero = jnp.zeros_like(x)
        return jnp.concatenate([jnp.where(first, x, zero), jnp.where(first, zero, x)], axis=0)

    def head_sums(x):
        return jnp.dot(x.astype(BF16), ones_bd, preferred_element_type=F32)

    def pdot(xs, y):
        hi, lo = _split(y)
        return _dot3(xs, (blockdiag(hi), blockdiag(lo)))

    gcs, betas = [], []
    for b in range(n_batch):
        x = qkv_ref[b]
        xbuf[b, hist:hist + c, :] = x
        y = x * cw_ref[GDN_CONV - 1:GDN_CONV, :]
        for j in range(GDN_CONV - 1):
            shift = GDN_CONV - 1 - j
            y = y + xbuf[b, hist - shift:hist - shift + c, :] * cw_ref[j:j + 1, :]
        xbuf[b, 0:hist, :] = x[c - hist:c, :]
        act[b] = _silu(y)
        ab = ab_ref[b]
        g_all = -jnp.exp(gp_ref[0:1, :]) * _softplus(ab + gp_ref[1:2, :])
        g_hi, g_rest = _split(g_all)
        g_mid = (g_all - g_hi.astype(F32) - g_rest.astype(F32)).astype(BF16)
        cum = jnp.dot(tri, jnp.concatenate([g_hi, g_rest, g_mid], axis=1), preferred_element_type=F32)
        gcs.append(cum[:, 0:LANES] + cum[:, LANES:2 * LANES] + cum[:, 2 * LANES:3 * LANES])
        betas.append(_sigmoid(ab))

    def pair_cols(a, p, offset):
        return jnp.where(first, a[:, offset + 2 * p:offset + 2 * p + 1], a[:, offset + 2 * p + 1:offset + 2 * p + 2])

    scale = dh ** -0.5
    q = [act[b, :, p * pw:(p + 1) * pw] for b, p in chains]
    k = [act[b, :, GDN_WIDTH + p * pw:GDN_WIDTH + (p + 1) * pw] for b, p in chains]
    v = [act[b, :, 2 * GDN_WIDTH + p * pw:2 * GDN_WIDTH + (p + 1) * pw] for b, p in chains]
    ssq = [head_sums(jnp.concatenate([q_ * q_, k_ * k_], axis=0)) for q_, k_ in zip(q, k)]
    q = [x * lax.rsqrt(s_[0:c] + NORM_EPS) * scale for x, s_ in zip(q, ssq)]
    k = [x * lax.rsqrt(s_[c:2 * c] + NORM_EPS) for x, s_ in zip(k, ssq)]
    gcol = [pair_cols(gcs[b], p, 0) for b, p in chains]
    bcol = [pair_cols(betas[b], p, GDN_HEADS) for b, p in chains]
    grow = [jnp.sum(g * eye, axis=0, keepdims=True) for g in gcol]
    g_last = [g[c - 1:c, :] for g in gcol]
    decay = [jnp.exp(jnp.where(incl, gc_ - gr_, -jnp.inf)) for gc_, gr_ in zip(gcol, grow)]
    kb = [k_ * b_ for k_, b_ in zip(k, bcol)]
    k_bd = [blockdiag(k_).astype(BF16) for k_ in k]
    both = [_dot_nt(jnp.concatenate([kb_, q_], axis=0), kd_) for kb_, q_, kd_ in zip(kb, q, k_bd)]
    m_low = [jnp.where(strict, x[0:c] * d_, 0.0) for x, d_ in zip(both, decay)]
    attn = [x[c:2 * c] * d_ for x, d_ in zip(both, decay)]

    d_pow = [jnp.where(diag_blk, m_, 0.0) for m_ in m_low]
    l_mat = [m_ - d_ for m_, d_ in zip(m_low, d_pow)]
    p_inv = [eye - d_ for d_ in d_pow]
    d_pow = [pdot(_split(d_), d_) for d_ in d_pow]
    n_steps = int(np.log2(SOLVE_BLOCK)) - 1
    for step in range(n_steps):
        if step == n_steps - 1:
            p_inv = [p_ + pdot(_split(p_), d_) for p_, d_ in zip(p_inv, d_pow)]
        else:
            prod = [pdot(_split(jnp.concatenate([p_, d_], axis=0)), d_) for p_, d_ in zip(p_inv, d_pow)]
            p_inv = [p_ + x[0:c] for p_, x in zip(p_inv, prod)]
            d_pow = [x[c:2 * c] for x in prod]
    n_pow = [pdot(_split(p_), l_) for p_, l_ in zip(p_inv, l_mat)]
    q_inv = [eye - n_ for n_ in n_pow]
    for _ in range(int(np.log2(c // SOLVE_BLOCK)) - 1):
        n_pow = [pdot(_split(n_), n_) for n_ in n_pow]
        q_inv = [q_ + pdot(_split(q_), n_) for q_, n_ in zip(q_inv, n_pow)]
    a_inv = [pdot(_split(q_), p_) for q_, p_ in zip(q_inv, p_inv)]
    uw = [_dot(a_, jnp.concatenate([blockdiag(v_ * b_), blockdiag(kb_ * jnp.exp(g_))], axis=1))
          for a_, v_, b_, kb_, g_ in zip(a_inv, v, bcol, kb, gcol)]
    u = [x[:, 0:pw] for x in uw]
    w = [x[:, pw:2 * pw] for x in uw]

    s_old = [state[b * n_pairs + p] for b, p in chains]
    on_s = [_dot(jnp.concatenate([w_, q_ * jnp.exp(g_)], axis=0), s_) for w_, q_, g_, s_ in zip(w, q, gcol, s_old)]
    v_new = [u_ - x[0:c] for u_, x in zip(u, on_s)]
    o = [x[c:2 * c] + _dot(a_, blockdiag(vn_)) for x, a_, vn_ in zip(on_s, attn, v_new)]
    k_dec_t = [(k_ * jnp.exp(gl_ - g_)).T for k_, gl_, g_ in zip(k, g_last, gcol)]
    for (b, p), s_, gl_, kt_, vn_ in zip(chains, s_old, g_last, k_dec_t, v_new):
        state[b * n_pairs + p] = s_ * jnp.exp(gl_) + jnp.where(same_head, _dot(kt_, vn_), 0.0)

    inv_dh = 1.0 / dh
    for (b, p), o_ in zip(chains, o):
        zp = z_ref[b, :, p * pw:(p + 1) * pw].astype(F32)
        y = o_ * lax.rsqrt(head_sums(o_ * o_) * inv_dh + NORM_EPS) * on_ref[...]
        o_ref[b, :, p * pw:(p + 1) * pw] = (y * _silu(zp)).astype(o_ref.dtype)


def _gdn(proj, gates, conv_w, gate_params, o_norm, batch, t_len):
    c = GDN_CHUNK
    n_chunks = t_len // c
    w3 = 3 * GDN_WIDTH
    ab_col = w3 // LANES
    proj = proj.reshape(batch, t_len, proj.shape[-1])
    gates = gates.reshape(batch, t_len, gates.shape[-1])
    on_pair = jnp.tile(o_norm.reshape(1, GDN_HEAD_DIM), (1, 2))
    out = pl.pallas_call(
        _gdn_kernel,
        grid=(n_chunks,),
        in_specs=[pl.BlockSpec((batch, c, w3), lambda i: (0, i, 0)),
                  pl.BlockSpec((batch, c, GDN_WIDTH), lambda i: (0, i, 0)),
                  pl.BlockSpec((batch, c, LANES), lambda i: (0, i, ab_col)),
                  pl.BlockSpec((GDN_CONV, w3), lambda i: (0, 0)),
                  pl.BlockSpec((SUBLANES, LANES), lambda i: (0, 0)),
                  pl.BlockSpec((1, 2 * GDN_HEAD_DIM), lambda i: (0, 0))],
        out_specs=pl.BlockSpec((batch, c, GDN_WIDTH), lambda i: (0, i, 0)),
        out_shape=jax.ShapeDtypeStruct((batch, t_len, GDN_WIDTH), BF16),
        scratch_shapes=[pltpu.VMEM((batch, c + SUBLANES, w3), F32),
                        pltpu.VMEM((batch, c, w3), F32),
                        pltpu.VMEM((batch * GDN_HEADS // 2, 2 * GDN_HEAD_DIM, 2 * GDN_HEAD_DIM), F32)],
        compiler_params=_params("arbitrary"),
        name="gdn",
    )(proj, gates, proj, conv_w, gate_params, on_pair)
    return out.reshape(batch * t_len, GDN_WIDTH)


def _mlp_tail(final, x, g_ref, wu_ref, wd_ref, fg_ref, o_ref):
    u = jnp.dot(_rms(x, g_ref[...]).astype(BF16), wu_ref[...], preferred_element_type=F32)
    u = jnp.square(jnp.maximum(u, 0.0))
    y = x + jnp.dot(u.astype(BF16), wd_ref[...], preferred_element_type=F32)
    if final:
        y = _rms(y, fg_ref[...])
    o_ref[...] = y


def _even_tail_kernel(tiles_per_batch, final, x_ref, ya_ref, bg_ref, cg_ref, hs_ref, cgp_ref, hsp_ref,
                      cw_ref, wa_ref, wb_ref, g_ref, wu_ref, wd_ref, fg_ref, o_ref, ubuf):
    tm = x_ref.shape[0]
    hist = cgp_ref.shape[0]
    first = (pl.program_id(0) % tiles_per_batch) == 0
    prev = cgp_ref[...].astype(F32) * hsp_ref[...].astype(F32)
    ubuf[0:hist, :] = jnp.where(first, 0.0, prev)
    u = cg_ref[...].astype(F32) * hs_ref[...].astype(F32)
    ubuf[hist:hist + tm, :] = u
    conv = u * cw_ref[SC_CONV - 1:SC_CONV, :]
    for j in range(SC_CONV - 1):
        shift = SC_CONV - 1 - j
        conv = conv + ubuf[hist - shift:hist - shift + tm, :] * cw_ref[j:j + 1, :]
    yb = bg_ref[...].astype(F32) * conv
    mix = jnp.dot(ya_ref[...], wa_ref[...], preferred_element_type=F32)
    mix = mix + jnp.dot(yb.astype(BF16), wb_ref[...], preferred_element_type=F32)
    _mlp_tail(final, x_ref[...] + mix, g_ref, wu_ref, wd_ref, fg_ref, o_ref)


def _odd_tail_kernel(final, x_ref, a_ref, w_ref, g_ref, wu_ref, wd_ref, fg_ref, o_ref):
    mix = jnp.dot(a_ref[...], w_ref[...], preferred_element_type=F32)
    _mlp_tail(final, x_ref[...] + mix, g_ref, wu_ref, wd_ref, fg_ref, o_ref)


def _resident(a):
    return pl.BlockSpec(a.shape, lambda i: (0, 0), pipeline_mode=pl.Buffered(1))


def _layer_tail(body, x, mixer_args, mixer_specs, mlp, final, scratch=(), tm=512):
    gain, w_up, w_down, final_gain = mlp
    m, d = x.shape
    row = pl.BlockSpec((tm, d), lambda i: (i, 0))
    vec = pl.BlockSpec((1, d), lambda i: (0, 0))
    return pl.pallas_call(
        functools.partial(body, final),
        grid=(m // tm,),
        in_specs=[row, *mixer_specs, vec, _resident(w_up), _resident(w_down), vec],
        out_specs=row,
        out_shape=jax.ShapeDtypeStruct((m, d), F32),
        scratch_shapes=list(scratch),
        compiler_params=_params("parallel"),
        name="layer_tail",
    )(x, *mixer_args, gain.reshape(1, d), w_up, w_down, final_gain.reshape(1, d))


def _even_tail(x, ya, gates, sc_conv, w_a, w_b, t_len, mlp, final, tm=512):
    wd = GDN_WIDTH
    hist = BF16_ROWS
    hb = tm // hist
    prev = lambda col: pl.BlockSpec((hist, wd), lambda i: (jnp.maximum(i * hb - 1, 0), col))
    cur = lambda col: pl.BlockSpec((tm, wd), lambda i: (i, col))
    specs = [cur(0), cur(1), cur(2), cur(3), prev(2), prev(3), _resident(sc_conv), _resident(w_a), _resident(w_b)]
    return _layer_tail(functools.partial(_even_tail_kernel, t_len // tm), x,
                       (ya, gates, gates, gates, gates, gates, sc_conv, w_a, w_b), specs, mlp, final,
                       scratch=[pltpu.VMEM((tm + hist, wd), F32)], tm=tm)


def _odd_tail(x, a, w, mlp, final, tm=512):
    specs = [pl.BlockSpec((tm, a.shape[1]), lambda i: (i, 0)), _resident(w)]
    return _layer_tail(_odd_tail_kernel, x, (a, w), specs, mlp, final, tm=tm)


def _rope(x, cos_t, sin_lo, sin_hi):
    half = ROPE_DIM // 2
    outs = []
    for j in range(x.shape[1] // LANES):
        xs = x[:, j * LANES:(j + 1) * LANES]
        up = pltpu.roll(xs, LANES - half, axis=1)
        down = pltpu.roll(xs, half, axis=1)
        outs.append(xs * cos_t + up * sin_lo + down * sin_hi)
    return jnp.concatenate(outs, axis=-1)


def _odd_proj_kernel(x_ref, g_ref, w_ref, cos_ref, slo_ref, shi_ref,
                     qn_ref, qr_ref, kc_ref, vc_ref, ks_ref, vs_ref, kw_ref, vw_ref, gl_ref):
    h = _rms(x_ref[...], g_ref[...])
    y = jnp.dot(h.astype(BF16), w_ref[...], preferred_element_type=F32)
    cos_t, sin_lo, sin_hi = cos_ref[...], slo_ref[...], shi_ref[...]
    dq = NSA_HEADS * NSA_HEAD_DIM
    kv = NSA_KV_WIDTH
    q = y[:, :dq] * (LOG2_E * NSA_HEAD_DIM ** -0.5)
    qn_ref[...] = q.astype(BF16)
    qr_ref[...] = _rope(q, cos_t, sin_lo, sin_hi).astype(BF16)
    dh = NSA_HEAD_DIM
    tm = y.shape[0]
    for g in range(NSA_KV_GROUPS):
        kc_ref[0, g] = y[:, dq + g * dh:dq + (g + 1) * dh].astype(BF16)
        vc_ref[0, g] = y[:, dq + kv + g * dh:dq + kv + (g + 1) * dh].astype(BF16)
    ones_rows = (lax.broadcasted_iota(jnp.int32, (VALUE_ROWS - dh, tm), 0) == 0).astype(BF16)
    for k_ref, v_ref, col in ((ks_ref, vs_ref, dq + 2 * kv), (kw_ref, vw_ref, dq + 4 * kv)):
        k = _rope(y[:, col:col + kv], cos_t, sin_lo, sin_hi).astype(BF16)
        v_t = y[:, col + kv:col + 2 * kv].T
        for g in range(NSA_KV_GROUPS):
            k_ref[0, g] = k[:, g * dh:(g + 1) * dh]
            v_ref[0, g, 0:dh, :] = v_t[g * dh:(g + 1) * dh].astype(BF16)
            v_ref[0, g, dh:VALUE_ROWS, :] = ones_rows
    gl_ref[...] = y[:, dq + 6 * kv:]


def _odd_proj(x, gain, w, tables, batch, t_len, tm=512):
    m, d = x.shape
    n = w.shape[1]
    n_g, dh = NSA_KV_GROUPS, NSA_HEAD_DIM
    dq = NSA_HEADS * dh
    kv = NSA_KV_WIDTH
    ng = n - dq - 6 * kv
    tpb = t_len // tm
    row = lambda width: pl.BlockSpec((tm, width), lambda i: (i, 0))
    tab = pl.BlockSpec((tm, LANES), lambda i: (i % tpb, 0))
    keys = (pl.BlockSpec((1, n_g, tm, dh), lambda i: (i // tpb, 0, i % tpb, 0)),
            jax.ShapeDtypeStruct((batch, n_g, t_len, dh), BF16))
    values_t = (pl.BlockSpec((1, n_g, VALUE_ROWS, tm), lambda i: (i // tpb, 0, 0, i % tpb)),
                jax.ShapeDtypeStruct((batch, n_g, VALUE_ROWS, t_len), BF16))
    flat = lambda width, dt: (row(width), jax.ShapeDtypeStruct((m, width), dt))
    outs = [flat(dq, BF16), flat(dq, BF16), keys, keys, keys, values_t, keys, values_t, flat(ng, F32)]
    return pl.pallas_call(
        _odd_proj_kernel,
        grid=(m // tm,),
        in_specs=[row(d), pl.BlockSpec((1, d), lambda i: (0, 0)),
                  pl.BlockSpec((d, n), lambda i: (0, 0)), tab, tab, tab],
        out_specs=[spec for spec, _ in outs],
        out_shape=[shape for _, shape in outs],
        compiler_params=_params("parallel"),
        name="odd_proj",
    )(x, gain.reshape(1, d), w, *tables)


def _compress_kernel(xa_ref, xb_ref, pos_ref, a1_ref, a2t_ref, b1_ref, b2_ref, at_ref, b_ref):
    half = xa_ref.shape[2]
    n = xa_ref.shape[1]

    def hidden(x_ref, w1_ref):
        x = x_ref[0]
        top = jnp.dot(x, w1_ref[0:half, :], preferred_element_type=F32)
        bot = jnp.dot(x, w1_ref[half:2 * half, :], preferred_element_type=F32)
        bias = (jnp.dot(pos_ref[:, 0:half], w1_ref[0:half, :], preferred_element_type=F32)
                + jnp.dot(pos_ref[:, half:2 * half], w1_ref[half:2 * half, :], preferred_element_type=F32))
        nxt = pltpu.roll(bot, n - 1, axis=0)
        return _silu(top + nxt + bias[0:1, :]).astype(BF16)

    at_ref[0] = lax.dot_general(a2t_ref[...], hidden(xa_ref, a1_ref), (((1,), (1,)), ((), ())),
                                preferred_element_type=F32).astype(BF16)
    b_ref[0] = jnp.dot(hidden(xb_ref, b1_ref), b2_ref[...], preferred_element_type=F32).astype(BF16)


def _compress(xa, xb, pos, a1, a2t, b1, b2):
    bg, n, half = xa.shape
    dh = NSA_HEAD_DIM
    full = lambda a: pl.BlockSpec(a.shape, lambda i: (0,) * a.ndim)
    return pl.pallas_call(
        _compress_kernel,
        grid=(bg,),
        in_specs=[pl.BlockSpec((1, n, half), lambda i: (i, 0, 0)),
                  pl.BlockSpec((1, n, half), lambda i: (i, 0, 0)),
                  full(pos), full(a1), full(a2t), full(b1), full(b2)],
        out_specs=[pl.BlockSpec((1, dh, n), lambda i: (i, 0, 0)),
                   pl.BlockSpec((1, n, dh), lambda i: (i, 0, 0))],
        out_shape=[jax.ShapeDtypeStruct((bg, dh, n), BF16), jax.ShapeDtypeStruct((bg, n, dh), BF16)],
        compiler_params=_params("parallel"),
        name="compress",
    )(xa, xb, pos, a1, a2t, b1, b2)


MASKED = -1e30


def _nsa_select_kernel(qn_ref, qr_ref, gl_ref, kc_ref, vo_ref, kw_ref, vwt_ref, part_ref, bias_ref):
    qb = Q_BLOCK
    dh = NSA_HEAD_DIM
    hpg = NSA_HPG
    n_cmp = kc_ref.shape[1]
    n_blk = bias_ref.shape[2] // 2
    q0 = pl.program_id(2) * qb
    t_row = q0 + lax.broadcasted_iota(jnp.int32, (1, qb), 1)

    def heads_t(ref):
        xt = ref[...].astype(F32).T
        return jnp.concatenate([xt[h * dh:(h + 1) * dh] for h in range(hpg)], axis=1).astype(BF16)

    qn_t = heads_t(qn_ref)
    qr_t = heads_t(qr_ref)

    def masked(s, mask, fill):
        w = mask.shape[1]
        return jnp.concatenate([jnp.where(mask, s[:, h * w:(h + 1) * w], fill) for h in range(hpg)], axis=1)

    half = qb // N_Q_PARTS

    def part_cols(x, part):
        return jnp.concatenate([x[:, h * qb + part * half:h * qb + (part + 1) * half] for h in range(hpg)], axis=1)

    def join_parts(parts):
        return jnp.concatenate([p[:, h * half:(h + 1) * half] for h in range(hpg) for p in parts], axis=1)

    def exp2_cols(s):
        m = jnp.max(s, axis=0, keepdims=True)
        return jnp.exp2(s - jnp.where(m == -jnp.inf, 0.0, m))

    def compressed(n_vis):
        cmp_end = lax.broadcasted_iota(jnp.int32, (n_vis, 1), 0) * CMP_STRIDE + (CMP_BLOCK - 1)
        s_c = jnp.dot(kc_ref[0, 0:n_vis, :], qn_t, preferred_element_type=F32)
        e_c = exp2_cols(masked(s_c, cmp_end <= t_row, -jnp.inf)).astype(BF16)
        both = jnp.dot(vo_ref[0, :, 0:n_vis], e_c, preferred_element_type=F32)
        acc = both[0:VALUE_ROWS]
        den = acc[dh:dh + 1, :]
        imp_all = both[VALUE_ROWS:VALUE_ROWS + n_blk] * (1.0 / jnp.where(den > 0.0, den, 1.0))
        imp = imp_all[:, 0:qb]
        for h in range(1, hpg):
            imp = imp + imp_all[:, h * qb:(h + 1) * qb]
        return acc, imp

    sizes = sorted({min(n_cmp, n) for n in range(CMP_PREFIX_STEP, n_cmp + CMP_PREFIX_STEP, CMP_PREFIX_STEP)})
    n_vis_max = (q0 + qb - CMP_BLOCK) // CMP_STRIDE + 1
    case = jnp.minimum((n_vis_max + CMP_PREFIX_STEP - 1) // CMP_PREFIX_STEP, len(sizes)) - 1
    acc_c, imp_t = lax.switch(case, [functools.partial(compressed, n) for n in sizes])

    span = WINDOW + half
    acc_w = []
    for part in range(N_Q_PARTS):
        w_start = pl.multiple_of(jnp.maximum(q0 + part * half - WINDOW, 0), half)
        dist = t_row[:, part * half:(part + 1) * half] - (w_start + lax.broadcasted_iota(jnp.int32, (span, 1), 0))
        s_w = jnp.dot(kw_ref[0, 0, pl.ds(w_start, span), :], part_cols(qr_t, part), preferred_element_type=F32)
        e_w = exp2_cols(masked(s_w, (dist >= 0) & (dist < WINDOW), -jnp.inf))
        acc_w.append(jnp.dot(vwt_ref[0, 0, :, pl.ds(w_start, span)], e_w.astype(BF16),
                             preferred_element_type=F32))
    acc_w = join_parts(acc_w)

    js = lax.broadcasted_iota(jnp.int32, (n_blk, qb), 0).astype(F32)
    cur = (t_row // SEL_BLOCK).astype(F32)
    forced = (js == 0.0) | (js == cur) | (js == cur - 1.0)
    val = jnp.where((js > cur) | forced, -jnp.inf, imp_t)
    for _ in range(N_SELECT - N_FORCED):
        best = jnp.max(val, axis=0, keepdims=True)
        first = jnp.min(jnp.where(val == best, js, float(n_blk)), axis=0, keepdims=True)
        val = jnp.where(js == first, -jnp.inf, val)
    chosen = val == -jnp.inf
    first_own = (q0 // SEL_BLOCK).astype(F32)
    bias_ref[0, 0, 0:n_blk, :] = jnp.where(chosen & (js < first_own), 0.0, MASKED).astype(BF16)
    bias_ref[0, 0, n_blk:2 * n_blk, :] = jnp.where(chosen & (js >= first_own) & (js <= cur), 0.0, MASKED).astype(BF16)

    gates_t = _sigmoid(gl_ref[...].T)
    outs = []
    for h in range(hpg):
        c = slice(h * qb, (h + 1) * qb)
        o_t = jnp.zeros((dh, qb), F32)
        for j, acc in ((0, acc_c), (2, acc_w)):
            den = acc[dh:dh + 1, c]
            o_t = o_t + (gates_t[3 * h + j:3 * h + j + 1, :] / jnp.where(den > 0.0, den, 1.0)) * acc[0:dh, c]
        outs.append(o_t)
    part_ref[...] = jnp.concatenate(outs, axis=0).T


def _nsa_attend_kernel(qr_ref, gl_ref, ks_ref, vst_ref, expt_ref, bias_ref, part_ref, o_ref,
                       s_even, s_odd, stats, acc_buf):
    qb = Q_BLOCK
    dh = NSA_HEAD_DIM
    hpg = NSA_HPG
    n_blk = expt_ref.shape[1]
    q0 = pl.program_id(2) * qb
    t_row = q0 + lax.broadcasted_iota(jnp.int32, (1, qb), 1)

    def heads_t(ref):
        xt = ref[...].astype(F32).T
        return jnp.concatenate([xt[h * dh:(h + 1) * dh] for h in range(hpg)], axis=1).astype(BF16)

    qr_t = heads_t(qr_ref)

    def masked(s, mask, fill):
        w = mask.shape[1]
        return jnp.concatenate([jnp.where(mask, s[:, h * w:(h + 1) * w], fill) for h in range(hpg)], axis=1)

    half = qb // N_Q_PARTS

    def part_cols(x, part):
        return jnp.concatenate([x[:, h * qb + part * half:h * qb + (part + 1) * half] for h in range(hpg)], axis=1)

    def join_parts(parts):
        return jnp.concatenate([p[:, h * half:(h + 1) * half] for h in range(hpg) for p in parts], axis=1)

    def exp2_cols(s):
        m = jnp.max(s, axis=0, keepdims=True)
        return jnp.exp2(s - jnp.where(m == -jnp.inf, 0.0, m))

    kt = SEL_KEY_TILE
    n_tiles = ks_ref.shape[2] // kt

    def extended(bias):
        return jnp.concatenate([jnp.concatenate([bias] * hpg, axis=1), qr_t], axis=0)

    q_ext_t = extended(bias_ref[0, 0, 0:n_blk, :])
    q_own_t = extended(bias_ref[0, 0, n_blk:2 * n_blk, :])


    def scores(i, s_ref):
        start = pl.multiple_of(jnp.minimum(i, n_tiles - 1) * kt, kt)
        k_ext = jnp.concatenate([expt_ref[pl.ds(start, kt), :], ks_ref[0, 0, pl.ds(start, kt), :]], axis=1)
        s = jnp.dot(k_ext, q_ext_t, preferred_element_type=F32)
        s_ref[...] = s
        return jnp.max(s, axis=0, keepdims=True)

    def accumulate(s, m_tile, v_t, carry):
        m_run, acc = carry
        m_new = jnp.maximum(m_run, m_tile)
        p = jnp.exp2(s - m_new)
        acc = jnp.exp2(m_run - m_new) * acc + jnp.dot(v_t, p.astype(BF16), preferred_element_type=F32)
        return m_new, acc

    def consume(i, s_ref, m_tile, carry):
        start = pl.multiple_of(i * kt, kt)
        return accumulate(s_ref[...], m_tile, vst_ref[0, 0, :, pl.ds(start, kt)], carry)

    def pair_step(j, carry):
        m_run, acc, m_even = carry
        m_odd = scores(2 * j + 1, s_odd)
        m_run, acc = consume(2 * j, s_even, m_even, (m_run, acc))
        m_even = scores(2 * j + 2, s_even)
        m_run, acc = consume(2 * j + 1, s_odd, m_odd, (m_run, acc))
        return m_run, acc, m_even

    cols = hpg * qb
    init = (jnp.full((1, cols), MASKED, F32), jnp.zeros((VALUE_ROWS, cols), F32), scores(0, s_even))
    n_past = (q0 + kt - 1) // kt
    m_run, acc_s, m_even = lax.fori_loop(0, (n_past - 1) // 2, pair_step, init)
    stats[0:1, :] = m_run
    stats[1:2, :] = m_even
    acc_buf[...] = acc_s
    carried = lambda: (stats[0:1, :], acc_buf[...])

    def hand_over(carry):
        stats[0:1, :], acc_buf[...] = carry

    @pl.when(n_past % 2 == 1)
    def _():
        hand_over(consume(n_past - 1, s_even, stats[1:2, :], carried()))

    @pl.when((n_past % 2 == 0) & (n_past > 0))
    def _():
        m_odd = scores(n_past - 1, s_odd)
        carry = consume(n_past - 2, s_even, stats[1:2, :], carried())
        hand_over(consume(n_past - 1, s_odd, m_odd, carry))

    m_run, acc_s = stats[0:1, :], acc_buf[...]
    acc_parts = []
    for part in range(N_Q_PARTS):
        n_own = (part + 1) * half
        kpos = q0 + lax.broadcasted_iota(jnp.int32, (n_own, 1), 0)
        k_own = jnp.concatenate([expt_ref[pl.ds(q0, n_own), :], ks_ref[0, 0, pl.ds(q0, n_own), :]], axis=1)
        s_own = masked(jnp.dot(k_own, part_cols(q_own_t, part), preferred_element_type=F32),
                       kpos <= t_row[:, part * half:(part + 1) * half], MASKED)
        _, acc_part = accumulate(s_own, jnp.max(s_own, axis=0, keepdims=True), vst_ref[0, 0, :, pl.ds(q0, n_own)],
                                 (part_cols(m_run, part), part_cols(acc_s, part)))
        acc_parts.append(acc_part)
    acc_s = join_parts(acc_parts)

    gates_t = _sigmoid(gl_ref[...].T)
    outs = []
    for h in range(hpg):
        c = slice(h * qb, (h + 1) * qb)
        den = acc_s[dh:dh + 1, c]
        outs.append((gates_t[3 * h + 1:3 * h + 2, :] / jnp.where(den > 0.0, den, 1.0)) * acc_s[0:dh, c])
    o_ref[...] = (part_ref[...] + jnp.concatenate(outs, axis=0).T).astype(o_ref.dtype)


def _nsa_attention(qn, qr, gl, kc, vo, ks, vst, kw, vwt, expand_t, batch, t_len):
    assert t_len % (2 * SEL_KEY_TILE) == 0, "key tiles of the selected branch are processed in pairs"
    m = qn.shape[0]
    n_g = NSA_KV_GROUPS
    gw = NSA_GROUP_WIDTH
    dh = NSA_HEAD_DIM
    nq = t_len // Q_BLOCK
    n_cmp = kc.shape[1]
    n_blk = expand_t.shape[1]
    qspec = pl.BlockSpec((Q_BLOCK, gw), lambda b, g, i: (b * nq + i, g))
    gates = pl.BlockSpec((Q_BLOCK, LANES), lambda b, g, i: (b * nq + i, g))
    keys = pl.BlockSpec((1, 1, t_len, dh), lambda b, g, i: (b, g, 0, 0))
    values_t = pl.BlockSpec((1, 1, VALUE_ROWS, t_len), lambda b, g, i: (b, g, 0, 0))
    bias = pl.BlockSpec((1, 1, 2 * n_blk, Q_BLOCK), lambda b, g, i: (b, g, 0, i))
    const = lambda a: pl.BlockSpec(a.shape, lambda b, g, i: (0, 0))
    cols = NSA_HPG * Q_BLOCK
    part, biases = pl.pallas_call(
        _nsa_select_kernel,
        grid=(batch, n_g, nq),
        in_specs=[qspec, qspec, gates,
                  pl.BlockSpec((1, n_cmp, dh), lambda b, g, i: (b * n_g + g, 0, 0)),
                  pl.BlockSpec((1,) + vo.shape[1:], lambda b, g, i: (b * n_g + g, 0, 0)),
                  keys, values_t],
        out_specs=[qspec, bias],
        out_shape=[jax.ShapeDtypeStruct((m, n_g * gw), F32),
                   jax.ShapeDtypeStruct((batch, n_g, 2 * n_blk, t_len), BF16)],
        compiler_params=_params("parallel", "parallel", "parallel"),
        name="nsa_select",
    )(qn, qr, gl, kc, vo, kw, vwt)
    return pl.pallas_call(
        _nsa_attend_kernel,
        grid=(batch, n_g, nq),
        in_specs=[qspec, gates, keys, values_t, const(expand_t), bias, qspec],
        out_specs=qspec,
        out_shape=jax.ShapeDtypeStruct((m, n_g * gw), BF16),
        scratch_shapes=[pltpu.VMEM((SEL_KEY_TILE, cols), F32)] * 2
        + [pltpu.VMEM((SUBLANES, cols), F32), pltpu.VMEM((VALUE_ROWS, cols), F32)],
        compiler_params=_params("parallel", "parallel", "arbitrary"),
        name="nsa_attend",
    )(qr, gl, ks, vst, expand_t, biases, part)


def _pad_cols(w, n):
    return jnp.pad(w, ((0, 0), (0, n - w.shape[1])))


def _even_layer(x, gain, w_in, qkv_conv, a_log, dt_bias, o_norm, sc_conv, w_out, batch, t_len, mlp, final):
    gw = GDN_WIDTH
    ab = _pad_cols(w_in[:, 4 * gw:4 * gw + 2 * GDN_HEADS], LANES)
    w_all = jnp.concatenate([w_in[:, :3 * gw], ab, w_in[:, 3 * gw:4 * gw], w_in[:, 4 * gw + 2 * GDN_HEADS:]],
                            axis=1).astype(BF16)
    proj, gates = _norm_matmul(x, gain, w_all, 3 * gw + LANES)
    gate_params = jnp.zeros((SUBLANES, LANES), F32)
    gate_params = gate_params.at[0, :GDN_HEADS].set(a_log).at[1, :GDN_HEADS].set(dt_bias)
    ya = _gdn(proj, gates, qkv_conv, gate_params, o_norm, batch, t_len)
    return _even_tail(x, ya, gates, sc_conv, w_out[:gw].astype(BF16), w_out[gw:].astype(BF16), t_len, mlp, final)


def _rope_tables(t_len):
    half = ROPE_DIM // 2
    inv_freq = ROPE_THETA ** (-jnp.arange(0, ROPE_DIM, 2, dtype=F32) / ROPE_DIM)
    ang = jnp.arange(t_len, dtype=F32)[:, None] * inv_freq[None, :]
    cos, sin = jnp.cos(ang), jnp.sin(ang)
    pad = NSA_HEAD_DIM - ROPE_DIM
    head = lambda a, b, fill: jnp.concatenate([a, b, jnp.full((t_len, pad), fill, F32)], axis=1)
    zeros = jnp.zeros((t_len, half), F32)
    reps = LANES // NSA_HEAD_DIM
    return (jnp.tile(head(cos, cos, 1.0), (1, reps)),
            jnp.tile(head(-sin, zeros, 0.0), (1, reps)),
            jnp.tile(head(zeros, sin, 0.0), (1, reps)))


def _odd_layer(x, gain, w_in, cmp_pos, k_w1, k_w2, v_w1, v_w2, w_out, batch, t_len, mlp, final):
    n_g, dh, hpg = NSA_KV_GROUPS, NSA_HEAD_DIM, NSA_HPG
    dq = NSA_HEADS * dh
    kv = NSA_KV_WIDTH
    gate_w = w_in[:, dq + 6 * kv:].reshape(-1, n_g, 3 * hpg)
    gate_w = jnp.pad(gate_w, ((0, 0), (0, 0), (0, LANES - 3 * hpg))).reshape(-1, n_g * LANES)
    w_all = jnp.concatenate([w_in[:, :dq + 6 * kv], gate_w], axis=1).astype(BF16)
    qn, qr, kc_in, vc_in, ks, vs_t, kw, vw_t, gl = _odd_proj(x, gain, w_all, _rope_tables(t_len), batch, t_len)

    n_chunk = t_len // CMP_STRIDE
    chunked = lambda a: a.reshape(batch * n_g, n_chunk, CMP_STRIDE * dh)

    pos = jnp.zeros((SUBLANES, CMP_BLOCK * dh), F32).at[0].set(cmp_pos.reshape(-1)).astype(BF16)
    vc_t, kc = _compress(chunked(vc_in), chunked(kc_in), pos, v_w1.astype(BF16),
                         v_w2.T.astype(BF16), k_w1.astype(BF16), k_w2.astype(BF16))

    def with_ones(a):
        n = a.shape[-1]
        return jnp.concatenate([a, jnp.ones((a.shape[0], 1, n), a.dtype),
                                jnp.zeros((a.shape[0], VALUE_ROWS - dh - 1, n), a.dtype)], axis=1)

    n_blk = _round_up(t_len // SEL_BLOCK, LANES)
    c_start = np.arange(n_chunk)[None, :] * CMP_STRIDE
    s_start = np.arange(n_blk)[:, None] * SEL_BLOCK
    overlap_t = jnp.asarray((c_start < s_start + SEL_BLOCK) & (c_start + CMP_BLOCK > s_start), BF16)
    expand_t = jnp.asarray((np.arange(t_len)[:, None] // SEL_BLOCK) == np.arange(n_blk)[None, :], BF16)
    vo = jnp.concatenate([with_ones(vc_t), jnp.broadcast_to(overlap_t, (batch * n_g,) + overlap_t.shape)], axis=1)
    o = _nsa_attention(qn, qr, gl, kc, vo, ks, vs_t, kw, vw_t, expand_t, batch, t_len)
    return _odd_tail(x, o, w_out.astype(BF16), mlp, final)


def kernel(x, mix_norm, mlp_norm, w_up, w_down, final_norm, ev_w_in, ev_qkv_conv, ev_a_log, ev_dt_bias,
           ev_o_norm, ev_sc_conv, ev_w_out, od_w_in, od_cmp_pos, od_cmp_k_w1, od_cmp_k_w2, od_cmp_v_w1,
           od_cmp_v_w2, od_w_out):
    batch, t_len, d = x.shape
    depth = mix_norm.shape[0]
    xs = x.reshape(batch * t_len, d)
    for layer in range(depth):
        i = layer // 2
        mlp = (mlp_norm[layer], w_up[layer].astype(BF16), w_down[layer].astype(BF16), final_norm)
        final = layer == depth - 1
        if layer % 2 == 0:
            xs = _even_layer(xs, mix_norm[layer], ev_w_in[i], ev_qkv_conv[i], ev_a_log[i], ev_dt_bias[i],
                             ev_o_norm[i], ev_sc_conv[i], ev_w_out[i], batch, t_len, mlp, final)
        else:
            xs = _odd_layer(xs, mix_norm[layer], od_w_in[i], od_cmp_pos[i], od_cmp_k_w1[i], od_cmp_k_w2[i],
                            od_cmp_v_w1[i], od_cmp_v_w2[i], od_w_out[i], batch, t_len, mlp, final)
    return xs.reshape(batch, t_len, d)
```
